```python
import math
import jax, jax.numpy as jnp
from jax import lax
import numpy as np

D_MODEL = 1024
BATCH = 8
SEQ = 2048
DEPTH = 1
DEC_BATCH = 128
DEC_SEQ = 4
PAST_LEN = 16384
PAGE_SIZE = 128

D_MIX = D_MODEL
D_RG = D_MIX // 2
RG_HEADS = 8
RG_HEAD_DIM = D_RG // RG_HEADS
CONV_WIDTH = 4
RG_C = 8.0
D_S5 = D_MIX - D_RG
S5_GROUP = 16
S5_GROUPS = D_S5 // S5_GROUP
S5_STATE = 64
N_EXPERTS = 64
TOP_K = 8
D_EXPERT = 256
D_SHARED = 256
ROUTED_SCALE = 2.5
DN_ALPHA = (2.0 * DEPTH) ** 0.25
DN_BETA = (8.0 * DEPTH) ** -0.25
LN_EPS = 1e-5

kernel_name = "hymba_rglru_s5_moe_deepnorm_step"


def layer_norm(x, g, b):
    xf = x.astype(jnp.float32)
    mu = jnp.mean(xf, axis=-1, keepdims=True)
    xc = xf - mu
    var = jnp.mean(xc * xc, axis=-1, keepdims=True)
    out = xc * lax.rsqrt(var + LN_EPS) * g.astype(jnp.float32) + b.astype(jnp.float32)
    return out.astype(x.dtype)


def causal_conv(x, buf, w, b):
    L = x.shape[1]
    xp = jnp.concatenate([buf.astype(x.dtype), x], axis=1)
    out = b + sum(xp[:, k:k + L] * w[k] for k in range(CONV_WIDTH))
    return out, xp[:, -(CONV_WIDTH - 1):]


def real_linear_scan(a, b, h0):
    b = b.at[:, 0].add(a[:, 0] * h0)

    def comb(e1, e2):
        a1, b1 = e1
        a2, b2 = e2
        return a1 * a2, a2 * b1 + b2

    _, h = lax.associative_scan(comb, (a, b), axis=1)
    return h


def complex_linear_scan(a_re, a_im, b_re, b_im):
    def comb(e1, e2):
        ar1, ai1, br1, bi1 = e1
        ar2, ai2, br2, bi2 = e2
        return (ar2 * ar1 - ai2 * ai1,
                ar2 * ai1 + ai2 * ar1,
                ar2 * br1 - ai2 * bi1 + br2,
                ar2 * bi1 + ai2 * br1 + bi2)

    _, _, h_re, h_im = lax.associative_scan(comb, (a_re, a_im, b_re, b_im), axis=1)
    return h_re, h_im


def rglru(xc, h0, w_a, b_a, w_i, b_i, lam):
    Bsz, L, _ = xc.shape
    f32 = jnp.float32
    xf = xc.astype(f32)
    xh = xf.reshape(Bsz, L, RG_HEADS, RG_HEAD_DIM)
    r = jax.nn.sigmoid(jnp.einsum('blhi,hij->blhj', xh, w_a.astype(f32)).reshape(Bsz, L, D_RG) + b_a.astype(f32))
    i = jax.nn.sigmoid(jnp.einsum('blhi,hij->blhj', xh, w_i.astype(f32)).reshape(Bsz, L, D_RG) + b_i.astype(f32))
    log_a = -RG_C * r * jax.nn.softplus(-lam.astype(f32))
    a = jnp.exp(log_a)
    bterm = jnp.sqrt(-jnp.expm1(2.0 * log_a)) * (i * xf)
    h = real_linear_scan(a, bterm, h0.astype(f32))
    return h, h[:, -1]


def s5_ssm(u, s_re, s_im, a_re, a_im, log_dt, b_re, b_im, c_re, c_im, d):
    Bsz, L, _ = u.shape
    f32 = jnp.float32
    uf = u.astype(f32)
    ug = uf.reshape(Bsz, L, S5_GROUPS, S5_GROUP)
    lr, li = a_re.astype(f32), a_im.astype(f32)
    dt = jnp.exp(log_dt.astype(f32))[:, None]
    mag = jnp.exp(lr * dt)
    abar_re, abar_im = mag * jnp.cos(li * dt), mag * jnp.sin(li * dt)
    den = lr * lr + li * li
    nr, ni = abar_re - 1.0, abar_im
    coef_re = (nr * lr + ni * li) / den
    coef_im = (ni * lr - nr * li) / den
    br, bi = b_re.astype(f32), b_im.astype(f32)
    bb_re = coef_re[..., None] * br - coef_im[..., None] * bi
    bb_im = coef_re[..., None] * bi + coef_im[..., None] * br
    bu_re = jnp.einsum('blgh,gph->blgp', ug, bb_re)
    bu_im = jnp.einsum('blgh,gph->blgp', ug, bb_im)
    sr, si = s_re.astype(f32), s_im.astype(f32)
    bu_re = bu_re.at[:, 0].add(abar_re * sr - abar_im * si)
    bu_im = bu_im.at[:, 0].add(abar_re * si + abar_im * sr)
    ar = jnp.broadcast_to(abar_re, bu_re.shape)
    ai = jnp.broadcast_to(abar_im, bu_im.shape)
    x_re, x_im = complex_linear_scan(ar, ai, bu_re, bu_im)
    y = (jnp.einsum('blgp,ghp->blgh', x_re, c_re.astype(f32))
         - jnp.einsum('blgp,ghp->blgh', x_im, c_im.astype(f32)))
    y = y.reshape(Bsz, L, D_S5) + d.astype(f32) * uf
    return y, x_re[:, -1], x_im[:, -1]


def mixer(x, conv_buf, rg_h, s5_re, s5_im, p):
    (w_in, conv_w, conv_b, rg_w_a, rg_b_a, rg_w_i, rg_b_i, rg_lam,
     s5_a_re, s5_a_im, s5_log_dt, s5_b_re, s5_b_im, s5_c_re, s5_c_im, s5_d,
     w_glu, b_glu, w_out) = p
    proj = x @ w_in
    x_rg = proj[..., :D_RG]
    g_rg = proj[..., D_RG:2 * D_RG]
    u_s5 = proj[..., 2 * D_RG:]
    xc, new_buf = causal_conv(x_rg, conv_buf, conv_w, conv_b)
    h, h_last = rglru(xc, rg_h, rg_w_a, rg_b_a, rg_w_i, rg_b_i, rg_lam)
    y_rg = h * jax.nn.gelu(g_rg.astype(jnp.float32))
    y_s5, new_re, new_im = s5_ssm(u_s5, s5_re, s5_im, s5_a_re, s5_a_im, s5_log_dt,
                                  s5_b_re, s5_b_im, s5_c_re, s5_c_im, s5_d)
    yg = jax.nn.gelu(y_s5)
    y_s5 = yg * jax.nn.sigmoid(yg @ w_glu.astype(jnp.float32) + b_glu.astype(jnp.float32))
    y = jnp.concatenate([y_rg, y_s5], axis=-1).astype(x.dtype) @ w_out
    return y, new_buf, h_last, new_re, new_im


def moe(x, w_router, router_bias, w_gate_up, w_down, w_shared_up, w_shared_down):
    Bsz, L, D = x.shape
    t = x.reshape(Bsz * L, D)
    T = t.shape[0]
    scores = jax.nn.sigmoid((t @ w_router).astype(jnp.float32))
    _, idx = lax.top_k(scores + router_bias.astype(jnp.float32), TOP_K)
    sel = jnp.take_along_axis(scores, idx, axis=1)
    wts = sel / jnp.sum(sel, axis=-1, keepdims=True) * ROUTED_SCALE
    gates = jnp.zeros_like(scores).at[jnp.arange(T)[:, None], idx].set(wts)
    routed = jnp.zeros((T, D), jnp.float32)
    for e in range(N_EXPERTS):
        gt, up = jnp.split(t @ w_gate_up[e], 2, axis=-1)
        routed = routed + ((jax.nn.silu(gt) * up) @ w_down[e]).astype(jnp.float32) * gates[:, e:e + 1]
    gs, us = jnp.split(t @ w_shared_up, 2, axis=-1)
    shared = ((jax.nn.silu(gs) * us) @ w_shared_down).astype(jnp.float32)
    return (routed + shared).astype(x.dtype).reshape(Bsz, L, D)


def trunk_layer(x, conv_buf, rg_h, s5_re, s5_im, p_mix, p_moe, ln1_g, ln1_b, ln2_g, ln2_b):
    mix, nb, nh, nre, nim = mixer(x, conv_buf, rg_h, s5_re, s5_im, p_mix)
    h = layer_norm(DN_ALPHA * x + mix, ln1_g, ln1_b)
    y = layer_norm(DN_ALPHA * h + moe(h, *p_moe), ln2_g, ln2_b)
    return y, nb, nh, nre, nim


def setup_inputs(seed: int = 0) -> dict:
    key = jax.random.key(seed)
    ks = jax.random.split(key, 40)
    nrm = jax.random.normal
    f32 = jnp.float32
    u = jax.random.uniform(ks[10], (DEPTH, D_RG), f32, 0.9, 0.999)
    s = u ** (1.0 / RG_C)
    return {
        "x_prompt": nrm(ks[0], (BATCH, SEQ, D_MODEL), f32),
        "x_sample": nrm(ks[1], (DEC_BATCH, DEC_SEQ, D_MODEL), f32),
        "state_rg_conv": nrm(ks[2], (DEPTH, DEC_BATCH, CONV_WIDTH - 1, D_RG), f32),
        "state_rg_h": 0.5 * nrm(ks[3], (DEPTH, DEC_BATCH, D_RG), f32),
        "state_s5_re": 0.5 * nrm(ks[4], (DEPTH, DEC_BATCH, S5_GROUPS, S5_STATE), f32),
        "state_s5_im": 0.5 * nrm(ks[5], (DEPTH, DEC_BATCH, S5_GROUPS, S5_STATE), f32),
        "w_in": nrm(ks[6], (DEPTH, D_MODEL, 2 * D_RG + D_S5), f32) * D_MODEL ** -0.5,
        "conv_w": 0.5 * nrm(ks[7], (DEPTH, CONV_WIDTH, D_RG), f32),
        "conv_b": 0.01 * nrm(ks[8], (DEPTH, D_RG), f32),
        "rg_w_a": nrm(ks[9], (DEPTH, RG_HEADS, RG_HEAD_DIM, RG_HEAD_DIM), f32) * RG_HEAD_DIM ** -0.5,
        "rg_b_a": 0.01 * nrm(ks[11], (DEPTH, D_RG), f32),
        "rg_w_i": nrm(ks[12], (DEPTH, RG_HEADS, RG_HEAD_DIM, RG_HEAD_DIM), f32) * RG_HEAD_DIM ** -0.5,
        "rg_b_i": 0.01 * nrm(ks[13], (DEPTH, D_RG), f32),
        "rg_lam": jnp.log(s) - jnp.log1p(-s),
        "s5_a_re": -0.5 + 0.01 * nrm(ks[14], (DEPTH, S5_GROUPS, S5_STATE), f32),
        "s5_a_im": math.pi * jnp.arange(S5_STATE, dtype=f32) + 0.01 * nrm(ks[15], (DEPTH, S5_GROUPS, S5_STATE), f32),
        "s5_log_dt": jax.random.uniform(ks[16], (DEPTH, S5_GROUPS), f32, math.log(0.001), math.log(0.1)),
        "s5_b_re": nrm(ks[17], (DEPTH, S5_GROUPS, S5_STATE, S5_GROUP), f32) * (2.0 * S5_GROUP) ** -0.5,
        "s5_b_im": nrm(ks[18], (DEPTH, S5_GROUPS, S5_STATE, S5_GROUP), f32) * (2.0 * S5_GROUP) ** -0.5,
        "s5_c_re": nrm(ks[19], (DEPTH, S5_GROUPS, S5_GROUP, S5_STATE), f32) * (2.0 * S5_STATE) ** -0.5,
        "s5_c_im": nrm(ks[20], (DEPTH, S5_GROUPS, S5_GROUP, S5_STATE), f32) * (2.0 * S5_STATE) ** -0.5,
        "s5_d": nrm(ks[21], (DEPTH, D_S5), f32),
        "w_glu": nrm(ks[22], (DEPTH, D_S5, D_S5), f32) * D_S5 ** -0.5,
        "b_glu": 0.01 * nrm(ks[23], (DEPTH, D_S5), f32),
        "w_out": nrm(ks[24], (DEPTH, D_MIX, D_MODEL), f32) * D_MIX ** -0.5 * DN_BETA,
        "ln1_g": 1.0 + 0.01 * nrm(ks[25], (DEPTH, D_MODEL), f32),
        "ln1_b": 0.01 * nrm(ks[26], (DEPTH, D_MODEL), f32),
        "w_router": nrm(ks[27], (DEPTH, D_MODEL, N_EXPERTS), f32) * D_MODEL ** -0.5,
        "router_bias": 0.01 * nrm(ks[28], (DEPTH, N_EXPERTS), f32),
        "w_gate_up": nrm(ks[29], (DEPTH, N_EXPERTS, D_MODEL, 2 * D_EXPERT), f32) * D_MODEL ** -0.5,
        "w_down": nrm(ks[30], (DEPTH, N_EXPERTS, D_EXPERT, D_MODEL), f32) * D_EXPERT ** -0.5 * DN_BETA,
        "w_shared_up": nrm(ks[31], (DEPTH, D_MODEL, 2 * D_SHARED), f32) * D_MODEL ** -0.5,
        "w_shared_down": nrm(ks[32], (DEPTH, D_SHARED, D_MODEL), f32) * D_SHARED ** -0.5 * DN_BETA,
        "ln2_g": 1.0 + 0.01 * nrm(ks[33], (DEPTH, D_MODEL), f32),
        "ln2_b": 0.01 * nrm(ks[34], (DEPTH, D_MODEL), f32),
    }


def reference(x_prompt, x_sample, state_rg_conv, state_rg_h, state_s5_re, state_s5_im,
              w_in, conv_w, conv_b, rg_w_a, rg_b_a, rg_w_i, rg_b_i, rg_lam,
              s5_a_re, s5_a_im, s5_log_dt, s5_b_re, s5_b_im, s5_c_re, s5_c_im, s5_d,
              w_glu, b_glu, w_out, ln1_g, ln1_b,
              w_router, router_bias, w_gate_up, w_down, w_shared_up, w_shared_down,
              ln2_g, ln2_b):
    yp, ys = x_prompt, x_sample
    Bp = x_prompt.shape[0]
    pc, ph, pre, pim = [], [], [], []
    sc, sh, sre, sim = [], [], [], []
    for l in range(DEPTH):
        p_mix = (w_in[l], conv_w[l], conv_b[l], rg_w_a[l], rg_b_a[l], rg_w_i[l], rg_b_i[l], rg_lam[l],
                 s5_a_re[l], s5_a_im[l], s5_log_dt[l], s5_b_re[l], s5_b_im[l], s5_c_re[l], s5_c_im[l], s5_d[l],
                 w_glu[l], b_glu[l], w_out[l])
        p_moe = (w_router[l], router_bias[l], w_gate_up[l], w_down[l], w_shared_up[l], w_shared_down[l])
        zc = jnp.zeros((Bp, CONV_WIDTH - 1, D_RG), x_prompt.dtype)
        zh = jnp.zeros((Bp, D_RG), jnp.float32)
        zs = jnp.zeros((Bp, S5_GROUPS, S5_STATE), jnp.float32)
        yp, c1, h1, r1, i1 = trunk_layer(yp, zc, zh, zs, zs, p_mix, p_moe,
                                         ln1_g[l], ln1_b[l], ln2_g[l], ln2_b[l])
        ys, c2, h2, r2, i2 = trunk_layer(ys, state_rg_conv[l], state_rg_h[l], state_s5_re[l], state_s5_im[l],
                                         p_mix, p_moe, ln1_g[l], ln1_b[l], ln2_g[l], ln2_b[l])
        pc.append(c1); ph.append(h1); pre.append(r1); pim.append(i1)
        sc.append(c2); sh.append(h2); sre.append(r2); sim.append(i2)
    return (yp, ys,
            jnp.stack(pc), jnp.stack(ph), jnp.stack(pre), jnp.stack(pim),
            jnp.stack(sc), jnp.stack(sh), jnp.stack(sre), jnp.stack(sim))
```

```python
import functools
import math

import jax
import jax.numpy as jnp
from jax.experimental import pallas as pl
from jax.experimental.pallas import tpu as pltpu

D_MODEL = 1024
D_RG = 512
RG_HEADS = 8
RG_HEAD_DIM = 64
CONV_WIDTH = 4
RG_C = 8.0
D_S5 = 512
S5_GROUP = 16
S5_GROUPS = 32
S5_STATE = 64
S5_N = S5_GROUPS * S5_STATE
N_EXPERTS = 64
TOP_K = 8
D_EXPERT = 256
D_SHARED = 256
ROUTED_SCALE = 2.5
DEPTH = 1
DN_ALPHA = (2.0 * DEPTH) ** 0.25
LN_EPS = 1e-5

SUBLANES = 8
LANES = 128
MXU_DIM = 256
S5_SCAN_COLS = 512
VMEM_LIMIT_BYTES = 56 * 1024 * 1024

bf16 = jnp.bfloat16
f32 = jnp.float32


def _gelu_tanh(x):
    c = math.sqrt(2.0 / math.pi)
    return x * (0.5 * (1.0 + jnp.tanh(c * (x + 0.044715 * (x * x * x)))))


def _layer_norm(x, g, b):
    mu = jnp.mean(x, axis=-1, keepdims=True)
    xc = x - mu
    var = jnp.mean(xc * xc, axis=-1, keepdims=True)
    return xc * jax.lax.rsqrt(var + LN_EPS) * g + b


def _s5_prep_kernel(lr_ref, li_ref, ldt_ref, bre_ref, bim_ref,
                    are_ref, aim_ref, bbre_ref, bbim_ref):
    lr = lr_ref[...]
    li = li_ref[...]
    dt = jnp.exp(ldt_ref[...])
    mag = jnp.exp(lr * dt)
    abar_re = mag * jnp.cos(li * dt)
    abar_im = mag * jnp.sin(li * dt)
    den = lr * lr + li * li
    nr = abar_re - 1.0
    ni = abar_im
    coef_re = (nr * lr + ni * li) / den
    coef_im = (ni * lr - nr * li) / den
    are_ref[...] = abar_re
    aim_ref[...] = abar_im
    half = S5_N // 2
    for k in range(2):
        cre = coef_re[:, k * half:(k + 1) * half]
        cim = coef_im[:, k * half:(k + 1) * half]
        br = bre_ref[k]
        bi = bim_ref[k]
        bbre_ref[k] = (cre * br - cim * bi).astype(bf16)
        bbim_ref[k] = (cre * bi + cim * br).astype(bf16)


def _s5_prep(lr, li, ldt, bre_t, bim_t):
    half = S5_N // 2
    return pl.pallas_call(
        _s5_prep_kernel,
        out_shape=(jax.ShapeDtypeStruct((1, S5_N), f32),
                   jax.ShapeDtypeStruct((1, S5_N), f32),
                   jax.ShapeDtypeStruct((2, MXU_DIM, half), bf16),
                   jax.ShapeDtypeStruct((2, MXU_DIM, half), bf16)),
        name="s5_prep",
    )(lr, li, ldt, bre_t, bim_t)


def _mixer_kernel(x_ref, conv0_ref, h0_ref, s0r_ref, s0i_ref,
                  w_in_ref, conv_w_ref, conv_b_ref, wa_ref, ba_ref, wi_ref, bi_ref, lam_ref,
                  are_ref, aim_ref, bbre_ref, bbim_ref, cre_ref, cim_ref, d_ref,
                  wglu_ref, bglu_ref, wout_ref, ln_g_ref, ln_b_ref,
                  hout_ref, conv_out_ref, hlast_ref, sre_out_ref, sim_out_ref,
                  pad_scr, a_scr, b_scr, bur_scr, bui_scr, hst_scr, sr_scr, si_scr,
                  *, nb, tc):
    rows = nb * tc
    tail = (CONV_WIDTH - 1) * nb
    c = pl.program_id(0)

    @pl.when(c == 0)
    def _():
        pad_scr[0:tail, :] = conv0_ref[...]
        hst_scr[...] = h0_ref[...]
        sr_scr[...] = s0r_ref[...]
        si_scr[...] = s0i_ref[...]

    x = x_ref[...]
    proj = jnp.dot(x.astype(bf16), w_in_ref[...], preferred_element_type=f32)
    x_rg = proj[:, :D_RG]
    g_rg = proj[:, D_RG:2 * D_RG]
    u = proj[:, 2 * D_RG:]

    pad_scr[tail:tail + rows, :] = x_rg
    conv_w = conv_w_ref[...]
    acc = conv_w[0:1, :] * pad_scr[0:rows, :]
    for k in range(1, CONV_WIDTH):
        acc = acc + conv_w[k:k + 1, :] * pad_scr[k * nb:k * nb + rows, :]
    xc = conv_b_ref[...] + acc
    new_tail = pad_scr[rows:rows + tail, :]
    pad_scr[0:tail, :] = new_tail

    xcb = xc.astype(bf16)
    ga = []
    gi = []
    for hh in range(D_RG // MXU_DIM):
        xs = xcb[:, hh * MXU_DIM:(hh + 1) * MXU_DIM]
        ga.append(jnp.dot(xs, wa_ref[hh], preferred_element_type=f32))
        gi.append(jnp.dot(xs, wi_ref[hh], preferred_element_type=f32))
    r = jax.nn.sigmoid(jnp.concatenate(ga, axis=1) + ba_ref[...])
    i = jax.nn.sigmoid(jnp.concatenate(gi, axis=1) + bi_ref[...])
    nlam = -lam_ref[...]
    softplus = jnp.maximum(nlam, 0.0) + jnp.log1p(jnp.exp(-jnp.abs(nlam)))
    log_a = (-RG_C) * r * softplus
    a_scr[...] = jnp.exp(log_a)
    th = jnp.tanh(log_a)
    b_scr[...] = jnp.sqrt((-2.0 * th) / (1.0 - th)) * (i * xc)

    for rg in range(nb // SUBLANES):
        r0 = rg * SUBLANES
        h = hst_scr[r0:r0 + SUBLANES, :]
        for t in range(tc):
            q = t * nb + r0
            h = a_scr[q:q + SUBLANES, :] * h + b_scr[q:q + SUBLANES, :]
            b_scr[q:q + SUBLANES, :] = h
        hst_scr[r0:r0 + SUBLANES, :] = h
    y_rg = b_scr[...] * _gelu_tanh(g_rg)

    ub = u.astype(bf16)
    half = S5_N // 2
    for k in range(2):
        us = ub[:, k * MXU_DIM:(k + 1) * MXU_DIM]
        bur_scr[:, k * half:(k + 1) * half] = jnp.dot(us, bbre_ref[k], preferred_element_type=f32)
        bui_scr[:, k * half:(k + 1) * half] = jnp.dot(us, bbim_ref[k], preferred_element_type=f32)
    for rg in range(nb // SUBLANES):
        r0 = rg * SUBLANES
        for cb in range(S5_N // S5_SCAN_COLS):
            c0 = cb * S5_SCAN_COLS
            ar = jnp.broadcast_to(are_ref[:, c0:c0 + S5_SCAN_COLS], (SUBLANES, S5_SCAN_COLS))
            ai = jnp.broadcast_to(aim_ref[:, c0:c0 + S5_SCAN_COLS], (SUBLANES, S5_SCAN_COLS))
            xr = sr_scr[r0:r0 + SUBLANES, c0:c0 + S5_SCAN_COLS]
            xi = si_scr[r0:r0 + SUBLANES, c0:c0 + S5_SCAN_COLS]
            for t in range(tc):
                q = t * nb + r0
                br = bur_scr[q:q + SUBLANES, c0:c0 + S5_SCAN_COLS]
                bi_ = bui_scr[q:q + SUBLANES, c0:c0 + S5_SCAN_COLS]
                nxr = ar * xr - ai * xi + br
                nxi = ar * xi + ai * xr + bi_
                bur_scr[q:q + SUBLANES, c0:c0 + S5_SCAN_COLS] = nxr
                bui_scr[q:q + SUBLANES, c0:c0 + S5_SCAN_COLS] = nxi
                xr, xi = nxr, nxi
            sr_scr[r0:r0 + SUBLANES, c0:c0 + S5_SCAN_COLS] = xr
            si_scr[r0:r0 + SUBLANES, c0:c0 + S5_SCAN_COLS] = xi
    ys = []
    for j in range(D_S5 // MXU_DIM):
        xrb = bur_scr[:, j * half:(j + 1) * half].astype(bf16)
        xib = bui_scr[:, j * half:(j + 1) * half].astype(bf16)
        ys.append(jnp.dot(xrb, cre_ref[j], preferred_element_type=f32)
                  - jnp.dot(xib, cim_ref[j], preferred_element_type=f32))
    y_s5 = jnp.concatenate(ys, axis=1) + d_ref[...] * u
    yg = _gelu_tanh(y_s5)
    glu = jnp.dot(yg.astype(bf16), wglu_ref[...], preferred_element_type=f32) + bglu_ref[...]
    y_s5 = yg * jax.nn.sigmoid(glu)

    ycat = jnp.concatenate([y_rg, y_s5], axis=1).astype(bf16)
    mix = jnp.dot(ycat, wout_ref[...], preferred_element_type=f32)
    hout_ref[...] = _layer_norm(DN_ALPHA * x + mix, ln_g_ref[...], ln_b_ref[...])

    @pl.when(c == pl.num_programs(0) - 1)
    def _():
        conv_out_ref[...] = pad_scr[0:tail, :]
        hlast_ref[...] = hst_scr[...]
        sre_out_ref[...] = sr_scr[...]
        sim_out_ref[...] = si_scr[...]


def _full(shape):
    n = len(shape)
    return pl.BlockSpec(shape, lambda c: (0,) * n)


def _mixer(x_tm, conv0, h0, s0r, s0i, params, *, nb, tc, name):
    rows_total = x_tm.shape[0]
    rows = nb * tc
    n_chunks = rows_total // rows
    tail = (CONV_WIDTH - 1) * nb
    small = (conv0, h0, s0r, s0i) + tuple(params)
    in_specs = [pl.BlockSpec((rows, D_MODEL), lambda c: (c, 0))]
    in_specs += [_full(a.shape) for a in small]
    out_shape = (jax.ShapeDtypeStruct((rows_total, D_MODEL), f32),
                 jax.ShapeDtypeStruct((tail, D_RG), f32),
                 jax.ShapeDtypeStruct((nb, D_RG), f32),
                 jax.ShapeDtypeStruct((nb, S5_N), f32),
                 jax.ShapeDtypeStruct((nb, S5_N), f32))
    out_specs = (pl.BlockSpec((rows, D_MODEL), lambda c: (c, 0)),
                 _full((tail, D_RG)), _full((nb, D_RG)), _full((nb, S5_N)), _full((nb, S5_N)))
    scratch = [pltpu.VMEM((rows + tail, D_RG), f32),
               pltpu.VMEM((rows, D_RG), f32),
               pltpu.VMEM((rows, D_RG), f32),
               pltpu.VMEM((rows, S5_N), f32),
               pltpu.VMEM((rows, S5_N), f32),
               pltpu.VMEM((nb, D_RG), f32),
               pltpu.VMEM((nb, S5_N), f32),
               pltpu.VMEM((nb, S5_N), f32)]
    return pl.pallas_call(
        functools.partial(_mixer_kernel, nb=nb, tc=tc),
        grid=(n_chunks,),
        in_specs=in_specs,
        out_specs=out_specs,
        out_shape=out_shape,
        scratch_shapes=scratch,
        compiler_params=pltpu.CompilerParams(
            dimension_semantics=("arbitrary",), vmem_limit_bytes=VMEM_LIMIT_BYTES),
        name=name,
    )(x_tm, *small)


def _moe_kernel(h_ref, wr_ref, rb_ref, wgu_ref, wd_ref, wsu_ref, wsd_ref, ln_g_ref, ln_b_ref,
                out_ref, acc_scr, gates_scr, hb_scr):
    e = pl.program_id(1)
    rows = h_ref.shape[0]
    lane = jax.lax.broadcasted_iota(jnp.int32, (rows, LANES), 1)

    @pl.when(e == 0)
    def _():
        hb = h_ref[...].astype(bf16)
        hb_scr[...] = hb
        logits = jnp.dot(hb, wr_ref[...], preferred_element_type=f32)
        scores = jax.nn.sigmoid(logits)
        valid = lane < N_EXPERTS
        biased = jnp.where(valid, scores + rb_ref[...], -jnp.inf)
        lane_f = lane.astype(f32)
        sel = jnp.zeros((rows, LANES), f32)
        for _ in range(TOP_K):
            m = jnp.max(biased, axis=1, keepdims=True)
            idx = jnp.min(jnp.where(biased == m, lane_f, float(LANES)), axis=1, keepdims=True)
            hit = lane_f == idx
            sel = jnp.where(hit, scores, sel)
            biased = jnp.where(hit, -jnp.inf, biased)
        gates_scr[...] = sel / jnp.sum(sel, axis=1, keepdims=True) * ROUTED_SCALE
        su = jnp.dot(hb, wsu_ref[...], preferred_element_type=f32)
        act = (jax.nn.silu(su[:, :D_SHARED]) * su[:, D_SHARED:]).astype(bf16)
        acc_scr[...] = jnp.dot(act, wsd_ref[...], preferred_element_type=f32)

    hb = hb_scr[...]
    gu = jnp.dot(hb, wgu_ref[0].astype(bf16), preferred_element_type=f32)
    act = (jax.nn.silu(gu[:, :D_EXPERT]) * gu[:, D_EXPERT:]).astype(bf16)
    y = jnp.dot(act, wd_ref[0].astype(bf16), preferred_element_type=f32)
    gate = jnp.sum(jnp.where(lane == e, gates_scr[...], 0.0), axis=1, keepdims=True)
    acc_scr[...] += y * gate

    @pl.when(e == pl.num_programs(1) - 1)
    def _():
        out_ref[...] = _layer_norm(DN_ALPHA * h_ref[...] + acc_scr[...], ln_g_ref[...], ln_b_ref[...])


def _moe(h, wr, rb, wgu, wd, wsu, wsd, ln_g, ln_b, *, rows):
    n_tok = h.shape[0]
    n_tiles = n_tok // rows
    const = lambda shape: pl.BlockSpec(shape, lambda i, e: (0,) * len(shape))
    in_specs = [pl.BlockSpec((rows, D_MODEL), lambda i, e: (i, 0)),
                const(wr.shape), const(rb.shape),
                pl.BlockSpec((1, D_MODEL, 2 * D_EXPERT), lambda i, e: (e, 0, 0)),
                pl.BlockSpec((1, D_EXPERT, D_MODEL), lambda i, e: (e, 0, 0)),
                const(wsu.shape), const(wsd.shape), const(ln_g.shape), const(ln_b.shape)]
    return pl.pallas_call(
        _moe_kernel,
        grid=(n_tiles, N_EXPERTS),
        in_specs=in_specs,
        out_specs=pl.BlockSpec((rows, D_MODEL), lambda i, e: (i, 0)),
        out_shape=jax.ShapeDtypeStruct((n_tok, D_MODEL), f32),
        scratch_shapes=[pltpu.VMEM((rows, D_MODEL), f32),
                        pltpu.VMEM((rows, LANES), f32),
                        pltpu.VMEM((rows, D_MODEL), bf16)],
        compiler_params=pltpu.CompilerParams(
            dimension_semantics=("arbitrary", "arbitrary"), vmem_limit_bytes=VMEM_LIMIT_BYTES),
        name="moe_dense",
    )(h, wr, rb, wgu, wd, wsu, wsd, ln_g, ln_b)


def _head_block_diag(w):
    heads_per_tile = MXU_DIM // RG_HEAD_DIM
    w4 = w.reshape(D_RG // MXU_DIM, heads_per_tile, RG_HEAD_DIM, RG_HEAD_DIM)
    eye = jnp.eye(heads_per_tile, dtype=w.dtype)
    return jnp.einsum('thij,hk->thikj', w4, eye).reshape(D_RG // MXU_DIM, MXU_DIM, MXU_DIM)


def _s5_in_tiles(b):
    gpt = S5_GROUPS // 2
    b4 = b.reshape(2, gpt, S5_STATE, S5_GROUP)
    eye = jnp.eye(gpt, dtype=b.dtype)
    return jnp.einsum('kgph,gm->kghmp', b4, eye).reshape(2, gpt * S5_GROUP, gpt * S5_STATE)


def _s5_out_tiles(cw):
    gpt = S5_GROUPS // 2
    c4 = cw.reshape(2, gpt, S5_GROUP, S5_STATE)
    eye = jnp.eye(gpt, dtype=cw.dtype)
    return jnp.einsum('kghp,gm->kgpmh', c4, eye).reshape(2, gpt * S5_STATE, gpt * S5_GROUP)


def _row(v):
    return v.reshape(1, -1)


def kernel(x_prompt, x_sample, state_rg_conv, state_rg_h, state_s5_re, state_s5_im, w_in, conv_w, conv_b, rg_w_a, rg_b_a, rg_w_i, rg_b_i, rg_lam, s5_a_re, s5_a_im, s5_log_dt, s5_b_re, s5_b_im, s5_c_re, s5_c_im, s5_d, w_glu, b_glu, w_out, ln1_g, ln1_b, w_router, router_bias, w_gate_up, w_down, w_shared_up, w_shared_down, ln2_g, ln2_b):
    l = 0
    bp, lp, _ = x_prompt.shape
    bs, ls, _ = x_sample.shape

    are, aim, bbre, bbim = _s5_prep(
        _row(s5_a_re[l]), _row(s5_a_im[l]),
        _row(jnp.repeat(s5_log_dt[l], S5_STATE)),
        _s5_in_tiles(s5_b_re[l]), _s5_in_tiles(s5_b_im[l]))
    params = (w_in[l].astype(bf16), conv_w[l], _row(conv_b[l]),
              _head_block_diag(rg_w_a[l]).astype(bf16), _row(rg_b_a[l]),
              _head_block_diag(rg_w_i[l]).astype(bf16), _row(rg_b_i[l]), _row(rg_lam[l]),
              are, aim, bbre, bbim,
              _s5_out_tiles(s5_c_re[l]).astype(bf16), _s5_out_tiles(s5_c_im[l]).astype(bf16),
              _row(s5_d[l]), w_glu[l].astype(bf16), _row(b_glu[l]), w_out[l].astype(bf16),
              _row(ln1_g[l]), _row(ln1_b[l]))

    tail = CONV_WIDTH - 1
    xp_tm = x_prompt.transpose(1, 0, 2).reshape(lp * bp, D_MODEL)
    hp, pc, ph, pre, pim = _mixer(
        xp_tm, jnp.zeros((tail * bp, D_RG), f32), jnp.zeros((bp, D_RG), f32),
        jnp.zeros((bp, S5_N), f32), jnp.zeros((bp, S5_N), f32), params,
        nb=bp, tc=32, name="mixer_prompt")

    xs_tm = x_sample.transpose(1, 0, 2).reshape(ls * bs, D_MODEL)
    hs, sc, sh, sre, sim = _mixer(
        xs_tm, state_rg_conv[l].transpose(1, 0, 2).reshape(tail * bs, D_RG), state_rg_h[l],
        state_s5_re[l].reshape(bs, S5_N), state_s5_im[l].reshape(bs, S5_N), params,
        nb=bs, tc=ls, name="mixer_sample")

    h_all = jnp.concatenate([hp, hs], axis=0)
    wr = jnp.pad(w_router[l], ((0, 0), (0, LANES - N_EXPERTS))).astype(bf16)
    rb = jnp.pad(_row(router_bias[l]), ((0, 0), (0, LANES - N_EXPERTS)))
    y_all = _moe(h_all, wr, rb, w_gate_up[l], w_down[l],
                 w_shared_up[l].astype(bf16), w_shared_down[l].astype(bf16),
                 _row(ln2_g[l]), _row(ln2_b[l]), rows=1536)
    yp = y_all[:lp * bp].reshape(lp, bp, D_MODEL).transpose(1, 0, 2)
    ys = y_all[lp * bp:].reshape(ls, bs, D_MODEL).transpose(1, 0, 2)

    def conv_out(cv, nbatch):
        return cv.reshape(tail, nbatch, D_RG).transpose(1, 0, 2)[None]

    return (yp, ys,
            conv_out(pc, bp), ph[None],
            pre.reshape(1, bp, S5_GROUPS, S5_STATE), pim.reshape(1, bp, S5_GROUPS, S5_STATE),
            conv_out(sc, bs), sh[None],
            sre.reshape(1, bs, S5_GROUPS, S5_STATE), sim.reshape(1, bs, S5_GROUPS, S5_STATE))
```

```python
import functools
import math

import jax
import jax.numpy as jnp
from jax.experimental import pallas as pl
from jax.experimental.pallas import tpu as pltpu

D_MODEL = 1024
D_RG = 512
RG_HEADS = 8
RG_HEAD_DIM = 64
CONV_WIDTH = 4
RG_C = 8.0
D_S5 = 512
S5_GROUP = 16
S5_GROUPS = 32
S5_STATE = 64
S5_N = S5_GROUPS * S5_STATE
N_EXPERTS = 64
TOP_K = 8
D_EXPERT = 256
D_SHARED = 256
ROUTED_SCALE = 2.5
DEPTH = 1
DN_ALPHA = (2.0 * DEPTH) ** 0.25
LN_EPS = 1e-5

SUBLANES = 8
LANES = 128
MXU_DIM = 256
S5_SCAN_COLS = 512
VMEM_LIMIT_BYTES = 56 * 1024 * 1024

bf16 = jnp.bfloat16
f32 = jnp.float32


def _gelu_tanh(x):
    c = math.sqrt(2.0 / math.pi)
    return x * (0.5 * (1.0 + jnp.tanh(c * (x + 0.044715 * (x * x * x)))))


def _layer_norm(x, g, b):
    mu = jnp.mean(x, axis=-1, keepdims=True)
    xc = x - mu
    var = jnp.mean(xc * xc, axis=-1, keepdims=True)
    return xc * jax.lax.rsqrt(var + LN_EPS) * g + b


def _s5_prep_kernel(lr_ref, li_ref, ldt_ref, bre_ref, bim_ref,
                    are_ref, aim_ref, bbre_ref, bbim_ref):
    lr = lr_ref[...]
    li = li_ref[...]
    dt = jnp.exp(ldt_ref[...])
    mag = jnp.exp(lr * dt)
    abar_re = mag * jnp.cos(li * dt)
    abar_im = mag * jnp.sin(li * dt)
    den = lr * lr + li * li
    nr = abar_re - 1.0
    ni = abar_im
    coef_re = (nr * lr + ni * li) / den
    coef_im = (ni * lr - nr * li) / den
    are_ref[...] = abar_re
    aim_ref[...] = abar_im
    half = S5_N // 2
    for k in range(2):
        cre = coef_re[:, k * half:(k + 1) * half]
        cim = coef_im[:, k * half:(k + 1) * half]
        br = bre_ref[k]
        bi = bim_ref[k]
        bbre_ref[k] = (cre * br - cim * bi).astype(bf16)
        bbim_ref[k] = (cre * bi + cim * br).astype(bf16)


def _s5_prep(lr, li, ldt, bre_t, bim_t):
    half = S5_N // 2
    return pl.pallas_call(
        _s5_prep_kernel,
        out_shape=(jax.ShapeDtypeStruct((1, S5_N), f32),
                   jax.ShapeDtypeStruct((1, S5_N), f32),
                   jax.ShapeDtypeStruct((2, MXU_DIM, half), bf16),
                   jax.ShapeDtypeStruct((2, MXU_DIM, half), bf16)),
        name="s5_prep",
    )(lr, li, ldt, bre_t, bim_t)


def _mixer_kernel(x_ref, conv0_ref, h0_ref, s0r_ref, s0i_ref,
                  w_in_ref, conv_w_ref, conv_b_ref, wa_ref, ba_ref, wi_ref, bi_ref, lam_ref,
                  are_ref, aim_ref, bbre_ref, bbim_ref, cre_ref, cim_ref, d_ref,
                  wglu_ref, bglu_ref, wout_ref, ln_g_ref, ln_b_ref,
                  hout_ref, conv_out_ref, hlast_ref, sre_out_ref, sim_out_ref,
                  pad_scr, a_scr, b_scr, bur_scr, bui_scr, hst_scr, sr_scr, si_scr,
                  *, nb, tc):
    rows = nb * tc
    tail = (CONV_WIDTH - 1) * nb
    c = pl.program_id(0)

    @pl.when(c == 0)
    def _():
        pad_scr[0:tail, :] = conv0_ref[...]
        hst_scr[...] = h0_ref[...]
        sr_scr[...] = s0r_ref[...]
        si_scr[...] = s0i_ref[...]

    x = x_ref[...]
    proj = jnp.dot(x.astype(bf16), w_in_ref[...], preferred_element_type=f32)
    x_rg = proj[:, :D_RG]
    g_rg = proj[:, D_RG:2 * D_RG]
    u = proj[:, 2 * D_RG:]

    pad_scr[tail:tail + rows, :] = x_rg
    conv_w = conv_w_ref[...]
    acc = conv_w[0:1, :] * pad_scr[0:rows, :]
    for k in range(1, CONV_WIDTH):
        acc = acc + conv_w[k:k + 1, :] * pad_scr[k * nb:k * nb + rows, :]
    xc = conv_b_ref[...] + acc
    new_tail = pad_scr[rows:rows + tail, :]
    pad_scr[0:tail, :] = new_tail

    xcb = xc.astype(bf16)
    ga = []
    gi = []
    for hh in range(D_RG // MXU_DIM):
        xs = xcb[:, hh * MXU_DIM:(hh + 1) * MXU_DIM]
        ga.append(jnp.dot(xs, wa_ref[hh], preferred_element_type=f32))
        gi.append(jnp.dot(xs, wi_ref[hh], preferred_element_type=f32))
    r = jax.nn.sigmoid(jnp.concatenate(ga, axis=1) + ba_ref[...])
    i = jax.nn.sigmoid(jnp.concatenate(gi, axis=1) + bi_ref[...])
    nlam = -lam_ref[...]
    softplus = jnp.maximum(nlam, 0.0) + jnp.log1p(jnp.exp(-jnp.abs(nlam)))
    log_a = (-RG_C) * r * softplus
    a_scr[...] = jnp.exp(log_a)
    th = jnp.tanh(log_a)
    b_scr[...] = jnp.sqrt((-2.0 * th) / (1.0 - th)) * (i * xc)

    for rg in range(nb // SUBLANES):
        r0 = rg * SUBLANES
        h = hst_scr[r0:r0 + SUBLANES, :]
        for t in range(tc):
            q = t * nb + r0
            h = a_scr[q:q + SUBLANES, :] * h + b_scr[q:q + SUBLANES, :]
            b_scr[q:q + SUBLANES, :] = h
        hst_scr[r0:r0 + SUBLANES, :] = h
    y_rg = b_scr[...] * _gelu_tanh(g_rg)

    ub = u.astype(bf16)
    half = S5_N // 2
    for k in range(2):
        us = ub[:, k * MXU_DIM:(k + 1) * MXU_DIM]
        bur_scr[:, k * half:(k + 1) * half] = jnp.dot(us, bbre_ref[k], preferred_element_type=f32)
        bui_scr[:, k * half:(k + 1) * half] = jnp.dot(us, bbim_ref[k], preferred_element_type=f32)
    for rg in range(nb // SUBLANES):
        r0 = rg * SUBLANES
        for cb in range(S5_N // S5_SCAN_COLS):
            c0 = cb * S5_SCAN_COLS
            ar = jnp.broadcast_to(are_ref[:, c0:c0 + S5_SCAN_COLS], (SUBLANES, S5_SCAN_COLS))
            ai = jnp.broadcast_to(aim_ref[:, c0:c0 + S5_SCAN_COLS], (SUBLANES, S5_SCAN_COLS))
            xr = sr_scr[r0:r0 + SUBLANES, c0:c0 + S5_SCAN_COLS]
            xi = si_scr[r0:r0 + SUBLANES, c0:c0 + S5_SCAN_COLS]
            for t in range(tc):
                q = t * nb + r0
                br = bur_scr[q:q + SUBLANES, c0:c0 + S5_SCAN_COLS]
                bi_ = bui_scr[q:q + SUBLANES, c0:c0 + S5_SCAN_COLS]
                nxr = ar * xr - ai * xi + br
                nxi = ar * xi + ai * xr + bi_
                bur_scr[q:q + SUBLANES, c0:c0 + S5_SCAN_COLS] = nxr
                bui_scr[q:q + SUBLANES, c0:c0 + S5_SCAN_COLS] = nxi
                xr, xi = nxr, nxi
            sr_scr[r0:r0 + SUBLANES, c0:c0 + S5_SCAN_COLS] = xr
            si_scr[r0:r0 + SUBLANES, c0:c0 + S5_SCAN_COLS] = xi
    ys = []
    for j in range(D_S5 // MXU_DIM):
        xrb = bur_scr[:, j * half:(j + 1) * half].astype(bf16)
        xib = bui_scr[:, j * half:(j + 1) * half].astype(bf16)
        ys.append(jnp.dot(xrb, cre_ref[j], preferred_element_type=f32)
                  - jnp.dot(xib, cim_ref[j], preferred_element_type=f32))
    y_s5 = jnp.concatenate(ys, axis=1) + d_ref[...] * u
    yg = _gelu_tanh(y_s5)
    glu = jnp.dot(yg.astype(bf16), wglu_ref[...], preferred_element_type=f32) + bglu_ref[...]
    y_s5 = yg * jax.nn.sigmoid(glu)

    ycat = jnp.concatenate([y_rg, y_s5], axis=1).astype(bf16)
    mix = jnp.dot(ycat, wout_ref[...], preferred_element_type=f32)
    hout_ref[...] = _layer_norm(DN_ALPHA * x + mix, ln_g_ref[...], ln_b_ref[...])

    @pl.when(c == pl.num_programs(0) - 1)
    def _():
        conv_out_ref[...] = pad_scr[0:tail, :]
        hlast_ref[...] = hst_scr[...]
        sre_out_ref[...] = sr_scr[...]
        sim_out_ref[...] = si_scr[...]


def _full(shape):
    n = len(shape)
    return pl.BlockSpec(shape, lambda c: (0,) * n)


def _mixer(x_tm, conv0, h0, s0r, s0i, params, *, nb, tc, name):
    rows_total = x_tm.shape[0]
    rows = nb * tc
    n_chunks = rows_total // rows
    tail = (CONV_WIDTH - 1) * nb
    small = (conv0, h0, s0r, s0i) + tuple(params)
    in_specs = [pl.BlockSpec((rows, D_MODEL), lambda c: (c, 0))]
    in_specs += [_full(a.shape) for a in small]
    out_shape = (jax.ShapeDtypeStruct((rows_total, D_MODEL), f32),
                 jax.ShapeDtypeStruct((tail, D_RG), f32),
                 jax.ShapeDtypeStruct((nb, D_RG), f32),
                 jax.ShapeDtypeStruct((nb, S5_N), f32),
                 jax.ShapeDtypeStruct((nb, S5_N), f32))
    out_specs = (pl.BlockSpec((rows, D_MODEL), lambda c: (c, 0)),
                 _full((tail, D_RG)), _full((nb, D_RG)), _full((nb, S5_N)), _full((nb, S5_N)))
    scratch = [pltpu.VMEM((rows + tail, D_RG), f32),
               pltpu.VMEM((rows, D_RG), f32),
               pltpu.VMEM((rows, D_RG), f32),
               pltpu.VMEM((rows, S5_N), f32),
               pltpu.VMEM((rows, S5_N), f32),
               pltpu.VMEM((nb, D_RG), f32),
               pltpu.VMEM((nb, S5_N), f32),
               pltpu.VMEM((nb, S5_N), f32)]
    return pl.pallas_call(
        functools.partial(_mixer_kernel, nb=nb, tc=tc),
        grid=(n_chunks,),
        in_specs=in_specs,
        out_specs=out_specs,
        out_shape=out_shape,
        scratch_shapes=scratch,
        compiler_params=pltpu.CompilerParams(
            dimension_semantics=("arbitrary",), vmem_limit_bytes=VMEM_LIMIT_BYTES),
        name=name,
    )(x_tm, *small)


TOK_TILE = 256
RUN_ALIGN = 16
SLOTS = 3072
SLOT_CHUNK = 512
MAX_CHUNKS = SLOTS // RUN_ALIGN
ROW_TILE = 512
EXT = D_MODEL + 2 * LANES
NO_RUN = 1.0e9


def _top_k_gates(scores, rb):
    rows = scores.shape[0]
    lane_f = jax.lax.broadcasted_iota(jnp.int32, (rows, LANES), 1).astype(f32)
    biased = jnp.where(lane_f < float(N_EXPERTS), scores + rb, -jnp.inf)
    sel = jnp.zeros((rows, LANES), f32)
    mask = jnp.zeros((rows, LANES), f32)
    for _ in range(TOP_K):
        m = jnp.max(biased, axis=1, keepdims=True)
        idx = jnp.min(jnp.where(biased == m, lane_f, float(LANES)), axis=1, keepdims=True)
        hit = lane_f == idx
        sel = jnp.where(hit, scores, sel)
        mask = jnp.where(hit, 1.0, mask)
        biased = jnp.where(hit, -jnp.inf, biased)
    gates = sel / jnp.sum(sel, axis=1, keepdims=True) * ROUTED_SCALE
    return mask, gates


def _router_kernel(h_ref, wr_ref, rb_ref, hext_ref, rankm_ref, cnt_ref):
    hb = h_ref[...].astype(bf16)
    scores = jax.nn.sigmoid(jnp.dot(hb, wr_ref[...], preferred_element_type=f32))
    mask, gates = _top_k_gates(scores, rb_ref[...])
    t_row = jax.lax.broadcasted_iota(jnp.int32, (TOK_TILE, TOK_TILE), 0)
    t_col = jax.lax.broadcasted_iota(jnp.int32, (TOK_TILE, TOK_TILE), 1)
    earlier = jnp.where(t_col < t_row, 1.0, 0.0).astype(bf16)
    rank = jnp.dot(earlier, mask.astype(bf16), preferred_element_type=f32)
    rankm_ref[...] = jnp.where(mask > 0.0, rank, -1.0).astype(bf16)
    cnt_ref[0] = jnp.broadcast_to(jnp.sum(mask, axis=0, keepdims=True), (SUBLANES, LANES))
    g_hi = gates.astype(bf16)
    g_lo = (gates - g_hi.astype(f32)).astype(bf16)
    hext_ref[:, :D_MODEL] = hb
    hext_ref[:, D_MODEL:D_MODEL + LANES] = g_hi
    hext_ref[:, D_MODEL + LANES:] = g_lo


def _router(h, wr, rb):
    n_tiles = h.shape[0] // TOK_TILE
    const = lambda shape: pl.BlockSpec(shape, lambda i: (0,) * len(shape))
    return pl.pallas_call(
        _router_kernel,
        grid=(n_tiles,),
        in_specs=[pl.BlockSpec((TOK_TILE, D_MODEL), lambda i: (i, 0)), const(wr.shape), const(rb.shape)],
        out_specs=(pl.BlockSpec((TOK_TILE, EXT), lambda i: (i, 0)),
                   pl.BlockSpec((TOK_TILE, LANES), lambda i: (i, 0)),
                   pl.BlockSpec((1, SUBLANES, LANES), lambda i: (i, 0, 0))),
        out_shape=(jax.ShapeDtypeStruct((h.shape[0], EXT), bf16),
                   jax.ShapeDtypeStruct((h.shape[0], LANES), bf16),
                   jax.ShapeDtypeStruct((n_tiles, SUBLANES, LANES), f32)),
        compiler_params=pltpu.CompilerParams(dimension_semantics=("arbitrary",)),
        name="moe_router",
    )(h, wr, rb)


def _run_copy(src, dst, sem):
    return pltpu.make_async_copy(src, dst, sem)


def _dispatch_kernel(nck_ref, dstk_ref, zst_ref, zch_ref,
                     hext_ref, rankm_ref, locs_ref, sorted_ref,
                     stage, zbuf, sem, zsem):
    i = pl.program_id(0)
    n = pl.num_programs(0)
    slot = i % 2
    loc_row = locs_ref[0, 0:1, :]
    end_row = locs_ref[0, 1:2, :]
    rankm = rankm_ref[...]
    hext = hext_ref[...]
    for ch in range(SLOTS // SLOT_CHUNK):
        s = (jax.lax.broadcasted_iota(jnp.int32, (SLOT_CHUNK, LANES), 0) + ch * SLOT_CHUNK).astype(f32)
        in_run = jnp.where(s >= loc_row, jnp.where(s < end_row, 1.0, 0.0), 0.0)
        r_col = s[:, 0:1] - jnp.sum(in_run * loc_row, axis=1, keepdims=True)
        q = jax.lax.dot_general(in_run.astype(bf16), rankm, (((1,), (1,)), ((), ())),
                                preferred_element_type=f32)
        p = jnp.where(q == r_col, 1.0, 0.0).astype(bf16)
        rows = jnp.dot(p, hext, preferred_element_type=f32)
        stage[slot, ch * SLOT_CHUNK:(ch + 1) * SLOT_CHUNK, :] = rows.astype(bf16)

    def start_run(k, carry):
        src = stage.at[slot, pl.ds(pl.multiple_of(k * RUN_ALIGN, RUN_ALIGN), RUN_ALIGN)]
        dst = sorted_ref.at[pl.ds(pl.multiple_of(dstk_ref[i, k], RUN_ALIGN), RUN_ALIGN)]
        _run_copy(src, dst, sem.at[slot]).start()
        return carry
    jax.lax.fori_loop(0, nck_ref[i], start_run, 0)

    @pl.when(i == 0)
    def _():
        zbuf[...] = jnp.zeros(zbuf.shape, bf16)

    @pl.when(i < N_EXPERTS)
    def _():
        def zero_copy(k):
            dst = sorted_ref.at[pl.ds(pl.multiple_of(zst_ref[i] + k * RUN_ALIGN, RUN_ALIGN), RUN_ALIGN)]
            return _run_copy(zbuf, dst, zsem.at[0])

        def start_zero(k, carry):
            zero_copy(k).start()
            return carry

        def wait_zero(k, carry):
            zero_copy(k).wait()
            return carry
        jax.lax.fori_loop(0, zch_ref[i], start_zero, 0)
        jax.lax.fori_loop(0, zch_ref[i], wait_zero, 0)

    def wait_runs(tile, slot_):
        def wait_run(k, carry):
            src = stage.at[slot_, pl.ds(0, RUN_ALIGN)]
            dst = sorted_ref.at[pl.ds(0, RUN_ALIGN)]
            _run_copy(src, dst, sem.at[slot_]).wait()
            return carry
        jax.lax.fori_loop(0, nck_ref[tile], wait_run, 0)

    @pl.when(i > 0)
    def _():
        wait_runs(i - 1, 1 - slot)

    @pl.when(i == n - 1)
    def _():
        wait_runs(i, slot)


def _dispatch(hext, rankm, locs, nck, dstk, zst, zch, *, max_rows):
    n_tiles = hext.shape[0] // TOK_TILE
    grid_spec = pltpu.PrefetchScalarGridSpec(
        num_scalar_prefetch=4,
        grid=(n_tiles,),
        in_specs=[pl.BlockSpec((TOK_TILE, EXT), lambda i, *_: (i, 0)),
                  pl.BlockSpec((TOK_TILE, LANES), lambda i, *_: (i, 0)),
                  pl.BlockSpec((1, SUBLANES, LANES), lambda i, *_: (i, 0, 0))],
        out_specs=pl.BlockSpec(memory_space=pl.ANY),
        scratch_shapes=[pltpu.VMEM((2, SLOTS, EXT), bf16),
                        pltpu.VMEM((RUN_ALIGN, EXT), bf16),
                        pltpu.SemaphoreType.DMA((2,)),
                        pltpu.SemaphoreType.DMA((1,))])
    return pl.pallas_call(
        _dispatch_kernel,
        grid_spec=grid_spec,
        out_shape=jax.ShapeDtypeStruct((max_rows, EXT), bf16),
        compiler_params=pltpu.CompilerParams(
            dimension_semantics=("arbitrary",), vmem_limit_bytes=VMEM_LIMIT_BYTES),
        name="moe_dispatch",
    )(nck, dstk, zst, zch, hext, rankm, locs)


def _experts_kernel(te_ref, tb_ref, nv_ref, x_ref, wgu_ref, wd_ref, y_ref, wgu_b, wd_b):
    j = pl.program_id(0)
    e = te_ref[j]
    changed = jnp.logical_or(j == 0, e != te_ref[jnp.maximum(j - 1, 0)])

    @pl.when(changed)
    def _():
        wgu_b[...] = wgu_ref[0].astype(bf16)
        wd_b[...] = wd_ref[0].astype(bf16)

    @pl.when(j < nv_ref[0])
    def _():
        x = x_ref[:, :D_MODEL]
        gu = jnp.dot(x, wgu_b[...], preferred_element_type=f32)
        act = (jax.nn.silu(gu[:, :D_EXPERT]) * gu[:, D_EXPERT:]).astype(bf16)
        y = jnp.dot(act, wd_b[...], preferred_element_type=f32)
        gate_all = (x_ref[:, D_MODEL:D_MODEL + LANES].astype(f32)
                    + x_ref[:, D_MODEL + LANES:].astype(f32))
        lane = jax.lax.broadcasted_iota(jnp.int32, (ROW_TILE, LANES), 1)
        gate = jnp.sum(jnp.where(lane == e, gate_all, 0.0), axis=1, keepdims=True)
        y_ref[...] = (y * gate).astype(bf16)


def _experts(xs, wgu, wd, te, tb, nv):
    n_row_tiles = xs.shape[0] // ROW_TILE
    grid_spec = pltpu.PrefetchScalarGridSpec(
        num_scalar_prefetch=3,
        grid=(n_row_tiles,),
        in_specs=[pl.BlockSpec((ROW_TILE, EXT), lambda j, te, tb, nv: (tb[j], 0)),
                  pl.BlockSpec((1, D_MODEL, 2 * D_EXPERT), lambda j, te, tb, nv: (te[j], 0, 0)),
                  pl.BlockSpec((1, D_EXPERT, D_MODEL), lambda j, te, tb, nv: (te[j], 0, 0))],
        out_specs=pl.BlockSpec((ROW_TILE, D_MODEL), lambda j, te, tb, nv: (tb[j], 0)),
        scratch_shapes=[pltpu.VMEM((D_MODEL, 2 * D_EXPERT), bf16),
                        pltpu.VMEM((D_EXPERT, D_MODEL), bf16)])
    return pl.pallas_call(
        _experts_kernel,
        grid_spec=grid_spec,
        out_shape=jax.ShapeDtypeStruct((xs.shape[0], D_MODEL), bf16),
        compiler_params=pltpu.CompilerParams(
            dimension_semantics=("arbitrary",), vmem_limit_bytes=VMEM_LIMIT_BYTES),
        name="moe_experts",
    )(te, tb, nv, xs, wgu, wd)


def _combine_kernel(nck_ref, dstk_ref, tot_ref,
                    h_ref, rankm_ref, locc_ref, ys_ref, wsu_ref, wsd_ref, ln_g_ref, ln_b_ref,
                    out_ref, yloc, sem):
    i = pl.program_id(0)
    n = pl.num_programs(0)
    slot = i % 2

    def run_copy(tile, slot_, k):
        src = ys_ref.at[pl.ds(pl.multiple_of(dstk_ref[tile, k], RUN_ALIGN), RUN_ALIGN)]
        dst = yloc.at[slot_, pl.ds(pl.multiple_of(k * RUN_ALIGN, RUN_ALIGN), RUN_ALIGN)]
        return _run_copy(src, dst, sem.at[slot_])

    def fetch(tile, slot_):
        def body(k, carry):
            run_copy(tile, slot_, k).start()
            return carry
        jax.lax.fori_loop(0, nck_ref[tile], body, 0)

    @pl.when(i == 0)
    def _():
        fetch(0, 0)

    @pl.when(i + 1 < n)
    def _():
        fetch(i + 1, 1 - slot)

    def zero_rows(k, carry):
        yloc[slot, pl.ds(pl.multiple_of(k * RUN_ALIGN, RUN_ALIGN), RUN_ALIGN), :] = jnp.zeros(
            (RUN_ALIGN, D_MODEL), bf16)
        return carry
    jax.lax.fori_loop(tot_ref[i] // RUN_ALIGN, MAX_CHUNKS, zero_rows, 0)

    h = h_ref[...]
    hb = h.astype(bf16)
    su = jnp.dot(hb, wsu_ref[...], preferred_element_type=f32)
    act = (jax.nn.silu(su[:, :D_SHARED]) * su[:, D_SHARED:]).astype(bf16)
    acc = jnp.dot(act, wsd_ref[...], preferred_element_type=f32)

    def wait_body(k, carry):
        run_copy(i, slot, k).wait()
        return carry
    jax.lax.fori_loop(0, nck_ref[i], wait_body, 0)

    rankm = rankm_ref[...]
    loc_col = locc_ref[0, :, 0:1]
    end_col = locc_ref[0, :, 1:2]
    for ch in range(SLOTS // SLOT_CHUNK):
        s = (jax.lax.broadcasted_iota(jnp.int32, (LANES, SLOT_CHUNK), 1) + ch * SLOT_CHUNK).astype(f32)
        in_run = jnp.where(s >= loc_col, jnp.where(s < end_col, 1.0, 0.0), 0.0)
        r_row = s[0:1, :] - jnp.sum(in_run * loc_col, axis=0, keepdims=True)
        q = jnp.dot(rankm, in_run.astype(bf16), preferred_element_type=f32)
        p = jnp.where(q == r_row, 1.0, 0.0).astype(bf16)
        acc = acc + jnp.dot(p, yloc[slot, ch * SLOT_CHUNK:(ch + 1) * SLOT_CHUNK, :],
                            preferred_element_type=f32)
    out_ref[...] = _layer_norm(DN_ALPHA * h + acc, ln_g_ref[...], ln_b_ref[...])


def _combine(h, rankm, locc, ys, wsu, wsd, ln_g, ln_b, nck, dstk, tot):
    n_tiles = h.shape[0] // TOK_TILE
    const = lambda shape: pl.BlockSpec(shape, lambda i, *_: (0,) * len(shape))
    grid_spec = pltpu.PrefetchScalarGridSpec(
        num_scalar_prefetch=3,
        grid=(n_tiles,),
        in_specs=[pl.BlockSpec((TOK_TILE, D_MODEL), lambda i, *_: (i, 0)),
                  pl.BlockSpec((TOK_TILE, LANES), lambda i, *_: (i, 0)),
                  pl.BlockSpec((1, LANES, 2), lambda i, *_: (i, 0, 0)),
                  pl.BlockSpec(memory_space=pl.ANY),
                  const(wsu.shape), const(wsd.shape), const(ln_g.shape), const(ln_b.shape)],
        out_specs=pl.BlockSpec((TOK_TILE, D_MODEL), lambda i, *_: (i, 0)),
        scratch_shapes=[pltpu.VMEM((2, SLOTS, D_MODEL), bf16),
                        pltpu.SemaphoreType.DMA((2,))])
    return pl.pallas_call(
        _combine_kernel,
        grid_spec=grid_spec,
        out_shape=jax.ShapeDtypeStruct(h.shape, f32),
        compiler_params=pltpu.CompilerParams(
            dimension_semantics=("arbitrary",), vmem_limit_bytes=VMEM_LIMIT_BYTES),
        name="moe_combine",
    )(nck, dstk, tot, h, rankm, locc, ys, wsu, wsd, ln_g, ln_b)


def _round_up(x, m):
    return (x + m - 1) // m * m


def _moe(h, wr, rb, wgu, wd, wsu, wsd, ln_g, ln_b):
    n_tok = h.shape[0]
    n_tiles = n_tok // TOK_TILE
    max_rows = _round_up(n_tok * TOP_K + n_tiles * N_EXPERTS * (RUN_ALIGN - 1)
                         + N_EXPERTS * (ROW_TILE - 1), ROW_TILE)
    n_row_tiles = max_rows // ROW_TILE

    hext, rankm, cnt = _router(h, wr, rb)

    i32 = jnp.int32
    cnt = cnt[:, 0, :N_EXPERTS].astype(i32)
    plen = _round_up(cnt, RUN_ALIGN)
    over_tiles = jnp.cumsum(plen, axis=0)
    region_rows = over_tiles[-1]
    region_size = _round_up(region_rows, ROW_TILE)
    region_start = jnp.cumsum(region_size) - region_size
    run_dst = region_start[None, :] + over_tiles - plen
    over_experts = jnp.cumsum(plen, axis=1)
    loc = over_experts - plen
    tot = over_experts[:, -1]
    nch = plen // RUN_ALIGN
    nch_incl = jnp.cumsum(nch, axis=1)
    nck = nch_incl[:, -1]
    k = jnp.arange(MAX_CHUNKS, dtype=i32)
    ek = jax.vmap(lambda c: jnp.searchsorted(c, k, side='right'))(nch_incl)
    ek = jnp.minimum(ek, N_EXPERTS - 1).astype(i32)
    first_chunk = jnp.take_along_axis(nch_incl - nch, ek, axis=1)
    dstk = jnp.take_along_axis(run_dst, ek, axis=1) + RUN_ALIGN * (k[None, :] - first_chunk)
    dstk = jnp.where(k[None, :] < nck[:, None], dstk, 0).astype(i32)
    zst = (region_start + region_rows).astype(i32)
    zch = ((region_size - region_rows) // RUN_ALIGN).astype(i32)
    tiles_incl = jnp.cumsum(region_size // ROW_TILE)
    n_valid = tiles_incl[-1]
    tb = jnp.minimum(jnp.arange(n_row_tiles, dtype=i32), n_valid - 1).astype(i32)
    te = jnp.minimum(jnp.searchsorted(tiles_incl, tb, side='right'), N_EXPERTS - 1).astype(i32)
    nv = n_valid.reshape(1).astype(i32)

    pad = ((0, 0), (0, LANES - N_EXPERTS))
    loc_f = jnp.pad(loc.astype(f32), pad, constant_values=NO_RUN)
    end_f = jnp.pad(over_experts.astype(f32), pad, constant_values=NO_RUN)
    locs = jnp.concatenate([loc_f[:, None, :], end_f[:, None, :],
                            jnp.zeros((n_tiles, SUBLANES - 2, LANES), f32)], axis=1)
    locc = jnp.stack([loc_f, end_f], axis=-1)

    xs = _dispatch(hext, rankm, locs, nck.astype(i32), dstk, zst, zch, max_rows=max_rows)
    ys = _experts(xs, wgu, wd, te, tb, nv)
    return _combine(h, rankm, locc, ys, wsu, wsd, ln_g, ln_b, nck.astype(i32), dstk, tot.astype(i32))


def _head_block_diag(w):
    heads_per_tile = MXU_DIM // RG_HEAD_DIM
    w4 = w.reshape(D_RG // MXU_DIM, heads_per_tile, RG_HEAD_DIM, RG_HEAD_DIM)
    eye = jnp.eye(heads_per_tile, dtype=w.dtype)
    return jnp.einsum('thij,hk->thikj', w4, eye).reshape(D_RG // MXU_DIM, MXU_DIM, MXU_DIM)


def _s5_in_tiles(b):
    gpt = S5_GROUPS // 2
    b4 = b.reshape(2, gpt, S5_STATE, S5_GROUP)
    eye = jnp.eye(gpt, dtype=b.dtype)
    return jnp.einsum('kgph,gm->kghmp', b4, eye).reshape(2, gpt * S5_GROUP, gpt * S5_STATE)


def _s5_out_tiles(cw):
    gpt = S5_GROUPS // 2
    c4 = cw.reshape(2, gpt, S5_GROUP, S5_STATE)
    eye = jnp.eye(gpt, dtype=cw.dtype)
    return jnp.einsum('kghp,gm->kgpmh', c4, eye).reshape(2, gpt * S5_STATE, gpt * S5_GROUP)


def _row(v):
    return v.reshape(1, -1)


def kernel(x_prompt, x_sample, state_rg_conv, state_rg_h, state_s5_re, state_s5_im, w_in, conv_w, conv_b, rg_w_a, rg_b_a, rg_w_i, rg_b_i, rg_lam, s5_a_re, s5_a_im, s5_log_dt, s5_b_re, s5_b_im, s5_c_re, s5_c_im, s5_d, w_glu, b_glu, w_out, ln1_g, ln1_b, w_router, router_bias, w_gate_up, w_down, w_shared_up, w_shared_down, ln2_g, ln2_b):
    l = 0
    bp, lp, _ = x_prompt.shape
    bs, ls, _ = x_sample.shape

    are, aim, bbre, bbim = _s5_prep(
        _row(s5_a_re[l]), _row(s5_a_im[l]),
        _row(jnp.repeat(s5_log_dt[l], S5_STATE)),
        _s5_in_tiles(s5_b_re[l]), _s5_in_tiles(s5_b_im[l]))
    params = (w_in[l].astype(bf16), conv_w[l], _row(conv_b[l]),
              _head_block_diag(rg_w_a[l]).astype(bf16), _row(rg_b_a[l]),
              _head_block_diag(rg_w_i[l]).astype(bf16), _row(rg_b_i[l]), _row(rg_lam[l]),
              are, aim, bbre, bbim,
              _s5_out_tiles(s5_c_re[l]).astype(bf16), _s5_out_tiles(s5_c_im[l]).astype(bf16),
              _row(s5_d[l]), w_glu[l].astype(bf16), _row(b_glu[l]), w_out[l].astype(bf16),
              _row(ln1_g[l]), _row(ln1_b[l]))

    tail = CONV_WIDTH - 1
    xp_tm = x_prompt.transpose(1, 0, 2).reshape(lp * bp, D_MODEL)
    hp, pc, ph, pre, pim = _mixer(
        xp_tm, jnp.zeros((tail * bp, D_RG), f32), jnp.zeros((bp, D_RG), f32),
        jnp.zeros((bp, S5_N), f32), jnp.zeros((bp, S5_N), f32), params,
        nb=bp, tc=32, name="mixer_prompt")

    xs_tm = x_sample.transpose(1, 0, 2).reshape(ls * bs, D_MODEL)
    hs, sc, sh, sre, sim = _mixer(
        xs_tm, state_rg_conv[l].transpose(1, 0, 2).reshape(tail * bs, D_RG), state_rg_h[l],
        state_s5_re[l].reshape(bs, S5_N), state_s5_im[l].reshape(bs, S5_N), params,
        nb=bs, tc=ls, name="mixer_sample")

    h_all = jnp.concatenate([hp, hs], axis=0)
    wr = jnp.pad(w_router[l], ((0, 0), (0, LANES - N_EXPERTS))).astype(bf16)
    rb = jnp.pad(_row(router_bias[l]), ((0, 0), (0, LANES - N_EXPERTS)))
    y_all = _moe(h_all, wr, rb, w_gate_up[l], w_down[l],
                 w_shared_up[l].astype(bf16), w_shared_down[l].astype(bf16),
                 _row(ln2_g[l]), _row(ln2_b[l]))
    yp = y_all[:lp * bp].reshape(lp, bp, D_MODEL).transpose(1, 0, 2)
    ys = y_all[lp * bp:].reshape(ls, bs, D_MODEL).transpose(1, 0, 2)

    def conv_out(cv, nbatch):
        return cv.reshape(tail, nbatch, D_RG).transpose(1, 0, 2)[None]

    return (yp, ys,
            conv_out(pc, bp), ph[None],
            pre.reshape(1, bp, S5_GROUPS, S5_STATE), pim.reshape(1, bp, S5_GROUPS, S5_STATE),
            conv_out(sc, bs), sh[None],
            sre.reshape(1, bs, S5_GROUPS, S5_STATE), sim.reshape(1, bs, S5_GROUPS, S5_STATE))
```

```python
import functools
import math

import jax
import jax.numpy as jnp
from jax.experimental import pallas as pl
from jax.experimental.pallas import tpu as pltpu

D_MODEL = 1024
D_RG = 512
RG_HEADS = 8
RG_HEAD_DIM = 64
CONV_WIDTH = 4
RG_C = 8.0
D_S5 = 512
S5_GROUP = 16
S5_GROUPS = 32
S5_STATE = 64
S5_N = S5_GROUPS * S5_STATE
N_EXPERTS = 64
TOP_K = 8
D_EXPERT = 256
D_SHARED = 256
ROUTED_SCALE = 2.5
DEPTH = 1
DN_ALPHA = (2.0 * DEPTH) ** 0.25
LN_EPS = 1e-5

SUBLANES = 8
LANES = 128
MXU_DIM = 256
S5_SCAN_COLS = 512
VMEM_LIMIT_BYTES = 56 * 1024 * 1024

bf16 = jnp.bfloat16
f32 = jnp.float32


def _gelu_tanh(x):
    c = math.sqrt(2.0 / math.pi)
    return x * (0.5 * (1.0 + jnp.tanh(c * (x + 0.044715 * (x * x * x)))))


def _layer_norm(x, g, b):
    mu = jnp.mean(x, axis=-1, keepdims=True)
    xc = x - mu
    var = jnp.mean(xc * xc, axis=-1, keepdims=True)
    return xc * jax.lax.rsqrt(var + LN_EPS) * g + b


def _s5_prep_kernel(lr_ref, li_ref, ldt_ref, bre_ref, bim_ref,
                    are_ref, aim_ref, bbre_ref, bbim_ref):
    lr = lr_ref[...]
    li = li_ref[...]
    dt = jnp.exp(ldt_ref[...])
    mag = jnp.exp(lr * dt)
    abar_re = mag * jnp.cos(li * dt)
    abar_im = mag * jnp.sin(li * dt)
    den = lr * lr + li * li
    nr = abar_re - 1.0
    ni = abar_im
    coef_re = (nr * lr + ni * li) / den
    coef_im = (ni * lr - nr * li) / den
    are_ref[...] = abar_re
    aim_ref[...] = abar_im
    half = S5_N // 2
    for k in range(2):
        cre = coef_re[:, k * half:(k + 1) * half]
        cim = coef_im[:, k * half:(k + 1) * half]
        br = bre_ref[k]
        bi = bim_ref[k]
        bbre_ref[k] = (cre * br - cim * bi).astype(bf16)
        bbim_ref[k] = (cre * bi + cim * br).astype(bf16)


def _s5_prep(lr, li, ldt, bre_t, bim_t):
    half = S5_N // 2
    return pl.pallas_call(
        _s5_prep_kernel,
        out_shape=(jax.ShapeDtypeStruct((1, S5_N), f32),
                   jax.ShapeDtypeStruct((1, S5_N), f32),
                   jax.ShapeDtypeStruct((2, MXU_DIM, half), bf16),
                   jax.ShapeDtypeStruct((2, MXU_DIM, half), bf16)),
        name="s5_prep",
    )(lr, li, ldt, bre_t, bim_t)


def _mixer_kernel(x_ref, conv0_ref, h0_ref, s0r_ref, s0i_ref,
                  w_in_ref, conv_w_ref, conv_b_ref, wa_ref, ba_ref, wi_ref, bi_ref, lam_ref,
                  are_ref, aim_ref, bbre_ref, bbim_ref, cre_ref, cim_ref, d_ref,
                  wglu_ref, bglu_ref, wout_ref, ln_g_ref, ln_b_ref,
                  hout_ref, conv_out_ref, hlast_ref, sre_out_ref, sim_out_ref,
                  pad_scr, a_scr, b_scr, bur_scr, bui_scr, hst_scr, sr_scr, si_scr,
                  *, nb, tc):
    rows = nb * tc
    tail = (CONV_WIDTH - 1) * nb
    c = pl.program_id(0)

    @pl.when(c == 0)
    def _():
        pad_scr[0:tail, :] = conv0_ref[...]
        hst_scr[...] = h0_ref[...]
        sr_scr[...] = s0r_ref[...]
        si_scr[...] = s0i_ref[...]

    x = x_ref[...]
    proj = jnp.dot(x.astype(bf16), w_in_ref[...], preferred_element_type=f32)
    x_rg = proj[:, :D_RG]
    g_rg = proj[:, D_RG:2 * D_RG]
    u = proj[:, 2 * D_RG:]

    pad_scr[tail:tail + rows, :] = x_rg
    conv_w = conv_w_ref[...]
    acc = conv_w[0:1, :] * pad_scr[0:rows, :]
    for k in range(1, CONV_WIDTH):
        acc = acc + conv_w[k:k + 1, :] * pad_scr[k * nb:k * nb + rows, :]
    xc = conv_b_ref[...] + acc
    new_tail = pad_scr[rows:rows + tail, :]
    pad_scr[0:tail, :] = new_tail

    xcb = xc.astype(bf16)
    ga = []
    gi = []
    for hh in range(D_RG // MXU_DIM):
        xs = xcb[:, hh * MXU_DIM:(hh + 1) * MXU_DIM]
        ga.append(jnp.dot(xs, wa_ref[hh], preferred_element_type=f32))
        gi.append(jnp.dot(xs, wi_ref[hh], preferred_element_type=f32))
    r = jax.nn.sigmoid(jnp.concatenate(ga, axis=1) + ba_ref[...])
    i = jax.nn.sigmoid(jnp.concatenate(gi, axis=1) + bi_ref[...])
    nlam = -lam_ref[...]
    softplus = jnp.maximum(nlam, 0.0) + jnp.log1p(jnp.exp(-jnp.abs(nlam)))
    log_a = (-RG_C) * r * softplus
    a_scr[...] = jnp.exp(log_a)
    th = jnp.tanh(log_a)
    b_scr[...] = jnp.sqrt((-2.0 * th) / (1.0 - th)) * (i * xc)

    for rg in range(nb // SUBLANES):
        r0 = rg * SUBLANES
        h = hst_scr[r0:r0 + SUBLANES, :]
        for t in range(tc):
            q = t * nb + r0
            h = a_scr[q:q + SUBLANES, :] * h + b_scr[q:q + SUBLANES, :]
            b_scr[q:q + SUBLANES, :] = h
        hst_scr[r0:r0 + SUBLANES, :] = h
    y_rg = b_scr[...] * _gelu_tanh(g_rg)

    ub = u.astype(bf16)
    half = S5_N // 2
    for k in range(2):
        us = ub[:, k * MXU_DIM:(k + 1) * MXU_DIM]
        bur_scr[:, k * half:(k + 1) * half] = jnp.dot(us, bbre_ref[k], preferred_element_type=f32)
        bui_scr[:, k * half:(k + 1) * half] = jnp.dot(us, bbim_ref[k], preferred_element_type=f32)
    for rg in range(nb // SUBLANES):
        r0 = rg * SUBLANES
        for cb in range(S5_N // S5_SCAN_COLS):
            c0 = cb * S5_SCAN_COLS
            ar = jnp.broadcast_to(are_ref[:, c0:c0 + S5_SCAN_COLS], (SUBLANES, S5_SCAN_COLS))
            ai = jnp.broadcast_to(aim_ref[:, c0:c0 + S5_SCAN_COLS], (SUBLANES, S5_SCAN_COLS))
            xr = sr_scr[r0:r0 + SUBLANES, c0:c0 + S5_SCAN_COLS]
            xi = si_scr[r0:r0 + SUBLANES, c0:c0 + S5_SCAN_COLS]
            for t in range(tc):
                q = t * nb + r0
                br = bur_scr[q:q + SUBLANES, c0:c0 + S5_SCAN_COLS]
                bi_ = bui_scr[q:q + SUBLANES, c0:c0 + S5_SCAN_COLS]
                nxr = ar * xr - ai * xi + br
                nxi = ar * xi + ai * xr + bi_
                bur_scr[q:q + SUBLANES, c0:c0 + S5_SCAN_COLS] = nxr
                bui_scr[q:q + SUBLANES, c0:c0 + S5_SCAN_COLS] = nxi
                xr, xi = nxr, nxi
            sr_scr[r0:r0 + SUBLANES, c0:c0 + S5_SCAN_COLS] = xr
            si_scr[r0:r0 + SUBLANES, c0:c0 + S5_SCAN_COLS] = xi
    ys = []
    for j in range(D_S5 // MXU_DIM):
        xrb = bur_scr[:, j * half:(j + 1) * half].astype(bf16)
        xib = bui_scr[:, j * half:(j + 1) * half].astype(bf16)
        ys.append(jnp.dot(xrb, cre_ref[j], preferred_element_type=f32)
                  - jnp.dot(xib, cim_ref[j], preferred_element_type=f32))
    y_s5 = jnp.concatenate(ys, axis=1) + d_ref[...] * u
    yg = _gelu_tanh(y_s5)
    glu = jnp.dot(yg.astype(bf16), wglu_ref[...], preferred_element_type=f32) + bglu_ref[...]
    y_s5 = yg * jax.nn.sigmoid(glu)

    ycat = jnp.concatenate([y_rg, y_s5], axis=1).astype(bf16)
    mix = jnp.dot(ycat, wout_ref[...], preferred_element_type=f32)
    hout_ref[...] = _layer_norm(DN_ALPHA * x + mix, ln_g_ref[...], ln_b_ref[...])

    @pl.when(c == pl.num_programs(0) - 1)
    def _():
        conv_out_ref[...] = pad_scr[0:tail, :]
        hlast_ref[...] = hst_scr[...]
        sre_out_ref[...] = sr_scr[...]
        sim_out_ref[...] = si_scr[...]


def _full(shape):
    n = len(shape)
    return pl.BlockSpec(shape, lambda c: (0,) * n)


def _mixer(x_tm, conv0, h0, s0r, s0i, params, *, nb, tc, name):
    rows_total = x_tm.shape[0]
    rows = nb * tc
    n_chunks = rows_total // rows
    tail = (CONV_WIDTH - 1) * nb
    small = (conv0, h0, s0r, s0i) + tuple(params)
    in_specs = [pl.BlockSpec((rows, D_MODEL), lambda c: (c, 0))]
    in_specs += [_full(a.shape) for a in small]
    out_shape = (jax.ShapeDtypeStruct((rows_total, D_MODEL), f32),
                 jax.ShapeDtypeStruct((tail, D_RG), f32),
                 jax.ShapeDtypeStruct((nb, D_RG), f32),
                 jax.ShapeDtypeStruct((nb, S5_N), f32),
                 jax.ShapeDtypeStruct((nb, S5_N), f32))
    out_specs = (pl.BlockSpec((rows, D_MODEL), lambda c: (c, 0)),
                 _full((tail, D_RG)), _full((nb, D_RG)), _full((nb, S5_N)), _full((nb, S5_N)))
    scratch = [pltpu.VMEM((rows + tail, D_RG), f32),
               pltpu.VMEM((rows, D_RG), f32),
               pltpu.VMEM((rows, D_RG), f32),
               pltpu.VMEM((rows, S5_N), f32),
               pltpu.VMEM((rows, S5_N), f32),
               pltpu.VMEM((nb, D_RG), f32),
               pltpu.VMEM((nb, S5_N), f32),
               pltpu.VMEM((nb, S5_N), f32)]
    return pl.pallas_call(
        functools.partial(_mixer_kernel, nb=nb, tc=tc),
        grid=(n_chunks,),
        in_specs=in_specs,
        out_specs=out_specs,
        out_shape=out_shape,
        scratch_shapes=scratch,
        compiler_params=pltpu.CompilerParams(
            dimension_semantics=("arbitrary",), vmem_limit_bytes=VMEM_LIMIT_BYTES),
        name=name,
    )(x_tm, *small)


TOK_TILE = 256
RUN_ALIGN = 16
SLOTS = 3072
SLOT_CHUNK = 512
MAX_CHUNKS = SLOTS // RUN_ALIGN
ROW_TILE = 512
EXT = D_MODEL + 2 * LANES
NO_RUN = 1.0e9


def _top_k_gates(scores, rb):
    rows = scores.shape[0]
    lane_f = jax.lax.broadcasted_iota(jnp.int32, (rows, LANES), 1).astype(f32)
    biased = jnp.where(lane_f < float(N_EXPERTS), scores + rb, -jnp.inf)
    sel = jnp.zeros((rows, LANES), f32)
    mask = jnp.zeros((rows, LANES), f32)
    for _ in range(TOP_K):
        m = jnp.max(biased, axis=1, keepdims=True)
        idx = jnp.min(jnp.where(biased == m, lane_f, float(LANES)), axis=1, keepdims=True)
        hit = lane_f == idx
        sel = jnp.where(hit, scores, sel)
        mask = jnp.where(hit, 1.0, mask)
        biased = jnp.where(hit, -jnp.inf, biased)
    gates = sel / jnp.sum(sel, axis=1, keepdims=True) * ROUTED_SCALE
    return mask, gates


def _router_kernel(h_ref, wr_ref, rb_ref, hext_ref, rankm_ref, cnt_ref):
    hb = h_ref[...].astype(bf16)
    scores = jax.nn.sigmoid(jnp.dot(hb, wr_ref[...], preferred_element_type=f32))
    mask, gates = _top_k_gates(scores, rb_ref[...])
    t_row = jax.lax.broadcasted_iota(jnp.int32, (TOK_TILE, TOK_TILE), 0)
    t_col = jax.lax.broadcasted_iota(jnp.int32, (TOK_TILE, TOK_TILE), 1)
    earlier = jnp.where(t_col < t_row, 1.0, 0.0).astype(bf16)
    rank = jnp.dot(earlier, mask.astype(bf16), preferred_element_type=f32)
    rankm_ref[...] = jnp.where(mask > 0.0, rank, -1.0).astype(bf16)
    cnt_ref[0] = jnp.broadcast_to(jnp.sum(mask, axis=0, keepdims=True), (SUBLANES, LANES))
    g_hi = gates.astype(bf16)
    g_lo = (gates - g_hi.astype(f32)).astype(bf16)
    hext_ref[:, :D_MODEL] = hb
    hext_ref[:, D_MODEL:D_MODEL + LANES] = g_hi
    hext_ref[:, D_MODEL + LANES:] = g_lo


def _router(h, wr, rb):
    n_tiles = h.shape[0] // TOK_TILE
    const = lambda shape: pl.BlockSpec(shape, lambda i: (0,) * len(shape))
    return pl.pallas_call(
        _router_kernel,
        grid=(n_tiles,),
        in_specs=[pl.BlockSpec((TOK_TILE, D_MODEL), lambda i: (i, 0)), const(wr.shape), const(rb.shape)],
        out_specs=(pl.BlockSpec((TOK_TILE, EXT), lambda i: (i, 0)),
                   pl.BlockSpec((TOK_TILE, LANES), lambda i: (i, 0)),
                   pl.BlockSpec((1, SUBLANES, LANES), lambda i: (i, 0, 0))),
        out_shape=(jax.ShapeDtypeStruct((h.shape[0], EXT), bf16),
                   jax.ShapeDtypeStruct((h.shape[0], LANES), bf16),
                   jax.ShapeDtypeStruct((n_tiles, SUBLANES, LANES), f32)),
        compiler_params=pltpu.CompilerParams(dimension_semantics=("arbitrary",)),
        name="moe_router",
    )(h, wr, rb)


def _run_copy(src, dst, sem):
    return pltpu.make_async_copy(src, dst, sem)


def _dispatch_kernel(tot_ref, dstk_ref, zst_ref, zch_ref,
                     hext_ref, rankm_ref, locs_ref, sorted_ref,
                     stage, zbuf, sem, zsem):
    i = pl.program_id(0)
    n = pl.num_programs(0)
    slot = i % 2
    loc_row = locs_ref[0, 0:1, :]
    end_row = locs_ref[0, 1:2, :]
    for ch in range(SLOTS // SLOT_CHUNK):
        def sort_chunk(ch=ch):
            s = (jax.lax.broadcasted_iota(jnp.int32, (SLOT_CHUNK, LANES), 0)
                 + ch * SLOT_CHUNK).astype(f32)
            in_run = jnp.where(s >= loc_row, jnp.where(s < end_row, 1.0, 0.0), 0.0)
            r_col = s[:, 0:1] - jnp.sum(in_run * loc_row, axis=1, keepdims=True)
            q = jax.lax.dot_general(in_run.astype(bf16), rankm_ref[...], (((1,), (1,)), ((), ())),
                                    preferred_element_type=f32)
            p = jnp.where(q == r_col, 1.0, 0.0).astype(bf16)
            rows = jnp.dot(p, hext_ref[...], preferred_element_type=f32)
            stage[slot, ch * SLOT_CHUNK:(ch + 1) * SLOT_CHUNK, :] = rows.astype(bf16)

        def blank_chunk(ch=ch):
            stage[slot, ch * SLOT_CHUNK:(ch + 1) * SLOT_CHUNK, :] = jnp.zeros((SLOT_CHUNK, EXT), bf16)

        if (ch + 1) * SLOT_CHUNK <= TOK_TILE * TOP_K:
            sort_chunk()
        else:
            pl.when(tot_ref[i] > ch * SLOT_CHUNK)(sort_chunk)
            pl.when(tot_ref[i] <= ch * SLOT_CHUNK)(blank_chunk)

    def start_run(k, carry):
        src = stage.at[slot, pl.ds(pl.multiple_of(k * RUN_ALIGN, RUN_ALIGN), RUN_ALIGN)]
        dst = sorted_ref.at[pl.ds(pl.multiple_of(dstk_ref[i, k], RUN_ALIGN), RUN_ALIGN)]
        _run_copy(src, dst, sem.at[slot]).start()
        return carry
    jax.lax.fori_loop(0, MAX_CHUNKS, start_run, 0, unroll=8)

    @pl.when(i == 0)
    def _():
        zbuf[...] = jnp.zeros(zbuf.shape, bf16)

    @pl.when(i < N_EXPERTS)
    def _():
        def zero_copy(k):
            dst = sorted_ref.at[pl.ds(pl.multiple_of(zst_ref[i] + k * RUN_ALIGN, RUN_ALIGN), RUN_ALIGN)]
            return _run_copy(zbuf, dst, zsem.at[0])

        def start_zero(k, carry):
            zero_copy(k).start()
            return carry

        def wait_zero(k, carry):
            zero_copy(k).wait()
            return carry
        jax.lax.fori_loop(0, zch_ref[i], start_zero, 0)
        jax.lax.fori_loop(0, zch_ref[i], wait_zero, 0)

    def wait_runs(slot_):
        for _ in range(MAX_CHUNKS):
            src = stage.at[slot_, pl.ds(0, RUN_ALIGN)]
            dst = sorted_ref.at[pl.ds(0, RUN_ALIGN)]
            _run_copy(src, dst, sem.at[slot_]).wait()

    @pl.when(i > 0)
    def _():
        wait_runs(1 - slot)

    @pl.when(i == n - 1)
    def _():
        wait_runs(slot)


def _dispatch(hext, rankm, locs, tot, dstk, zst, zch, *, total_rows):
    n_tiles = hext.shape[0] // TOK_TILE
    grid_spec = pltpu.PrefetchScalarGridSpec(
        num_scalar_prefetch=4,
        grid=(n_tiles,),
        in_specs=[pl.BlockSpec((TOK_TILE, EXT), lambda i, *_: (i, 0)),
                  pl.BlockSpec((TOK_TILE, LANES), lambda i, *_: (i, 0)),
                  pl.BlockSpec((1, SUBLANES, LANES), lambda i, *_: (i, 0, 0))],
        out_specs=pl.BlockSpec(memory_space=pl.ANY),
        scratch_shapes=[pltpu.VMEM((2, SLOTS, EXT), bf16),
                        pltpu.VMEM((RUN_ALIGN, EXT), bf16),
                        pltpu.SemaphoreType.DMA((2,)),
                        pltpu.SemaphoreType.DMA((1,))])
    return pl.pallas_call(
        _dispatch_kernel,
        grid_spec=grid_spec,
        out_shape=jax.ShapeDtypeStruct((total_rows, EXT), bf16),
        compiler_params=pltpu.CompilerParams(
            dimension_semantics=("arbitrary",), vmem_limit_bytes=VMEM_LIMIT_BYTES),
        name="moe_dispatch",
    )(tot, dstk, zst, zch, hext, rankm, locs)


def _experts_kernel(te_ref, tb_ref, nv_ref, x_ref, wgu_ref, wd_ref, y_ref, wgu_b, wd_b):
    j = pl.program_id(0)
    e = te_ref[j]
    changed = jnp.logical_or(j == 0, e != te_ref[jnp.maximum(j - 1, 0)])

    @pl.when(changed)
    def _():
        wgu_b[...] = wgu_ref[0].astype(bf16)
        wd_b[...] = wd_ref[0].astype(bf16)

    @pl.when(j < nv_ref[0])
    def _():
        x = x_ref[:, :D_MODEL]
        gu = jnp.dot(x, wgu_b[...], preferred_element_type=f32)
        act = (jax.nn.silu(gu[:, :D_EXPERT]) * gu[:, D_EXPERT:]).astype(bf16)
        y = jnp.dot(act, wd_b[...], preferred_element_type=f32)
        gate_all = (x_ref[:, D_MODEL:D_MODEL + LANES].astype(f32)
                    + x_ref[:, D_MODEL + LANES:].astype(f32))
        lane = jax.lax.broadcasted_iota(jnp.int32, (ROW_TILE, LANES), 1)
        gate = jnp.sum(jnp.where(lane == e, gate_all, 0.0), axis=1, keepdims=True)
        y_ref[...] = (y * gate).astype(bf16)


def _experts(xs, wgu, wd, te, tb, nv, *, n_row_tiles):
    grid_spec = pltpu.PrefetchScalarGridSpec(
        num_scalar_prefetch=3,
        grid=(n_row_tiles,),
        in_specs=[pl.BlockSpec((ROW_TILE, EXT), lambda j, te, tb, nv: (tb[j], 0)),
                  pl.BlockSpec((1, D_MODEL, 2 * D_EXPERT), lambda j, te, tb, nv: (te[j], 0, 0)),
                  pl.BlockSpec((1, D_EXPERT, D_MODEL), lambda j, te, tb, nv: (te[j], 0, 0))],
        out_specs=pl.BlockSpec((ROW_TILE, D_MODEL), lambda j, te, tb, nv: (tb[j], 0)),
        scratch_shapes=[pltpu.VMEM((D_MODEL, 2 * D_EXPERT), bf16),
                        pltpu.VMEM((D_EXPERT, D_MODEL), bf16)])
    return pl.pallas_call(
        _experts_kernel,
        grid_spec=grid_spec,
        out_shape=jax.ShapeDtypeStruct((n_row_tiles * ROW_TILE, D_MODEL), bf16),
        compiler_params=pltpu.CompilerParams(
            dimension_semantics=("arbitrary",), vmem_limit_bytes=VMEM_LIMIT_BYTES),
        name="moe_experts",
    )(te, tb, nv, xs, wgu, wd)


def _combine_kernel(tot_ref, srck_ref,
                    h_ref, rankm_ref, locc_ref, ys_ref, wsu_ref, wsd_ref, ln_g_ref, ln_b_ref,
                    out_ref, yloc, acc_scr, sem):
    i = pl.program_id(0)
    n = pl.num_programs(0)
    slot = i % 2

    def fetch(tile, slot_):
        def body(k, carry):
            src = ys_ref.at[pl.ds(pl.multiple_of(srck_ref[tile, k], RUN_ALIGN), RUN_ALIGN)]
            dst = yloc.at[slot_, pl.ds(pl.multiple_of(k * RUN_ALIGN, RUN_ALIGN), RUN_ALIGN)]
            _run_copy(src, dst, sem.at[slot_]).start()
            return carry
        jax.lax.fori_loop(0, MAX_CHUNKS, body, 0, unroll=8)

    @pl.when(i == 0)
    def _():
        fetch(0, 0)

    @pl.when(i + 1 < n)
    def _():
        fetch(i + 1, 1 - slot)

    h = h_ref[...]
    hb = h.astype(bf16)
    su = jnp.dot(hb, wsu_ref[...], preferred_element_type=f32)
    act = (jax.nn.silu(su[:, :D_SHARED]) * su[:, D_SHARED:]).astype(bf16)
    acc_scr[...] = jnp.dot(act, wsd_ref[...], preferred_element_type=f32)

    for _ in range(MAX_CHUNKS):
        _run_copy(ys_ref.at[pl.ds(0, RUN_ALIGN)], yloc.at[slot, pl.ds(0, RUN_ALIGN)],
                  sem.at[slot]).wait()

    loc_col = locc_ref[0, :, 0:1]
    end_col = locc_ref[0, :, 1:2]
    for ch in range(SLOTS // SLOT_CHUNK):
        def gather_chunk(ch=ch):
            s = (jax.lax.broadcasted_iota(jnp.int32, (LANES, SLOT_CHUNK), 1)
                 + ch * SLOT_CHUNK).astype(f32)
            in_run = jnp.where(s >= loc_col, jnp.where(s < end_col, 1.0, 0.0), 0.0)
            r_row = s[0:1, :] - jnp.sum(in_run * loc_col, axis=0, keepdims=True)
            q = jnp.dot(rankm_ref[...], in_run.astype(bf16), preferred_element_type=f32)
            p = jnp.where(q == r_row, 1.0, 0.0).astype(bf16)
            acc_scr[...] += jnp.dot(p, yloc[slot, ch * SLOT_CHUNK:(ch + 1) * SLOT_CHUNK, :],
                                    preferred_element_type=f32)

        if (ch + 1) * SLOT_CHUNK <= TOK_TILE * TOP_K:
            gather_chunk()
        else:
            pl.when(tot_ref[i] > ch * SLOT_CHUNK)(gather_chunk)
    out_ref[...] = _layer_norm(DN_ALPHA * h + acc_scr[...], ln_g_ref[...], ln_b_ref[...])


def _combine(h, rankm, locc, ys, wsu, wsd, ln_g, ln_b, tot, srck):
    n_tiles = h.shape[0] // TOK_TILE
    const = lambda shape: pl.BlockSpec(shape, lambda i, *_: (0,) * len(shape))
    grid_spec = pltpu.PrefetchScalarGridSpec(
        num_scalar_prefetch=2,
        grid=(n_tiles,),
        in_specs=[pl.BlockSpec((TOK_TILE, D_MODEL), lambda i, *_: (i, 0)),
                  pl.BlockSpec((TOK_TILE, LANES), lambda i, *_: (i, 0)),
                  pl.BlockSpec((1, LANES, 2), lambda i, *_: (i, 0, 0)),
                  pl.BlockSpec(memory_space=pl.ANY),
                  const(wsu.shape), const(wsd.shape), const(ln_g.shape), const(ln_b.shape)],
        out_specs=pl.BlockSpec((TOK_TILE, D_MODEL), lambda i, *_: (i, 0)),
        scratch_shapes=[pltpu.VMEM((2, SLOTS, D_MODEL), bf16),
                        pltpu.VMEM((TOK_TILE, D_MODEL), f32),
                        pltpu.SemaphoreType.DMA((2,))])
    return pl.pallas_call(
        _combine_kernel,
        grid_spec=grid_spec,
        out_shape=jax.ShapeDtypeStruct(h.shape, f32),
        compiler_params=pltpu.CompilerParams(
            dimension_semantics=("arbitrary",), vmem_limit_bytes=VMEM_LIMIT_BYTES),
        name="moe_combine",
    )(tot, srck, h, rankm, locc, ys, wsu, wsd, ln_g, ln_b)


def _round_up(x, m):
    return (x + m - 1) // m * m


def _moe(h, wr, rb, wgu, wd, wsu, wsd, ln_g, ln_b):
    n_tok = h.shape[0]
    n_tiles = n_tok // TOK_TILE
    max_rows = _round_up(n_tok * TOP_K + n_tiles * N_EXPERTS * (RUN_ALIGN - 1)
                         + N_EXPERTS * (ROW_TILE - 1), ROW_TILE)
    n_row_tiles = max_rows // ROW_TILE

    hext, rankm, cnt = _router(h, wr, rb)

    i32 = jnp.int32
    cnt = cnt[:, 0, :N_EXPERTS].astype(i32)
    plen = _round_up(cnt, RUN_ALIGN)
    over_tiles = jnp.cumsum(plen, axis=0)
    region_rows = over_tiles[-1]
    region_size = _round_up(region_rows, ROW_TILE)
    region_start = jnp.cumsum(region_size) - region_size
    run_dst = region_start[None, :] + over_tiles - plen
    over_experts = jnp.cumsum(plen, axis=1)
    loc = over_experts - plen
    tot = over_experts[:, -1]
    ch_end = (over_experts // RUN_ALIGN)[:, None, :]
    ch_beg = (loc // RUN_ALIGN)[:, None, :]
    k = jnp.arange(MAX_CHUNKS, dtype=i32)
    kk = k[None, :, None]
    mine = jnp.logical_and(ch_beg <= kk, kk < ch_end)
    chunk_dst = jnp.sum(jnp.where(mine, run_dst[:, None, :] + RUN_ALIGN * (kk - ch_beg), 0), axis=-1)
    live = k[None, :] * RUN_ALIGN < tot[:, None]
    tile_ids = jnp.arange(n_tiles, dtype=i32)[:, None]
    spare = max_rows + (tile_ids % 2) * SLOTS + k[None, :] * RUN_ALIGN
    dstk = jnp.where(live, chunk_dst, spare).astype(i32)
    srck = jnp.where(live, chunk_dst, 0).astype(i32)
    zst = (region_start + region_rows).astype(i32)
    zch = ((region_size - region_rows) // RUN_ALIGN).astype(i32)
    tiles_incl = jnp.cumsum(region_size // ROW_TILE)
    n_valid = tiles_incl[-1]
    tb = jnp.minimum(jnp.arange(n_row_tiles, dtype=i32), n_valid - 1).astype(i32)
    te = jnp.sum((tiles_incl[None, :] <= tb[:, None]).astype(i32), axis=1)
    te = jnp.minimum(te, N_EXPERTS - 1).astype(i32)
    nv = n_valid.reshape(1).astype(i32)

    pad = ((0, 0), (0, LANES - N_EXPERTS))
    loc_f = jnp.pad(loc.astype(f32), pad, constant_values=NO_RUN)
    end_f = jnp.pad(over_experts.astype(f32), pad, constant_values=NO_RUN)
    locs = jnp.concatenate([loc_f[:, None, :], end_f[:, None, :],
                            jnp.zeros((n_tiles, SUBLANES - 2, LANES), f32)], axis=1)
    locc = jnp.stack([loc_f, end_f], axis=-1)

    tot = tot.astype(i32)
    xs = _dispatch(hext, rankm, locs, tot, dstk, zst, zch, total_rows=max_rows + 2 * SLOTS)
    ys = _experts(xs, wgu, wd, te, tb, nv, n_row_tiles=n_row_tiles)
    return _combine(h, rankm, locc, ys, wsu, wsd, ln_g, ln_b, tot, srck)


def _head_block_diag(w):
    heads_per_tile = MXU_DIM // RG_HEAD_DIM
    w4 = w.reshape(D_RG // MXU_DIM, heads_per_tile, RG_HEAD_DIM, RG_HEAD_DIM)
    eye = jnp.eye(heads_per_tile, dtype=w.dtype)
    return jnp.einsum('thij,hk->thikj', w4, eye).reshape(D_RG // MXU_DIM, MXU_DIM, MXU_DIM)


def _s5_in_tiles(b):
    gpt = S5_GROUPS // 2
    b4 = b.reshape(2, gpt, S5_STATE, S5_GROUP)
    eye = jnp.eye(gpt, dtype=b.dtype)
    return jnp.einsum('kgph,gm->kghmp', b4, eye).reshape(2, gpt * S5_GROUP, gpt * S5_STATE)


def _s5_out_tiles(cw):
    gpt = S5_GROUPS // 2
    c4 = cw.reshape(2, gpt, S5_GROUP, S5_STATE)
    eye = jnp.eye(gpt, dtype=cw.dtype)
    return jnp.einsum('kghp,gm->kgpmh', c4, eye).reshape(2, gpt * S5_STATE, gpt * S5_GROUP)


def _row(v):
    return v.reshape(1, -1)


def kernel(x_prompt, x_sample, state_rg_conv, state_rg_h, state_s5_re, state_s5_im, w_in, conv_w, conv_b, rg_w_a, rg_b_a, rg_w_i, rg_b_i, rg_lam, s5_a_re, s5_a_im, s5_log_dt, s5_b_re, s5_b_im, s5_c_re, s5_c_im, s5_d, w_glu, b_glu, w_out, ln1_g, ln1_b, w_router, router_bias, w_gate_up, w_down, w_shared_up, w_shared_down, ln2_g, ln2_b):
    l = 0
    bp, lp, _ = x_prompt.shape
    bs, ls, _ = x_sample.shape

    are, aim, bbre, bbim = _s5_prep(
        _row(s5_a_re[l]), _row(s5_a_im[l]),
        _row(jnp.repeat(s5_log_dt[l], S5_STATE)),
        _s5_in_tiles(s5_b_re[l]), _s5_in_tiles(s5_b_im[l]))
    params = (w_in[l].astype(bf16), conv_w[l], _row(conv_b[l]),
              _head_block_diag(rg_w_a[l]).astype(bf16), _row(rg_b_a[l]),
              _head_block_diag(rg_w_i[l]).astype(bf16), _row(rg_b_i[l]), _row(rg_lam[l]),
              are, aim, bbre, bbim,
              _s5_out_tiles(s5_c_re[l]).astype(bf16), _s5_out_tiles(s5_c_im[l]).astype(bf16),
              _row(s5_d[l]), w_glu[l].astype(bf16), _row(b_glu[l]), w_out[l].astype(bf16),
              _row(ln1_g[l]), _row(ln1_b[l]))

    tail = CONV_WIDTH - 1
    xp_tm = x_prompt.transpose(1, 0, 2).reshape(lp * bp, D_MODEL)
    hp, pc, ph, pre, pim = _mixer(
        xp_tm, jnp.zeros((tail * bp, D_RG), f32), jnp.zeros((bp, D_RG), f32),
        jnp.zeros((bp, S5_N), f32), jnp.zeros((bp, S5_N), f32), params,
        nb=bp, tc=32, name="mixer_prompt")

    xs_tm = x_sample.transpose(1, 0, 2).reshape(ls * bs, D_MODEL)
    hs, sc, sh, sre, sim = _mixer(
        xs_tm, state_rg_conv[l].transpose(1, 0, 2).reshape(tail * bs, D_RG), state_rg_h[l],
        state_s5_re[l].reshape(bs, S5_N), state_s5_im[l].reshape(bs, S5_N), params,
        nb=bs, tc=ls, name="mixer_sample")

    h_all = jnp.concatenate([hp, hs], axis=0)
    wr = jnp.pad(w_router[l], ((0, 0), (0, LANES - N_EXPERTS))).astype(bf16)
    rb = jnp.pad(_row(router_bias[l]), ((0, 0), (0, LANES - N_EXPERTS)))
    y_all = _moe(h_all, wr, rb, w_gate_up[l], w_down[l],
                 w_shared_up[l].astype(bf16), w_shared_down[l].astype(bf16),
                 _row(ln2_g[l]), _row(ln2_b[l]))
    yp = y_all[:lp * bp].reshape(lp, bp, D_MODEL).transpose(1, 0, 2)
    ys = y_all[lp * bp:].reshape(ls, bs, D_MODEL).transpose(1, 0, 2)

    def conv_out(cv, nbatch):
        return cv.reshape(tail, nbatch, D_RG).transpose(1, 0, 2)[None]

    return (yp, ys,
            conv_out(pc, bp), ph[None],
            pre.reshape(1, bp, S5_GROUPS, S5_STATE), pim.reshape(1, bp, S5_GROUPS, S5_STATE),
            conv_out(sc, bs), sh[None],
            sre.reshape(1, bs, S5_GROUPS, S5_STATE), sim.reshape(1, bs, S5_GROUPS, S5_STATE))
```

```python
import functools
import math

import jax
import jax.numpy as jnp
from jax.experimental import pallas as pl
from jax.experimental.pallas import tpu as pltpu

D_MODEL = 1024
D_RG = 512
RG_HEADS = 8
RG_HEAD_DIM = 64
CONV_WIDTH = 4
RG_C = 8.0
D_S5 = 512
S5_GROUP = 16
S5_GROUPS = 32
S5_STATE = 64
S5_N = S5_GROUPS * S5_STATE
N_EXPERTS = 64
TOP_K = 8
D_EXPERT = 256
D_SHARED = 256
ROUTED_SCALE = 2.5
DEPTH = 1
DN_ALPHA = (2.0 * DEPTH) ** 0.25
LN_EPS = 1e-5

SUBLANES = 8
LANES = 128
MXU_DIM = 256
S5_SCAN_COLS = 512
VMEM_LIMIT_BYTES = 56 * 1024 * 1024

bf16 = jnp.bfloat16
f32 = jnp.float32


def _gelu_tanh(x):
    c = math.sqrt(2.0 / math.pi)
    return x * (0.5 * (1.0 + jnp.tanh(c * (x + 0.044715 * (x * x * x)))))


def _layer_norm(x, g, b):
    mu = jnp.mean(x, axis=-1, keepdims=True)
    xc = x - mu
    var = jnp.mean(xc * xc, axis=-1, keepdims=True)
    return xc * jax.lax.rsqrt(var + LN_EPS) * g + b


def _s5_prep_kernel(lr_ref, li_ref, ldt_ref, bre_ref, bim_ref,
                    are_ref, aim_ref, bbre_ref, bbim_ref):
    lr = lr_ref[...]
    li = li_ref[...]
    dt = jnp.exp(ldt_ref[...])
    mag = jnp.exp(lr * dt)
    abar_re = mag * jnp.cos(li * dt)
    abar_im = mag * jnp.sin(li * dt)
    den = lr * lr + li * li
    nr = abar_re - 1.0
    ni = abar_im
    coef_re = (nr * lr + ni * li) / den
    coef_im = (ni * lr - nr * li) / den
    are_ref[...] = abar_re
    aim_ref[...] = abar_im
    half = S5_N // 2
    for k in range(2):
        cre = coef_re[:, k * half:(k + 1) * half]
        cim = coef_im[:, k * half:(k + 1) * half]
        br = bre_ref[k]
        bi = bim_ref[k]
        bbre_ref[k] = (cre * br - cim * bi).astype(bf16)
        bbim_ref[k] = (cre * bi + cim * br).astype(bf16)


def _s5_prep(lr, li, ldt, bre_t, bim_t):
    half = S5_N // 2
    return pl.pallas_call(
        _s5_prep_kernel,
        out_shape=(jax.ShapeDtypeStruct((1, S5_N), f32),
                   jax.ShapeDtypeStruct((1, S5_N), f32),
                   jax.ShapeDtypeStruct((2, MXU_DIM, half), bf16),
                   jax.ShapeDtypeStruct((2, MXU_DIM, half), bf16)),
        name="s5_prep",
    )(lr, li, ldt, bre_t, bim_t)


def _mixer_kernel(h_all_ref, x_ref, conv0_ref, h0_ref, s0r_ref, s0i_ref,
                  w_in_ref, conv_w_ref, conv_b_ref, wa_ref, ba_ref, wi_ref, bi_ref, lam_ref,
                  are_ref, aim_ref, bbre_ref, bbim_ref, cre_ref, cim_ref, d_ref,
                  wglu_ref, bglu_ref, wout_ref, ln_g_ref, ln_b_ref,
                  hout_ref, conv_out_ref, hlast_ref, sre_out_ref, sim_out_ref,
                  pad_scr, a_scr, b_scr, bur_scr, bui_scr, hst_scr, sr_scr, si_scr,
                  *, nb, tc, batch_major_input):
    del h_all_ref
    rows = nb * tc
    tail = (CONV_WIDTH - 1) * nb
    c = pl.program_id(0)

    @pl.when(c == 0)
    def _():
        pad_scr[0:tail, :] = conv0_ref[...]
        hst_scr[...] = h0_ref[...]
        sr_scr[...] = s0r_ref[...]
        si_scr[...] = s0i_ref[...]

    if batch_major_input:
        x = pltpu.einshape("bth->tbh", x_ref[...]).reshape(rows, D_MODEL)
    else:
        x = x_ref[...]
    proj = jnp.dot(x.astype(bf16), w_in_ref[...], preferred_element_type=f32)
    x_rg = proj[:, :D_RG]
    g_rg = proj[:, D_RG:2 * D_RG]
    u = proj[:, 2 * D_RG:]

    pad_scr[tail:tail + rows, :] = x_rg
    conv_w = conv_w_ref[...]
    acc = conv_w[0:1, :] * pad_scr[0:rows, :]
    for k in range(1, CONV_WIDTH):
        acc = acc + conv_w[k:k + 1, :] * pad_scr[k * nb:k * nb + rows, :]
    xc = conv_b_ref[...] + acc
    new_tail = pad_scr[rows:rows + tail, :]
    pad_scr[0:tail, :] = new_tail

    xcb = xc.astype(bf16)
    ga = []
    gi = []
    for hh in range(D_RG // MXU_DIM):
        xs = xcb[:, hh * MXU_DIM:(hh + 1) * MXU_DIM]
        ga.append(jnp.dot(xs, wa_ref[hh], preferred_element_type=f32))
        gi.append(jnp.dot(xs, wi_ref[hh], preferred_element_type=f32))
    r = jax.nn.sigmoid(jnp.concatenate(ga, axis=1) + ba_ref[...])
    i = jax.nn.sigmoid(jnp.concatenate(gi, axis=1) + bi_ref[...])
    nlam = -lam_ref[...]
    softplus = jnp.maximum(nlam, 0.0) + jnp.log1p(jnp.exp(-jnp.abs(nlam)))
    log_a = (-RG_C) * r * softplus
    a_scr[...] = jnp.exp(log_a)
    th = jnp.tanh(log_a)
    b_scr[...] = jnp.sqrt((-2.0 * th) / (1.0 - th)) * (i * xc)

    for rg in range(nb // SUBLANES):
        r0 = rg * SUBLANES
        h = hst_scr[r0:r0 + SUBLANES, :]
        for t in range(tc):
            q = t * nb + r0
            h = a_scr[q:q + SUBLANES, :] * h + b_scr[q:q + SUBLANES, :]
            b_scr[q:q + SUBLANES, :] = h
        hst_scr[r0:r0 + SUBLANES, :] = h
    y_rg = b_scr[...] * _gelu_tanh(g_rg)

    ub = u.astype(bf16)
    half = S5_N // 2
    for k in range(2):
        us = ub[:, k * MXU_DIM:(k + 1) * MXU_DIM]
        bur_scr[:, k * half:(k + 1) * half] = jnp.dot(us, bbre_ref[k], preferred_element_type=f32)
        bui_scr[:, k * half:(k + 1) * half] = jnp.dot(us, bbim_ref[k], preferred_element_type=f32)
    for rg in range(nb // SUBLANES):
        r0 = rg * SUBLANES
        for cb in range(S5_N // S5_SCAN_COLS):
            c0 = cb * S5_SCAN_COLS
            ar = jnp.broadcast_to(are_ref[:, c0:c0 + S5_SCAN_COLS], (SUBLANES, S5_SCAN_COLS))
            ai = jnp.broadcast_to(aim_ref[:, c0:c0 + S5_SCAN_COLS], (SUBLANES, S5_SCAN_COLS))
            xr = sr_scr[r0:r0 + SUBLANES, c0:c0 + S5_SCAN_COLS]
            xi = si_scr[r0:r0 + SUBLANES, c0:c0 + S5_SCAN_COLS]
            for t in range(tc):
                q = t * nb + r0
                br = bur_scr[q:q + SUBLANES, c0:c0 + S5_SCAN_COLS]
                bi_ = bui_scr[q:q + SUBLANES, c0:c0 + S5_SCAN_COLS]
                nxr = ar * xr - ai * xi + br
                nxi = ar * xi + ai * xr + bi_
                bur_scr[q:q + SUBLANES, c0:c0 + S5_SCAN_COLS] = nxr
                bui_scr[q:q + SUBLANES, c0:c0 + S5_SCAN_COLS] = nxi
                xr, xi = nxr, nxi
            sr_scr[r0:r0 + SUBLANES, c0:c0 + S5_SCAN_COLS] = xr
            si_scr[r0:r0 + SUBLANES, c0:c0 + S5_SCAN_COLS] = xi
    ys = []
    for j in range(D_S5 // MXU_DIM):
        xrb = bur_scr[:, j * half:(j + 1) * half].astype(bf16)
        xib = bui_scr[:, j * half:(j + 1) * half].astype(bf16)
        ys.append(jnp.dot(xrb, cre_ref[j], preferred_element_type=f32)
                  - jnp.dot(xib, cim_ref[j], preferred_element_type=f32))
    y_s5 = jnp.concatenate(ys, axis=1) + d_ref[...] * u
    yg = _gelu_tanh(y_s5)
    glu = jnp.dot(yg.astype(bf16), wglu_ref[...], preferred_element_type=f32) + bglu_ref[...]
    y_s5 = yg * jax.nn.sigmoid(glu)

    ycat = jnp.concatenate([y_rg, y_s5], axis=1).astype(bf16)
    mix = jnp.dot(ycat, wout_ref[...], preferred_element_type=f32)
    hout_ref[...] = _layer_norm(DN_ALPHA * x + mix, ln_g_ref[...], ln_b_ref[...])

    @pl.when(c == pl.num_programs(0) - 1)
    def _():
        conv_out_ref[...] = pad_scr[0:tail, :]
        hlast_ref[...] = hst_scr[...]
        sre_out_ref[...] = sr_scr[...]
        sim_out_ref[...] = si_scr[...]


def _full(shape):
    n = len(shape)
    return pl.BlockSpec(shape, lambda c: (0,) * n)


def _mixer(x, conv0, h0, s0r, s0i, params, *, nb, tc, name, h_all, total_rows, first_row):
    batch_major_input = x.ndim == 3
    rows = nb * tc
    if batch_major_input:
        n_chunks = x.shape[1] // tc
        x_spec = pl.BlockSpec((nb, tc, D_MODEL), lambda c: (0, c, 0))
    else:
        n_chunks = x.shape[0] // rows
        x_spec = pl.BlockSpec((rows, D_MODEL), lambda c: (c, 0))
    first_block = first_row // rows
    aliased = h_all.shape == (total_rows, D_MODEL)
    tail = (CONV_WIDTH - 1) * nb
    small = (conv0, h0, s0r, s0i) + tuple(params)
    in_specs = [pl.BlockSpec(memory_space=pl.ANY), x_spec]
    in_specs += [_full(a.shape) for a in small]
    out_shape = (jax.ShapeDtypeStruct((total_rows, D_MODEL), f32),
                 jax.ShapeDtypeStruct((tail, D_RG), f32),
                 jax.ShapeDtypeStruct((nb, D_RG), f32),
                 jax.ShapeDtypeStruct((nb, S5_N), f32),
                 jax.ShapeDtypeStruct((nb, S5_N), f32))
    out_specs = (pl.BlockSpec((rows, D_MODEL), lambda c: (c + first_block, 0)),
                 _full((tail, D_RG)), _full((nb, D_RG)), _full((nb, S5_N)), _full((nb, S5_N)))
    scratch = [pltpu.VMEM((rows + tail, D_RG), f32),
               pltpu.VMEM((rows, D_RG), f32),
               pltpu.VMEM((rows, D_RG), f32),
               pltpu.VMEM((rows, S5_N), f32),
               pltpu.VMEM((rows, S5_N), f32),
               pltpu.VMEM((nb, D_RG), f32),
               pltpu.VMEM((nb, S5_N), f32),
               pltpu.VMEM((nb, S5_N), f32)]
    return pl.pallas_call(
        functools.partial(_mixer_kernel, nb=nb, tc=tc, batch_major_input=batch_major_input),
        grid=(n_chunks,),
        in_specs=in_specs,
        out_specs=out_specs,
        out_shape=out_shape,
        scratch_shapes=scratch,
        input_output_aliases={0: 0} if aliased else {},
        compiler_params=pltpu.CompilerParams(
            dimension_semantics=("arbitrary",), vmem_limit_bytes=VMEM_LIMIT_BYTES),
        name=name,
    )(h_all, x, *small)


TOK_TILE = 256
RUN_ALIGN = 16
SLOTS = 3072
SLOT_CHUNK = 512
MAX_CHUNKS = SLOTS // RUN_ALIGN
ROW_TILE = 512
EXT = D_MODEL + 2 * LANES
NO_RUN = 1.0e9


def _top_k_gates(scores, rb):
    rows = scores.shape[0]
    lane_f = jax.lax.broadcasted_iota(jnp.int32, (rows, LANES), 1).astype(f32)
    biased = jnp.where(lane_f < float(N_EXPERTS), scores + rb, -jnp.inf)
    sel = jnp.zeros((rows, LANES), f32)
    mask = jnp.zeros((rows, LANES), f32)
    for _ in range(TOP_K):
        m = jnp.max(biased, axis=1, keepdims=True)
        idx = jnp.min(jnp.where(biased == m, lane_f, float(LANES)), axis=1, keepdims=True)
        hit = lane_f == idx
        sel = jnp.where(hit, scores, sel)
        mask = jnp.where(hit, 1.0, mask)
        biased = jnp.where(hit, -jnp.inf, biased)
    gates = sel / jnp.sum(sel, axis=1, keepdims=True) * ROUTED_SCALE
    return mask, gates


def _router_kernel(h_ref, wr_ref, rb_ref, hext_ref, rankm_ref, cnt_ref):
    hb = h_ref[...].astype(bf16)
    scores = jax.nn.sigmoid(jnp.dot(hb, wr_ref[...], preferred_element_type=f32))
    mask, gates = _top_k_gates(scores, rb_ref[...])
    t_row = jax.lax.broadcasted_iota(jnp.int32, (TOK_TILE, TOK_TILE), 0)
    t_col = jax.lax.broadcasted_iota(jnp.int32, (TOK_TILE, TOK_TILE), 1)
    earlier = jnp.where(t_col < t_row, 1.0, 0.0).astype(bf16)
    rank = jnp.dot(earlier, mask.astype(bf16), preferred_element_type=f32)
    rankm_ref[...] = jnp.where(mask > 0.0, rank, -1.0).astype(bf16)
    cnt_ref[0] = jnp.broadcast_to(jnp.sum(mask, axis=0, keepdims=True), (SUBLANES, LANES))
    g_hi = gates.astype(bf16)
    g_lo = (gates - g_hi.astype(f32)).astype(bf16)
    hext_ref[:, :D_MODEL] = hb
    hext_ref[:, D_MODEL:D_MODEL + LANES] = g_hi
    hext_ref[:, D_MODEL + LANES:] = g_lo


def _router(h, wr, rb):
    n_tiles = h.shape[0] // TOK_TILE
    const = lambda shape: pl.BlockSpec(shape, lambda i: (0,) * len(shape))
    return pl.pallas_call(
        _router_kernel,
        grid=(n_tiles,),
        in_specs=[pl.BlockSpec((TOK_TILE, D_MODEL), lambda i: (i, 0)), const(wr.shape), const(rb.shape)],
        out_specs=(pl.BlockSpec((TOK_TILE, EXT), lambda i: (i, 0)),
                   pl.BlockSpec((TOK_TILE, LANES), lambda i: (i, 0)),
                   pl.BlockSpec((1, SUBLANES, LANES), lambda i: (i, 0, 0))),
        out_shape=(jax.ShapeDtypeStruct((h.shape[0], EXT), bf16),
                   jax.ShapeDtypeStruct((h.shape[0], LANES), bf16),
                   jax.ShapeDtypeStruct((n_tiles, SUBLANES, LANES), f32)),
        compiler_params=pltpu.CompilerParams(dimension_semantics=("arbitrary",)),
        name="moe_router",
    )(h, wr, rb)


def _run_copy(src, dst, sem):
    return pltpu.make_async_copy(src, dst, sem)


def _dispatch_kernel(tot_ref, dstk_ref, zst_ref, zch_ref,
                     hext_ref, rankm_ref, locs_ref, sorted_ref,
                     stage, zbuf, sem, zsem):
    i = pl.program_id(0)
    n = pl.num_programs(0)
    slot = i % 2
    loc_row = locs_ref[0, 0:1, :]
    end_row = locs_ref[0, 1:2, :]
    for ch in range(SLOTS // SLOT_CHUNK):
        def sort_chunk(ch=ch):
            s = (jax.lax.broadcasted_iota(jnp.int32, (SLOT_CHUNK, LANES), 0)
                 + ch * SLOT_CHUNK).astype(f32)
            in_run = jnp.where(s >= loc_row, jnp.where(s < end_row, 1.0, 0.0), 0.0)
            r_col = s[:, 0:1] - jnp.sum(in_run * loc_row, axis=1, keepdims=True)
            q = jax.lax.dot_general(in_run.astype(bf16), rankm_ref[...], (((1,), (1,)), ((), ())),
                                    preferred_element_type=f32)
            p = jnp.where(q == r_col, 1.0, 0.0).astype(bf16)
            rows = jnp.dot(p, hext_ref[...], preferred_element_type=f32)
            stage[slot, ch * SLOT_CHUNK:(ch + 1) * SLOT_CHUNK, :] = rows.astype(bf16)

        def blank_chunk(ch=ch):
            stage[slot, ch * SLOT_CHUNK:(ch + 1) * SLOT_CHUNK, :] = jnp.zeros((SLOT_CHUNK, EXT), bf16)

        if (ch + 1) * SLOT_CHUNK <= TOK_TILE * TOP_K:
            sort_chunk()
        else:
            pl.when(tot_ref[i] > ch * SLOT_CHUNK)(sort_chunk)
            pl.when(tot_ref[i] <= ch * SLOT_CHUNK)(blank_chunk)

    def start_run(k, carry):
        src = stage.at[slot, pl.ds(pl.multiple_of(k * RUN_ALIGN, RUN_ALIGN), RUN_ALIGN)]
        dst = sorted_ref.at[pl.ds(pl.multiple_of(dstk_ref[i, k], RUN_ALIGN), RUN_ALIGN)]
        _run_copy(src, dst, sem.at[slot]).start()
        return carry
    jax.lax.fori_loop(0, MAX_CHUNKS, start_run, 0, unroll=8)

    @pl.when(i == 0)
    def _():
        zbuf[...] = jnp.zeros(zbuf.shape, bf16)

    @pl.when(i < N_EXPERTS)
    def _():
        def zero_copy(k):
            dst = sorted_ref.at[pl.ds(pl.multiple_of(zst_ref[i] + k * RUN_ALIGN, RUN_ALIGN), RUN_ALIGN)]
            return _run_copy(zbuf, dst, zsem.at[0])

        def start_zero(k, carry):
            zero_copy(k).start()
            return carry

        def wait_zero(k, carry):
            zero_copy(k).wait()
            return carry
        jax.lax.fori_loop(0, zch_ref[i], start_zero, 0)
        jax.lax.fori_loop(0, zch_ref[i], wait_zero, 0)

    def wait_runs(slot_):
        for _ in range(MAX_CHUNKS):
            src = stage.at[slot_, pl.ds(0, RUN_ALIGN)]
            dst = sorted_ref.at[pl.ds(0, RUN_ALIGN)]
            _run_copy(src, dst, sem.at[slot_]).wait()

    @pl.when(i > 0)
    def _():
        wait_runs(1 - slot)

    @pl.when(i == n - 1)
    def _():
        wait_runs(slot)


def _dispatch(hext, rankm, locs, tot, dstk, zst, zch, *, total_rows):
    n_tiles = hext.shape[0] // TOK_TILE
    grid_spec = pltpu.PrefetchScalarGridSpec(
        num_scalar_prefetch=4,
        grid=(n_tiles,),
        in_specs=[pl.BlockSpec((TOK_TILE, EXT), lambda i, *_: (i, 0)),
                  pl.BlockSpec((TOK_TILE, LANES), lambda i, *_: (i, 0)),
                  pl.BlockSpec((1, SUBLANES, LANES), lambda i, *_: (i, 0, 0))],
        out_specs=pl.BlockSpec(memory_space=pl.ANY),
        scratch_shapes=[pltpu.VMEM((2, SLOTS, EXT), bf16),
                        pltpu.VMEM((RUN_ALIGN, EXT), bf16),
                        pltpu.SemaphoreType.DMA((2,)),
                        pltpu.SemaphoreType.DMA((1,))])
    return pl.pallas_call(
        _dispatch_kernel,
        grid_spec=grid_spec,
        out_shape=jax.ShapeDtypeStruct((total_rows, EXT), bf16),
        compiler_params=pltpu.CompilerParams(
            dimension_semantics=("arbitrary",), vmem_limit_bytes=VMEM_LIMIT_BYTES),
        name="moe_dispatch",
    )(tot, dstk, zst, zch, hext, rankm, locs)


def _experts_kernel(te_ref, tb_ref, nv_ref, x_ref, wgu_ref, wd_ref, y_ref, wgu_b, wd_b):
    j = pl.program_id(0)
    e = te_ref[j]
    changed = jnp.logical_or(j == 0, e != te_ref[jnp.maximum(j - 1, 0)])

    @pl.when(changed)
    def _():
        wgu_b[...] = wgu_ref[0].astype(bf16)
        wd_b[...] = wd_ref[0].astype(bf16)

    @pl.when(j < nv_ref[0])
    def _():
        x = x_ref[:, :D_MODEL]
        gu = jnp.dot(x, wgu_b[...], preferred_element_type=f32)
        act = (jax.nn.silu(gu[:, :D_EXPERT]) * gu[:, D_EXPERT:]).astype(bf16)
        y = jnp.dot(act, wd_b[...], preferred_element_type=f32)
        gate_all = (x_ref[:, D_MODEL:D_MODEL + LANES].astype(f32)
                    + x_ref[:, D_MODEL + LANES:].astype(f32))
        lane = jax.lax.broadcasted_iota(jnp.int32, (ROW_TILE, LANES), 1)
        gate = jnp.sum(jnp.where(lane == e, gate_all, 0.0), axis=1, keepdims=True)
        y_ref[...] = (y * gate).astype(bf16)


def _experts(xs, wgu, wd, te, tb, nv, *, n_row_tiles):
    grid_spec = pltpu.PrefetchScalarGridSpec(
        num_scalar_prefetch=3,
        grid=(n_row_tiles,),
        in_specs=[pl.BlockSpec((ROW_TILE, EXT), lambda j, te, tb, nv: (tb[j], 0)),
                  pl.BlockSpec((1, D_MODEL, 2 * D_EXPERT), lambda j, te, tb, nv: (te[j], 0, 0)),
                  pl.BlockSpec((1, D_EXPERT, D_MODEL), lambda j, te, tb, nv: (te[j], 0, 0))],
        out_specs=pl.BlockSpec((ROW_TILE, D_MODEL), lambda j, te, tb, nv: (tb[j], 0)),
        scratch_shapes=[pltpu.VMEM((D_MODEL, 2 * D_EXPERT), bf16),
                        pltpu.VMEM((D_EXPERT, D_MODEL), bf16)])
    return pl.pallas_call(
        _experts_kernel,
        grid_spec=grid_spec,
        out_shape=jax.ShapeDtypeStruct((n_row_tiles * ROW_TILE, D_MODEL), bf16),
        compiler_params=pltpu.CompilerParams(
            dimension_semantics=("arbitrary",), vmem_limit_bytes=VMEM_LIMIT_BYTES),
        name="moe_experts",
    )(te, tb, nv, xs, wgu, wd)


def _combine_kernel(tot_ref, srck_ref,
                    h_ref, rankm_ref, locc_ref, ys_ref, wsu_ref, wsd_ref, ln_g_ref, ln_b_ref,
                    out_bm_ref, out_tm_ref, yloc, acc_scr, sem, *, n_bm_tiles):
    i = pl.program_id(0)
    n = pl.num_programs(0)
    slot = i % 2

    def fetch(tile, slot_):
        def body(k, carry):
            src = ys_ref.at[pl.ds(pl.multiple_of(srck_ref[tile, k], RUN_ALIGN), RUN_ALIGN)]
            dst = yloc.at[slot_, pl.ds(pl.multiple_of(k * RUN_ALIGN, RUN_ALIGN), RUN_ALIGN)]
            _run_copy(src, dst, sem.at[slot_]).start()
            return carry
        jax.lax.fori_loop(0, MAX_CHUNKS, body, 0, unroll=8)

    @pl.when(i == 0)
    def _():
        fetch(0, 0)

    @pl.when(i + 1 < n)
    def _():
        fetch(i + 1, 1 - slot)

    h = h_ref[...]
    hb = h.astype(bf16)
    su = jnp.dot(hb, wsu_ref[...], preferred_element_type=f32)
    act = (jax.nn.silu(su[:, :D_SHARED]) * su[:, D_SHARED:]).astype(bf16)
    acc_scr[...] = jnp.dot(act, wsd_ref[...], preferred_element_type=f32)

    for _ in range(MAX_CHUNKS):
        _run_copy(ys_ref.at[pl.ds(0, RUN_ALIGN)], yloc.at[slot, pl.ds(0, RUN_ALIGN)],
                  sem.at[slot]).wait()

    loc_col = locc_ref[0, :, 0:1]
    end_col = locc_ref[0, :, 1:2]
    for ch in range(SLOTS // SLOT_CHUNK):
        def gather_chunk(ch=ch):
            s = (jax.lax.broadcasted_iota(jnp.int32, (LANES, SLOT_CHUNK), 1)
                 + ch * SLOT_CHUNK).astype(f32)
            in_run = jnp.where(s >= loc_col, jnp.where(s < end_col, 1.0, 0.0), 0.0)
            r_row = s[0:1, :] - jnp.sum(in_run * loc_col, axis=0, keepdims=True)
            q = jnp.dot(rankm_ref[...], in_run.astype(bf16), preferred_element_type=f32)
            p = jnp.where(q == r_row, 1.0, 0.0).astype(bf16)
            acc_scr[...] += jnp.dot(p, yloc[slot, ch * SLOT_CHUNK:(ch + 1) * SLOT_CHUNK, :],
                                    preferred_element_type=f32)

        if (ch + 1) * SLOT_CHUNK <= TOK_TILE * TOP_K:
            gather_chunk()
        else:
            pl.when(tot_ref[i] > ch * SLOT_CHUNK)(gather_chunk)
    y = _layer_norm(DN_ALPHA * h + acc_scr[...], ln_g_ref[...], ln_b_ref[...])

    @pl.when(i < n_bm_tiles)
    def _():
        nb, tc, _ = out_bm_ref.shape
        out_bm_ref[...] = pltpu.einshape("tbh->bth", y.reshape(tc, nb, D_MODEL))

    @pl.when(i >= n_bm_tiles)
    def _():
        out_tm_ref[...] = y


def _combine(h, rankm, locc, ys, wsu, wsd, ln_g, ln_b, tot, srck, *, bm_shape):
    n_tiles = h.shape[0] // TOK_TILE
    nb, length, _ = bm_shape
    tc = TOK_TILE // nb
    n_bm_tiles = length // tc
    n_tm_tiles = n_tiles - n_bm_tiles
    const = lambda shape: pl.BlockSpec(shape, lambda i, *_: (0,) * len(shape))
    grid_spec = pltpu.PrefetchScalarGridSpec(
        num_scalar_prefetch=2,
        grid=(n_tiles,),
        in_specs=[pl.BlockSpec((TOK_TILE, D_MODEL), lambda i, *_: (i, 0)),
                  pl.BlockSpec((TOK_TILE, LANES), lambda i, *_: (i, 0)),
                  pl.BlockSpec((1, LANES, 2), lambda i, *_: (i, 0, 0)),
                  pl.BlockSpec(memory_space=pl.ANY),
                  const(wsu.shape), const(wsd.shape), const(ln_g.shape), const(ln_b.shape)],
        out_specs=(pl.BlockSpec((nb, tc, D_MODEL),
                                lambda i, *_: (0, jnp.minimum(i, n_bm_tiles - 1), 0)),
                   pl.BlockSpec((TOK_TILE, D_MODEL),
                                lambda i, *_: (jnp.maximum(i - n_bm_tiles, 0), 0))),
        scratch_shapes=[pltpu.VMEM((2, SLOTS, D_MODEL), bf16),
                        pltpu.VMEM((TOK_TILE, D_MODEL), f32),
                        pltpu.SemaphoreType.DMA((2,))])
    return pl.pallas_call(
        functools.partial(_combine_kernel, n_bm_tiles=n_bm_tiles),
        grid_spec=grid_spec,
        out_shape=(jax.ShapeDtypeStruct(bm_shape, f32),
                   jax.ShapeDtypeStruct((n_tm_tiles * TOK_TILE, D_MODEL), f32)),
        compiler_params=pltpu.CompilerParams(
            dimension_semantics=("arbitrary",), vmem_limit_bytes=VMEM_LIMIT_BYTES),
        name="moe_combine",
    )(tot, srck, h, rankm, locc, ys, wsu, wsd, ln_g, ln_b)


def _round_up(x, m):
    return (x + m - 1) // m * m


def _moe(h, wr, rb, wgu, wd, wsu, wsd, ln_g, ln_b, *, bm_shape):
    n_tok = h.shape[0]
    n_tiles = n_tok // TOK_TILE
    max_rows = _round_up(n_tok * TOP_K + n_tiles * N_EXPERTS * (RUN_ALIGN - 1)
                         + N_EXPERTS * (ROW_TILE - 1), ROW_TILE)
    n_row_tiles = max_rows // ROW_TILE

    hext, rankm, cnt = _router(h, wr, rb)

    i32 = jnp.int32
    cnt = cnt[:, 0, :N_EXPERTS].astype(i32)
    plen = _round_up(cnt, RUN_ALIGN)
    over_tiles = jnp.cumsum(plen, axis=0)
    region_rows = over_tiles[-1]
    region_size = _round_up(region_rows, ROW_TILE)
    region_start = jnp.cumsum(region_size) - region_size
    run_dst = region_start[None, :] + over_tiles - plen
    over_experts = jnp.cumsum(plen, axis=1)
    loc = over_experts - plen
    tot = over_experts[:, -1]
    ch_end = (over_experts // RUN_ALIGN)[:, None, :]
    ch_beg = (loc // RUN_ALIGN)[:, None, :]
    k = jnp.arange(MAX_CHUNKS, dtype=i32)
    kk = k[None, :, None]
    mine = jnp.logical_and(ch_beg <= kk, kk < ch_end)
    chunk_dst = jnp.sum(jnp.where(mine, run_dst[:, None, :] + RUN_ALIGN * (kk - ch_beg), 0), axis=-1)
    live = k[None, :] * RUN_ALIGN < tot[:, None]
    tile_ids = jnp.arange(n_tiles, dtype=i32)[:, None]
    spare = max_rows + (tile_ids % 2) * SLOTS + k[None, :] * RUN_ALIGN
    dstk = jnp.where(live, chunk_dst, spare).astype(i32)
    srck = jnp.where(live, chunk_dst, 0).astype(i32)
    zst = (region_start + region_rows).astype(i32)
    zch = ((region_size - region_rows) // RUN_ALIGN).astype(i32)
    tiles_incl = jnp.cumsum(region_size // ROW_TILE)
    n_valid = tiles_incl[-1]
    tb = jnp.minimum(jnp.arange(n_row_tiles, dtype=i32), n_valid - 1).astype(i32)
    te = jnp.sum((tiles_incl[None, :] <= tb[:, None]).astype(i32), axis=1)
    te = jnp.minimum(te, N_EXPERTS - 1).astype(i32)
    nv = n_valid.reshape(1).astype(i32)

    pad = ((0, 0), (0, LANES - N_EXPERTS))
    loc_f = jnp.pad(loc.astype(f32), pad, constant_values=NO_RUN)
    end_f = jnp.pad(over_experts.astype(f32), pad, constant_values=NO_RUN)
    locs = jnp.concatenate([loc_f[:, None, :], end_f[:, None, :],
                            jnp.zeros((n_tiles, SUBLANES - 2, LANES), f32)], axis=1)
    locc = jnp.stack([loc_f, end_f], axis=-1)

    tot = tot.astype(i32)
    xs = _dispatch(hext, rankm, locs, tot, dstk, zst, zch, total_rows=max_rows + 2 * SLOTS)
    ys = _experts(xs, wgu, wd, te, tb, nv, n_row_tiles=n_row_tiles)
    return _combine(h, rankm, locc, ys, wsu, wsd, ln_g, ln_b, tot, srck, bm_shape=bm_shape)


def _head_block_diag(w):
    heads_per_tile = MXU_DIM // RG_HEAD_DIM
    w4 = w.reshape(D_RG // MXU_DIM, heads_per_tile, RG_HEAD_DIM, RG_HEAD_DIM)
    eye = jnp.eye(heads_per_tile, dtype=w.dtype)
    return jnp.einsum('thij,hk->thikj', w4, eye).reshape(D_RG // MXU_DIM, MXU_DIM, MXU_DIM)


def _s5_in_tiles(b):
    gpt = S5_GROUPS // 2
    b4 = b.reshape(2, gpt, S5_STATE, S5_GROUP)
    eye = jnp.eye(gpt, dtype=b.dtype)
    return jnp.einsum('kgph,gm->kghmp', b4, eye).reshape(2, gpt * S5_GROUP, gpt * S5_STATE)


def _s5_out_tiles(cw):
    gpt = S5_GROUPS // 2
    c4 = cw.reshape(2, gpt, S5_GROUP, S5_STATE)
    eye = jnp.eye(gpt, dtype=cw.dtype)
    return jnp.einsum('kghp,gm->kgpmh', c4, eye).reshape(2, gpt * S5_STATE, gpt * S5_GROUP)


def _row(v):
    return v.reshape(1, -1)


def kernel(x_prompt, x_sample, state_rg_conv, state_rg_h, state_s5_re, state_s5_im, w_in, conv_w, conv_b, rg_w_a, rg_b_a, rg_w_i, rg_b_i, rg_lam, s5_a_re, s5_a_im, s5_log_dt, s5_b_re, s5_b_im, s5_c_re, s5_c_im, s5_d, w_glu, b_glu, w_out, ln1_g, ln1_b, w_router, router_bias, w_gate_up, w_down, w_shared_up, w_shared_down, ln2_g, ln2_b):
    l = 0
    bp, lp, _ = x_prompt.shape
    bs, ls, _ = x_sample.shape

    are, aim, bbre, bbim = _s5_prep(
        _row(s5_a_re[l]), _row(s5_a_im[l]),
        _row(jnp.repeat(s5_log_dt[l], S5_STATE)),
        _s5_in_tiles(s5_b_re[l]), _s5_in_tiles(s5_b_im[l]))
    params = (w_in[l].astype(bf16), conv_w[l], _row(conv_b[l]),
              _head_block_diag(rg_w_a[l]).astype(bf16), _row(rg_b_a[l]),
              _head_block_diag(rg_w_i[l]).astype(bf16), _row(rg_b_i[l]), _row(rg_lam[l]),
              are, aim, bbre, bbim,
              _s5_out_tiles(s5_c_re[l]).astype(bf16), _s5_out_tiles(s5_c_im[l]).astype(bf16),
              _row(s5_d[l]), w_glu[l].astype(bf16), _row(b_glu[l]), w_out[l].astype(bf16),
              _row(ln1_g[l]), _row(ln1_b[l]))

    tail = CONV_WIDTH - 1
    n_tok = lp * bp + ls * bs
    xs_tm = x_sample.transpose(1, 0, 2).reshape(ls * bs, D_MODEL)
    h_all, sc, sh, sre, sim = _mixer(
        xs_tm, state_rg_conv[l].transpose(1, 0, 2).reshape(tail * bs, D_RG), state_rg_h[l],
        state_s5_re[l].reshape(bs, S5_N), state_s5_im[l].reshape(bs, S5_N), params,
        nb=bs, tc=ls, name="mixer_sample",
        h_all=jnp.zeros((SUBLANES, LANES), f32), total_rows=n_tok, first_row=lp * bp)
    h_all, pc, ph, pre, pim = _mixer(
        x_prompt, jnp.zeros((tail * bp, D_RG), f32), jnp.zeros((bp, D_RG), f32),
        jnp.zeros((bp, S5_N), f32), jnp.zeros((bp, S5_N), f32), params,
        nb=bp, tc=TOK_TILE // bp, name="mixer_prompt",
        h_all=h_all, total_rows=n_tok, first_row=0)
    wr = jnp.pad(w_router[l], ((0, 0), (0, LANES - N_EXPERTS))).astype(bf16)
    rb = jnp.pad(_row(router_bias[l]), ((0, 0), (0, LANES - N_EXPERTS)))
    yp, ys_tm = _moe(h_all, wr, rb, w_gate_up[l], w_down[l],
                     w_shared_up[l].astype(bf16), w_shared_down[l].astype(bf16),
                     _row(ln2_g[l]), _row(ln2_b[l]), bm_shape=x_prompt.shape)
    ys = ys_tm.reshape(ls, bs, D_MODEL).transpose(1, 0, 2)

    def conv_out(cv, nbatch):
        return cv.reshape(tail, nbatch, D_RG).transpose(1, 0, 2)[None]

    return (yp, ys,
            conv_out(pc, bp), ph[None],
            pre.reshape(1, bp, S5_GROUPS, S5_STATE), pim.reshape(1, bp, S5_GROUPS, S5_STATE),
            conv_out(sc, bs), sh[None],
            sre.reshape(1, bs, S5_GROUPS, S5_STATE), sim.reshape(1, bs, S5_GROUPS, S5_STATE))
```

```python
import functools
import math

import jax
import jax.numpy as jnp
from jax.experimental import pallas as pl
from jax.experimental.pallas import tpu as pltpu

D_MODEL = 1024
D_RG = 512
RG_HEADS = 8
RG_HEAD_DIM = 64
CONV_WIDTH = 4
RG_C = 8.0
D_S5 = 512
S5_GROUP = 16
S5_GROUPS = 32
S5_STATE = 64
S5_N = S5_GROUPS * S5_STATE
N_EXPERTS = 64
TOP_K = 8
D_EXPERT = 256
D_SHARED = 256
ROUTED_SCALE = 2.5
DEPTH = 1
DN_ALPHA = (2.0 * DEPTH) ** 0.25
LN_EPS = 1e-5

SUBLANES = 8
LANES = 128
MXU_DIM = 256
S5_SCAN_COLS = 512
VMEM_LIMIT_BYTES = 56 * 1024 * 1024

bf16 = jnp.bfloat16
f32 = jnp.float32


def _gelu_tanh(x):
    c = math.sqrt(2.0 / math.pi)
    return x * (0.5 * (1.0 + jnp.tanh(c * (x + 0.044715 * (x * x * x)))))


def _layer_norm(x, g, b):
    mu = jnp.mean(x, axis=-1, keepdims=True)
    xc = x - mu
    var = jnp.mean(xc * xc, axis=-1, keepdims=True)
    return xc * jax.lax.rsqrt(var + LN_EPS) * g + b


def _s5_prep_kernel(lr_ref, li_ref, ldt_ref, bre_ref, bim_ref,
                    are_ref, aim_ref, bbre_ref, bbim_ref):
    lr = lr_ref[...]
    li = li_ref[...]
    dt = jnp.exp(ldt_ref[...])
    mag = jnp.exp(lr * dt)
    abar_re = mag * jnp.cos(li * dt)
    abar_im = mag * jnp.sin(li * dt)
    den = lr * lr + li * li
    nr = abar_re - 1.0
    ni = abar_im
    coef_re = (nr * lr + ni * li) / den
    coef_im = (ni * lr - nr * li) / den
    are_ref[...] = abar_re
    aim_ref[...] = abar_im
    half = S5_N // 2
    for k in range(2):
        cre = coef_re[:, k * half:(k + 1) * half]
        cim = coef_im[:, k * half:(k + 1) * half]
        br = bre_ref[k]
        bi = bim_ref[k]
        bbre_ref[k] = (cre * br - cim * bi).astype(bf16)
        bbim_ref[k] = (cre * bi + cim * br).astype(bf16)


def _s5_prep(lr, li, ldt, bre_t, bim_t):
    half = S5_N // 2
    return pl.pallas_call(
        _s5_prep_kernel,
        out_shape=(jax.ShapeDtypeStruct((1, S5_N), f32),
                   jax.ShapeDtypeStruct((1, S5_N), f32),
                   jax.ShapeDtypeStruct((2, MXU_DIM, half), bf16),
                   jax.ShapeDtypeStruct((2, MXU_DIM, half), bf16)),
        name="s5_prep",
    )(lr, li, ldt, bre_t, bim_t)


def _mixer_kernel(h_all_ref, x_ref, conv0_ref, h0_ref, s0r_ref, s0i_ref,
                  w_in_ref, conv_w_ref, conv_b_ref, wa_ref, ba_ref, wi_ref, bi_ref, lam_ref,
                  are_ref, aim_ref, bbre_ref, bbim_ref, cre_ref, cim_ref, d_ref,
                  wglu_ref, bglu_ref, wout_ref, ln_g_ref, ln_b_ref,
                  hout_ref, conv_out_ref, hlast_ref, sre_out_ref, sim_out_ref,
                  pad_scr, a_scr, b_scr, bur_scr, bui_scr, hst_scr, sr_scr, si_scr,
                  *, nb, tc, batch_major_input):
    del h_all_ref
    rows = nb * tc
    tail = (CONV_WIDTH - 1) * nb
    c = pl.program_id(0)

    @pl.when(c == 0)
    def _():
        pad_scr[0:tail, :] = conv0_ref[...]
        hst_scr[...] = h0_ref[...]
        sr_scr[...] = s0r_ref[...]
        si_scr[...] = s0i_ref[...]

    if batch_major_input:
        x = jnp.transpose(x_ref[...], (1, 0, 2)).reshape(rows, D_MODEL)
    else:
        x = x_ref[...]
    proj = jnp.dot(x.astype(bf16), w_in_ref[...], preferred_element_type=f32)
    x_rg = proj[:, :D_RG]
    g_rg = proj[:, D_RG:2 * D_RG]
    u = proj[:, 2 * D_RG:]

    pad_scr[tail:tail + rows, :] = x_rg
    conv_w = conv_w_ref[...]
    acc = conv_w[0:1, :] * pad_scr[0:rows, :]
    for k in range(1, CONV_WIDTH):
        acc = acc + conv_w[k:k + 1, :] * pad_scr[k * nb:k * nb + rows, :]
    xc = conv_b_ref[...] + acc
    new_tail = pad_scr[rows:rows + tail, :]
    pad_scr[0:tail, :] = new_tail

    xcb = xc.astype(bf16)
    ga = []
    gi = []
    for hh in range(D_RG // MXU_DIM):
        xs = xcb[:, hh * MXU_DIM:(hh + 1) * MXU_DIM]
        ga.append(jnp.dot(xs, wa_ref[hh], preferred_element_type=f32))
        gi.append(jnp.dot(xs, wi_ref[hh], preferred_element_type=f32))
    r = jax.nn.sigmoid(jnp.concatenate(ga, axis=1) + ba_ref[...])
    i = jax.nn.sigmoid(jnp.concatenate(gi, axis=1) + bi_ref[...])
    nlam = -lam_ref[...]
    softplus = jnp.maximum(nlam, 0.0) + jnp.log1p(jnp.exp(-jnp.abs(nlam)))
    log_a = (-RG_C) * r * softplus
    a_scr[...] = jnp.exp(log_a)
    th = jnp.tanh(log_a)
    b_scr[...] = jnp.sqrt((-2.0 * th) / (1.0 - th)) * (i * xc)

    for rg in range(nb // SUBLANES):
        r0 = rg * SUBLANES
        h = hst_scr[r0:r0 + SUBLANES, :]
        for t in range(tc):
            q = t * nb + r0
            h = a_scr[q:q + SUBLANES, :] * h + b_scr[q:q + SUBLANES, :]
            b_scr[q:q + SUBLANES, :] = h
        hst_scr[r0:r0 + SUBLANES, :] = h
    y_rg = b_scr[...] * _gelu_tanh(g_rg)

    ub = u.astype(bf16)
    half = S5_N // 2
    for k in range(2):
        us = ub[:, k * MXU_DIM:(k + 1) * MXU_DIM]
        bur_scr[:, k * half:(k + 1) * half] = jnp.dot(us, bbre_ref[k], preferred_element_type=f32)
        bui_scr[:, k * half:(k + 1) * half] = jnp.dot(us, bbim_ref[k], preferred_element_type=f32)
    for rg in range(nb // SUBLANES):
        r0 = rg * SUBLANES
        for cb in range(S5_N // S5_SCAN_COLS):
            c0 = cb * S5_SCAN_COLS
            ar = jnp.broadcast_to(are_ref[:, c0:c0 + S5_SCAN_COLS], (SUBLANES, S5_SCAN_COLS))
            ai = jnp.broadcast_to(aim_ref[:, c0:c0 + S5_SCAN_COLS], (SUBLANES, S5_SCAN_COLS))
            xr = sr_scr[r0:r0 + SUBLANES, c0:c0 + S5_SCAN_COLS]
            xi = si_scr[r0:r0 + SUBLANES, c0:c0 + S5_SCAN_COLS]
            for t in range(tc):
                q = t * nb + r0
                br = bur_scr[q:q + SUBLANES, c0:c0 + S5_SCAN_COLS]
                bi_ = bui_scr[q:q + SUBLANES, c0:c0 + S5_SCAN_COLS]
                nxr = ar * xr - ai * xi + br
                nxi = ar * xi + ai * xr + bi_
                bur_scr[q:q + SUBLANES, c0:c0 + S5_SCAN_COLS] = nxr
                bui_scr[q:q + SUBLANES, c0:c0 + S5_SCAN_COLS] = nxi
                xr, xi = nxr, nxi
            sr_scr[r0:r0 + SUBLANES, c0:c0 + S5_SCAN_COLS] = xr
            si_scr[r0:r0 + SUBLANES, c0:c0 + S5_SCAN_COLS] = xi
    ys = []
    for j in range(D_S5 // MXU_DIM):
        xrb = bur_scr[:, j * half:(j + 1) * half].astype(bf16)
        xib = bui_scr[:, j * half:(j + 1) * half].astype(bf16)
        ys.append(jnp.dot(xrb, cre_ref[j], preferred_element_type=f32)
                  - jnp.dot(xib, cim_ref[j], preferred_element_type=f32))
    y_s5 = jnp.concatenate(ys, axis=1) + d_ref[...] * u
    yg = _gelu_tanh(y_s5)
    glu = jnp.dot(yg.astype(bf16), wglu_ref[...], preferred_element_type=f32) + bglu_ref[...]
    y_s5 = yg * jax.nn.sigmoid(glu)

    ycat = jnp.concatenate([y_rg, y_s5], axis=1).astype(bf16)
    mix = jnp.dot(ycat, wout_ref[...], preferred_element_type=f32)
    hout_ref[...] = _layer_norm(DN_ALPHA * x + mix, ln_g_ref[...], ln_b_ref[...])

    @pl.when(c == pl.num_programs(0) - 1)
    def _():
        conv_out_ref[...] = pad_scr[0:tail, :]
        hlast_ref[...] = hst_scr[...]
        sre_out_ref[...] = sr_scr[...]
        sim_out_ref[...] = si_scr[...]


def _full(shape):
    n = len(shape)
    return pl.BlockSpec(shape, lambda c: (0,) * n)


def _mixer(x, conv0, h0, s0r, s0i, params, *, nb, tc, name, h_all, total_rows, first_row):
    batch_major_input = x.ndim == 3
    rows = nb * tc
    if batch_major_input:
        n_chunks = x.shape[1] // tc
        x_spec = pl.BlockSpec((nb, tc, D_MODEL), lambda c: (0, c, 0))
    else:
        n_chunks = x.shape[0] // rows
        x_spec = pl.BlockSpec((rows, D_MODEL), lambda c: (c, 0))
    first_block = first_row // rows
    aliased = h_all.shape == (total_rows, D_MODEL)
    tail = (CONV_WIDTH - 1) * nb
    small = (conv0, h0, s0r, s0i) + tuple(params)
    in_specs = [pl.BlockSpec(memory_space=pl.ANY), x_spec]
    in_specs += [_full(a.shape) for a in small]
    out_shape = (jax.ShapeDtypeStruct((total_rows, D_MODEL), f32),
                 jax.ShapeDtypeStruct((tail, D_RG), f32),
                 jax.ShapeDtypeStruct((nb, D_RG), f32),
                 jax.ShapeDtypeStruct((nb, S5_N), f32),
                 jax.ShapeDtypeStruct((nb, S5_N), f32))
    out_specs = (pl.BlockSpec((rows, D_MODEL), lambda c: (c + first_block, 0)),
                 _full((tail, D_RG)), _full((nb, D_RG)), _full((nb, S5_N)), _full((nb, S5_N)))
    scratch = [pltpu.VMEM((rows + tail, D_RG), f32),
               pltpu.VMEM((rows, D_RG), f32),
               pltpu.VMEM((rows, D_RG), f32),
               pltpu.VMEM((rows, S5_N), f32),
               pltpu.VMEM((rows, S5_N), f32),
               pltpu.VMEM((nb, D_RG), f32),
               pltpu.VMEM((nb, S5_N), f32),
               pltpu.VMEM((nb, S5_N), f32)]
    return pl.pallas_call(
        functools.partial(_mixer_kernel, nb=nb, tc=tc, batch_major_input=batch_major_input),
        grid=(n_chunks,),
        in_specs=in_specs,
        out_specs=out_specs,
        out_shape=out_shape,
        scratch_shapes=scratch,
        input_output_aliases={0: 0} if aliased else {},
        compiler_params=pltpu.CompilerParams(
            dimension_semantics=("arbitrary",), vmem_limit_bytes=VMEM_LIMIT_BYTES),
        name=name,
    )(h_all, x, *small)


TOK_TILE = 256
RUN_ALIGN = 16
SLOTS = 3072
SLOT_CHUNK = 512
MAX_CHUNKS = SLOTS // RUN_ALIGN
ROW_TILE = 256
EXT = D_MODEL + 2 * LANES
NO_RUN = 1.0e9


def _top_k_gates(scores, rb):
    rows = scores.shape[0]
    lane_f = jax.lax.broadcasted_iota(jnp.int32, (rows, LANES), 1).astype(f32)
    biased = jnp.where(lane_f < float(N_EXPERTS), scores + rb, -jnp.inf)
    sel = jnp.zeros((rows, LANES), f32)
    mask = jnp.zeros((rows, LANES), f32)
    for _ in range(TOP_K):
        m = jnp.max(biased, axis=1, keepdims=True)
        idx = jnp.min(jnp.where(biased == m, lane_f, float(LANES)), axis=1, keepdims=True)
        hit = lane_f == idx
        sel = jnp.where(hit, scores, sel)
        mask = jnp.where(hit, 1.0, mask)
        biased = jnp.where(hit, -jnp.inf, biased)
    gates = sel / jnp.sum(sel, axis=1, keepdims=True) * ROUTED_SCALE
    return mask, gates


def _router_kernel(h_ref, wr_ref, rb_ref, hext_ref, rankm_ref, cnt_ref):
    hb = h_ref[...].astype(bf16)
    scores = jax.nn.sigmoid(jnp.dot(hb, wr_ref[...], preferred_element_type=f32))
    mask, gates = _top_k_gates(scores, rb_ref[...])
    t_row = jax.lax.broadcasted_iota(jnp.int32, (TOK_TILE, TOK_TILE), 0)
    t_col = jax.lax.broadcasted_iota(jnp.int32, (TOK_TILE, TOK_TILE), 1)
    earlier = jnp.where(t_col < t_row, 1.0, 0.0).astype(bf16)
    rank = jnp.dot(earlier, mask.astype(bf16), preferred_element_type=f32)
    rankm_ref[...] = jnp.where(mask > 0.0, rank, -1.0).astype(bf16)
    cnt_ref[0] = jnp.broadcast_to(jnp.sum(mask, axis=0, keepdims=True), (SUBLANES, LANES))
    g_hi = gates.astype(bf16)
    g_lo = (gates - g_hi.astype(f32)).astype(bf16)
    hext_ref[:, :D_MODEL] = hb
    hext_ref[:, D_MODEL:D_MODEL + LANES] = g_hi
    hext_ref[:, D_MODEL + LANES:] = g_lo


def _router(h, wr, rb):
    n_tiles = h.shape[0] // TOK_TILE
    const = lambda shape: pl.BlockSpec(shape, lambda i: (0,) * len(shape))
    return pl.pallas_call(
        _router_kernel,
        grid=(n_tiles,),
        in_specs=[pl.BlockSpec((TOK_TILE, D_MODEL), lambda i: (i, 0)), const(wr.shape), const(rb.shape)],
        out_specs=(pl.BlockSpec((TOK_TILE, EXT), lambda i: (i, 0)),
                   pl.BlockSpec((TOK_TILE, LANES), lambda i: (i, 0)),
                   pl.BlockSpec((1, SUBLANES, LANES), lambda i: (i, 0, 0))),
        out_shape=(jax.ShapeDtypeStruct((h.shape[0], EXT), bf16),
                   jax.ShapeDtypeStruct((h.shape[0], LANES), bf16),
                   jax.ShapeDtypeStruct((n_tiles, SUBLANES, LANES), f32)),
        compiler_params=pltpu.CompilerParams(dimension_semantics=("arbitrary",)),
        name="moe_router",
    )(h, wr, rb)


def _run_copy(src, dst, sem):
    return pltpu.make_async_copy(src, dst, sem)


def _dispatch_kernel(tot_ref, dstk_ref, zst_ref, zch_ref,
                     hext_ref, rankm_ref, locs_ref, sorted_ref,
                     stage, zbuf, sem, zsem):
    i = pl.program_id(0)
    n = pl.num_programs(0)
    slot = i % 2
    loc_row = locs_ref[0, 0:1, :]
    end_row = locs_ref[0, 1:2, :]
    for ch in range(SLOTS // SLOT_CHUNK):
        def sort_chunk(ch=ch):
            s = (jax.lax.broadcasted_iota(jnp.int32, (SLOT_CHUNK, LANES), 0)
                 + ch * SLOT_CHUNK).astype(f32)
            in_run = jnp.where(s >= loc_row, jnp.where(s < end_row, 1.0, 0.0), 0.0)
            r_col = s[:, 0:1] - jnp.sum(in_run * loc_row, axis=1, keepdims=True)
            q = jax.lax.dot_general(in_run.astype(bf16), rankm_ref[...], (((1,), (1,)), ((), ())),
                                    preferred_element_type=f32)
            p = jnp.where(q == r_col, 1.0, 0.0).astype(bf16)
            rows = jnp.dot(p, hext_ref[...], preferred_element_type=f32)
            stage[slot, ch * SLOT_CHUNK:(ch + 1) * SLOT_CHUNK, :] = rows.astype(bf16)

        def blank_chunk(ch=ch):
            stage[slot, ch * SLOT_CHUNK:(ch + 1) * SLOT_CHUNK, :] = jnp.zeros((SLOT_CHUNK, EXT), bf16)

        if (ch + 1) * SLOT_CHUNK <= TOK_TILE * TOP_K:
            sort_chunk()
        else:
            pl.when(tot_ref[i] > ch * SLOT_CHUNK)(sort_chunk)
            pl.when(tot_ref[i] <= ch * SLOT_CHUNK)(blank_chunk)

    def start_run(k, carry):
        src = stage.at[slot, pl.ds(pl.multiple_of(k * RUN_ALIGN, RUN_ALIGN), RUN_ALIGN)]
        dst = sorted_ref.at[pl.ds(pl.multiple_of(dstk_ref[i, k], RUN_ALIGN), RUN_ALIGN)]
        _run_copy(src, dst, sem.at[slot]).start()
        return carry
    jax.lax.fori_loop(0, MAX_CHUNKS, start_run, 0, unroll=8)

    @pl.when(i == 0)
    def _():
        zbuf[...] = jnp.zeros(zbuf.shape, bf16)

    @pl.when(i < N_EXPERTS)
    def _():
        def zero_copy(k):
            dst = sorted_ref.at[pl.ds(pl.multiple_of(zst_ref[i] + k * RUN_ALIGN, RUN_ALIGN), RUN_ALIGN)]
            return _run_copy(zbuf, dst, zsem.at[0])

        def start_zero(k, carry):
            zero_copy(k).start()
            return carry

        def wait_zero(k, carry):
            zero_copy(k).wait()
            return carry
        jax.lax.fori_loop(0, zch_ref[i], start_zero, 0)
        jax.lax.fori_loop(0, zch_ref[i], wait_zero, 0)

    def wait_runs(slot_):
        for _ in range(MAX_CHUNKS):
            src = stage.at[slot_, pl.ds(0, RUN_ALIGN)]
            dst = sorted_ref.at[pl.ds(0, RUN_ALIGN)]
            _run_copy(src, dst, sem.at[slot_]).wait()

    @pl.when(i > 0)
    def _():
        wait_runs(1 - slot)

    @pl.when(i == n - 1)
    def _():
        wait_runs(slot)


def _dispatch(hext, rankm, locs, tot, dstk, zst, zch, *, total_rows):
    n_tiles = hext.shape[0] // TOK_TILE
    grid_spec = pltpu.PrefetchScalarGridSpec(
        num_scalar_prefetch=4,
        grid=(n_tiles,),
        in_specs=[pl.BlockSpec((TOK_TILE, EXT), lambda i, *_: (i, 0)),
                  pl.BlockSpec((TOK_TILE, LANES), lambda i, *_: (i, 0)),
                  pl.BlockSpec((1, SUBLANES, LANES), lambda i, *_: (i, 0, 0))],
        out_specs=pl.BlockSpec(memory_space=pl.ANY),
        scratch_shapes=[pltpu.VMEM((2, SLOTS, EXT), bf16),
                        pltpu.VMEM((RUN_ALIGN, EXT), bf16),
                        pltpu.SemaphoreType.DMA((2,)),
                        pltpu.SemaphoreType.DMA((1,))])
    return pl.pallas_call(
        _dispatch_kernel,
        grid_spec=grid_spec,
        out_shape=jax.ShapeDtypeStruct((total_rows, EXT), bf16),
        compiler_params=pltpu.CompilerParams(
            dimension_semantics=("arbitrary",), vmem_limit_bytes=VMEM_LIMIT_BYTES),
        name="moe_dispatch",
    )(tot, dstk, zst, zch, hext, rankm, locs)


def _experts_kernel(ts_ref, xs_ref, wgu_ref, wd_ref, ys_ref,
                    xbuf, ybuf, wgu_b, wd_b, xsem, ysem):
    e = pl.program_id(0)
    n_valid = ts_ref[N_EXPERTS]
    wgu_b[...] = wgu_ref[0].astype(bf16)
    wd_b[...] = wd_ref[0].astype(bf16)

    def x_copy(j, slot):
        rows = pl.ds(pl.multiple_of(j * ROW_TILE, ROW_TILE), ROW_TILE)
        return pltpu.make_async_copy(xs_ref.at[rows], xbuf.at[slot], xsem.at[slot])

    def y_copy(j, slot):
        rows = pl.ds(pl.multiple_of(j * ROW_TILE, ROW_TILE), ROW_TILE)
        return pltpu.make_async_copy(ybuf.at[slot], ys_ref.at[rows], ysem.at[slot])

    @pl.when(e == 0)
    def _():
        x_copy(0, 0).start()

    def tile(j, carry):
        slot = j % 2
        x_copy(j, slot).wait()

        @pl.when(j + 1 < n_valid)
        def _():
            x_copy(j + 1, 1 - slot).start()

        @pl.when(j >= 2)
        def _():
            y_copy(j - 2, slot).wait()

        x = xbuf[slot, :, :D_MODEL]
        gu = jnp.dot(x, wgu_b[...], preferred_element_type=f32)
        act = (jax.nn.silu(gu[:, :D_EXPERT]) * gu[:, D_EXPERT:]).astype(bf16)
        y = jnp.dot(act, wd_b[...], preferred_element_type=f32)
        gate_all = (xbuf[slot, :, D_MODEL:D_MODEL + LANES].astype(f32)
                    + xbuf[slot, :, D_MODEL + LANES:].astype(f32))
        lane = jax.lax.broadcasted_iota(jnp.int32, (ROW_TILE, LANES), 1)
        gate = jnp.sum(jnp.where(lane == e, gate_all, 0.0), axis=1, keepdims=True)
        ybuf[slot] = (y * gate).astype(bf16)
        y_copy(j, slot).start()
        return carry
    jax.lax.fori_loop(ts_ref[e], ts_ref[e + 1], tile, 0)

    @pl.when(e == pl.num_programs(0) - 1)
    def _():
        @pl.when(n_valid >= 2)
        def _():
            y_copy(n_valid - 2, n_valid % 2).wait()
        y_copy(n_valid - 1, (n_valid - 1) % 2).wait()


def _experts(xs, wgu, wd, ts, *, n_rows):
    grid_spec = pltpu.PrefetchScalarGridSpec(
        num_scalar_prefetch=1,
        grid=(N_EXPERTS,),
        in_specs=[pl.BlockSpec(memory_space=pl.ANY),
                  pl.BlockSpec((1, D_MODEL, 2 * D_EXPERT), lambda e, ts: (e, 0, 0)),
                  pl.BlockSpec((1, D_EXPERT, D_MODEL), lambda e, ts: (e, 0, 0))],
        out_specs=pl.BlockSpec(memory_space=pl.ANY),
        scratch_shapes=[pltpu.VMEM((2, ROW_TILE, EXT), bf16),
                        pltpu.VMEM((2, ROW_TILE, D_MODEL), bf16),
                        pltpu.VMEM((D_MODEL, 2 * D_EXPERT), bf16),
                        pltpu.VMEM((D_EXPERT, D_MODEL), bf16),
                        pltpu.SemaphoreType.DMA((2,)),
                        pltpu.SemaphoreType.DMA((2,))])
    return pl.pallas_call(
        _experts_kernel,
        grid_spec=grid_spec,
        out_shape=jax.ShapeDtypeStruct((n_rows, D_MODEL), bf16),
        compiler_params=pltpu.CompilerParams(
            dimension_semantics=("arbitrary",), vmem_limit_bytes=VMEM_LIMIT_BYTES),
        name="moe_experts",
    )(ts, xs, wgu, wd)


def _combine_kernel(tot_ref, srck_ref,
                    h_ref, rankm_ref, locc_ref, ys_ref, wsu_ref, wsd_ref, ln_g_ref, ln_b_ref,
                    out_bm_ref, out_tm_ref, yloc, acc_scr, sem, *, n_bm_tiles):
    i = pl.program_id(0)
    n = pl.num_programs(0)
    slot = i % 2

    def fetch(tile, slot_):
        def body(k, carry):
            src = ys_ref.at[pl.ds(pl.multiple_of(srck_ref[tile, k], RUN_ALIGN), RUN_ALIGN)]
            dst = yloc.at[slot_, pl.ds(pl.multiple_of(k * RUN_ALIGN, RUN_ALIGN), RUN_ALIGN)]
            _run_copy(src, dst, sem.at[slot_]).start()
            return carry
        jax.lax.fori_loop(0, MAX_CHUNKS, body, 0, unroll=8)

    @pl.when(i == 0)
    def _():
        fetch(0, 0)

    @pl.when(i + 1 < n)
    def _():
        fetch(i + 1, 1 - slot)

    h = h_ref[...]
    hb = h.astype(bf16)
    su = jnp.dot(hb, wsu_ref[...], preferred_element_type=f32)
    act = (jax.nn.silu(su[:, :D_SHARED]) * su[:, D_SHARED:]).astype(bf16)
    acc_scr[...] = jnp.dot(act, wsd_ref[...], preferred_element_type=f32)

    for _ in range(MAX_CHUNKS):
        _run_copy(ys_ref.at[pl.ds(0, RUN_ALIGN)], yloc.at[slot, pl.ds(0, RUN_ALIGN)],
                  sem.at[slot]).wait()

    loc_col = locc_ref[0, :, 0:1]
    end_col = locc_ref[0, :, 1:2]
    for ch in range(SLOTS // SLOT_CHUNK):
        def gather_chunk(ch=ch):
            s = (jax.lax.broadcasted_iota(jnp.int32, (LANES, SLOT_CHUNK), 1)
                 + ch * SLOT_CHUNK).astype(f32)
            in_run = jnp.where(s >= loc_col, jnp.where(s < end_col, 1.0, 0.0), 0.0)
            r_row = s[0:1, :] - jnp.sum(in_run * loc_col, axis=0, keepdims=True)
            q = jnp.dot(rankm_ref[...], in_run.astype(bf16), preferred_element_type=f32)
            p = jnp.where(q == r_row, 1.0, 0.0).astype(bf16)
            acc_scr[...] += jnp.dot(p, yloc[slot, ch * SLOT_CHUNK:(ch + 1) * SLOT_CHUNK, :],
                                    preferred_element_type=f32)

        if (ch + 1) * SLOT_CHUNK <= TOK_TILE * TOP_K:
            gather_chunk()
        else:
            pl.when(tot_ref[i] > ch * SLOT_CHUNK)(gather_chunk)
    y = _layer_norm(DN_ALPHA * h + acc_scr[...], ln_g_ref[...], ln_b_ref[...])

    @pl.when(i < n_bm_tiles)
    def _():
        nb, tc, _ = out_bm_ref.shape
        out_bm_ref[...] = jnp.transpose(y.reshape(tc, nb, D_MODEL), (1, 0, 2))

    @pl.when(i >= n_bm_tiles)
    def _():
        out_tm_ref[...] = y


def _combine(h, rankm, locc, ys, wsu, wsd, ln_g, ln_b, tot, srck, *, bm_shape):
    n_tiles = h.shape[0] // TOK_TILE
    nb, length, _ = bm_shape
    tc = TOK_TILE // nb
    n_bm_tiles = length // tc
    n_tm_tiles = n_tiles - n_bm_tiles
    const = lambda shape: pl.BlockSpec(shape, lambda i, *_: (0,) * len(shape))
    grid_spec = pltpu.PrefetchScalarGridSpec(
        num_scalar_prefetch=2,
        grid=(n_tiles,),
        in_specs=[pl.BlockSpec((TOK_TILE, D_MODEL), lambda i, *_: (i, 0)),
                  pl.BlockSpec((TOK_TILE, LANES), lambda i, *_: (i, 0)),
                  pl.BlockSpec((1, LANES, 2), lambda i, *_: (i, 0, 0)),
                  pl.BlockSpec(memory_space=pl.ANY),
                  const(wsu.shape), const(wsd.shape), const(ln_g.shape), const(ln_b.shape)],
        out_specs=(pl.BlockSpec((nb, tc, D_MODEL),
                                lambda i, *_: (0, jnp.minimum(i, n_bm_tiles - 1), 0)),
                   pl.BlockSpec((TOK_TILE, D_MODEL),
                                lambda i, *_: (jnp.maximum(i - n_bm_tiles, 0), 0))),
        scratch_shapes=[pltpu.VMEM((2, SLOTS, D_MODEL), bf16),
                        pltpu.VMEM((TOK_TILE, D_MODEL), f32),
                        pltpu.SemaphoreType.DMA((2,))])
    return pl.pallas_call(
        functools.partial(_combine_kernel, n_bm_tiles=n_bm_tiles),
        grid_spec=grid_spec,
        out_shape=(jax.ShapeDtypeStruct(bm_shape, f32),
                   jax.ShapeDtypeStruct((n_tm_tiles * TOK_TILE, D_MODEL), f32)),
        compiler_params=pltpu.CompilerParams(
            dimension_semantics=("arbitrary",), vmem_limit_bytes=VMEM_LIMIT_BYTES),
        name="moe_combine",
    )(tot, srck, h, rankm, locc, ys, wsu, wsd, ln_g, ln_b)


def _round_up(x, m):
    return (x + m - 1) // m * m


def _moe(h, wr, rb, wgu, wd, wsu, wsd, ln_g, ln_b, *, bm_shape):
    n_tok = h.shape[0]
    n_tiles = n_tok // TOK_TILE
    max_rows = _round_up(n_tok * TOP_K + n_tiles * N_EXPERTS * (RUN_ALIGN - 1)
                         + N_EXPERTS * (ROW_TILE - 1), ROW_TILE)

    hext, rankm, cnt = _router(h, wr, rb)

    i32 = jnp.int32
    cnt = cnt[:, 0, :N_EXPERTS].astype(i32)
    plen = _round_up(cnt, RUN_ALIGN)
    over_tiles = jnp.cumsum(plen, axis=0)
    region_rows = over_tiles[-1]
    region_size = _round_up(region_rows, ROW_TILE)
    region_start = jnp.cumsum(region_size) - region_size
    run_dst = region_start[None, :] + over_tiles - plen
    over_experts = jnp.cumsum(plen, axis=1)
    loc = over_experts - plen
    tot = over_experts[:, -1]
    ch_end = (over_experts // RUN_ALIGN)[:, None, :]
    ch_beg = (loc // RUN_ALIGN)[:, None, :]
    k = jnp.arange(MAX_CHUNKS, dtype=i32)
    kk = k[None, :, None]
    mine = jnp.logical_and(ch_beg <= kk, kk < ch_end)
    chunk_dst = jnp.sum(jnp.where(mine, run_dst[:, None, :] + RUN_ALIGN * (kk - ch_beg), 0), axis=-1)
    live = k[None, :] * RUN_ALIGN < tot[:, None]
    tile_ids = jnp.arange(n_tiles, dtype=i32)[:, None]
    spare = max_rows + (tile_ids % 2) * SLOTS + k[None, :] * RUN_ALIGN
    dstk = jnp.where(live, chunk_dst, spare).astype(i32)
    srck = jnp.where(live, chunk_dst, 0).astype(i32)
    zst = (region_start + region_rows).astype(i32)
    zch = ((region_size - region_rows) // RUN_ALIGN).astype(i32)
    tile_start = jnp.concatenate([region_start, region_start[-1:] + region_size[-1:]]) // ROW_TILE
    tile_start = tile_start.astype(i32)

    pad = ((0, 0), (0, LANES - N_EXPERTS))
    loc_f = jnp.pad(loc.astype(f32), pad, constant_values=NO_RUN)
    end_f = jnp.pad(over_experts.astype(f32), pad, constant_values=NO_RUN)
    locs = jnp.concatenate([loc_f[:, None, :], end_f[:, None, :],
                            jnp.zeros((n_tiles, SUBLANES - 2, LANES), f32)], axis=1)
    locc = jnp.stack([loc_f, end_f], axis=-1)

    tot = tot.astype(i32)
    xs = _dispatch(hext, rankm, locs, tot, dstk, zst, zch, total_rows=max_rows + 2 * SLOTS)
    ys = _experts(xs, wgu, wd, tile_start, n_rows=max_rows)
    return _combine(h, rankm, locc, ys, wsu, wsd, ln_g, ln_b, tot, srck, bm_shape=bm_shape)


def _head_block_diag(w):
    heads_per_tile = MXU_DIM // RG_HEAD_DIM
    w4 = w.reshape(D_RG // MXU_DIM, heads_per_tile, RG_HEAD_DIM, RG_HEAD_DIM)
    eye = jnp.eye(heads_per_tile, dtype=w.dtype)
    return jnp.einsum('thij,hk->thikj', w4, eye).reshape(D_RG // MXU_DIM, MXU_DIM, MXU_DIM)


def _s5_in_tiles(b):
    gpt = S5_GROUPS // 2
    b4 = b.reshape(2, gpt, S5_STATE, S5_GROUP)
    eye = jnp.eye(gpt, dtype=b.dtype)
    return jnp.einsum('kgph,gm->kghmp', b4, eye).reshape(2, gpt * S5_GROUP, gpt * S5_STATE)


def _s5_out_tiles(cw):
    gpt = S5_GROUPS // 2
    c4 = cw.reshape(2, gpt, S5_GROUP, S5_STATE)
    eye = jnp.eye(gpt, dtype=cw.dtype)
    return jnp.einsum('kghp,gm->kgpmh', c4, eye).reshape(2, gpt * S5_STATE, gpt * S5_GROUP)


def _row(v):
    return v.reshape(1, -1)


def kernel(x_prompt, x_sample, state_rg_conv, state_rg_h, state_s5_re, state_s5_im, w_in, conv_w, conv_b, rg_w_a, rg_b_a, rg_w_i, rg_b_i, rg_lam, s5_a_re, s5_a_im, s5_log_dt, s5_b_re, s5_b_im, s5_c_re, s5_c_im, s5_d, w_glu, b_glu, w_out, ln1_g, ln1_b, w_router, router_bias, w_gate_up, w_down, w_shared_up, w_shared_down, ln2_g, ln2_b):
    l = 0
    bp, lp, _ = x_prompt.shape
    bs, ls, _ = x_sample.shape

    are, aim, bbre, bbim = _s5_prep(
        _row(s5_a_re[l]), _row(s5_a_im[l]),
        _row(jnp.repeat(s5_log_dt[l], S5_STATE)),
        _s5_in_tiles(s5_b_re[l]), _s5_in_tiles(s5_b_im[l]))
    params = (w_in[l].astype(bf16), conv_w[l], _row(conv_b[l]),
              _head_block_diag(rg_w_a[l]).astype(bf16), _row(rg_b_a[l]),
              _head_block_diag(rg_w_i[l]).astype(bf16), _row(rg_b_i[l]), _row(rg_lam[l]),
              are, aim, bbre, bbim,
              _s5_out_tiles(s5_c_re[l]).astype(bf16), _s5_out_tiles(s5_c_im[l]).astype(bf16),
              _row(s5_d[l]), w_glu[l].astype(bf16), _row(b_glu[l]), w_out[l].astype(bf16),
              _row(ln1_g[l]), _row(ln1_b[l]))

    tail = CONV_WIDTH - 1
    n_tok = lp * bp + ls * bs
    xs_tm = x_sample.transpose(1, 0, 2).reshape(ls * bs, D_MODEL)
    h_all, sc, sh, sre, sim = _mixer(
        xs_tm, state_rg_conv[l].transpose(1, 0, 2).reshape(tail * bs, D_RG), state_rg_h[l],
        state_s5_re[l].reshape(bs, S5_N), state_s5_im[l].reshape(bs, S5_N), params,
        nb=bs, tc=ls, name="mixer_sample",
        h_all=jnp.zeros((SUBLANES, LANES), f32), total_rows=n_tok, first_row=lp * bp)
    h_all, pc, ph, pre, pim = _mixer(
        x_prompt, jnp.zeros((tail * bp, D_RG), f32), jnp.zeros((bp, D_RG), f32),
        jnp.zeros((bp, S5_N), f32), jnp.zeros((bp, S5_N), f32), params,
        nb=bp, tc=TOK_TILE // bp, name="mixer_prompt",
        h_all=h_all, total_rows=n_tok, first_row=0)
    wr = jnp.pad(w_router[l], ((0, 0), (0, LANES - N_EXPERTS))).astype(bf16)
    rb = jnp.pad(_row(router_bias[l]), ((0, 0), (0, LANES - N_EXPERTS)))
    yp, ys_tm = _moe(h_all, wr, rb, w_gate_up[l], w_down[l],
                     w_shared_up[l].astype(bf16), w_shared_down[l].astype(bf16),
                     _row(ln2_g[l]), _row(ln2_b[l]), bm_shape=x_prompt.shape)
    ys = ys_tm.reshape(ls, bs, D_MODEL).transpose(1, 0, 2)

    def conv_out(cv, nbatch):
        return cv.reshape(tail, nbatch, D_RG).transpose(1, 0, 2)[None]

    return (yp, ys,
            conv_out(pc, bp), ph[None],
            pre.reshape(1, bp, S5_GROUPS, S5_STATE), pim.reshape(1, bp, S5_GROUPS, S5_STATE),
            conv_out(sc, bs), sh[None],
            sre.reshape(1, bs, S5_GROUPS, S5_STATE), sim.reshape(1, bs, S5_GROUPS, S5_STATE))
```

```python
import functools
import math

import jax
import jax.numpy as jnp
from jax.experimental import pallas as pl
from jax.experimental.pallas import tpu as pltpu

D_MODEL = 1024
D_RG = 512
RG_HEADS = 8
RG_HEAD_DIM = 64
CONV_WIDTH = 4
RG_C = 8.0
D_S5 = 512
S5_GROUP = 16
S5_GROUPS = 32
S5_STATE = 64
S5_N = S5_GROUPS * S5_STATE
N_EXPERTS = 64
TOP_K = 8
D_EXPERT = 256
D_SHARED = 256
ROUTED_SCALE = 2.5
DEPTH = 1
DN_ALPHA = (2.0 * DEPTH) ** 0.25
LN_EPS = 1e-5

SUBLANES = 8
LANES = 128
MXU_DIM = 256
S5_SCAN_COLS = 512
VMEM_LIMIT_BYTES = 56 * 1024 * 1024

bf16 = jnp.bfloat16
f32 = jnp.float32


def _gelu_tanh(x):
    c = math.sqrt(2.0 / math.pi)
    return x * (0.5 * (1.0 + jnp.tanh(c * (x + 0.044715 * (x * x * x)))))


def _layer_norm(x, g, b):
    mu = jnp.mean(x, axis=-1, keepdims=True)
    xc = x - mu
    var = jnp.mean(xc * xc, axis=-1, keepdims=True)
    return xc * jax.lax.rsqrt(var + LN_EPS) * g + b


def _s5_prep_kernel(lr_ref, li_ref, ldt_ref, bre_ref, bim_ref,
                    are_ref, aim_ref, bbre_ref, bbim_ref):
    lr = lr_ref[...]
    li = li_ref[...]
    dt = jnp.exp(ldt_ref[...])
    mag = jnp.exp(lr * dt)
    abar_re = mag * jnp.cos(li * dt)
    abar_im = mag * jnp.sin(li * dt)
    den = lr * lr + li * li
    nr = abar_re - 1.0
    ni = abar_im
    coef_re = (nr * lr + ni * li) / den
    coef_im = (ni * lr - nr * li) / den
    are_ref[...] = abar_re
    aim_ref[...] = abar_im
    half = S5_N // 2
    for k in range(2):
        cre = coef_re[:, k * half:(k + 1) * half]
        cim = coef_im[:, k * half:(k + 1) * half]
        br = bre_ref[k]
        bi = bim_ref[k]
        bbre_ref[k] = (cre * br - cim * bi).astype(bf16)
        bbim_ref[k] = (cre * bi + cim * br).astype(bf16)


def _s5_prep(lr, li, ldt, bre_t, bim_t):
    half = S5_N // 2
    return pl.pallas_call(
        _s5_prep_kernel,
        out_shape=(jax.ShapeDtypeStruct((1, S5_N), f32),
                   jax.ShapeDtypeStruct((1, S5_N), f32),
                   jax.ShapeDtypeStruct((2, MXU_DIM, half), bf16),
                   jax.ShapeDtypeStruct((2, MXU_DIM, half), bf16)),
        name="s5_prep",
    )(lr, li, ldt, bre_t, bim_t)


def _mixer_kernel(h_all_ref, x_ref, conv0_ref, h0_ref, s0r_ref, s0i_ref,
                  w_in_ref, conv_w_ref, conv_b_ref, wa_ref, ba_ref, wi_ref, bi_ref, lam_ref,
                  are_ref, aim_ref, bbre_ref, bbim_ref, cre_ref, cim_ref, d_ref,
                  wglu_ref, bglu_ref, wout_ref, ln_g_ref, ln_b_ref,
                  hout_ref, conv_out_ref, hlast_ref, sre_out_ref, sim_out_ref,
                  pad_scr, a_scr, b_scr, bur_scr, bui_scr, hst_scr, sr_scr, si_scr,
                  *, nb, tc, batch_major_input):
    del h_all_ref
    rows = nb * tc
    tail = (CONV_WIDTH - 1) * nb
    c = pl.program_id(0)

    @pl.when(c == 0)
    def _():
        pad_scr[0:tail, :] = conv0_ref[...]
        hst_scr[...] = h0_ref[...]
        sr_scr[...] = s0r_ref[...]
        si_scr[...] = s0i_ref[...]

    if batch_major_input:
        x = jnp.transpose(x_ref[...], (1, 0, 2)).reshape(rows, D_MODEL)
    else:
        x = x_ref[...]
    proj = jnp.dot(x.astype(bf16), w_in_ref[...], preferred_element_type=f32)
    x_rg = proj[:, :D_RG]
    g_rg = proj[:, D_RG:2 * D_RG]
    u = proj[:, 2 * D_RG:]

    pad_scr[tail:tail + rows, :] = x_rg
    conv_w = conv_w_ref[...]
    acc = conv_w[0:1, :] * pad_scr[0:rows, :]
    for k in range(1, CONV_WIDTH):
        acc = acc + conv_w[k:k + 1, :] * pad_scr[k * nb:k * nb + rows, :]
    xc = conv_b_ref[...] + acc
    new_tail = pad_scr[rows:rows + tail, :]
    pad_scr[0:tail, :] = new_tail

    xcb = xc.astype(bf16)
    ga = []
    gi = []
    for hh in range(D_RG // MXU_DIM):
        xs = xcb[:, hh * MXU_DIM:(hh + 1) * MXU_DIM]
        ga.append(jnp.dot(xs, wa_ref[hh], preferred_element_type=f32))
        gi.append(jnp.dot(xs, wi_ref[hh], preferred_element_type=f32))
    r = jax.nn.sigmoid(jnp.concatenate(ga, axis=1) + ba_ref[...])
    i = jax.nn.sigmoid(jnp.concatenate(gi, axis=1) + bi_ref[...])
    nlam = -lam_ref[...]
    softplus = jnp.maximum(nlam, 0.0) + jnp.log1p(jnp.exp(-jnp.abs(nlam)))
    log_a = (-RG_C) * r * softplus
    a_scr[...] = jnp.exp(log_a)
    th = jnp.tanh(log_a)
    b_scr[...] = jnp.sqrt((-2.0 * th) / (1.0 - th)) * (i * xc)

    for rg in range(nb // SUBLANES):
        r0 = rg * SUBLANES
        h = hst_scr[r0:r0 + SUBLANES, :]
        for t in range(tc):
            q = t * nb + r0
            h = a_scr[q:q + SUBLANES, :] * h + b_scr[q:q + SUBLANES, :]
            b_scr[q:q + SUBLANES, :] = h
        hst_scr[r0:r0 + SUBLANES, :] = h
    y_rg = b_scr[...] * _gelu_tanh(g_rg)

    ub = u.astype(bf16)
    half = S5_N // 2
    for k in range(2):
        us = ub[:, k * MXU_DIM:(k + 1) * MXU_DIM]
        bur_scr[:, k * half:(k + 1) * half] = jnp.dot(us, bbre_ref[k], preferred_element_type=f32)
        bui_scr[:, k * half:(k + 1) * half] = jnp.dot(us, bbim_ref[k], preferred_element_type=f32)
    for rg in range(nb // SUBLANES):
        r0 = rg * SUBLANES
        for cb in range(S5_N // S5_SCAN_COLS):
            c0 = cb * S5_SCAN_COLS
            ar = jnp.broadcast_to(are_ref[:, c0:c0 + S5_SCAN_COLS], (SUBLANES, S5_SCAN_COLS))
            ai = jnp.broadcast_to(aim_ref[:, c0:c0 + S5_SCAN_COLS], (SUBLANES, S5_SCAN_COLS))
            xr = sr_scr[r0:r0 + SUBLANES, c0:c0 + S5_SCAN_COLS]
            xi = si_scr[r0:r0 + SUBLANES, c0:c0 + S5_SCAN_COLS]
            for t in range(tc):
                q = t * nb + r0
                br = bur_scr[q:q + SUBLANES, c0:c0 + S5_SCAN_COLS]
                bi_ = bui_scr[q:q + SUBLANES, c0:c0 + S5_SCAN_COLS]
                nxr = ar * xr - ai * xi + br
                nxi = ar * xi + ai * xr + bi_
                bur_scr[q:q + SUBLANES, c0:c0 + S5_SCAN_COLS] = nxr
                bui_scr[q:q + SUBLANES, c0:c0 + S5_SCAN_COLS] = nxi
                xr, xi = nxr, nxi
            sr_scr[r0:r0 + SUBLANES, c0:c0 + S5_SCAN_COLS] = xr
            si_scr[r0:r0 + SUBLANES, c0:c0 + S5_SCAN_COLS] = xi
    ys = []
    for j in range(D_S5 // MXU_DIM):
        xrb = bur_scr[:, j * half:(j + 1) * half].astype(bf16)
        xib = bui_scr[:, j * half:(j + 1) * half].astype(bf16)
        ys.append(jnp.dot(xrb, cre_ref[j], preferred_element_type=f32)
                  - jnp.dot(xib, cim_ref[j], preferred_element_type=f32))
    y_s5 = jnp.concatenate(ys, axis=1) + d_ref[...] * u
    yg = _gelu_tanh(y_s5)
    glu = jnp.dot(yg.astype(bf16), wglu_ref[...], preferred_element_type=f32) + bglu_ref[...]
    y_s5 = yg * jax.nn.sigmoid(glu)

    ycat = jnp.concatenate([y_rg, y_s5], axis=1).astype(bf16)
    mix = jnp.dot(ycat, wout_ref[...], preferred_element_type=f32)
    hout_ref[...] = _layer_norm(DN_ALPHA * x + mix, ln_g_ref[...], ln_b_ref[...])

    @pl.when(c == pl.num_programs(0) - 1)
    def _():
        conv_out_ref[...] = pad_scr[0:tail, :]
        hlast_ref[...] = hst_scr[...]
        sre_out_ref[...] = sr_scr[...]
        sim_out_ref[...] = si_scr[...]


def _full(shape):
    n = len(shape)
    return pl.BlockSpec(shape, lambda c: (0,) * n)


def _mixer(x, conv0, h0, s0r, s0i, params, *, nb, tc, name, h_all, total_rows, first_row):
    batch_major_input = x.ndim == 3
    rows = nb * tc
    if batch_major_input:
        n_chunks = x.shape[1] // tc
        x_spec = pl.BlockSpec((nb, tc, D_MODEL), lambda c: (0, c, 0))
    else:
        n_chunks = x.shape[0] // rows
        x_spec = pl.BlockSpec((rows, D_MODEL), lambda c: (c, 0))
    first_block = first_row // rows
    aliased = h_all.shape == (total_rows, D_MODEL)
    tail = (CONV_WIDTH - 1) * nb
    small = (conv0, h0, s0r, s0i) + tuple(params)
    in_specs = [pl.BlockSpec(memory_space=pl.ANY), x_spec]
    in_specs += [_full(a.shape) for a in small]
    out_shape = (jax.ShapeDtypeStruct((total_rows, D_MODEL), f32),
                 jax.ShapeDtypeStruct((tail, D_RG), f32),
                 jax.ShapeDtypeStruct((nb, D_RG), f32),
                 jax.ShapeDtypeStruct((nb, S5_N), f32),
                 jax.ShapeDtypeStruct((nb, S5_N), f32))
    out_specs = (pl.BlockSpec((rows, D_MODEL), lambda c: (c + first_block, 0)),
                 _full((tail, D_RG)), _full((nb, D_RG)), _full((nb, S5_N)), _full((nb, S5_N)))
    scratch = [pltpu.VMEM((rows + tail, D_RG), f32),
               pltpu.VMEM((rows, D_RG), f32),
               pltpu.VMEM((rows, D_RG), f32),
               pltpu.VMEM((rows, S5_N), f32),
               pltpu.VMEM((rows, S5_N), f32),
               pltpu.VMEM((nb, D_RG), f32),
               pltpu.VMEM((nb, S5_N), f32),
               pltpu.VMEM((nb, S5_N), f32)]
    return pl.pallas_call(
        functools.partial(_mixer_kernel, nb=nb, tc=tc, batch_major_input=batch_major_input),
        grid=(n_chunks,),
        in_specs=in_specs,
        out_specs=out_specs,
        out_shape=out_shape,
        scratch_shapes=scratch,
        input_output_aliases={0: 0} if aliased else {},
        compiler_params=pltpu.CompilerParams(
            dimension_semantics=("arbitrary",), vmem_limit_bytes=VMEM_LIMIT_BYTES),
        name=name,
    )(h_all, x, *small)


TOK_TILE = 256
RUN_ALIGN = 16
SLOTS = 3072
SLOT_CHUNK = 512
MAX_CHUNKS = SLOTS // RUN_ALIGN
ROW_TILE = 256
X_BUFFERS = 4
EXT = D_MODEL + LANES
ROUTER_TILES = 6
NO_RUN = 1.0e9


def _top_k_gates(scores, rb):
    rows = scores.shape[0]
    lane_f = jax.lax.broadcasted_iota(jnp.int32, (rows, LANES), 1).astype(f32)
    biased = jnp.where(lane_f < float(N_EXPERTS), scores + rb, -jnp.inf)
    sel = jnp.zeros((rows, LANES), f32)
    mask = jnp.zeros((rows, LANES), f32)
    for _ in range(TOP_K):
        m = jnp.max(biased, axis=1, keepdims=True)
        idx = jnp.min(jnp.where(biased == m, lane_f, float(LANES)), axis=1, keepdims=True)
        hit = lane_f == idx
        sel = jnp.where(hit, scores, sel)
        mask = jnp.where(hit, 1.0, mask)
        biased = jnp.where(hit, -jnp.inf, biased)
    gates = sel / jnp.sum(sel, axis=1, keepdims=True) * ROUTED_SCALE
    return mask, gates


def _router_kernel(h_ref, wr_ref, rb_ref, hext_ref, rankm_ref, cnt_ref):
    hb = h_ref[...].astype(bf16)
    scores = jax.nn.sigmoid(jnp.dot(hb, wr_ref[...], preferred_element_type=f32))
    mask, gates = _top_k_gates(scores, rb_ref[...])
    t_row = jax.lax.broadcasted_iota(jnp.int32, (TOK_TILE, TOK_TILE), 0)
    t_col = jax.lax.broadcasted_iota(jnp.int32, (TOK_TILE, TOK_TILE), 1)
    earlier = jnp.where(t_col < t_row, 1.0, 0.0).astype(bf16)
    for sub in range(ROUTER_TILES):
        m = mask[sub * TOK_TILE:(sub + 1) * TOK_TILE]
        rank = jnp.dot(earlier, m.astype(bf16), preferred_element_type=f32)
        rankm_ref[sub * TOK_TILE:(sub + 1) * TOK_TILE, :] = jnp.where(m > 0.0, rank, -1.0).astype(bf16)
        cnt_ref[sub] = jnp.broadcast_to(jnp.sum(m, axis=0, keepdims=True), (SUBLANES, LANES))
    g_hi = gates.astype(bf16).astype(f32)
    g_pack = g_hi + pltpu.roll(gates - g_hi, N_EXPERTS, 1)
    hext_ref[:, :D_MODEL] = hb
    hext_ref[:, D_MODEL:] = g_pack.astype(bf16)


def _router(h, wr, rb):
    n_tiles = h.shape[0] // TOK_TILE
    assert n_tiles % ROUTER_TILES == 0
    rows = ROUTER_TILES * TOK_TILE
    const = lambda shape: pl.BlockSpec(shape, lambda i: (0,) * len(shape))
    return pl.pallas_call(
        _router_kernel,
        grid=(n_tiles // ROUTER_TILES,),
        in_specs=[pl.BlockSpec((rows, D_MODEL), lambda i: (i, 0)), const(wr.shape), const(rb.shape)],
        out_specs=(pl.BlockSpec((rows, EXT), lambda i: (i, 0)),
                   pl.BlockSpec((rows, LANES), lambda i: (i, 0)),
                   pl.BlockSpec((ROUTER_TILES, SUBLANES, LANES), lambda i: (i, 0, 0))),
        out_shape=(jax.ShapeDtypeStruct((h.shape[0], EXT), bf16),
                   jax.ShapeDtypeStruct((h.shape[0], LANES), bf16),
                   jax.ShapeDtypeStruct((n_tiles, SUBLANES, LANES), f32)),
        compiler_params=pltpu.CompilerParams(
            dimension_semantics=("arbitrary",), vmem_limit_bytes=VMEM_LIMIT_BYTES),
        name="moe_router",
    )(h, wr, rb)


def _run_copy(src, dst, sem):
    return pltpu.make_async_copy(src, dst, sem)


def _dispatch_kernel(tot_ref, dstk_ref, zst_ref, zch_ref,
                     hext_ref, rankm_ref, locs_ref, sorted_ref,
                     stage, zbuf, sem, zsem):
    i = pl.program_id(0)
    n = pl.num_programs(0)
    slot = i % 2
    loc_row = locs_ref[0, 0:1, :]
    end_row = locs_ref[0, 1:2, :]
    for ch in range(SLOTS // SLOT_CHUNK):
        def sort_chunk(ch=ch):
            s = (jax.lax.broadcasted_iota(jnp.int32, (SLOT_CHUNK, LANES), 0)
                 + ch * SLOT_CHUNK).astype(f32)
            in_run = jnp.where(s >= loc_row, jnp.where(s < end_row, 1.0, 0.0), 0.0)
            r_col = s[:, 0:1] - jnp.sum(in_run * loc_row, axis=1, keepdims=True)
            q = jax.lax.dot_general(in_run.astype(bf16), rankm_ref[...], (((1,), (1,)), ((), ())),
                                    preferred_element_type=f32)
            p = jnp.where(q == r_col, 1.0, 0.0).astype(bf16)
            rows = jnp.dot(p, hext_ref[...], preferred_element_type=f32)
            stage[slot, ch * SLOT_CHUNK:(ch + 1) * SLOT_CHUNK, :] = rows.astype(bf16)

        def blank_chunk(ch=ch):
            stage[slot, ch * SLOT_CHUNK:(ch + 1) * SLOT_CHUNK, :] = jnp.zeros((SLOT_CHUNK, EXT), bf16)

        if (ch + 1) * SLOT_CHUNK <= TOK_TILE * TOP_K:
            sort_chunk()
        else:
            pl.when(tot_ref[i] > ch * SLOT_CHUNK)(sort_chunk)
            pl.when(tot_ref[i] <= ch * SLOT_CHUNK)(blank_chunk)

    def start_run(k, carry):
        src = stage.at[slot, pl.ds(pl.multiple_of(k * RUN_ALIGN, RUN_ALIGN), RUN_ALIGN)]
        dst = sorted_ref.at[pl.ds(pl.multiple_of(dstk_ref[i, k], RUN_ALIGN), RUN_ALIGN)]
        _run_copy(src, dst, sem.at[slot]).start()
        return carry
    jax.lax.fori_loop(0, MAX_CHUNKS, start_run, 0, unroll=8)

    @pl.when(i == 0)
    def _():
        zbuf[...] = jnp.zeros(zbuf.shape, bf16)

    @pl.when(i < N_EXPERTS)
    def _():
        def zero_copy(k):
            dst = sorted_ref.at[pl.ds(pl.multiple_of(zst_ref[i] + k * RUN_ALIGN, RUN_ALIGN), RUN_ALIGN)]
            return _run_copy(zbuf, dst, zsem.at[0])

        def start_zero(k, carry):
            zero_copy(k).start()
            return carry

        def wait_zero(k, carry):
            zero_copy(k).wait()
            return carry
        jax.lax.fori_loop(0, zch_ref[i], start_zero, 0)
        jax.lax.fori_loop(0, zch_ref[i], wait_zero, 0)

    def wait_runs(slot_):
        for _ in range(MAX_CHUNKS):
            src = stage.at[slot_, pl.ds(0, RUN_ALIGN)]
            dst = sorted_ref.at[pl.ds(0, RUN_ALIGN)]
            _run_copy(src, dst, sem.at[slot_]).wait()

    @pl.when(i > 0)
    def _():
        wait_runs(1 - slot)

    @pl.when(i == n - 1)
    def _():
        wait_runs(slot)


def _dispatch(hext, rankm, locs, tot, dstk, zst, zch, *, total_rows):
    n_tiles = hext.shape[0] // TOK_TILE
    grid_spec = pltpu.PrefetchScalarGridSpec(
        num_scalar_prefetch=4,
        grid=(n_tiles,),
        in_specs=[pl.BlockSpec((TOK_TILE, EXT), lambda i, *_: (i, 0)),
                  pl.BlockSpec((TOK_TILE, LANES), lambda i, *_: (i, 0)),
                  pl.BlockSpec((1, SUBLANES, LANES), lambda i, *_: (i, 0, 0))],
        out_specs=pl.BlockSpec(memory_space=pl.ANY),
        scratch_shapes=[pltpu.VMEM((2, SLOTS, EXT), bf16),
                        pltpu.VMEM((RUN_ALIGN, EXT), bf16),
                        pltpu.SemaphoreType.DMA((2,)),
                        pltpu.SemaphoreType.DMA((1,))])
    return pl.pallas_call(
        _dispatch_kernel,
        grid_spec=grid_spec,
        out_shape=jax.ShapeDtypeStruct((total_rows, EXT), bf16),
        compiler_params=pltpu.CompilerParams(
            dimension_semantics=("arbitrary",), vmem_limit_bytes=VMEM_LIMIT_BYTES),
        name="moe_dispatch",
    )(tot, dstk, zst, zch, hext, rankm, locs)


def _experts_kernel(ts_ref, xs_ref, wgu_ref, wd_ref, ys_ref,
                    xbuf, ybuf, wgu_b, wd_b, xsem, ysem):
    e = pl.program_id(0)
    n_valid = ts_ref[N_EXPERTS]
    wgu_b[...] = wgu_ref[0].astype(bf16)
    wd_b[...] = wd_ref[0].astype(bf16)

    def x_copy(j, slot):
        rows = pl.ds(pl.multiple_of(j * ROW_TILE, ROW_TILE), ROW_TILE)
        return pltpu.make_async_copy(xs_ref.at[rows], xbuf.at[slot], xsem.at[slot])

    def y_copy(j, slot):
        rows = pl.ds(pl.multiple_of(j * ROW_TILE, ROW_TILE), ROW_TILE)
        return pltpu.make_async_copy(ybuf.at[slot], ys_ref.at[rows], ysem.at[slot])

    ahead = X_BUFFERS - 1

    @pl.when(e == 0)
    def _():
        for a in range(ahead):
            @pl.when(a < n_valid)
            def _():
                x_copy(a, a).start()

    def tile(j, carry):
        xslot = j % X_BUFFERS
        slot = j % 2
        x_copy(j, xslot).wait()

        @pl.when(j + ahead < n_valid)
        def _():
            x_copy(j + ahead, (j + ahead) % X_BUFFERS).start()

        @pl.when(j >= 2)
        def _():
            y_copy(j - 2, slot).wait()

        x = xbuf[xslot, :, :D_MODEL]
        gu = jnp.dot(x, wgu_b[...], preferred_element_type=f32)
        act = (jax.nn.silu(gu[:, :D_EXPERT]) * gu[:, D_EXPERT:]).astype(bf16)
        y = jnp.dot(act, wd_b[...], preferred_element_type=f32)
        g_pack = xbuf[xslot, :, D_MODEL:].astype(f32)
        lane = jax.lax.broadcasted_iota(jnp.int32, (ROW_TILE, LANES), 1)
        mine = jnp.where(lane == e, g_pack, jnp.where(lane == e + N_EXPERTS, g_pack, 0.0))
        gate = jnp.sum(mine, axis=1, keepdims=True)
        ybuf[slot] = (y * gate).astype(bf16)
        y_copy(j, slot).start()
        return carry
    jax.lax.fori_loop(ts_ref[e], ts_ref[e + 1], tile, 0)

    @pl.when(e == pl.num_programs(0) - 1)
    def _():
        @pl.when(n_valid >= 2)
        def _():
            y_copy(n_valid - 2, n_valid % 2).wait()
        y_copy(n_valid - 1, (n_valid - 1) % 2).wait()


def _experts(xs, wgu, wd, ts, *, n_rows):
    grid_spec = pltpu.PrefetchScalarGridSpec(
        num_scalar_prefetch=1,
        grid=(N_EXPERTS,),
        in_specs=[pl.BlockSpec(memory_space=pl.ANY),
                  pl.BlockSpec((1, D_MODEL, 2 * D_EXPERT), lambda e, ts: (e, 0, 0)),
                  pl.BlockSpec((1, D_EXPERT, D_MODEL), lambda e, ts: (e, 0, 0))],
        out_specs=pl.BlockSpec(memory_space=pl.ANY),
        scratch_shapes=[pltpu.VMEM((X_BUFFERS, ROW_TILE, EXT), bf16),
                        pltpu.VMEM((2, ROW_TILE, D_MODEL), bf16),
                        pltpu.VMEM((D_MODEL, 2 * D_EXPERT), bf16),
                        pltpu.VMEM((D_EXPERT, D_MODEL), bf16),
                        pltpu.SemaphoreType.DMA((X_BUFFERS,)),
                        pltpu.SemaphoreType.DMA((2,))])
    return pl.pallas_call(
        _experts_kernel,
        grid_spec=grid_spec,
        out_shape=jax.ShapeDtypeStruct((n_rows, D_MODEL), bf16),
        compiler_params=pltpu.CompilerParams(
            dimension_semantics=("arbitrary",), vmem_limit_bytes=VMEM_LIMIT_BYTES),
        name="moe_experts",
    )(ts, xs, wgu, wd)


def _combine_kernel(tot_ref, srck_ref,
                    h_ref, rankm_ref, locc_ref, ys_ref, wsu_ref, wsd_ref, ln_g_ref, ln_b_ref,
                    out_bm_ref, out_tm_ref, yloc, acc_scr, p_scr, sem, *, n_bm_tiles):
    i = pl.program_id(0)
    n = pl.num_programs(0)
    slot = i % 2

    def fetch(tile, slot_):
        def body(k, carry):
            src = ys_ref.at[pl.ds(pl.multiple_of(srck_ref[tile, k], RUN_ALIGN), RUN_ALIGN)]
            dst = yloc.at[slot_, pl.ds(pl.multiple_of(k * RUN_ALIGN, RUN_ALIGN), RUN_ALIGN)]
            _run_copy(src, dst, sem.at[slot_]).start()
            return carry
        jax.lax.fori_loop(0, MAX_CHUNKS, body, 0, unroll=8)

    def drain(slot_):
        for _ in range(MAX_CHUNKS):
            _run_copy(ys_ref.at[pl.ds(0, RUN_ALIGN)], yloc.at[slot_, pl.ds(0, RUN_ALIGN)],
                      sem.at[slot_]).wait()

    @pl.when(i == 0)
    def _():
        fetch(0, 0)

    @pl.when(i + 1 < n)
    def _():
        fetch(i + 1, 1 - slot)

    h = h_ref[...]
    hb = h.astype(bf16)
    su = jnp.dot(hb, wsu_ref[...], preferred_element_type=f32)
    act = (jax.nn.silu(su[:, :D_SHARED]) * su[:, D_SHARED:]).astype(bf16)
    acc_scr[...] = jnp.dot(act, wsd_ref[...], preferred_element_type=f32)

    drain(slot)

    loc_col = locc_ref[0, :, 0:1]
    end_col = locc_ref[0, :, 1:2]
    for ch in range(SLOTS // SLOT_CHUNK):
        s = (jax.lax.broadcasted_iota(jnp.int32, (LANES, SLOT_CHUNK), 1)
             + ch * SLOT_CHUNK).astype(f32)
        in_run = jnp.where(s >= loc_col, jnp.where(s < end_col, 1.0, 0.0), 0.0)
        r_row = s[0:1, :] - jnp.sum(in_run * loc_col, axis=0, keepdims=True)
        q = jnp.dot(rankm_ref[...], in_run.astype(bf16), preferred_element_type=f32)
        p_scr[:, ch * SLOT_CHUNK:(ch + 1) * SLOT_CHUNK] = jnp.where(q == r_row, 1.0, 0.0).astype(bf16)

    for used in range(TOK_TILE * TOP_K + SLOT_CHUNK, SLOTS + 1, SLOT_CHUNK):
        def gather(used=used):
            acc_scr[...] += jnp.dot(p_scr[:, :used], yloc[slot, :used, :], preferred_element_type=f32)
        lo = used - SLOT_CHUNK if used > TOK_TILE * TOP_K + SLOT_CHUNK else 0
        if used == SLOTS:
            pl.when(tot_ref[i] > lo)(gather)
        else:
            pl.when(jnp.logical_and(tot_ref[i] > lo, tot_ref[i] <= used))(gather)
    y = _layer_norm(DN_ALPHA * h + acc_scr[...], ln_g_ref[...], ln_b_ref[...])

    @pl.when(i < n_bm_tiles)
    def _():
        nb, tc, _ = out_bm_ref.shape
        out_bm_ref[...] = jnp.transpose(y.reshape(tc, nb, D_MODEL), (1, 0, 2))

    @pl.when(i >= n_bm_tiles)
    def _():
        out_tm_ref[...] = y


def _combine(h, rankm, locc, ys, wsu, wsd, ln_g, ln_b, tot, srck, *, bm_shape):
    n_tiles = h.shape[0] // TOK_TILE
    nb, length, _ = bm_shape
    tc = TOK_TILE // nb
    n_bm_tiles = length // tc
    n_tm_tiles = n_tiles - n_bm_tiles
    const = lambda shape: pl.BlockSpec(shape, lambda i, *_: (0,) * len(shape))
    grid_spec = pltpu.PrefetchScalarGridSpec(
        num_scalar_prefetch=2,
        grid=(n_tiles,),
        in_specs=[pl.BlockSpec((TOK_TILE, D_MODEL), lambda i, *_: (i, 0)),
                  pl.BlockSpec((TOK_TILE, LANES), lambda i, *_: (i, 0)),
                  pl.BlockSpec((1, LANES, 2), lambda i, *_: (i, 0, 0)),
                  pl.BlockSpec(memory_space=pl.ANY),
                  const(wsu.shape), const(wsd.shape), const(ln_g.shape), const(ln_b.shape)],
        out_specs=(pl.BlockSpec((nb, tc, D_MODEL),
                                lambda i, *_: (0, jnp.minimum(i, n_bm_tiles - 1), 0)),
                   pl.BlockSpec((TOK_TILE, D_MODEL),
                                lambda i, *_: (jnp.maximum(i - n_bm_tiles, 0), 0))),
        scratch_shapes=[pltpu.VMEM((2, SLOTS, D_MODEL), bf16),
                        pltpu.VMEM((TOK_TILE, D_MODEL), f32),
                        pltpu.VMEM((TOK_TILE, SLOTS), bf16),
                        pltpu.SemaphoreType.DMA((2,))])
    return pl.pallas_call(
        functools.partial(_combine_kernel, n_bm_tiles=n_bm_tiles),
        grid_spec=grid_spec,
        out_shape=(jax.ShapeDtypeStruct(bm_shape, f32),
                   jax.ShapeDtypeStruct((n_tm_tiles * TOK_TILE, D_MODEL), f32)),
        compiler_params=pltpu.CompilerParams(
            dimension_semantics=("arbitrary",), vmem_limit_bytes=VMEM_LIMIT_BYTES),
        name="moe_combine",
    )(tot, srck, h, rankm, locc, ys, wsu, wsd, ln_g, ln_b)


def _round_up(x, m):
    return (x + m - 1) // m * m


def _moe(h, wr, rb, wgu, wd, wsu, wsd, ln_g, ln_b, *, bm_shape):
    n_tok = h.shape[0]
    n_tiles = n_tok // TOK_TILE
    max_rows = _round_up(n_tok * TOP_K + n_tiles * N_EXPERTS * (RUN_ALIGN - 1)
                         + N_EXPERTS * (ROW_TILE - 1), ROW_TILE)

    hext, rankm, cnt = _router(h, wr, rb)

    i32 = jnp.int32
    cnt = cnt[:, 0, :N_EXPERTS].astype(i32)
    plen = _round_up(cnt, RUN_ALIGN)
    over_tiles = jnp.cumsum(plen, axis=0)
    region_rows = over_tiles[-1]
    region_size = _round_up(region_rows, ROW_TILE)
    region_start = jnp.cumsum(region_size) - region_size
    run_dst = region_start[None, :] + over_tiles - plen
    over_experts = jnp.cumsum(plen, axis=1)
    loc = over_experts - plen
    tot = over_experts[:, -1]
    ch_end = (over_experts // RUN_ALIGN)[:, None, :]
    ch_beg = (loc // RUN_ALIGN)[:, None, :]
    k = jnp.arange(MAX_CHUNKS, dtype=i32)
    kk = k[None, :, None]
    mine = jnp.logical_and(ch_beg <= kk, kk < ch_end)
    chunk_dst = jnp.sum(jnp.where(mine, run_dst[:, None, :] + RUN_ALIGN * (kk - ch_beg), 0), axis=-1)
    live = k[None, :] * RUN_ALIGN < tot[:, None]
    tile_ids = jnp.arange(n_tiles, dtype=i32)[:, None]
    spare = max_rows + (tile_ids % 2) * SLOTS + k[None, :] * RUN_ALIGN
    dstk = jnp.where(live, chunk_dst, spare).astype(i32)
    srck = jnp.where(live, chunk_dst, 0).astype(i32)
    zst = (region_start + region_rows).astype(i32)
    zch = ((region_size - region_rows) // RUN_ALIGN).astype(i32)
    tile_start = jnp.concatenate([region_start, region_start[-1:] + region_size[-1:]]) // ROW_TILE
    tile_start = tile_start.astype(i32)

    pad = ((0, 0), (0, LANES - N_EXPERTS))
    loc_f = jnp.pad(loc.astype(f32), pad, constant_values=NO_RUN)
    end_f = jnp.pad(over_experts.astype(f32), pad, constant_values=NO_RUN)
    locs = jnp.concatenate([loc_f[:, None, :], end_f[:, None, :],
                            jnp.zeros((n_tiles, SUBLANES - 2, LANES), f32)], axis=1)
    locc = jnp.stack([loc_f, end_f], axis=-1)

    tot = tot.astype(i32)
    xs = _dispatch(hext, rankm, locs, tot, dstk, zst, zch, total_rows=max_rows + 2 * SLOTS)
    ys = _experts(xs, wgu, wd, tile_start, n_rows=max_rows)
    return _combine(h, rankm, locc, ys, wsu, wsd, ln_g, ln_b, tot, srck, bm_shape=bm_shape)


def _head_block_diag(w):
    heads_per_tile = MXU_DIM // RG_HEAD_DIM
    w4 = w.reshape(D_RG // MXU_DIM, heads_per_tile, RG_HEAD_DIM, RG_HEAD_DIM)
    eye = jnp.eye(heads_per_tile, dtype=w.dtype)
    return jnp.einsum('thij,hk->thikj', w4, eye).reshape(D_RG // MXU_DIM, MXU_DIM, MXU_DIM)


def _s5_in_tiles(b):
    gpt = S5_GROUPS // 2
    b4 = b.reshape(2, gpt, S5_STATE, S5_GROUP)
    eye = jnp.eye(gpt, dtype=b.dtype)
    return jnp.einsum('kgph,gm->kghmp', b4, eye).reshape(2, gpt * S5_GROUP, gpt * S5_STATE)


def _s5_out_tiles(cw):
    gpt = S5_GROUPS // 2
    c4 = cw.reshape(2, gpt, S5_GROUP, S5_STATE)
    eye = jnp.eye(gpt, dtype=cw.dtype)
    return jnp.einsum('kghp,gm->kgpmh', c4, eye).reshape(2, gpt * S5_STATE, gpt * S5_GROUP)


def _row(v):
    return v.reshape(1, -1)


def kernel(x_prompt, x_sample, state_rg_conv, state_rg_h, state_s5_re, state_s5_im, w_in, conv_w, conv_b, rg_w_a, rg_b_a, rg_w_i, rg_b_i, rg_lam, s5_a_re, s5_a_im, s5_log_dt, s5_b_re, s5_b_im, s5_c_re, s5_c_im, s5_d, w_glu, b_glu, w_out, ln1_g, ln1_b, w_router, router_bias, w_gate_up, w_down, w_shared_up, w_shared_down, ln2_g, ln2_b):
    l = 0
    bp, lp, _ = x_prompt.shape
    bs, ls, _ = x_sample.shape

    are, aim, bbre, bbim = _s5_prep(
        _row(s5_a_re[l]), _row(s5_a_im[l]),
        _row(jnp.repeat(s5_log_dt[l], S5_STATE)),
        _s5_in_tiles(s5_b_re[l]), _s5_in_tiles(s5_b_im[l]))
    params = (w_in[l].astype(bf16), conv_w[l], _row(conv_b[l]),
              _head_block_diag(rg_w_a[l]).astype(bf16), _row(rg_b_a[l]),
              _head_block_diag(rg_w_i[l]).astype(bf16), _row(rg_b_i[l]), _row(rg_lam[l]),
              are, aim, bbre, bbim,
              _s5_out_tiles(s5_c_re[l]).astype(bf16), _s5_out_tiles(s5_c_im[l]).astype(bf16),
              _row(s5_d[l]), w_glu[l].astype(bf16), _row(b_glu[l]), w_out[l].astype(bf16),
              _row(ln1_g[l]), _row(ln1_b[l]))

    tail = CONV_WIDTH - 1
    n_tok = lp * bp + ls * bs
    xs_tm = x_sample.transpose(1, 0, 2).reshape(ls * bs, D_MODEL)
    h_all, sc, sh, sre, sim = _mixer(
        xs_tm, state_rg_conv[l].transpose(1, 0, 2).reshape(tail * bs, D_RG), state_rg_h[l],
        state_s5_re[l].reshape(bs, S5_N), state_s5_im[l].reshape(bs, S5_N), params,
        nb=bs, tc=ls, name="mixer_sample",
        h_all=jnp.zeros((SUBLANES, LANES), f32), total_rows=n_tok, first_row=lp * bp)
    h_all, pc, ph, pre, pim = _mixer(
        x_prompt, jnp.zeros((tail * bp, D_RG), f32), jnp.zeros((bp, D_RG), f32),
        jnp.zeros((bp, S5_N), f32), jnp.zeros((bp, S5_N), f32), params,
        nb=bp, tc=TOK_TILE // bp, name="mixer_prompt",
        h_all=h_all, total_rows=n_tok, first_row=0)
    wr = jnp.pad(w_router[l], ((0, 0), (0, LANES - N_EXPERTS))).astype(bf16)
    rb = jnp.pad(_row(router_bias[l]), ((0, 0), (0, LANES - N_EXPERTS)))
    yp, ys_tm = _moe(h_all, wr, rb, w_gate_up[l], w_down[l],
                     w_shared_up[l].astype(bf16), w_shared_down[l].astype(bf16),
                     _row(ln2_g[l]), _row(ln2_b[l]), bm_shape=x_prompt.shape)
    ys = ys_tm.reshape(ls, bs, D_MODEL).transpose(1, 0, 2)

    def conv_out(cv, nbatch):
        return cv.reshape(tail, nbatch, D_RG).transpose(1, 0, 2)[None]

    return (yp, ys,
            conv_out(pc, bp), ph[None],
            pre.reshape(1, bp, S5_GROUPS, S5_STATE), pim.reshape(1, bp, S5_GROUPS, S5_STATE),
            conv_out(sc, bs), sh[None],
            sre.reshape(1, bs, S5_GROUPS, S5_STATE), sim.reshape(1, bs, S5_GROUPS, S5_STATE))
```

```python
import functools
import math

import jax
import jax.numpy as jnp
from jax.experimental import pallas as pl
from jax.experimental.pallas import tpu as pltpu

D_MODEL = 1024
D_RG = 512
RG_HEADS = 8
RG_HEAD_DIM = 64
CONV_WIDTH = 4
RG_C = 8.0
D_S5 = 512
S5_GROUP = 16
S5_GROUPS = 32
S5_STATE = 64
S5_N = S5_GROUPS * S5_STATE
N_EXPERTS = 64
TOP_K = 8
D_EXPERT = 256
D_SHARED = 256
ROUTED_SCALE = 2.5
DEPTH = 1
DN_ALPHA = (2.0 * DEPTH) ** 0.25
LN_EPS = 1e-5

SUBLANES = 8
LANES = 128
MXU_DIM = 256
S5_SCAN_COLS = 512
VMEM_LIMIT_BYTES = 56 * 1024 * 1024

bf16 = jnp.bfloat16
f32 = jnp.float32


def _gelu_tanh(x):
    c = math.sqrt(2.0 / math.pi)
    return x * (0.5 * (1.0 + jnp.tanh(c * (x + 0.044715 * (x * x * x)))))


def _layer_norm(x, g, b):
    mu = jnp.mean(x, axis=-1, keepdims=True)
    xc = x - mu
    var = jnp.mean(xc * xc, axis=-1, keepdims=True)
    return xc * jax.lax.rsqrt(var + LN_EPS) * g + b


def _s5_prep_kernel(lr_ref, li_ref, ldt_ref, bre_ref, bim_ref,
                    are_ref, aim_ref, bbre_ref, bbim_ref):
    lr = lr_ref[...]
    li = li_ref[...]
    dt = jnp.exp(ldt_ref[...])
    mag = jnp.exp(lr * dt)
    abar_re = mag * jnp.cos(li * dt)
    abar_im = mag * jnp.sin(li * dt)
    den = lr * lr + li * li
    nr = abar_re - 1.0
    ni = abar_im
    coef_re = (nr * lr + ni * li) / den
    coef_im = (ni * lr - nr * li) / den
    are_ref[...] = abar_re
    aim_ref[...] = abar_im
    half = S5_N // 2
    for k in range(2):
        cre = coef_re[:, k * half:(k + 1) * half]
        cim = coef_im[:, k * half:(k + 1) * half]
        br = bre_ref[k]
        bi = bim_ref[k]
        bbre_ref[k] = (cre * br - cim * bi).astype(bf16)
        bbim_ref[k] = (cre * bi + cim * br).astype(bf16)


def _s5_prep(lr, li, ldt, bre_t, bim_t):
    half = S5_N // 2
    return pl.pallas_call(
        _s5_prep_kernel,
        out_shape=(jax.ShapeDtypeStruct((1, S5_N), f32),
                   jax.ShapeDtypeStruct((1, S5_N), f32),
                   jax.ShapeDtypeStruct((2, MXU_DIM, half), bf16),
                   jax.ShapeDtypeStruct((2, MXU_DIM, half), bf16)),
        name="s5_prep",
    )(lr, li, ldt, bre_t, bim_t)


def _mixer_kernel(h_all_ref, x_ref, conv0_ref, h0_ref, s0r_ref, s0i_ref,
                  w_in_ref, conv_w_ref, conv_b_ref, wa_ref, ba_ref, wi_ref, bi_ref, lam_ref,
                  are_ref, aim_ref, bbre_ref, bbim_ref, cre_ref, cim_ref, d_ref,
                  wglu_ref, bglu_ref, wout_ref, ln_g_ref, ln_b_ref,
                  hout_ref, conv_out_ref, hlast_ref, sre_out_ref, sim_out_ref,
                  pad_scr, a_scr, b_scr, bur_scr, bui_scr, hst_scr, sr_scr, si_scr,
                  *, nb, tc, batch_major_input):
    del h_all_ref
    rows = nb * tc
    tail = (CONV_WIDTH - 1) * nb
    c = pl.program_id(0)

    @pl.when(c == 0)
    def _():
        pad_scr[0:tail, :] = conv0_ref[...]
        hst_scr[...] = h0_ref[...]
        sr_scr[...] = s0r_ref[...]
        si_scr[...] = s0i_ref[...]

    if batch_major_input:
        x = jnp.transpose(x_ref[...], (1, 0, 2)).reshape(rows, D_MODEL)
    else:
        x = x_ref[...]
    proj = jnp.dot(x.astype(bf16), w_in_ref[...], preferred_element_type=f32)
    x_rg = proj[:, :D_RG]
    g_rg = proj[:, D_RG:2 * D_RG]
    u = proj[:, 2 * D_RG:]

    pad_scr[tail:tail + rows, :] = x_rg
    conv_w = conv_w_ref[...]
    acc = conv_w[0:1, :] * pad_scr[0:rows, :]
    for k in range(1, CONV_WIDTH):
        acc = acc + conv_w[k:k + 1, :] * pad_scr[k * nb:k * nb + rows, :]
    xc = conv_b_ref[...] + acc
    new_tail = pad_scr[rows:rows + tail, :]
    pad_scr[0:tail, :] = new_tail

    xcb = xc.astype(bf16)
    ga = []
    gi = []
    for hh in range(D_RG // MXU_DIM):
        xs = xcb[:, hh * MXU_DIM:(hh + 1) * MXU_DIM]
        ga.append(jnp.dot(xs, wa_ref[hh], preferred_element_type=f32))
        gi.append(jnp.dot(xs, wi_ref[hh], preferred_element_type=f32))
    r = jax.nn.sigmoid(jnp.concatenate(ga, axis=1) + ba_ref[...])
    i = jax.nn.sigmoid(jnp.concatenate(gi, axis=1) + bi_ref[...])
    nlam = -lam_ref[...]
    softplus = jnp.maximum(nlam, 0.0) + jnp.log1p(jnp.exp(-jnp.abs(nlam)))
    log_a = (-RG_C) * r * softplus
    a_scr[...] = jnp.exp(log_a)
    th = jnp.tanh(log_a)
    b_scr[...] = jnp.sqrt((-2.0 * th) / (1.0 - th)) * (i * xc)

    for rg in range(nb // SUBLANES):
        r0 = rg * SUBLANES
        h = hst_scr[r0:r0 + SUBLANES, :]
        for t in range(tc):
            q = t * nb + r0
            h = a_scr[q:q + SUBLANES, :] * h + b_scr[q:q + SUBLANES, :]
            b_scr[q:q + SUBLANES, :] = h
        hst_scr[r0:r0 + SUBLANES, :] = h
    y_rg = b_scr[...] * _gelu_tanh(g_rg)

    ub = u.astype(bf16)
    half = S5_N // 2
    for k in range(2):
        us = ub[:, k * MXU_DIM:(k + 1) * MXU_DIM]
        bur_scr[:, k * half:(k + 1) * half] = jnp.dot(us, bbre_ref[k], preferred_element_type=f32)
        bui_scr[:, k * half:(k + 1) * half] = jnp.dot(us, bbim_ref[k], preferred_element_type=f32)
    for rg in range(nb // SUBLANES):
        r0 = rg * SUBLANES
        for cb in range(S5_N // S5_SCAN_COLS):
            c0 = cb * S5_SCAN_COLS
            ar = jnp.broadcast_to(are_ref[:, c0:c0 + S5_SCAN_COLS], (SUBLANES, S5_SCAN_COLS))
            ai = jnp.broadcast_to(aim_ref[:, c0:c0 + S5_SCAN_COLS], (SUBLANES, S5_SCAN_COLS))
            xr = sr_scr[r0:r0 + SUBLANES, c0:c0 + S5_SCAN_COLS]
            xi = si_scr[r0:r0 + SUBLANES, c0:c0 + S5_SCAN_COLS]
            for t in range(tc):
                q = t * nb + r0
                br = bur_scr[q:q + SUBLANES, c0:c0 + S5_SCAN_COLS]
                bi_ = bui_scr[q:q + SUBLANES, c0:c0 + S5_SCAN_COLS]
                nxr = ar * xr - ai * xi + br
                nxi = ar * xi + ai * xr + bi_
                bur_scr[q:q + SUBLANES, c0:c0 + S5_SCAN_COLS] = nxr
                bui_scr[q:q + SUBLANES, c0:c0 + S5_SCAN_COLS] = nxi
                xr, xi = nxr, nxi
            sr_scr[r0:r0 + SUBLANES, c0:c0 + S5_SCAN_COLS] = xr
            si_scr[r0:r0 + SUBLANES, c0:c0 + S5_SCAN_COLS] = xi
    ys = []
    for j in range(D_S5 // MXU_DIM):
        xrb = bur_scr[:, j * half:(j + 1) * half].astype(bf16)
        xib = bui_scr[:, j * half:(j + 1) * half].astype(bf16)
        ys.append(jnp.dot(xrb, cre_ref[j], preferred_element_type=f32)
                  - jnp.dot(xib, cim_ref[j], preferred_element_type=f32))
    y_s5 = jnp.concatenate(ys, axis=1) + d_ref[...] * u
    yg = _gelu_tanh(y_s5)
    glu = jnp.dot(yg.astype(bf16), wglu_ref[...], preferred_element_type=f32) + bglu_ref[...]
    y_s5 = yg * jax.nn.sigmoid(glu)

    ycat = jnp.concatenate([y_rg, y_s5], axis=1).astype(bf16)
    mix = jnp.dot(ycat, wout_ref[...], preferred_element_type=f32)
    hout_ref[...] = _layer_norm(DN_ALPHA * x + mix, ln_g_ref[...], ln_b_ref[...])

    @pl.when(c == pl.num_programs(0) - 1)
    def _():
        conv_out_ref[...] = pad_scr[0:tail, :]
        hlast_ref[...] = hst_scr[...]
        sre_out_ref[...] = sr_scr[...]
        sim_out_ref[...] = si_scr[...]


def _full(shape):
    n = len(shape)
    return pl.BlockSpec(shape, lambda c: (0,) * n)


def _mixer(x, conv0, h0, s0r, s0i, params, *, nb, tc, name, h_all, total_rows, first_row):
    batch_major_input = x.ndim == 3
    rows = nb * tc
    if batch_major_input:
        n_chunks = x.shape[1] // tc
        x_spec = pl.BlockSpec((nb, tc, D_MODEL), lambda c: (0, c, 0))
    else:
        n_chunks = x.shape[0] // rows
        x_spec = pl.BlockSpec((rows, D_MODEL), lambda c: (c, 0))
    first_block = first_row // rows
    aliased = h_all.shape == (total_rows, D_MODEL)
    tail = (CONV_WIDTH - 1) * nb
    small = (conv0, h0, s0r, s0i) + tuple(params)
    in_specs = [pl.BlockSpec(memory_space=pl.ANY), x_spec]
    in_specs += [_full(a.shape) for a in small]
    out_shape = (jax.ShapeDtypeStruct((total_rows, D_MODEL), f32),
                 jax.ShapeDtypeStruct((tail, D_RG), f32),
                 jax.ShapeDtypeStruct((nb, D_RG), f32),
                 jax.ShapeDtypeStruct((nb, S5_N), f32),
                 jax.ShapeDtypeStruct((nb, S5_N), f32))
    out_specs = (pl.BlockSpec((rows, D_MODEL), lambda c: (c + first_block, 0)),
                 _full((tail, D_RG)), _full((nb, D_RG)), _full((nb, S5_N)), _full((nb, S5_N)))
    scratch = [pltpu.VMEM((rows + tail, D_RG), f32),
               pltpu.VMEM((rows, D_RG), f32),
               pltpu.VMEM((rows, D_RG), f32),
               pltpu.VMEM((rows, S5_N), f32),
               pltpu.VMEM((rows, S5_N), f32),
               pltpu.VMEM((nb, D_RG), f32),
               pltpu.VMEM((nb, S5_N), f32),
               pltpu.VMEM((nb, S5_N), f32)]
    return pl.pallas_call(
        functools.partial(_mixer_kernel, nb=nb, tc=tc, batch_major_input=batch_major_input),
        grid=(n_chunks,),
        in_specs=in_specs,
        out_specs=out_specs,
        out_shape=out_shape,
        scratch_shapes=scratch,
        input_output_aliases={0: 0} if aliased else {},
        compiler_params=pltpu.CompilerParams(
            dimension_semantics=("arbitrary",), vmem_limit_bytes=VMEM_LIMIT_BYTES),
        name=name,
    )(h_all, x, *small)


TOK_TILE = 256
RUN_ALIGN = 16
SLOTS = 3072
SLOT_CHUNK = 512
MAX_CHUNKS = SLOTS // RUN_ALIGN
ROW_TILE = 512
X_BUFFERS = 4
EXT = D_MODEL + LANES
ROUTER_TILES = 6
NO_RUN = 1.0e9


def _top_k_gates(scores, rb):
    rows = scores.shape[0]
    lane_f = jax.lax.broadcasted_iota(jnp.int32, (rows, LANES), 1).astype(f32)
    biased = jnp.where(lane_f < float(N_EXPERTS), scores + rb, -jnp.inf)
    sel = jnp.zeros((rows, LANES), f32)
    mask = jnp.zeros((rows, LANES), f32)
    for _ in range(TOP_K):
        m = jnp.max(biased, axis=1, keepdims=True)
        idx = jnp.min(jnp.where(biased == m, lane_f, float(LANES)), axis=1, keepdims=True)
        hit = lane_f == idx
        sel = jnp.where(hit, scores, sel)
        mask = jnp.where(hit, 1.0, mask)
        biased = jnp.where(hit, -jnp.inf, biased)
    gates = sel / jnp.sum(sel, axis=1, keepdims=True) * ROUTED_SCALE
    return mask, gates


def _router_kernel(h_ref, wr_ref, rb_ref, hext_ref, rankm_ref, cnt_ref):
    hb = h_ref[...].astype(bf16)
    scores = jax.nn.sigmoid(jnp.dot(hb, wr_ref[...], preferred_element_type=f32))
    mask, gates = _top_k_gates(scores, rb_ref[...])
    t_row = jax.lax.broadcasted_iota(jnp.int32, (TOK_TILE, TOK_TILE), 0)
    t_col = jax.lax.broadcasted_iota(jnp.int32, (TOK_TILE, TOK_TILE), 1)
    earlier = jnp.where(t_col < t_row, 1.0, 0.0).astype(bf16)
    for sub in range(ROUTER_TILES):
        m = mask[sub * TOK_TILE:(sub + 1) * TOK_TILE]
        rank = jnp.dot(earlier, m.astype(bf16), preferred_element_type=f32)
        rankm_ref[sub * TOK_TILE:(sub + 1) * TOK_TILE, :] = jnp.where(m > 0.0, rank, -1.0).astype(bf16)
        cnt_ref[sub] = jnp.broadcast_to(jnp.sum(m, axis=0, keepdims=True), (SUBLANES, LANES))
    g_hi = gates.astype(bf16).astype(f32)
    g_pack = g_hi + pltpu.roll(gates - g_hi, N_EXPERTS, 1)
    hext_ref[:, :D_MODEL] = hb
    hext_ref[:, D_MODEL:] = g_pack.astype(bf16)


def _router(h, wr, rb):
    n_tiles = h.shape[0] // TOK_TILE
    assert n_tiles % ROUTER_TILES == 0
    rows = ROUTER_TILES * TOK_TILE
    const = lambda shape: pl.BlockSpec(shape, lambda i: (0,) * len(shape))
    return pl.pallas_call(
        _router_kernel,
        grid=(n_tiles // ROUTER_TILES,),
        in_specs=[pl.BlockSpec((rows, D_MODEL), lambda i: (i, 0)), const(wr.shape), const(rb.shape)],
        out_specs=(pl.BlockSpec((rows, EXT), lambda i: (i, 0)),
                   pl.BlockSpec((rows, LANES), lambda i: (i, 0)),
                   pl.BlockSpec((ROUTER_TILES, SUBLANES, LANES), lambda i: (i, 0, 0))),
        out_shape=(jax.ShapeDtypeStruct((h.shape[0], EXT), bf16),
                   jax.ShapeDtypeStruct((h.shape[0], LANES), bf16),
                   jax.ShapeDtypeStruct((n_tiles, SUBLANES, LANES), f32)),
        compiler_params=pltpu.CompilerParams(
            dimension_semantics=("arbitrary",), vmem_limit_bytes=VMEM_LIMIT_BYTES),
        name="moe_router",
    )(h, wr, rb)


def _run_copy(src, dst, sem):
    return pltpu.make_async_copy(src, dst, sem)


def _dispatch_kernel(tot_ref, dstk_ref, zst_ref, zch_ref,
                     hext_ref, rankm_ref, locs_ref, sorted_ref,
                     stage, zbuf, p_scr, sem, zsem):
    i = pl.program_id(0)
    n = pl.num_programs(0)
    slot = i % 2
    loc_row = locs_ref[0, 0:1, :]
    end_row = locs_ref[0, 1:2, :]
    for ch in range(SLOTS // SLOT_CHUNK):
        s = (jax.lax.broadcasted_iota(jnp.int32, (SLOT_CHUNK, LANES), 0)
             + ch * SLOT_CHUNK).astype(f32)
        in_run = jnp.where(s >= loc_row, jnp.where(s < end_row, 1.0, 0.0), 0.0)
        r_col = s[:, 0:1] - jnp.sum(in_run * loc_row, axis=1, keepdims=True)
        q = jax.lax.dot_general(in_run.astype(bf16), rankm_ref[...], (((1,), (1,)), ((), ())),
                                preferred_element_type=f32)
        p_scr[ch * SLOT_CHUNK:(ch + 1) * SLOT_CHUNK, :] = jnp.where(q == r_col, 1.0, 0.0).astype(bf16)

    for used in range(TOK_TILE * TOP_K + SLOT_CHUNK, SLOTS + 1, SLOT_CHUNK):
        def sort_rows(used=used):
            rows = jnp.dot(p_scr[:used, :], hext_ref[...], preferred_element_type=f32)
            stage[slot, :used, :] = rows.astype(bf16)
            if used < SLOTS:
                stage[slot, used:, :] = jnp.zeros((SLOTS - used, EXT), bf16)
        lo = used - SLOT_CHUNK if used > TOK_TILE * TOP_K + SLOT_CHUNK else 0
        if used == SLOTS:
            pl.when(tot_ref[i] > lo)(sort_rows)
        else:
            pl.when(jnp.logical_and(tot_ref[i] > lo, tot_ref[i] <= used))(sort_rows)

    def start_run(k, carry):
        src = stage.at[slot, pl.ds(pl.multiple_of(k * RUN_ALIGN, RUN_ALIGN), RUN_ALIGN)]
        dst = sorted_ref.at[pl.ds(pl.multiple_of(dstk_ref[i, k], RUN_ALIGN), RUN_ALIGN)]
        _run_copy(src, dst, sem.at[slot]).start()
        return carry
    jax.lax.fori_loop(0, MAX_CHUNKS, start_run, 0, unroll=8)

    @pl.when(i == 0)
    def _():
        zbuf[...] = jnp.zeros(zbuf.shape, bf16)

    @pl.when(i < N_EXPERTS)
    def _():
        def zero_copy(k):
            dst = sorted_ref.at[pl.ds(pl.multiple_of(zst_ref[i] + k * RUN_ALIGN, RUN_ALIGN), RUN_ALIGN)]
            return _run_copy(zbuf, dst, zsem.at[0])

        def start_zero(k, carry):
            zero_copy(k).start()
            return carry

        def wait_zero(k, carry):
            zero_copy(k).wait()
            return carry
        jax.lax.fori_loop(0, zch_ref[i], start_zero, 0)
        jax.lax.fori_loop(0, zch_ref[i], wait_zero, 0)

    def wait_runs(slot_):
        for _ in range(MAX_CHUNKS):
            src = stage.at[slot_, pl.ds(0, RUN_ALIGN)]
            dst = sorted_ref.at[pl.ds(0, RUN_ALIGN)]
            _run_copy(src, dst, sem.at[slot_]).wait()

    @pl.when(i > 0)
    def _():
        wait_runs(1 - slot)

    @pl.when(i == n - 1)
    def _():
        wait_runs(slot)


def _dispatch(hext, rankm, locs, tot, dstk, zst, zch, *, total_rows):
    n_tiles = hext.shape[0] // TOK_TILE
    grid_spec = pltpu.PrefetchScalarGridSpec(
        num_scalar_prefetch=4,
        grid=(n_tiles,),
        in_specs=[pl.BlockSpec((TOK_TILE, EXT), lambda i, *_: (i, 0)),
                  pl.BlockSpec((TOK_TILE, LANES), lambda i, *_: (i, 0)),
                  pl.BlockSpec((1, SUBLANES, LANES), lambda i, *_: (i, 0, 0))],
        out_specs=pl.BlockSpec(memory_space=pl.ANY),
        scratch_shapes=[pltpu.VMEM((2, SLOTS, EXT), bf16),
                        pltpu.VMEM((RUN_ALIGN, EXT), bf16),
                        pltpu.VMEM((SLOTS, TOK_TILE), bf16),
                        pltpu.SemaphoreType.DMA((2,)),
                        pltpu.SemaphoreType.DMA((1,))])
    return pl.pallas_call(
        _dispatch_kernel,
        grid_spec=grid_spec,
        out_shape=jax.ShapeDtypeStruct((total_rows, EXT), bf16),
        compiler_params=pltpu.CompilerParams(
            dimension_semantics=("arbitrary",), vmem_limit_bytes=VMEM_LIMIT_BYTES),
        name="moe_dispatch",
    )(tot, dstk, zst, zch, hext, rankm, locs)


def _experts_kernel(ts_ref, xs_ref, wgu_ref, wd_ref, ys_ref,
                    xbuf, ybuf, wgu_b, wd_b, xsem, ysem):
    e = pl.program_id(0)
    n_valid = ts_ref[N_EXPERTS]
    wgu_b[...] = wgu_ref[0].astype(bf16)
    wd_b[...] = wd_ref[0].astype(bf16)

    def x_copy(j, slot):
        rows = pl.ds(pl.multiple_of(j * ROW_TILE, ROW_TILE), ROW_TILE)
        return pltpu.make_async_copy(xs_ref.at[rows], xbuf.at[slot], xsem.at[slot])

    def y_copy(j, slot):
        rows = pl.ds(pl.multiple_of(j * ROW_TILE, ROW_TILE), ROW_TILE)
        return pltpu.make_async_copy(ybuf.at[slot], ys_ref.at[rows], ysem.at[slot])

    ahead = X_BUFFERS - 1

    @pl.when(e == 0)
    def _():
        for a in range(ahead):
            @pl.when(a < n_valid)
            def _():
                x_copy(a, a).start()

    def tile(j, carry):
        xslot = j % X_BUFFERS
        slot = j % 2
        x_copy(j, xslot).wait()

        @pl.when(j + ahead < n_valid)
        def _():
            x_copy(j + ahead, (j + ahead) % X_BUFFERS).start()

        @pl.when(j >= 2)
        def _():
            y_copy(j - 2, slot).wait()

        x = xbuf[xslot, :, :D_MODEL]
        gu = jnp.dot(x, wgu_b[...], preferred_element_type=f32)
        act = (jax.nn.silu(gu[:, :D_EXPERT]) * gu[:, D_EXPERT:]).astype(bf16)
        y = jnp.dot(act, wd_b[...], preferred_element_type=f32)
        g_pack = xbuf[xslot, :, D_MODEL:].astype(f32)
        lane = jax.lax.broadcasted_iota(jnp.int32, (ROW_TILE, LANES), 1)
        mine = jnp.where(lane == e, g_pack, jnp.where(lane == e + N_EXPERTS, g_pack, 0.0))
        gate = jnp.sum(mine, axis=1, keepdims=True)
        ybuf[slot] = (y * gate).astype(bf16)
        y_copy(j, slot).start()
        return carry
    jax.lax.fori_loop(ts_ref[e], ts_ref[e + 1], tile, 0)

    @pl.when(e == pl.num_programs(0) - 1)
    def _():
        @pl.when(n_valid >= 2)
        def _():
            y_copy(n_valid - 2, n_valid % 2).wait()
        y_copy(n_valid - 1, (n_valid - 1) % 2).wait()


def _experts(xs, wgu, wd, ts, *, n_rows):
    grid_spec = pltpu.PrefetchScalarGridSpec(
        num_scalar_prefetch=1,
        grid=(N_EXPERTS,),
        in_specs=[pl.BlockSpec(memory_space=pl.ANY),
                  pl.BlockSpec((1, D_MODEL, 2 * D_EXPERT), lambda e, ts: (e, 0, 0)),
                  pl.BlockSpec((1, D_EXPERT, D_MODEL), lambda e, ts: (e, 0, 0))],
        out_specs=pl.BlockSpec(memory_space=pl.ANY),
        scratch_shapes=[pltpu.VMEM((X_BUFFERS, ROW_TILE, EXT), bf16),
                        pltpu.VMEM((2, ROW_TILE, D_MODEL), bf16),
                        pltpu.VMEM((D_MODEL, 2 * D_EXPERT), bf16),
                        pltpu.VMEM((D_EXPERT, D_MODEL), bf16),
                        pltpu.SemaphoreType.DMA((X_BUFFERS,)),
                        pltpu.SemaphoreType.DMA((2,))])
    return pl.pallas_call(
        _experts_kernel,
        grid_spec=grid_spec,
        out_shape=jax.ShapeDtypeStruct((n_rows, D_MODEL), bf16),
        compiler_params=pltpu.CompilerParams(
            dimension_semantics=("arbitrary",), vmem_limit_bytes=VMEM_LIMIT_BYTES),
        name="moe_experts",
    )(ts, xs, wgu, wd)


def _combine_kernel(tot_ref, srck_ref,
                    h_ref, rankm_ref, locc_ref, ys_ref, wsu_ref, wsd_ref, ln_g_ref, ln_b_ref,
                    out_bm_ref, out_tm_ref, yloc, acc_scr, p_scr, sem, *, n_bm_tiles):
    i = pl.program_id(0)
    n = pl.num_programs(0)
    slot = i % 2

    def fetch(tile, slot_):
        def body(k, carry):
            src = ys_ref.at[pl.ds(pl.multiple_of(srck_ref[tile, k], RUN_ALIGN), RUN_ALIGN)]
            dst = yloc.at[slot_, pl.ds(pl.multiple_of(k * RUN_ALIGN, RUN_ALIGN), RUN_ALIGN)]
            _run_copy(src, dst, sem.at[slot_]).start()
            return carry
        jax.lax.fori_loop(0, MAX_CHUNKS, body, 0, unroll=8)

    def drain(slot_):
        for _ in range(MAX_CHUNKS):
            _run_copy(ys_ref.at[pl.ds(0, RUN_ALIGN)], yloc.at[slot_, pl.ds(0, RUN_ALIGN)],
                      sem.at[slot_]).wait()

    @pl.when(i == 0)
    def _():
        fetch(0, 0)

    @pl.when(i + 1 < n)
    def _():
        fetch(i + 1, 1 - slot)

    h = h_ref[...]
    hb = h.astype(bf16)
    su = jnp.dot(hb, wsu_ref[...], preferred_element_type=f32)
    act = (jax.nn.silu(su[:, :D_SHARED]) * su[:, D_SHARED:]).astype(bf16)
    acc_scr[...] = jnp.dot(act, wsd_ref[...], preferred_element_type=f32)

    drain(slot)

    loc_col = locc_ref[0, :, 0:1]
    end_col = locc_ref[0, :, 1:2]
    for ch in range(SLOTS // SLOT_CHUNK):
        s = (jax.lax.broadcasted_iota(jnp.int32, (LANES, SLOT_CHUNK), 1)
             + ch * SLOT_CHUNK).astype(f32)
        in_run = jnp.where(s >= loc_col, jnp.where(s < end_col, 1.0, 0.0), 0.0)
        r_row = s[0:1, :] - jnp.sum(in_run * loc_col, axis=0, keepdims=True)
        q = jnp.dot(rankm_ref[...], in_run.astype(bf16), preferred_element_type=f32)
        p_scr[:, ch * SLOT_CHUNK:(ch + 1) * SLOT_CHUNK] = jnp.where(q == r_row, 1.0, 0.0).astype(bf16)

    for used in range(TOK_TILE * TOP_K + SLOT_CHUNK, SLOTS + 1, SLOT_CHUNK):
        def gather(used=used):
            acc_scr[...] += jnp.dot(p_scr[:, :used], yloc[slot, :used, :], preferred_element_type=f32)
        lo = used - SLOT_CHUNK if used > TOK_TILE * TOP_K + SLOT_CHUNK else 0
        if used == SLOTS:
            pl.when(tot_ref[i] > lo)(gather)
        else:
            pl.when(jnp.logical_and(tot_ref[i] > lo, tot_ref[i] <= used))(gather)
    y = _layer_norm(DN_ALPHA * h + acc_scr[...], ln_g_ref[...], ln_b_ref[...])

    @pl.when(i < n_bm_tiles)
    def _():
        nb, tc, _ = out_bm_ref.shape
        out_bm_ref[...] = jnp.transpose(y.reshape(tc, nb, D_MODEL), (1, 0, 2))

    @pl.when(i >= n_bm_tiles)
    def _():
        out_tm_ref[...] = y


def _combine(h, rankm, locc, ys, wsu, wsd, ln_g, ln_b, tot, srck, *, bm_shape):
    n_tiles = h.shape[0] // TOK_TILE
    nb, length, _ = bm_shape
    tc = TOK_TILE // nb
    n_bm_tiles = length // tc
    n_tm_tiles = n_tiles - n_bm_tiles
    const = lambda shape: pl.BlockSpec(shape, lambda i, *_: (0,) * len(shape))
    grid_spec = pltpu.PrefetchScalarGridSpec(
        num_scalar_prefetch=2,
        grid=(n_tiles,),
        in_specs=[pl.BlockSpec((TOK_TILE, D_MODEL), lambda i, *_: (i, 0)),
                  pl.BlockSpec((TOK_TILE, LANES), lambda i, *_: (i, 0)),
                  pl.BlockSpec((1, LANES, 2), lambda i, *_: (i, 0, 0)),
                  pl.BlockSpec(memory_space=pl.ANY),
                  const(wsu.shape), const(wsd.shape), const(ln_g.shape), const(ln_b.shape)],
        out_specs=(pl.BlockSpec((nb, tc, D_MODEL),
                                lambda i, *_: (0, jnp.minimum(i, n_bm_tiles - 1), 0)),
                   pl.BlockSpec((TOK_TILE, D_MODEL),
                                lambda i, *_: (jnp.maximum(i - n_bm_tiles, 0), 0))),
        scratch_shapes=[pltpu.VMEM((2, SLOTS, D_MODEL), bf16),
                        pltpu.VMEM((TOK_TILE, D_MODEL), f32),
                        pltpu.VMEM((TOK_TILE, SLOTS), bf16),
                        pltpu.SemaphoreType.DMA((2,))])
    return pl.pallas_call(
        functools.partial(_combine_kernel, n_bm_tiles=n_bm_tiles),
        grid_spec=grid_spec,
        out_shape=(jax.ShapeDtypeStruct(bm_shape, f32),
                   jax.ShapeDtypeStruct((n_tm_tiles * TOK_TILE, D_MODEL), f32)),
        compiler_params=pltpu.CompilerParams(
            dimension_semantics=("arbitrary",), vmem_limit_bytes=VMEM_LIMIT_BYTES),
        name="moe_combine",
    )(tot, srck, h, rankm, locc, ys, wsu, wsd, ln_g, ln_b)


def _round_up(x, m):
    return (x + m - 1) // m * m


def _moe(h, wr, rb, wgu, wd, wsu, wsd, ln_g, ln_b, *, bm_shape):
    n_tok = h.shape[0]
    n_tiles = n_tok // TOK_TILE
    max_rows = _round_up(n_tok * TOP_K + n_tiles * N_EXPERTS * (RUN_ALIGN - 1)
                         + N_EXPERTS * (ROW_TILE - 1), ROW_TILE)

    hext, rankm, cnt = _router(h, wr, rb)

    i32 = jnp.int32
    cnt = cnt[:, 0, :N_EXPERTS].astype(i32)
    plen = _round_up(cnt, RUN_ALIGN)
    over_tiles = jnp.cumsum(plen, axis=0)
    region_rows = over_tiles[-1]
    region_size = _round_up(region_rows, ROW_TILE)
    region_start = jnp.cumsum(region_size) - region_size
    run_dst = region_start[None, :] + over_tiles - plen
    over_experts = jnp.cumsum(plen, axis=1)
    loc = over_experts - plen
    tot = over_experts[:, -1]
    ch_end = (over_experts // RUN_ALIGN)[:, None, :]
    ch_beg = (loc // RUN_ALIGN)[:, None, :]
    k = jnp.arange(MAX_CHUNKS, dtype=i32)
    kk = k[None, :, None]
    mine = jnp.logical_and(ch_beg <= kk, kk < ch_end)
    chunk_dst = jnp.sum(jnp.where(mine, run_dst[:, None, :] + RUN_ALIGN * (kk - ch_beg), 0), axis=-1)
    live = k[None, :] * RUN_ALIGN < tot[:, None]
    tile_ids = jnp.arange(n_tiles, dtype=i32)[:, None]
    spare = max_rows + (tile_ids % 2) * SLOTS + k[None, :] * RUN_ALIGN
    dstk = jnp.where(live, chunk_dst, spare).astype(i32)
    srck = jnp.where(live, chunk_dst, 0).astype(i32)
    zst = (region_start + region_rows).astype(i32)
    zch = ((region_size - region_rows) // RUN_ALIGN).astype(i32)
    tile_start = jnp.concatenate([region_start, region_start[-1:] + region_size[-1:]]) // ROW_TILE
    tile_start = tile_start.astype(i32)

    pad = ((0, 0), (0, LANES - N_EXPERTS))
    loc_f = jnp.pad(loc.astype(f32), pad, constant_values=NO_RUN)
    end_f = jnp.pad(over_experts.astype(f32), pad, constant_values=NO_RUN)
    locs = jnp.concatenate([loc_f[:, None, :], end_f[:, None, :],
                            jnp.zeros((n_tiles, SUBLANES - 2, LANES), f32)], axis=1)
    locc = jnp.stack([loc_f, end_f], axis=-1)

    tot = tot.astype(i32)
    xs = _dispatch(hext, rankm, locs, tot, dstk, zst, zch, total_rows=max_rows + 2 * SLOTS)
    ys = _experts(xs, wgu, wd, tile_start, n_rows=max_rows)
    return _combine(h, rankm, locc, ys, wsu, wsd, ln_g, ln_b, tot, srck, bm_shape=bm_shape)


def _head_block_diag(w):
    heads_per_tile = MXU_DIM // RG_HEAD_DIM
    w4 = w.reshape(D_RG // MXU_DIM, heads_per_tile, RG_HEAD_DIM, RG_HEAD_DIM)
    eye = jnp.eye(heads_per_tile, dtype=w.dtype)
    return jnp.einsum('thij,hk->thikj', w4, eye).reshape(D_RG // MXU_DIM, MXU_DIM, MXU_DIM)


def _s5_in_tiles(b):
    gpt = S5_GROUPS // 2
    b4 = b.reshape(2, gpt, S5_STATE, S5_GROUP)
    eye = jnp.eye(gpt, dtype=b.dtype)
    return jnp.einsum('kgph,gm->kghmp', b4, eye).reshape(2, gpt * S5_GROUP, gpt * S5_STATE)


def _s5_out_tiles(cw):
    gpt = S5_GROUPS // 2
    c4 = cw.reshape(2, gpt, S5_GROUP, S5_STATE)
    eye = jnp.eye(gpt, dtype=cw.dtype)
    return jnp.einsum('kghp,gm->kgpmh', c4, eye).reshape(2, gpt * S5_STATE, gpt * S5_GROUP)


def _row(v):
    return v.reshape(1, -1)


def kernel(x_prompt, x_sample, state_rg_conv, state_rg_h, state_s5_re, state_s5_im, w_in, conv_w, conv_b, rg_w_a, rg_b_a, rg_w_i, rg_b_i, rg_lam, s5_a_re, s5_a_im, s5_log_dt, s5_b_re, s5_b_im, s5_c_re, s5_c_im, s5_d, w_glu, b_glu, w_out, ln1_g, ln1_b, w_router, router_bias, w_gate_up, w_down, w_shared_up, w_shared_down, ln2_g, ln2_b):
    l = 0
    bp, lp, _ = x_prompt.shape
    bs, ls, _ = x_sample.shape

    are, aim, bbre, bbim = _s5_prep(
        _row(s5_a_re[l]), _row(s5_a_im[l]),
        _row(jnp.repeat(s5_log_dt[l], S5_STATE)),
        _s5_in_tiles(s5_b_re[l]), _s5_in_tiles(s5_b_im[l]))
    params = (w_in[l].astype(bf16), conv_w[l], _row(conv_b[l]),
              _head_block_diag(rg_w_a[l]).astype(bf16), _row(rg_b_a[l]),
              _head_block_diag(rg_w_i[l]).astype(bf16), _row(rg_b_i[l]), _row(rg_lam[l]),
              are, aim, bbre, bbim,
              _s5_out_tiles(s5_c_re[l]).astype(bf16), _s5_out_tiles(s5_c_im[l]).astype(bf16),
              _row(s5_d[l]), w_glu[l].astype(bf16), _row(b_glu[l]), w_out[l].astype(bf16),
              _row(ln1_g[l]), _row(ln1_b[l]))

    tail = CONV_WIDTH - 1
    n_tok = lp * bp + ls * bs
    xs_tm = x_sample.transpose(1, 0, 2).reshape(ls * bs, D_MODEL)
    h_all, sc, sh, sre, sim = _mixer(
        xs_tm, state_rg_conv[l].transpose(1, 0, 2).reshape(tail * bs, D_RG), state_rg_h[l],
        state_s5_re[l].reshape(bs, S5_N), state_s5_im[l].reshape(bs, S5_N), params,
        nb=bs, tc=ls, name="mixer_sample",
        h_all=jnp.zeros((SUBLANES, LANES), f32), total_rows=n_tok, first_row=lp * bp)
    h_all, pc, ph, pre, pim = _mixer(
        x_prompt, jnp.zeros((tail * bp, D_RG), f32), jnp.zeros((bp, D_RG), f32),
        jnp.zeros((bp, S5_N), f32), jnp.zeros((bp, S5_N), f32), params,
        nb=bp, tc=TOK_TILE // bp, name="mixer_prompt",
        h_all=h_all, total_rows=n_tok, first_row=0)
    wr = jnp.pad(w_router[l], ((0, 0), (0, LANES - N_EXPERTS))).astype(bf16)
    rb = jnp.pad(_row(router_bias[l]), ((0, 0), (0, LANES - N_EXPERTS)))
    yp, ys_tm = _moe(h_all, wr, rb, w_gate_up[l], w_down[l],
                     w_shared_up[l].astype(bf16), w_shared_down[l].astype(bf16),
                     _row(ln2_g[l]), _row(ln2_b[l]), bm_shape=x_prompt.shape)
    ys = ys_tm.reshape(ls, bs, D_MODEL).transpose(1, 0, 2)

    def conv_out(cv, nbatch):
        return cv.reshape(tail, nbatch, D_RG).transpose(1, 0, 2)[None]

    return (yp, ys,
            conv_out(pc, bp), ph[None],
            pre.reshape(1, bp, S5_GROUPS, S5_STATE), pim.reshape(1, bp, S5_GROUPS, S5_STATE),
            conv_out(sc, bs), sh[None],
            sre.reshape(1, bs, S5_GROUPS, S5_STATE), sim.reshape(1, bs, S5_GROUPS, S5_STATE))
```

```python
import functools
import math

import jax
import jax.numpy as jnp
from jax.experimental import pallas as pl
from jax.experimental.pallas import tpu as pltpu

D_MODEL = 1024
D_RG = 512
RG_HEADS = 8
RG_HEAD_DIM = 64
CONV_WIDTH = 4
RG_C = 8.0
D_S5 = 512
S5_GROUP = 16
S5_GROUPS = 32
S5_STATE = 64
S5_N = S5_GROUPS * S5_STATE
N_EXPERTS = 64
TOP_K = 8
D_EXPERT = 256
D_SHARED = 256
ROUTED_SCALE = 2.5
DEPTH = 1
DN_ALPHA = (2.0 * DEPTH) ** 0.25
LN_EPS = 1e-5

SUBLANES = 8
LANES = 128
MXU_DIM = 256
S5_SCAN_COLS = 512
PROMPT_CHUNK_ROWS = 512
VMEM_LIMIT_BYTES = 56 * 1024 * 1024

bf16 = jnp.bfloat16
f32 = jnp.float32


def _gelu_tanh(x):
    c = math.sqrt(2.0 / math.pi)
    return x * (0.5 * (1.0 + jnp.tanh(c * (x + 0.044715 * (x * x * x)))))


def _layer_norm(x, g, b):
    mu = jnp.mean(x, axis=-1, keepdims=True)
    xc = x - mu
    var = jnp.mean(xc * xc, axis=-1, keepdims=True)
    return xc * jax.lax.rsqrt(var + LN_EPS) * g + b


def _s5_prep_kernel(lr_ref, li_ref, ldt_ref, bre_ref, bim_ref,
                    are_ref, aim_ref, bbre_ref, bbim_ref):
    lr = lr_ref[...]
    li = li_ref[...]
    dt = jnp.exp(ldt_ref[...])
    mag = jnp.exp(lr * dt)
    abar_re = mag * jnp.cos(li * dt)
    abar_im = mag * jnp.sin(li * dt)
    den = lr * lr + li * li
    nr = abar_re - 1.0
    ni = abar_im
    coef_re = (nr * lr + ni * li) / den
    coef_im = (ni * lr - nr * li) / den
    are_ref[...] = abar_re
    aim_ref[...] = abar_im
    half = S5_N // 2
    for k in range(2):
        cre = coef_re[:, k * half:(k + 1) * half]
        cim = coef_im[:, k * half:(k + 1) * half]
        br = bre_ref[k]
        bi = bim_ref[k]
        bbre_ref[k] = (cre * br - cim * bi).astype(bf16)
        bbim_ref[k] = (cre * bi + cim * br).astype(bf16)


def _s5_prep(lr, li, ldt, bre_t, bim_t):
    half = S5_N // 2
    return pl.pallas_call(
        _s5_prep_kernel,
        out_shape=(jax.ShapeDtypeStruct((1, S5_N), f32),
                   jax.ShapeDtypeStruct((1, S5_N), f32),
                   jax.ShapeDtypeStruct((2, MXU_DIM, half), bf16),
                   jax.ShapeDtypeStruct((2, MXU_DIM, half), bf16)),
        name="s5_prep",
    )(lr, li, ldt, bre_t, bim_t)


def _mixer_kernel(h_all_ref, x_ref, conv0_ref, h0_ref, s0r_ref, s0i_ref,
                  w_in_ref, conv_w_ref, conv_b_ref, wa_ref, ba_ref, wi_ref, bi_ref, lam_ref,
                  are_ref, aim_ref, bbre_ref, bbim_ref, cre_ref, cim_ref, d_ref,
                  wglu_ref, bglu_ref, wout_ref, ln_g_ref, ln_b_ref,
                  hout_ref, conv_out_ref, hlast_ref, sre_out_ref, sim_out_ref,
                  pad_scr, a_scr, b_scr, bur_scr, bui_scr, hst_scr, sr_scr, si_scr,
                  *, nb, tc, batch_major_input):
    del h_all_ref
    rows = nb * tc
    tail = (CONV_WIDTH - 1) * nb
    c = pl.program_id(0)

    @pl.when(c == 0)
    def _():
        pad_scr[0:tail, :] = conv0_ref[...]
        hst_scr[...] = h0_ref[...]
        sr_scr[...] = s0r_ref[...]
        si_scr[...] = s0i_ref[...]

    if batch_major_input:
        x = jnp.transpose(x_ref[...], (1, 0, 2)).reshape(rows, D_MODEL)
    else:
        x = x_ref[...]
    proj = jnp.dot(x.astype(bf16), w_in_ref[...], preferred_element_type=f32)
    x_rg = proj[:, :D_RG]
    g_rg = proj[:, D_RG:2 * D_RG]
    u = proj[:, 2 * D_RG:]

    pad_scr[tail:tail + rows, :] = x_rg
    conv_w = conv_w_ref[...]
    acc = conv_w[0:1, :] * pad_scr[0:rows, :]
    for k in range(1, CONV_WIDTH):
        acc = acc + conv_w[k:k + 1, :] * pad_scr[k * nb:k * nb + rows, :]
    xc = conv_b_ref[...] + acc
    new_tail = pad_scr[rows:rows + tail, :]
    pad_scr[0:tail, :] = new_tail

    xcb = xc.astype(bf16)
    ga = []
    gi = []
    for hh in range(D_RG // MXU_DIM):
        xs = xcb[:, hh * MXU_DIM:(hh + 1) * MXU_DIM]
        ga.append(jnp.dot(xs, wa_ref[hh], preferred_element_type=f32))
        gi.append(jnp.dot(xs, wi_ref[hh], preferred_element_type=f32))
    r = jax.nn.sigmoid(jnp.concatenate(ga, axis=1) + ba_ref[...])
    i = jax.nn.sigmoid(jnp.concatenate(gi, axis=1) + bi_ref[...])
    nlam = -lam_ref[...]
    softplus = jnp.maximum(nlam, 0.0) + jnp.log1p(jnp.exp(-jnp.abs(nlam)))
    log_a = (-RG_C) * r * softplus
    a_scr[...] = jnp.exp(log_a)
    th = jnp.tanh(log_a)
    b_scr[...] = jnp.sqrt((-2.0 * th) / (1.0 - th)) * (i * xc)

    for rg in range(nb // SUBLANES):
        r0 = rg * SUBLANES
        h = hst_scr[r0:r0 + SUBLANES, :]
        for t in range(tc):
            q = t * nb + r0
            h = a_scr[q:q + SUBLANES, :] * h + b_scr[q:q + SUBLANES, :]
            b_scr[q:q + SUBLANES, :] = h
        hst_scr[r0:r0 + SUBLANES, :] = h
    y_rg = b_scr[...] * _gelu_tanh(g_rg)

    ub = u.astype(bf16)
    half = S5_N // 2
    for k in range(2):
        us = ub[:, k * MXU_DIM:(k + 1) * MXU_DIM]
        bur_scr[:, k * half:(k + 1) * half] = jnp.dot(us, bbre_ref[k], preferred_element_type=f32)
        bui_scr[:, k * half:(k + 1) * half] = jnp.dot(us, bbim_ref[k], preferred_element_type=f32)
    for rg in range(nb // SUBLANES):
        r0 = rg * SUBLANES
        for cb in range(S5_N // S5_SCAN_COLS):
            c0 = cb * S5_SCAN_COLS
            ar = jnp.broadcast_to(are_ref[:, c0:c0 + S5_SCAN_COLS], (SUBLANES, S5_SCAN_COLS))
            ai = jnp.broadcast_to(aim_ref[:, c0:c0 + S5_SCAN_COLS], (SUBLANES, S5_SCAN_COLS))
            xr = sr_scr[r0:r0 + SUBLANES, c0:c0 + S5_SCAN_COLS]
            xi = si_scr[r0:r0 + SUBLANES, c0:c0 + S5_SCAN_COLS]
            for t in range(tc):
                q = t * nb + r0
                br = bur_scr[q:q + SUBLANES, c0:c0 + S5_SCAN_COLS]
                bi_ = bui_scr[q:q + SUBLANES, c0:c0 + S5_SCAN_COLS]
                nxr = ar * xr - ai * xi + br
                nxi = ar * xi + ai * xr + bi_
                bur_scr[q:q + SUBLANES, c0:c0 + S5_SCAN_COLS] = nxr
                bui_scr[q:q + SUBLANES, c0:c0 + S5_SCAN_COLS] = nxi
                xr, xi = nxr, nxi
            sr_scr[r0:r0 + SUBLANES, c0:c0 + S5_SCAN_COLS] = xr
            si_scr[r0:r0 + SUBLANES, c0:c0 + S5_SCAN_COLS] = xi
    ys = []
    for j in range(D_S5 // MXU_DIM):
        xrb = bur_scr[:, j * half:(j + 1) * half].astype(bf16)
        xib = bui_scr[:, j * half:(j + 1) * half].astype(bf16)
        ys.append(jnp.dot(xrb, cre_ref[j], preferred_element_type=f32)
                  - jnp.dot(xib, cim_ref[j], preferred_element_type=f32))
    y_s5 = jnp.concatenate(ys, axis=1) + d_ref[...] * u
    yg = _gelu_tanh(y_s5)
    glu = jnp.dot(yg.astype(bf16), wglu_ref[...], preferred_element_type=f32) + bglu_ref[...]
    y_s5 = yg * jax.nn.sigmoid(glu)

    ycat = jnp.concatenate([y_rg, y_s5], axis=1).astype(bf16)
    mix = jnp.dot(ycat, wout_ref[...], preferred_element_type=f32)
    hout_ref[...] = _layer_norm(DN_ALPHA * x + mix, ln_g_ref[...], ln_b_ref[...])

    @pl.when(c == pl.num_programs(0) - 1)
    def _():
        conv_out_ref[...] = pad_scr[0:tail, :]
        hlast_ref[...] = hst_scr[...]
        sre_out_ref[...] = sr_scr[...]
        sim_out_ref[...] = si_scr[...]


def _full(shape):
    n = len(shape)
    return pl.BlockSpec(shape, lambda c: (0,) * n)


def _mixer(x, conv0, h0, s0r, s0i, params, *, nb, tc, name, h_all, total_rows, first_row):
    batch_major_input = x.ndim == 3
    rows = nb * tc
    if batch_major_input:
        n_chunks = x.shape[1] // tc
        x_spec = pl.BlockSpec((nb, tc, D_MODEL), lambda c: (0, c, 0))
    else:
        n_chunks = x.shape[0] // rows
        x_spec = pl.BlockSpec((rows, D_MODEL), lambda c: (c, 0))
    first_block = first_row // rows
    aliased = h_all.shape == (total_rows, D_MODEL)
    tail = (CONV_WIDTH - 1) * nb
    small = (conv0, h0, s0r, s0i) + tuple(params)
    in_specs = [pl.BlockSpec(memory_space=pl.ANY), x_spec]
    in_specs += [_full(a.shape) for a in small]
    out_shape = (jax.ShapeDtypeStruct((total_rows, D_MODEL), f32),
                 jax.ShapeDtypeStruct((tail, D_RG), f32),
                 jax.ShapeDtypeStruct((nb, D_RG), f32),
                 jax.ShapeDtypeStruct((nb, S5_N), f32),
                 jax.ShapeDtypeStruct((nb, S5_N), f32))
    out_specs = (pl.BlockSpec((rows, D_MODEL), lambda c: (c + first_block, 0)),
                 _full((tail, D_RG)), _full((nb, D_RG)), _full((nb, S5_N)), _full((nb, S5_N)))
    scratch = [pltpu.VMEM((rows + tail, D_RG), f32),
               pltpu.VMEM((rows, D_RG), f32),
               pltpu.VMEM((rows, D_RG), f32),
               pltpu.VMEM((rows, S5_N), f32),
               pltpu.VMEM((rows, S5_N), f32),
               pltpu.VMEM((nb, D_RG), f32),
               pltpu.VMEM((nb, S5_N), f32),
               pltpu.VMEM((nb, S5_N), f32)]
    return pl.pallas_call(
        functools.partial(_mixer_kernel, nb=nb, tc=tc, batch_major_input=batch_major_input),
        grid=(n_chunks,),
        in_specs=in_specs,
        out_specs=out_specs,
        out_shape=out_shape,
        scratch_shapes=scratch,
        input_output_aliases={0: 0} if aliased else {},
        compiler_params=pltpu.CompilerParams(
            dimension_semantics=("arbitrary",), vmem_limit_bytes=VMEM_LIMIT_BYTES),
        name=name,
    )(h_all, x, *small)


TOK_TILE = 256
RUN_ALIGN = 16
SLOTS = 3072
SLOT_CHUNK = 512
MAX_CHUNKS = SLOTS // RUN_ALIGN
ROW_TILE = 512
X_BUFFERS = 4
EXT = D_MODEL + LANES
ROUTER_TILES = 6
NO_RUN = 1.0e9


def _top_k_gates(scores, rb):
    rows = scores.shape[0]
    lane_f = jax.lax.broadcasted_iota(jnp.int32, (rows, LANES), 1).astype(f32)
    biased = jnp.where(lane_f < float(N_EXPERTS), scores + rb, -jnp.inf)
    sel = jnp.zeros((rows, LANES), f32)
    mask = jnp.zeros((rows, LANES), f32)
    for _ in range(TOP_K):
        m = jnp.max(biased, axis=1, keepdims=True)
        idx = jnp.min(jnp.where(biased == m, lane_f, float(LANES)), axis=1, keepdims=True)
        hit = lane_f == idx
        sel = jnp.where(hit, scores, sel)
        mask = jnp.where(hit, 1.0, mask)
        biased = jnp.where(hit, -jnp.inf, biased)
    gates = sel / jnp.sum(sel, axis=1, keepdims=True) * ROUTED_SCALE
    return mask, gates


def _router_kernel(h_ref, wr_ref, rb_ref, hext_ref, rankm_ref, cnt_ref):
    hb = h_ref[...].astype(bf16)
    scores = jax.nn.sigmoid(jnp.dot(hb, wr_ref[...], preferred_element_type=f32))
    mask, gates = _top_k_gates(scores, rb_ref[...])
    t_row = jax.lax.broadcasted_iota(jnp.int32, (TOK_TILE, TOK_TILE), 0)
    t_col = jax.lax.broadcasted_iota(jnp.int32, (TOK_TILE, TOK_TILE), 1)
    earlier = jnp.where(t_col < t_row, 1.0, 0.0).astype(bf16)
    for sub in range(ROUTER_TILES):
        m = mask[sub * TOK_TILE:(sub + 1) * TOK_TILE]
        rank = jnp.dot(earlier, m.astype(bf16), preferred_element_type=f32)
        rankm_ref[sub * TOK_TILE:(sub + 1) * TOK_TILE, :] = jnp.where(m > 0.0, rank, -1.0).astype(bf16)
        cnt_ref[sub] = jnp.broadcast_to(jnp.sum(m, axis=0, keepdims=True), (SUBLANES, LANES))
    g_hi = gates.astype(bf16).astype(f32)
    g_pack = g_hi + pltpu.roll(gates - g_hi, N_EXPERTS, 1)
    hext_ref[:, :D_MODEL] = hb
    hext_ref[:, D_MODEL:] = g_pack.astype(bf16)


def _router(h, wr, rb):
    n_tiles = h.shape[0] // TOK_TILE
    assert n_tiles % ROUTER_TILES == 0
    rows = ROUTER_TILES * TOK_TILE
    const = lambda shape: pl.BlockSpec(shape, lambda i: (0,) * len(shape))
    return pl.pallas_call(
        _router_kernel,
        grid=(n_tiles // ROUTER_TILES,),
        in_specs=[pl.BlockSpec((rows, D_MODEL), lambda i: (i, 0)), const(wr.shape), const(rb.shape)],
        out_specs=(pl.BlockSpec((rows, EXT), lambda i: (i, 0)),
                   pl.BlockSpec((rows, LANES), lambda i: (i, 0)),
                   pl.BlockSpec((ROUTER_TILES, SUBLANES, LANES), lambda i: (i, 0, 0))),
        out_shape=(jax.ShapeDtypeStruct((h.shape[0], EXT), bf16),
                   jax.ShapeDtypeStruct((h.shape[0], LANES), bf16),
                   jax.ShapeDtypeStruct((n_tiles, SUBLANES, LANES), f32)),
        compiler_params=pltpu.CompilerParams(
            dimension_semantics=("arbitrary",), vmem_limit_bytes=VMEM_LIMIT_BYTES),
        name="moe_router",
    )(h, wr, rb)


def _run_copy(src, dst, sem):
    return pltpu.make_async_copy(src, dst, sem)


def _dispatch_kernel(tot_ref, dstk_ref, zst_ref, zch_ref,
                     hext_ref, rankm_ref, locs_ref, sorted_ref,
                     stage, zbuf, p_scr, sem, zsem):
    i = pl.program_id(0)
    n = pl.num_programs(0)
    slot = i % 2
    loc_row = locs_ref[0, 0:1, :]
    end_row = locs_ref[0, 1:2, :]
    for ch in range(SLOTS // SLOT_CHUNK):
        s = (jax.lax.broadcasted_iota(jnp.int32, (SLOT_CHUNK, LANES), 0)
             + ch * SLOT_CHUNK).astype(f32)
        in_run = jnp.where(s >= loc_row, jnp.where(s < end_row, 1.0, 0.0), 0.0)
        r_col = s[:, 0:1] - jnp.sum(in_run * loc_row, axis=1, keepdims=True)
        q = jax.lax.dot_general(in_run.astype(bf16), rankm_ref[...], (((1,), (1,)), ((), ())),
                                preferred_element_type=f32)
        p_scr[ch * SLOT_CHUNK:(ch + 1) * SLOT_CHUNK, :] = jnp.where(q == r_col, 1.0, 0.0).astype(bf16)

    for used in range(TOK_TILE * TOP_K + SLOT_CHUNK, SLOTS + 1, SLOT_CHUNK):
        def sort_rows(used=used):
            rows = jnp.dot(p_scr[:used, :], hext_ref[...], preferred_element_type=f32)
            stage[slot, :used, :] = rows.astype(bf16)
            if used < SLOTS:
                stage[slot, used:, :] = jnp.zeros((SLOTS - used, EXT), bf16)
        lo = used - SLOT_CHUNK if used > TOK_TILE * TOP_K + SLOT_CHUNK else 0
        if used == SLOTS:
            pl.when(tot_ref[i] > lo)(sort_rows)
        else:
            pl.when(jnp.logical_and(tot_ref[i] > lo, tot_ref[i] <= used))(sort_rows)

    def start_run(k, carry):
        src = stage.at[slot, pl.ds(pl.multiple_of(k * RUN_ALIGN, RUN_ALIGN), RUN_ALIGN)]
        dst = sorted_ref.at[pl.ds(pl.multiple_of(dstk_ref[i, k], RUN_ALIGN), RUN_ALIGN)]
        _run_copy(src, dst, sem.at[slot]).start()
        return carry
    jax.lax.fori_loop(0, MAX_CHUNKS, start_run, 0, unroll=8)

    @pl.when(i == 0)
    def _():
        zbuf[...] = jnp.zeros(zbuf.shape, bf16)

    def zero_copy(e, k):
        dst = sorted_ref.at[pl.ds(pl.multiple_of(zst_ref[e] + k * RUN_ALIGN, RUN_ALIGN), RUN_ALIGN)]
        return _run_copy(zbuf, dst, zsem.at[0])

    @pl.when(jnp.logical_and(i > 0, i <= N_EXPERTS))
    def _():
        def wait_zero(k, carry):
            zero_copy(i - 1, k).wait()
            return carry
        jax.lax.fori_loop(0, zch_ref[i - 1], wait_zero, 0)

    @pl.when(i < N_EXPERTS)
    def _():
        def start_zero(k, carry):
            zero_copy(i, k).start()
            return carry
        jax.lax.fori_loop(0, zch_ref[i], start_zero, 0)

    def wait_runs(slot_):
        for _ in range(MAX_CHUNKS):
            src = stage.at[slot_, pl.ds(0, RUN_ALIGN)]
            dst = sorted_ref.at[pl.ds(0, RUN_ALIGN)]
            _run_copy(src, dst, sem.at[slot_]).wait()

    @pl.when(i > 0)
    def _():
        wait_runs(1 - slot)

    @pl.when(i == n - 1)
    def _():
        wait_runs(slot)


def _dispatch(hext, rankm, locs, tot, dstk, zst, zch, *, total_rows):
    n_tiles = hext.shape[0] // TOK_TILE
    assert n_tiles > N_EXPERTS
    grid_spec = pltpu.PrefetchScalarGridSpec(
        num_scalar_prefetch=4,
        grid=(n_tiles,),
        in_specs=[pl.BlockSpec((TOK_TILE, EXT), lambda i, *_: (i, 0)),
                  pl.BlockSpec((TOK_TILE, LANES), lambda i, *_: (i, 0)),
                  pl.BlockSpec((1, SUBLANES, LANES), lambda i, *_: (i, 0, 0))],
        out_specs=pl.BlockSpec(memory_space=pl.ANY),
        scratch_shapes=[pltpu.VMEM((2, SLOTS, EXT), bf16),
                        pltpu.VMEM((RUN_ALIGN, EXT), bf16),
                        pltpu.VMEM((SLOTS, TOK_TILE), bf16),
                        pltpu.SemaphoreType.DMA((2,)),
                        pltpu.SemaphoreType.DMA((1,))])
    return pl.pallas_call(
        _dispatch_kernel,
        grid_spec=grid_spec,
        out_shape=jax.ShapeDtypeStruct((total_rows, EXT), bf16),
        compiler_params=pltpu.CompilerParams(
            dimension_semantics=("arbitrary",), vmem_limit_bytes=VMEM_LIMIT_BYTES),
        name="moe_dispatch",
    )(tot, dstk, zst, zch, hext, rankm, locs)


def _experts_kernel(ts_ref, xs_ref, wgu_ref, wd_ref, ys_ref,
                    xbuf, ybuf, wgu_b, wd_b, xsem, ysem):
    e = pl.program_id(0)
    n_valid = ts_ref[N_EXPERTS]
    wgu_b[...] = wgu_ref[0].astype(bf16)
    wd_b[...] = wd_ref[0].astype(bf16)

    def x_copy(j, slot):
        rows = pl.ds(pl.multiple_of(j * ROW_TILE, ROW_TILE), ROW_TILE)
        return pltpu.make_async_copy(xs_ref.at[rows], xbuf.at[slot], xsem.at[slot])

    def y_copy(j, slot):
        rows = pl.ds(pl.multiple_of(j * ROW_TILE, ROW_TILE), ROW_TILE)
        return pltpu.make_async_copy(ybuf.at[slot], ys_ref.at[rows], ysem.at[slot])

    ahead = X_BUFFERS - 1

    @pl.when(e == 0)
    def _():
        for a in range(ahead):
            @pl.when(a < n_valid)
            def _():
                x_copy(a, a).start()

    def tile(j, carry):
        xslot = j % X_BUFFERS
        slot = j % 2
        x_copy(j, xslot).wait()

        @pl.when(j + ahead < n_valid)
        def _():
            x_copy(j + ahead, (j + ahead) % X_BUFFERS).start()

        @pl.when(j >= 2)
        def _():
            y_copy(j - 2, slot).wait()

        x = xbuf[xslot, :, :D_MODEL]
        gu = jnp.dot(x, wgu_b[...], preferred_element_type=f32)
        act = (jax.nn.silu(gu[:, :D_EXPERT]) * gu[:, D_EXPERT:]).astype(bf16)
        y = jnp.dot(act, wd_b[...], preferred_element_type=f32)
        g_pack = xbuf[xslot, :, D_MODEL:].astype(f32)
        lane = jax.lax.broadcasted_iota(jnp.int32, (ROW_TILE, LANES), 1)
        mine = jnp.where(lane == e, g_pack, jnp.where(lane == e + N_EXPERTS, g_pack, 0.0))
        gate = jnp.sum(mine, axis=1, keepdims=True)
        ybuf[slot] = (y * gate).astype(bf16)
        y_copy(j, slot).start()
        return carry
    jax.lax.fori_loop(ts_ref[e], ts_ref[e + 1], tile, 0)

    @pl.when(e == pl.num_programs(0) - 1)
    def _():
        @pl.when(n_valid >= 2)
        def _():
            y_copy(n_valid - 2, n_valid % 2).wait()
        y_copy(n_valid - 1, (n_valid - 1) % 2).wait()


def _experts(xs, wgu, wd, ts, *, n_rows):
    grid_spec = pltpu.PrefetchScalarGridSpec(
        num_scalar_prefetch=1,
        grid=(N_EXPERTS,),
        in_specs=[pl.BlockSpec(memory_space=pl.ANY),
                  pl.BlockSpec((1, D_MODEL, 2 * D_EXPERT), lambda e, ts: (e, 0, 0)),
                  pl.BlockSpec((1, D_EXPERT, D_MODEL), lambda e, ts: (e, 0, 0))],
        out_specs=pl.BlockSpec(memory_space=pl.ANY),
        scratch_shapes=[pltpu.VMEM((X_BUFFERS, ROW_TILE, EXT), bf16),
                        pltpu.VMEM((2, ROW_TILE, D_MODEL), bf16),
                        pltpu.VMEM((D_MODEL, 2 * D_EXPERT), bf16),
                        pltpu.VMEM((D_EXPERT, D_MODEL), bf16),
                        pltpu.SemaphoreType.DMA((X_BUFFERS,)),
                        pltpu.SemaphoreType.DMA((2,))])
    return pl.pallas_call(
        _experts_kernel,
        grid_spec=grid_spec,
        out_shape=jax.ShapeDtypeStruct((n_rows, D_MODEL), bf16),
        compiler_params=pltpu.CompilerParams(
            dimension_semantics=("arbitrary",), vmem_limit_bytes=VMEM_LIMIT_BYTES),
        name="moe_experts",
    )(ts, xs, wgu, wd)


def _combine_kernel(tot_ref, srck_ref,
                    h_ref, rankm_ref, locc_ref, ys_ref, wsu_ref, wsd_ref, ln_g_ref, ln_b_ref,
                    out_bm_ref, out_tm_ref, yloc, acc_scr, p_scr, sem, *, n_bm_tiles):
    i = pl.program_id(0)
    n = pl.num_programs(0)
    slot = i % 2

    def fetch(tile, slot_):
        def body(k, carry):
            src = ys_ref.at[pl.ds(pl.multiple_of(srck_ref[tile, k], RUN_ALIGN), RUN_ALIGN)]
            dst = yloc.at[slot_, pl.ds(pl.multiple_of(k * RUN_ALIGN, RUN_ALIGN), RUN_ALIGN)]
            _run_copy(src, dst, sem.at[slot_]).start()
            return carry
        jax.lax.fori_loop(0, MAX_CHUNKS, body, 0, unroll=8)

    def drain(slot_):
        for _ in range(MAX_CHUNKS):
            _run_copy(ys_ref.at[pl.ds(0, RUN_ALIGN)], yloc.at[slot_, pl.ds(0, RUN_ALIGN)],
                      sem.at[slot_]).wait()

    @pl.when(i == 0)
    def _():
        fetch(0, 0)

    @pl.when(i + 1 < n)
    def _():
        fetch(i + 1, 1 - slot)

    h = h_ref[...]
    hb = h.astype(bf16)
    su = jnp.dot(hb, wsu_ref[...], preferred_element_type=f32)
    act = (jax.nn.silu(su[:, :D_SHARED]) * su[:, D_SHARED:]).astype(bf16)
    acc_scr[...] = jnp.dot(act, wsd_ref[...], preferred_element_type=f32)

    drain(slot)

    loc_col = locc_ref[0, :, 0:1]
    end_col = locc_ref[0, :, 1:2]
    for ch in range(SLOTS // SLOT_CHUNK):
        s = (jax.lax.broadcasted_iota(jnp.int32, (LANES, SLOT_CHUNK), 1)
             + ch * SLOT_CHUNK).astype(f32)
        in_run = jnp.where(s >= loc_col, jnp.where(s < end_col, 1.0, 0.0), 0.0)
        r_row = s[0:1, :] - jnp.sum(in_run * loc_col, axis=0, keepdims=True)
        q = jnp.dot(rankm_ref[...], in_run.astype(bf16), preferred_element_type=f32)
        p_scr[:, ch * SLOT_CHUNK:(ch + 1) * SLOT_CHUNK] = jnp.where(q == r_row, 1.0, 0.0).astype(bf16)

    for used in range(TOK_TILE * TOP_K + SLOT_CHUNK, SLOTS + 1, SLOT_CHUNK):
        def gather(used=used):
            acc_scr[...] += jnp.dot(p_scr[:, :used], yloc[slot, :used, :], preferred_element_type=f32)
        lo = used - SLOT_CHUNK if used > TOK_TILE * TOP_K + SLOT_CHUNK else 0
        if used == SLOTS:
            pl.when(tot_ref[i] > lo)(gather)
        else:
            pl.when(jnp.logical_and(tot_ref[i] > lo, tot_ref[i] <= used))(gather)
    y = _layer_norm(DN_ALPHA * h + acc_scr[...], ln_g_ref[...], ln_b_ref[...])

    @pl.when(i < n_bm_tiles)
    def _():
        nb, tc, _ = out_bm_ref.shape
        out_bm_ref[...] = jnp.transpose(y.reshape(tc, nb, D_MODEL), (1, 0, 2))

    @pl.when(i >= n_bm_tiles)
    def _():
        out_tm_ref[...] = y


def _combine(h, rankm, locc, ys, wsu, wsd, ln_g, ln_b, tot, srck, *, bm_shape):
    n_tiles = h.shape[0] // TOK_TILE
    nb, length, _ = bm_shape
    tc = TOK_TILE // nb
    n_bm_tiles = length // tc
    n_tm_tiles = n_tiles - n_bm_tiles
    const = lambda shape: pl.BlockSpec(shape, lambda i, *_: (0,) * len(shape))
    grid_spec = pltpu.PrefetchScalarGridSpec(
        num_scalar_prefetch=2,
        grid=(n_tiles,),
        in_specs=[pl.BlockSpec((TOK_TILE, D_MODEL), lambda i, *_: (i, 0)),
                  pl.BlockSpec((TOK_TILE, LANES), lambda i, *_: (i, 0)),
                  pl.BlockSpec((1, LANES, 2), lambda i, *_: (i, 0, 0)),
                  pl.BlockSpec(memory_space=pl.ANY),
                  const(wsu.shape), const(wsd.shape), const(ln_g.shape), const(ln_b.shape)],
        out_specs=(pl.BlockSpec((nb, tc, D_MODEL),
                                lambda i, *_: (0, jnp.minimum(i, n_bm_tiles - 1), 0)),
                   pl.BlockSpec((TOK_TILE, D_MODEL),
                                lambda i, *_: (jnp.maximum(i - n_bm_tiles, 0), 0))),
        scratch_shapes=[pltpu.VMEM((2, SLOTS, D_MODEL), bf16),
                        pltpu.VMEM((TOK_TILE, D_MODEL), f32),
                        pltpu.VMEM((TOK_TILE, SLOTS), bf16),
                        pltpu.SemaphoreType.DMA((2,))])
    return pl.pallas_call(
        functools.partial(_combine_kernel, n_bm_tiles=n_bm_tiles),
        grid_spec=grid_spec,
        out_shape=(jax.ShapeDtypeStruct(bm_shape, f32),
                   jax.ShapeDtypeStruct((n_tm_tiles * TOK_TILE, D_MODEL), f32)),
        compiler_params=pltpu.CompilerParams(
            dimension_semantics=("arbitrary",), vmem_limit_bytes=VMEM_LIMIT_BYTES),
        name="moe_combine",
    )(tot, srck, h, rankm, locc, ys, wsu, wsd, ln_g, ln_b)


def _round_up(x, m):
    return (x + m - 1) // m * m


def _moe(h, wr, rb, wgu, wd, wsu, wsd, ln_g, ln_b, *, bm_shape):
    n_tok = h.shape[0]
    n_tiles = n_tok // TOK_TILE
    max_rows = _round_up(n_tok * TOP_K + n_tiles * N_EXPERTS * (RUN_ALIGN - 1)
                         + N_EXPERTS * (ROW_TILE - 1), ROW_TILE)

    hext, rankm, cnt = _router(h, wr, rb)

    i32 = jnp.int32
    cnt = cnt[:, 0, :N_EXPERTS].astype(i32)
    plen = _round_up(cnt, RUN_ALIGN)
    over_tiles = jnp.cumsum(plen, axis=0)
    region_rows = over_tiles[-1]
    region_size = _round_up(region_rows, ROW_TILE)
    region_start = jnp.cumsum(region_size) - region_size
    run_dst = region_start[None, :] + over_tiles - plen
    over_experts = jnp.cumsum(plen, axis=1)
    loc = over_experts - plen
    tot = over_experts[:, -1]
    ch_end = (over_experts // RUN_ALIGN)[:, None, :]
    ch_beg = (loc // RUN_ALIGN)[:, None, :]
    k = jnp.arange(MAX_CHUNKS, dtype=i32)
    kk = k[None, :, None]
    mine = jnp.logical_and(ch_beg <= kk, kk < ch_end)
    chunk_dst = jnp.sum(jnp.where(mine, run_dst[:, None, :] + RUN_ALIGN * (kk - ch_beg), 0), axis=-1)
    live = k[None, :] * RUN_ALIGN < tot[:, None]
    tile_ids = jnp.arange(n_tiles, dtype=i32)[:, None]
    spare = max_rows + (tile_ids % 2) * SLOTS + k[None, :] * RUN_ALIGN
    dstk = jnp.where(live, chunk_dst, spare).astype(i32)
    srck = jnp.where(live, chunk_dst, 0).astype(i32)
    zst = (region_start + region_rows).astype(i32)
    zch = ((region_size - region_rows) // RUN_ALIGN).astype(i32)
    tile_start = jnp.concatenate([region_start, region_start[-1:] + region_size[-1:]]) // ROW_TILE
    tile_start = tile_start.astype(i32)

    pad = ((0, 0), (0, LANES - N_EXPERTS))
    loc_f = jnp.pad(loc.astype(f32), pad, constant_values=NO_RUN)
    end_f = jnp.pad(over_experts.astype(f32), pad, constant_values=NO_RUN)
    locs = jnp.concatenate([loc_f[:, None, :], end_f[:, None, :],
                            jnp.zeros((n_tiles, SUBLANES - 2, LANES), f32)], axis=1)
    locc = jnp.stack([loc_f, end_f], axis=-1)

    tot = tot.astype(i32)
    xs = _dispatch(hext, rankm, locs, tot, dstk, zst, zch, total_rows=max_rows + 2 * SLOTS)
    ys = _experts(xs, wgu, wd, tile_start, n_rows=max_rows)
    return _combine(h, rankm, locc, ys, wsu, wsd, ln_g, ln_b, tot, srck, bm_shape=bm_shape)


def _head_block_diag(w):
    heads_per_tile = MXU_DIM // RG_HEAD_DIM
    w4 = w.reshape(D_RG // MXU_DIM, heads_per_tile, RG_HEAD_DIM, RG_HEAD_DIM)
    eye = jnp.eye(heads_per_tile, dtype=w.dtype)
    return jnp.einsum('thij,hk->thikj', w4, eye).reshape(D_RG // MXU_DIM, MXU_DIM, MXU_DIM)


def _s5_in_tiles(b):
    gpt = S5_GROUPS // 2
    b4 = b.reshape(2, gpt, S5_STATE, S5_GROUP)
    eye = jnp.eye(gpt, dtype=b.dtype)
    return jnp.einsum('kgph,gm->kghmp', b4, eye).reshape(2, gpt * S5_GROUP, gpt * S5_STATE)


def _s5_out_tiles(cw):
    gpt = S5_GROUPS // 2
    c4 = cw.reshape(2, gpt, S5_GROUP, S5_STATE)
    eye = jnp.eye(gpt, dtype=cw.dtype)
    return jnp.einsum('kghp,gm->kgpmh', c4, eye).reshape(2, gpt * S5_STATE, gpt * S5_GROUP)


def _row(v):
    return v.reshape(1, -1)


def kernel(x_prompt, x_sample, state_rg_conv, state_rg_h, state_s5_re, state_s5_im, w_in, conv_w, conv_b, rg_w_a, rg_b_a, rg_w_i, rg_b_i, rg_lam, s5_a_re, s5_a_im, s5_log_dt, s5_b_re, s5_b_im, s5_c_re, s5_c_im, s5_d, w_glu, b_glu, w_out, ln1_g, ln1_b, w_router, router_bias, w_gate_up, w_down, w_shared_up, w_shared_down, ln2_g, ln2_b):
    l = 0
    bp, lp, _ = x_prompt.shape
    bs, ls, _ = x_sample.shape

    are, aim, bbre, bbim = _s5_prep(
        _row(s5_a_re[l]), _row(s5_a_im[l]),
        _row(jnp.repeat(s5_log_dt[l], S5_STATE)),
        _s5_in_tiles(s5_b_re[l]), _s5_in_tiles(s5_b_im[l]))
    params = (w_in[l].astype(bf16), conv_w[l], _row(conv_b[l]),
              _head_block_diag(rg_w_a[l]).astype(bf16), _row(rg_b_a[l]),
              _head_block_diag(rg_w_i[l]).astype(bf16), _row(rg_b_i[l]), _row(rg_lam[l]),
              are, aim, bbre, bbim,
              _s5_out_tiles(s5_c_re[l]).astype(bf16), _s5_out_tiles(s5_c_im[l]).astype(bf16),
              _row(s5_d[l]), w_glu[l].astype(bf16), _row(b_glu[l]), w_out[l].astype(bf16),
              _row(ln1_g[l]), _row(ln1_b[l]))

    tail = CONV_WIDTH - 1
    n_tok = lp * bp + ls * bs
    xs_tm = x_sample.transpose(1, 0, 2).reshape(ls * bs, D_MODEL)
    h_all, sc, sh, sre, sim = _mixer(
        xs_tm, state_rg_conv[l].transpose(1, 0, 2).reshape(tail * bs, D_RG), state_rg_h[l],
        state_s5_re[l].reshape(bs, S5_N), state_s5_im[l].reshape(bs, S5_N), params,
        nb=bs, tc=ls, name="mixer_sample",
        h_all=jnp.zeros((SUBLANES, LANES), f32), total_rows=n_tok, first_row=lp * bp)
    h_all, pc, ph, pre, pim = _mixer(
        x_prompt, jnp.zeros((tail * bp, D_RG), f32), jnp.zeros((bp, D_RG), f32),
        jnp.zeros((bp, S5_N), f32), jnp.zeros((bp, S5_N), f32), params,
        nb=bp, tc=PROMPT_CHUNK_ROWS // bp, name="mixer_prompt",
        h_all=h_all, total_rows=n_tok, first_row=0)
    wr = jnp.pad(w_router[l], ((0, 0), (0, LANES - N_EXPERTS))).astype(bf16)
    rb = jnp.pad(_row(router_bias[l]), ((0, 0), (0, LANES - N_EXPERTS)))
    yp, ys_tm = _moe(h_all, wr, rb, w_gate_up[l], w_down[l],
                     w_shared_up[l].astype(bf16), w_shared_down[l].astype(bf16),
                     _row(ln2_g[l]), _row(ln2_b[l]), bm_shape=x_prompt.shape)
    ys = ys_tm.reshape(ls, bs, D_MODEL).transpose(1, 0, 2)

    def conv_out(cv, nbatch):
        return cv.reshape(tail, nbatch, D_RG).transpose(1, 0, 2)[None]

    return (yp, ys,
            conv_out(pc, bp), ph[None],
            pre.reshape(1, bp, S5_GROUPS, S5_STATE), pim.reshape(1, bp, S5_GROUPS, S5_STATE),
            conv_out(sc, bs), sh[None],
            sre.reshape(1, bs, S5_GROUPS, S5_STATE), sim.reshape(1, bs, S5_GROUPS, S5_STATE))
```

```python
import functools
import math

import jax
import jax.numpy as jnp
from jax.experimental import pallas as pl
from jax.experimental.pallas import tpu as pltpu

D_MODEL = 1024
D_RG = 512
RG_HEADS = 8
RG_HEAD_DIM = 64
CONV_WIDTH = 4
RG_C = 8.0
D_S5 = 512
S5_GROUP = 16
S5_GROUPS = 32
S5_STATE = 64
S5_N = S5_GROUPS * S5_STATE
N_EXPERTS = 64
TOP_K = 8
D_EXPERT = 256
D_SHARED = 256
ROUTED_SCALE = 2.5
DEPTH = 1
DN_ALPHA = (2.0 * DEPTH) ** 0.25
LN_EPS = 1e-5

SUBLANES = 8
LANES = 128
MXU_DIM = 256
S5_SCAN_COLS = 512
PROMPT_CHUNK_ROWS = 512
VMEM_LIMIT_BYTES = 56 * 1024 * 1024

bf16 = jnp.bfloat16
f32 = jnp.float32


def _gelu_tanh(x):
    c = math.sqrt(2.0 / math.pi)
    return x * (0.5 * (1.0 + jnp.tanh(c * (x + 0.044715 * (x * x * x)))))


def _layer_norm(x, g, b):
    mu = jnp.mean(x, axis=-1, keepdims=True)
    xc = x - mu
    var = jnp.mean(xc * xc, axis=-1, keepdims=True)
    return xc * jax.lax.rsqrt(var + LN_EPS) * g + b


def _s5_prep_kernel(lr_ref, li_ref, ldt_ref, bre_ref, bim_ref,
                    are_ref, aim_ref, bbre_ref, bbim_ref):
    lr = lr_ref[...]
    li = li_ref[...]
    dt = jnp.exp(ldt_ref[...])
    mag = jnp.exp(lr * dt)
    abar_re = mag * jnp.cos(li * dt)
    abar_im = mag * jnp.sin(li * dt)
    den = lr * lr + li * li
    nr = abar_re - 1.0
    ni = abar_im
    coef_re = (nr * lr + ni * li) / den
    coef_im = (ni * lr - nr * li) / den
    are_ref[...] = abar_re
    aim_ref[...] = abar_im
    half = S5_N // 2
    for k in range(2):
        cre = coef_re[:, k * half:(k + 1) * half]
        cim = coef_im[:, k * half:(k + 1) * half]
        br = bre_ref[k]
        bi = bim_ref[k]
        bbre_ref[k] = (cre * br - cim * bi).astype(bf16)
        bbim_ref[k] = (cre * bi + cim * br).astype(bf16)


def _s5_prep(lr, li, ldt, bre_t, bim_t):
    half = S5_N // 2
    return pl.pallas_call(
        _s5_prep_kernel,
        out_shape=(jax.ShapeDtypeStruct((1, S5_N), f32),
                   jax.ShapeDtypeStruct((1, S5_N), f32),
                   jax.ShapeDtypeStruct((2, MXU_DIM, half), bf16),
                   jax.ShapeDtypeStruct((2, MXU_DIM, half), bf16)),
        name="s5_prep",
    )(lr, li, ldt, bre_t, bim_t)


def _mixer_kernel(h_all_ref, x_ref, conv0_ref, h0_ref, s0r_ref, s0i_ref,
                  w_in_ref, conv_w_ref, conv_b_ref, wa_ref, ba_ref, wi_ref, bi_ref, lam_ref,
                  are_ref, aim_ref, bbre_ref, bbim_ref, cre_ref, cim_ref, d_ref,
                  wglu_ref, bglu_ref, wout_ref, ln_g_ref, ln_b_ref,
                  hout_ref, conv_out_ref, hlast_ref, sre_out_ref, sim_out_ref,
                  pad_scr, a_scr, b_scr, bur_scr, bui_scr, hst_scr, sr_scr, si_scr,
                  *, nb, tc, batch_major_input):
    del h_all_ref
    rows = nb * tc
    tail = (CONV_WIDTH - 1) * nb
    c = pl.program_id(0)

    @pl.when(c == 0)
    def _():
        pad_scr[0:tail, :] = conv0_ref[...]
        hst_scr[...] = h0_ref[...]
        sr_scr[...] = s0r_ref[...]
        si_scr[...] = s0i_ref[...]

    if batch_major_input:
        x = jnp.transpose(x_ref[...], (1, 0, 2)).reshape(rows, D_MODEL)
    else:
        x = x_ref[...]
    proj = jnp.dot(x.astype(bf16), w_in_ref[...], preferred_element_type=f32)
    x_rg = proj[:, :D_RG]
    g_rg = proj[:, D_RG:2 * D_RG]
    u = proj[:, 2 * D_RG:]

    pad_scr[tail:tail + rows, :] = x_rg
    conv_w = conv_w_ref[...]
    acc = conv_w[0:1, :] * pad_scr[0:rows, :]
    for k in range(1, CONV_WIDTH):
        acc = acc + conv_w[k:k + 1, :] * pad_scr[k * nb:k * nb + rows, :]
    xc = conv_b_ref[...] + acc
    new_tail = pad_scr[rows:rows + tail, :]
    pad_scr[0:tail, :] = new_tail

    xcb = xc.astype(bf16)
    ga = []
    gi = []
    for hh in range(D_RG // MXU_DIM):
        xs = xcb[:, hh * MXU_DIM:(hh + 1) * MXU_DIM]
        ga.append(jnp.dot(xs, wa_ref[hh], preferred_element_type=f32))
        gi.append(jnp.dot(xs, wi_ref[hh], preferred_element_type=f32))
    r = jax.nn.sigmoid(jnp.concatenate(ga, axis=1) + ba_ref[...])
    i = jax.nn.sigmoid(jnp.concatenate(gi, axis=1) + bi_ref[...])
    nlam = -lam_ref[...]
    softplus = jnp.maximum(nlam, 0.0) + jnp.log1p(jnp.exp(-jnp.abs(nlam)))
    log_a = (-RG_C) * r * softplus
    a_scr[...] = jnp.exp(log_a)
    th = jnp.tanh(log_a)
    b_scr[...] = jnp.sqrt((-2.0 * th) / (1.0 - th)) * (i * xc)

    for rg in range(nb // SUBLANES):
        r0 = rg * SUBLANES
        h = hst_scr[r0:r0 + SUBLANES, :]
        for t in range(tc):
            q = t * nb + r0
            h = a_scr[q:q + SUBLANES, :] * h + b_scr[q:q + SUBLANES, :]
            b_scr[q:q + SUBLANES, :] = h
        hst_scr[r0:r0 + SUBLANES, :] = h
    y_rg = b_scr[...] * _gelu_tanh(g_rg)

    ub = u.astype(bf16)
    half = S5_N // 2
    for k in range(2):
        us = ub[:, k * MXU_DIM:(k + 1) * MXU_DIM]
        bur_scr[:, k * half:(k + 1) * half] = jnp.dot(us, bbre_ref[k], preferred_element_type=f32)
        bui_scr[:, k * half:(k + 1) * half] = jnp.dot(us, bbim_ref[k], preferred_element_type=f32)
    for rg in range(nb // SUBLANES):
        r0 = rg * SUBLANES
        for cb in range(S5_N // S5_SCAN_COLS):
            c0 = cb * S5_SCAN_COLS
            ar = jnp.broadcast_to(are_ref[:, c0:c0 + S5_SCAN_COLS], (SUBLANES, S5_SCAN_COLS))
            ai = jnp.broadcast_to(aim_ref[:, c0:c0 + S5_SCAN_COLS], (SUBLANES, S5_SCAN_COLS))
            xr = sr_scr[r0:r0 + SUBLANES, c0:c0 + S5_SCAN_COLS]
            xi = si_scr[r0:r0 + SUBLANES, c0:c0 + S5_SCAN_COLS]
            for t in range(tc):
                q = t * nb + r0
                br = bur_scr[q:q + SUBLANES, c0:c0 + S5_SCAN_COLS]
                bi_ = bui_scr[q:q + SUBLANES, c0:c0 + S5_SCAN_COLS]
                nxr = ar * xr - ai * xi + br
                nxi = ar * xi + ai * xr + bi_
                bur_scr[q:q + SUBLANES, c0:c0 + S5_SCAN_COLS] = nxr
                bui_scr[q:q + SUBLANES, c0:c0 + S5_SCAN_COLS] = nxi
                xr, xi = nxr, nxi
            sr_scr[r0:r0 + SUBLANES, c0:c0 + S5_SCAN_COLS] = xr
            si_scr[r0:r0 + SUBLANES, c0:c0 + S5_SCAN_COLS] = xi
    ys = []
    for j in range(D_S5 // MXU_DIM):
        xrb = bur_scr[:, j * half:(j + 1) * half].astype(bf16)
        xib = bui_scr[:, j * half:(j + 1) * half].astype(bf16)
        ys.append(jnp.dot(xrb, cre_ref[j], preferred_element_type=f32)
                  - jnp.dot(xib, cim_ref[j], preferred_element_type=f32))
    y_s5 = jnp.concatenate(ys, axis=1) + d_ref[...] * u
    yg = _gelu_tanh(y_s5)
    glu = jnp.dot(yg.astype(bf16), wglu_ref[...], preferred_element_type=f32) + bglu_ref[...]
    y_s5 = yg * jax.nn.sigmoid(glu)

    ycat = jnp.concatenate([y_rg, y_s5], axis=1).astype(bf16)
    mix = jnp.dot(ycat, wout_ref[...], preferred_element_type=f32)
    hout_ref[...] = _layer_norm(DN_ALPHA * x + mix, ln_g_ref[...], ln_b_ref[...])

    @pl.when(c == pl.num_programs(0) - 1)
    def _():
        conv_out_ref[...] = pad_scr[0:tail, :]
        hlast_ref[...] = hst_scr[...]
        sre_out_ref[...] = sr_scr[...]
        sim_out_ref[...] = si_scr[...]


def _full(shape):
    n = len(shape)
    return pl.BlockSpec(shape, lambda c: (0,) * n)


def _mixer(x, conv0, h0, s0r, s0i, params, *, nb, tc, name, h_all, total_rows, first_row):
    batch_major_input = x.ndim == 3
    rows = nb * tc
    if batch_major_input:
        n_chunks = x.shape[1] // tc
        x_spec = pl.BlockSpec((nb, tc, D_MODEL), lambda c: (0, c, 0))
    else:
        n_chunks = x.shape[0] // rows
        x_spec = pl.BlockSpec((rows, D_MODEL), lambda c: (c, 0))
    first_block = first_row // rows
    aliased = h_all.shape == (total_rows, D_MODEL)
    tail = (CONV_WIDTH - 1) * nb
    small = (conv0, h0, s0r, s0i) + tuple(params)
    in_specs = [pl.BlockSpec(memory_space=pl.ANY), x_spec]
    in_specs += [_full(a.shape) for a in small]
    out_shape = (jax.ShapeDtypeStruct((total_rows, D_MODEL), f32),
                 jax.ShapeDtypeStruct((tail, D_RG), f32),
                 jax.ShapeDtypeStruct((nb, D_RG), f32),
                 jax.ShapeDtypeStruct((nb, S5_N), f32),
                 jax.ShapeDtypeStruct((nb, S5_N), f32))
    out_specs = (pl.BlockSpec((rows, D_MODEL), lambda c: (c + first_block, 0)),
                 _full((tail, D_RG)), _full((nb, D_RG)), _full((nb, S5_N)), _full((nb, S5_N)))
    scratch = [pltpu.VMEM((rows + tail, D_RG), f32),
               pltpu.VMEM((rows, D_RG), f32),
               pltpu.VMEM((rows, D_RG), f32),
               pltpu.VMEM((rows, S5_N), f32),
               pltpu.VMEM((rows, S5_N), f32),
               pltpu.VMEM((nb, D_RG), f32),
               pltpu.VMEM((nb, S5_N), f32),
               pltpu.VMEM((nb, S5_N), f32)]
    return pl.pallas_call(
        functools.partial(_mixer_kernel, nb=nb, tc=tc, batch_major_input=batch_major_input),
        grid=(n_chunks,),
        in_specs=in_specs,
        out_specs=out_specs,
        out_shape=out_shape,
        scratch_shapes=scratch,
        input_output_aliases={0: 0} if aliased else {},
        compiler_params=pltpu.CompilerParams(
            dimension_semantics=("arbitrary",), vmem_limit_bytes=VMEM_LIMIT_BYTES),
        name=name,
    )(h_all, x, *small)


TOK_TILE = 256
RUN_ALIGN = 16
FIX = 32
FIX_SLOTS = N_EXPERTS * FIX
OVF_SLOTS = 2048
OVF_CHUNKS = OVF_SLOTS // RUN_ALIGN
SLOT_CHUNK = 512
FIX_ROWS_STEP = 528
ROW_TILE = 256
X_BUFFERS = 4
COPY_GROUP = 4
EXT = D_MODEL + LANES
ROUTER_TILES = 6
NO_RUN = 1.0e9


def _top_k_gates(scores, rb):
    rows = scores.shape[0]
    lane_f = jax.lax.broadcasted_iota(jnp.int32, (rows, LANES), 1).astype(f32)
    biased = jnp.where(lane_f < float(N_EXPERTS), scores + rb, -jnp.inf)
    sel = jnp.zeros((rows, LANES), f32)
    mask = jnp.zeros((rows, LANES), f32)
    for _ in range(TOP_K):
        m = jnp.max(biased, axis=1, keepdims=True)
        idx = jnp.min(jnp.where(biased == m, lane_f, float(LANES)), axis=1, keepdims=True)
        hit = lane_f == idx
        sel = jnp.where(hit, scores, sel)
        mask = jnp.where(hit, 1.0, mask)
        biased = jnp.where(hit, -jnp.inf, biased)
    gates = sel / jnp.sum(sel, axis=1, keepdims=True) * ROUTED_SCALE
    return mask, gates


def _router_kernel(h_ref, wr_ref, rb_ref, hext_ref, rankm_ref, cnt_ref):
    hb = h_ref[...].astype(bf16)
    scores = jax.nn.sigmoid(jnp.dot(hb, wr_ref[...], preferred_element_type=f32))
    mask, gates = _top_k_gates(scores, rb_ref[...])
    t_row = jax.lax.broadcasted_iota(jnp.int32, (TOK_TILE, TOK_TILE), 0)
    t_col = jax.lax.broadcasted_iota(jnp.int32, (TOK_TILE, TOK_TILE), 1)
    earlier = jnp.where(t_col < t_row, 1.0, 0.0).astype(bf16)
    for sub in range(ROUTER_TILES):
        m = mask[sub * TOK_TILE:(sub + 1) * TOK_TILE]
        rank = jnp.dot(earlier, m.astype(bf16), preferred_element_type=f32)
        rankm_ref[sub * TOK_TILE:(sub + 1) * TOK_TILE, :] = jnp.where(m > 0.0, rank, -1.0).astype(bf16)
        cnt_ref[sub] = jnp.broadcast_to(jnp.sum(m, axis=0, keepdims=True), (SUBLANES, LANES))
    g_hi = gates.astype(bf16).astype(f32)
    g_pack = g_hi + pltpu.roll(gates - g_hi, N_EXPERTS, 1)
    hext_ref[:, :D_MODEL] = hb
    hext_ref[:, D_MODEL:] = g_pack.astype(bf16)


def _router(h, wr, rb):
    n_tiles = h.shape[0] // TOK_TILE
    assert n_tiles % ROUTER_TILES == 0
    rows = ROUTER_TILES * TOK_TILE
    const = lambda shape: pl.BlockSpec(shape, lambda i: (0,) * len(shape))
    return pl.pallas_call(
        _router_kernel,
        grid=(n_tiles // ROUTER_TILES,),
        in_specs=[pl.BlockSpec((rows, D_MODEL), lambda i: (i, 0)), const(wr.shape), const(rb.shape)],
        out_specs=(pl.BlockSpec((rows, EXT), lambda i: (i, 0)),
                   pl.BlockSpec((rows, LANES), lambda i: (i, 0)),
                   pl.BlockSpec((ROUTER_TILES, SUBLANES, LANES), lambda i: (i, 0, 0))),
        out_shape=(jax.ShapeDtypeStruct((h.shape[0], EXT), bf16),
                   jax.ShapeDtypeStruct((h.shape[0], LANES), bf16),
                   jax.ShapeDtypeStruct((n_tiles, SUBLANES, LANES), f32)),
        compiler_params=pltpu.CompilerParams(
            dimension_semantics=("arbitrary",), vmem_limit_bytes=VMEM_LIMIT_BYTES),
        name="moe_router",
    )(h, wr, rb)


def _run_copy(src, dst, sem):
    return pltpu.make_async_copy(src, dst, sem)


def _fixed_ranges(shape, axis):
    e = jax.lax.broadcasted_iota(jnp.int32, shape, axis).astype(f32)
    loc = jnp.where(e < float(N_EXPERTS), e * float(FIX), NO_RUN)
    return loc, loc + float(FIX)


def _slot_rows(first_slot, loc_row, end_row, base, rankm):
    s = (jax.lax.broadcasted_iota(jnp.int32, (SLOT_CHUNK, LANES), 0) + first_slot).astype(f32)
    in_run = jnp.where(s >= loc_row, jnp.where(s < end_row, 1.0, 0.0), 0.0)
    r_col = s[:, 0:1] - jnp.sum(in_run * (loc_row - base), axis=1, keepdims=True)
    q = jax.lax.dot_general(in_run.astype(bf16), rankm, (((1,), (1,)), ((), ())),
                            preferred_element_type=f32)
    return jnp.where(q == r_col, 1.0, 0.0).astype(bf16)


def _slot_cols(first_slot, loc_col, end_col, base, rankm):
    s = (jax.lax.broadcasted_iota(jnp.int32, (LANES, SLOT_CHUNK), 1) + first_slot).astype(f32)
    in_run = jnp.where(s >= loc_col, jnp.where(s < end_col, 1.0, 0.0), 0.0)
    r_row = s[0:1, :] - jnp.sum(in_run * (loc_col - base), axis=0, keepdims=True)
    q = jnp.dot(rankm, in_run.astype(bf16), preferred_element_type=f32)
    return jnp.where(q == r_row, 1.0, 0.0).astype(bf16)


def _wait_groups(n_groups, make_copy):
    max_bits = (OVF_CHUNKS // COPY_GROUP).bit_length()
    for b in range(max_bits):
        rows = (COPY_GROUP * RUN_ALIGN) << b

        @pl.when(jnp.bitwise_and(jnp.right_shift(n_groups, b), 1) == 1)
        def _():
            make_copy(rows).wait()


def _dispatch_kernel(totc_ref, ngrp_ref, dstk_ref, zst_ref, zch_ref,
                     hext_ref, rankm_ref, locs_ref, sorted_f_ref, sorted_o_ref,
                     stage_f, stage_o, zbuf, semf, semo, zsem):
    i = pl.program_id(0)
    n = pl.num_programs(0)
    slot = i % 2
    rankm = rankm_ref[...]

    fx_loc, fx_end = _fixed_ranges((1, LANES), 1)
    experts_per_chunk = SLOT_CHUNK // FIX
    for ch in range(FIX_SLOTS // SLOT_CHUNK):
        p = _slot_rows(ch * SLOT_CHUNK, fx_loc, fx_end, 0.0, rankm)
        rows = jnp.dot(p, hext_ref[...], preferred_element_type=f32).astype(bf16)
        stage_f[slot, ch * experts_per_chunk:(ch + 1) * experts_per_chunk] = rows.reshape(
            experts_per_chunk, FIX, EXT)

    ov_loc = locs_ref[0, 0:1, :]
    ov_end = locs_ref[0, 1:2, :]
    for ch in range(OVF_SLOTS // SLOT_CHUNK):
        def sort_overflow(ch=ch):
            p = _slot_rows(FIX_SLOTS + ch * SLOT_CHUNK, ov_loc, ov_end, float(FIX), rankm)
            rows = jnp.dot(p, hext_ref[...], preferred_element_type=f32)
            stage_o[slot, ch * SLOT_CHUNK:(ch + 1) * SLOT_CHUNK, :] = rows.astype(bf16)
        pl.when(totc_ref[i] * RUN_ALIGN > ch * SLOT_CHUNK)(sort_overflow)

    def fixed_copy(tile, slot_):
        dst = sorted_f_ref.at[:, pl.ds(pl.multiple_of(tile * FIX, FIX), FIX), :]
        return _run_copy(stage_f.at[slot_], dst, semf.at[slot_])

    def overflow_wait(rows, slot_):
        return _run_copy(stage_o.at[slot_, pl.ds(0, rows)], sorted_o_ref.at[pl.ds(0, rows)],
                         semo.at[slot_])

    fixed_copy(i, slot).start()

    def start_group(g, carry):
        for u in range(COPY_GROUP):
            k = g * COPY_GROUP + u
            src = stage_o.at[slot, pl.ds(pl.multiple_of(k * RUN_ALIGN, RUN_ALIGN), RUN_ALIGN)]
            dst = sorted_o_ref.at[pl.ds(pl.multiple_of(dstk_ref[i, k], RUN_ALIGN), RUN_ALIGN)]
            _run_copy(src, dst, semo.at[slot]).start()
        return carry
    jax.lax.fori_loop(0, ngrp_ref[i], start_group, 0)

    @pl.when(i == 0)
    def _():
        zbuf[...] = jnp.zeros(zbuf.shape, bf16)

    def zero_copy(e, k):
        dst = sorted_o_ref.at[pl.ds(pl.multiple_of(zst_ref[e] + k * RUN_ALIGN, RUN_ALIGN), RUN_ALIGN)]
        return _run_copy(zbuf, dst, zsem.at[0])

    @pl.when(jnp.logical_and(i > 0, i <= N_EXPERTS))
    def _():
        def wait_zero(k, carry):
            zero_copy(i - 1, k).wait()
            return carry
        jax.lax.fori_loop(0, zch_ref[i - 1], wait_zero, 0)

    @pl.when(i < N_EXPERTS)
    def _():
        def start_zero(k, carry):
            zero_copy(i, k).start()
            return carry
        jax.lax.fori_loop(0, zch_ref[i], start_zero, 0)

    @pl.when(i > 0)
    def _():
        fixed_copy(i - 1, 1 - slot).wait()
        _wait_groups(ngrp_ref[i - 1], lambda rows: overflow_wait(rows, 1 - slot))

    @pl.when(i == n - 1)
    def _():
        fixed_copy(i, slot).wait()
        _wait_groups(ngrp_ref[i], lambda rows: overflow_wait(rows, slot))


def _dispatch(hext, rankm, locs, totc, ngrp, dstk, zst, zch, *, overflow_rows):
    n_tiles = hext.shape[0] // TOK_TILE
    assert n_tiles > N_EXPERTS
    grid_spec = pltpu.PrefetchScalarGridSpec(
        num_scalar_prefetch=5,
        grid=(n_tiles,),
        in_specs=[pl.BlockSpec((TOK_TILE, EXT), lambda i, *_: (i, 0)),
                  pl.BlockSpec((TOK_TILE, LANES), lambda i, *_: (i, 0)),
                  pl.BlockSpec((1, SUBLANES, LANES), lambda i, *_: (i, 0, 0))],
        out_specs=(pl.BlockSpec(memory_space=pl.ANY), pl.BlockSpec(memory_space=pl.ANY)),
        scratch_shapes=[pltpu.VMEM((2, N_EXPERTS, FIX, EXT), bf16),
                        pltpu.VMEM((2, OVF_SLOTS, EXT), bf16),
                        pltpu.VMEM((RUN_ALIGN, EXT), bf16),
                        pltpu.SemaphoreType.DMA((2,)),
                        pltpu.SemaphoreType.DMA((2,)),
                        pltpu.SemaphoreType.DMA((1,))])
    return pl.pallas_call(
        _dispatch_kernel,
        grid_spec=grid_spec,
        out_shape=(jax.ShapeDtypeStruct((N_EXPERTS, n_tiles * FIX, EXT), bf16),
                   jax.ShapeDtypeStruct((overflow_rows, EXT), bf16)),
        compiler_params=pltpu.CompilerParams(
            dimension_semantics=("arbitrary",), vmem_limit_bytes=VMEM_LIMIT_BYTES),
        name="moe_dispatch",
    )(totc, ngrp, dstk, zst, zch, hext, rankm, locs)


def _experts_kernel(ts_ref, xf_ref, xo_ref, wgu_ref, wd_ref, yf_ref, yo_ref,
                    xbuf, ybuf, wgu_b, wd_b, xsem, ysem):
    e = pl.program_id(0)
    n_valid = ts_ref[N_EXPERTS]
    wgu_b[...] = wgu_ref[0].astype(bf16)
    wd_b[...] = wd_ref[0].astype(bf16)

    def ffn(x_ext):
        m = x_ext.shape[0]
        gu = jnp.dot(x_ext[:, :D_MODEL], wgu_b[...], preferred_element_type=f32)
        act = (jax.nn.silu(gu[:, :D_EXPERT]) * gu[:, D_EXPERT:]).astype(bf16)
        y = jnp.dot(act, wd_b[...], preferred_element_type=f32)
        g_pack = x_ext[:, D_MODEL:].astype(f32)
        lane = jax.lax.broadcasted_iota(jnp.int32, (m, LANES), 1)
        mine = jnp.where(lane == e, g_pack, jnp.where(lane == e + N_EXPERTS, g_pack, 0.0))
        return (y * jnp.sum(mine, axis=1, keepdims=True)).astype(bf16)

    def x_copy(j, slot):
        rows = pl.ds(pl.multiple_of(j * ROW_TILE, ROW_TILE), ROW_TILE)
        return pltpu.make_async_copy(xo_ref.at[rows], xbuf.at[slot], xsem.at[slot])

    def y_copy(j, slot):
        rows = pl.ds(pl.multiple_of(j * ROW_TILE, ROW_TILE), ROW_TILE)
        return pltpu.make_async_copy(ybuf.at[slot], yo_ref.at[rows], ysem.at[slot])

    ahead = X_BUFFERS - 1

    @pl.when(e == 0)
    def _():
        for a in range(ahead):
            @pl.when(a < n_valid)
            def _():
                x_copy(a, a).start()

    for c in range(xf_ref.shape[1] // FIX_ROWS_STEP):
        rows = slice(c * FIX_ROWS_STEP, (c + 1) * FIX_ROWS_STEP)
        yf_ref[0, rows, :] = ffn(xf_ref[0, rows, :])

    def tile(j, carry):
        xslot = j % X_BUFFERS
        slot = j % 2
        x_copy(j, xslot).wait()

        @pl.when(j + ahead < n_valid)
        def _():
            x_copy(j + ahead, (j + ahead) % X_BUFFERS).start()

        @pl.when(j >= 2)
        def _():
            y_copy(j - 2, slot).wait()

        ybuf[slot] = ffn(xbuf[xslot])
        y_copy(j, slot).start()
        return carry
    jax.lax.fori_loop(ts_ref[e], ts_ref[e + 1], tile, 0)

    @pl.when(e == pl.num_programs(0) - 1)
    def _():
        @pl.when(n_valid >= 2)
        def _():
            y_copy(n_valid - 2, n_valid % 2).wait()

        @pl.when(n_valid >= 1)
        def _():
            y_copy(n_valid - 1, (n_valid - 1) % 2).wait()


def _experts(xf, xo, wgu, wd, ts, *, overflow_rows):
    fixed_rows = xf.shape[1]
    assert fixed_rows % FIX_ROWS_STEP == 0
    grid_spec = pltpu.PrefetchScalarGridSpec(
        num_scalar_prefetch=1,
        grid=(N_EXPERTS,),
        in_specs=[pl.BlockSpec((1, fixed_rows, EXT), lambda e, ts: (e, 0, 0)),
                  pl.BlockSpec(memory_space=pl.ANY),
                  pl.BlockSpec((1, D_MODEL, 2 * D_EXPERT), lambda e, ts: (e, 0, 0)),
                  pl.BlockSpec((1, D_EXPERT, D_MODEL), lambda e, ts: (e, 0, 0))],
        out_specs=(pl.BlockSpec((1, fixed_rows, D_MODEL), lambda e, ts: (e, 0, 0)),
                   pl.BlockSpec(memory_space=pl.ANY)),
        scratch_shapes=[pltpu.VMEM((X_BUFFERS, ROW_TILE, EXT), bf16),
                        pltpu.VMEM((2, ROW_TILE, D_MODEL), bf16),
                        pltpu.VMEM((D_MODEL, 2 * D_EXPERT), bf16),
                        pltpu.VMEM((D_EXPERT, D_MODEL), bf16),
                        pltpu.SemaphoreType.DMA((X_BUFFERS,)),
                        pltpu.SemaphoreType.DMA((2,))])
    return pl.pallas_call(
        _experts_kernel,
        grid_spec=grid_spec,
        out_shape=(jax.ShapeDtypeStruct((N_EXPERTS, fixed_rows, D_MODEL), bf16),
                   jax.ShapeDtypeStruct((overflow_rows, D_MODEL), bf16)),
        compiler_params=pltpu.CompilerParams(
            dimension_semantics=("arbitrary",), vmem_limit_bytes=VMEM_LIMIT_BYTES),
        name="moe_experts",
    )(ts, xf, xo, wgu, wd)


def _combine_kernel(totc_ref, ngrp_ref, srck_ref,
                    h_ref, rankm_ref, locc_ref, yf_ref, yo_ref, wsu_ref, wsd_ref, ln_g_ref, ln_b_ref,
                    out_bm_ref, out_tm_ref, yloc_f, yloc_o, acc_scr, p_scr, semf, semo,
                    *, n_bm_tiles):
    i = pl.program_id(0)
    n = pl.num_programs(0)
    slot = i % 2

    def fixed_copy(tile, slot_):
        src = yf_ref.at[:, pl.ds(pl.multiple_of(tile * FIX, FIX), FIX), :]
        return _run_copy(src, yloc_f.at[slot_], semf.at[slot_])

    def overflow_wait(rows, slot_):
        return _run_copy(yo_ref.at[pl.ds(0, rows)], yloc_o.at[slot_, pl.ds(0, rows)], semo.at[slot_])

    def fetch(tile, slot_):
        fixed_copy(tile, slot_).start()

        def start_group(g, carry):
            for u in range(COPY_GROUP):
                k = g * COPY_GROUP + u
                src = yo_ref.at[pl.ds(pl.multiple_of(srck_ref[tile, k], RUN_ALIGN), RUN_ALIGN)]
                dst = yloc_o.at[slot_, pl.ds(pl.multiple_of(k * RUN_ALIGN, RUN_ALIGN), RUN_ALIGN)]
                _run_copy(src, dst, semo.at[slot_]).start()
            return carry
        jax.lax.fori_loop(0, ngrp_ref[tile], start_group, 0)

    @pl.when(i == 0)
    def _():
        yloc_o[...] = jnp.zeros(yloc_o.shape, bf16)
        fetch(0, 0)

    @pl.when(i + 1 < n)
    def _():
        fetch(i + 1, 1 - slot)

    h = h_ref[...]
    hb = h.astype(bf16)
    su = jnp.dot(hb, wsu_ref[...], preferred_element_type=f32)
    act = (jax.nn.silu(su[:, :D_SHARED]) * su[:, D_SHARED:]).astype(bf16)
    acc_scr[...] = jnp.dot(act, wsd_ref[...], preferred_element_type=f32)

    rankm = rankm_ref[...]
    fx_loc, fx_end = _fixed_ranges((LANES, 1), 0)
    for ch in range(FIX_SLOTS // SLOT_CHUNK):
        p_scr[:, ch * SLOT_CHUNK:(ch + 1) * SLOT_CHUNK] = _slot_cols(
            ch * SLOT_CHUNK, fx_loc, fx_end, 0.0, rankm)

    fixed_copy(i, slot).wait()
    _wait_groups(ngrp_ref[i], lambda rows: overflow_wait(rows, slot))

    acc_scr[...] += jnp.dot(p_scr[...], yloc_f[slot].reshape(FIX_SLOTS, D_MODEL),
                            preferred_element_type=f32)
    ov_loc = locc_ref[0, :, 0:1]
    ov_end = locc_ref[0, :, 1:2]
    for ch in range(OVF_SLOTS // SLOT_CHUNK):
        def gather_overflow(ch=ch):
            p = _slot_cols(FIX_SLOTS + ch * SLOT_CHUNK, ov_loc, ov_end, float(FIX), rankm)
            acc_scr[...] += jnp.dot(p, yloc_o[slot, ch * SLOT_CHUNK:(ch + 1) * SLOT_CHUNK, :],
                                    preferred_element_type=f32)
        pl.when(totc_ref[i] * RUN_ALIGN > ch * SLOT_CHUNK)(gather_overflow)
    y = _layer_norm(DN_ALPHA * h + acc_scr[...], ln_g_ref[...], ln_b_ref[...])

    @pl.when(i < n_bm_tiles)
    def _():
        nb, tc, _ = out_bm_ref.shape
        out_bm_ref[...] = jnp.transpose(y.reshape(tc, nb, D_MODEL), (1, 0, 2))

    @pl.when(i >= n_bm_tiles)
    def _():
        out_tm_ref[...] = y


def _combine(h, rankm, locc, yf, yo, wsu, wsd, ln_g, ln_b, totc, ngrp, srck, *, bm_shape):
    n_tiles = h.shape[0] // TOK_TILE
    nb, length, _ = bm_shape
    tc = TOK_TILE // nb
    n_bm_tiles = length // tc
    n_tm_tiles = n_tiles - n_bm_tiles
    const = lambda shape: pl.BlockSpec(shape, lambda i, *_: (0,) * len(shape))
    grid_spec = pltpu.PrefetchScalarGridSpec(
        num_scalar_prefetch=3,
        grid=(n_tiles,),
        in_specs=[pl.BlockSpec((TOK_TILE, D_MODEL), lambda i, *_: (i, 0)),
                  pl.BlockSpec((TOK_TILE, LANES), lambda i, *_: (i, 0)),
                  pl.BlockSpec((1, LANES, 2), lambda i, *_: (i, 0, 0)),
                  pl.BlockSpec(memory_space=pl.ANY),
                  pl.BlockSpec(memory_space=pl.ANY),
                  const(wsu.shape), const(wsd.shape), const(ln_g.shape), const(ln_b.shape)],
        out_specs=(pl.BlockSpec((nb, tc, D_MODEL),
                                lambda i, *_: (0, jnp.minimum(i, n_bm_tiles - 1), 0)),
                   pl.BlockSpec((TOK_TILE, D_MODEL),
                                lambda i, *_: (jnp.maximum(i - n_bm_tiles, 0), 0))),
        scratch_shapes=[pltpu.VMEM((2, N_EXPERTS, FIX, D_MODEL), bf16),
                        pltpu.VMEM((2, OVF_SLOTS, D_MODEL), bf16),
                        pltpu.VMEM((TOK_TILE, D_MODEL), f32),
                        pltpu.VMEM((TOK_TILE, FIX_SLOTS), bf16),
                        pltpu.SemaphoreType.DMA((2,)),
                        pltpu.SemaphoreType.DMA((2,))])
    return pl.pallas_call(
        functools.partial(_combine_kernel, n_bm_tiles=n_bm_tiles),
        grid_spec=grid_spec,
        out_shape=(jax.ShapeDtypeStruct(bm_shape, f32),
                   jax.ShapeDtypeStruct((n_tm_tiles * TOK_TILE, D_MODEL), f32)),
        compiler_params=pltpu.CompilerParams(
            dimension_semantics=("arbitrary",), vmem_limit_bytes=VMEM_LIMIT_BYTES),
        name="moe_combine",
    )(totc, ngrp, srck, h, rankm, locc, yf, yo, wsu, wsd, ln_g, ln_b)


def _round_up(x, m):
    return (x + m - 1) // m * m


def _moe(h, wr, rb, wgu, wd, wsu, wsd, ln_g, ln_b, *, bm_shape):
    n_tok = h.shape[0]
    n_tiles = n_tok // TOK_TILE
    max_rows = _round_up(n_tiles * (OVF_CHUNKS - 1) * RUN_ALIGN + N_EXPERTS * (ROW_TILE - 1), ROW_TILE)

    hext, rankm, cnt = _router(h, wr, rb)

    i32 = jnp.int32
    cnt = cnt[:, 0, :N_EXPERTS].astype(i32)
    oc = (jnp.maximum(cnt - FIX, 0) + RUN_ALIGN - 1) // RUN_ALIGN
    over_tiles = jnp.cumsum(oc, axis=0)
    region_rows = over_tiles[-1] * RUN_ALIGN
    region_size = _round_up(region_rows, ROW_TILE)
    region_start = jnp.cumsum(region_size) - region_size
    run_dst = region_start[None, :] + RUN_ALIGN * (over_tiles - oc)
    ch_end = jnp.cumsum(oc, axis=1)
    ch_beg = ch_end - oc
    totc = ch_end[:, -1]
    k = jnp.arange(OVF_CHUNKS, dtype=i32)
    kk = k[None, :, None]
    mine = jnp.logical_and(ch_beg[:, None, :] <= kk, kk < ch_end[:, None, :])
    chunk_dst = jnp.sum(jnp.where(mine, run_dst[:, None, :] + RUN_ALIGN * (kk - ch_beg[:, None, :]), 0),
                        axis=-1)
    live = k[None, :] < totc[:, None]
    tile_ids = jnp.arange(n_tiles, dtype=i32)[:, None]
    spare = max_rows + (tile_ids % 2) * OVF_SLOTS + k[None, :] * RUN_ALIGN
    dstk = jnp.where(live, chunk_dst, spare).astype(i32)
    srck = jnp.where(live, chunk_dst, 0).astype(i32)
    ngrp = ((totc + COPY_GROUP - 1) // COPY_GROUP).astype(i32)
    zst = (region_start + region_rows).astype(i32)
    zch = ((region_size - region_rows) // RUN_ALIGN).astype(i32)
    tile_start = jnp.concatenate([region_start, region_start[-1:] + region_size[-1:]]) // ROW_TILE
    tile_start = tile_start.astype(i32)

    pad = ((0, 0), (0, LANES - N_EXPERTS))
    loc_f = jnp.pad((FIX_SLOTS + RUN_ALIGN * ch_beg).astype(f32), pad, constant_values=NO_RUN)
    end_f = jnp.pad((FIX_SLOTS + RUN_ALIGN * ch_end).astype(f32), pad, constant_values=NO_RUN)
    locs = jnp.concatenate([loc_f[:, None, :], end_f[:, None, :],
                            jnp.zeros((n_tiles, SUBLANES - 2, LANES), f32)], axis=1)
    locc = jnp.stack([loc_f, end_f], axis=-1)

    totc = totc.astype(i32)
    overflow_rows = max_rows + 2 * OVF_SLOTS
    xf, xo = _dispatch(hext, rankm, locs, totc, ngrp, dstk, zst, zch, overflow_rows=overflow_rows)
    yf, yo = _experts(xf, xo, wgu, wd, tile_start, overflow_rows=max_rows)
    return _combine(h, rankm, locc, yf, yo, wsu, wsd, ln_g, ln_b, totc, ngrp, srck, bm_shape=bm_shape)


def _head_block_diag(w):
    heads_per_tile = MXU_DIM // RG_HEAD_DIM
    w4 = w.reshape(D_RG // MXU_DIM, heads_per_tile, RG_HEAD_DIM, RG_HEAD_DIM)
    eye = jnp.eye(heads_per_tile, dtype=w.dtype)
    return jnp.einsum('thij,hk->thikj', w4, eye).reshape(D_RG // MXU_DIM, MXU_DIM, MXU_DIM)


def _s5_in_tiles(b):
    gpt = S5_GROUPS // 2
    b4 = b.reshape(2, gpt, S5_STATE, S5_GROUP)
    eye = jnp.eye(gpt, dtype=b.dtype)
    return jnp.einsum('kgph,gm->kghmp', b4, eye).reshape(2, gpt * S5_GROUP, gpt * S5_STATE)


def _s5_out_tiles(cw):
    gpt = S5_GROUPS // 2
    c4 = cw.reshape(2, gpt, S5_GROUP, S5_STATE)
    eye = jnp.eye(gpt, dtype=cw.dtype)
    return jnp.einsum('kghp,gm->kgpmh', c4, eye).reshape(2, gpt * S5_STATE, gpt * S5_GROUP)


def _row(v):
    return v.reshape(1, -1)


def kernel(x_prompt, x_sample, state_rg_conv, state_rg_h, state_s5_re, state_s5_im, w_in, conv_w, conv_b, rg_w_a, rg_b_a, rg_w_i, rg_b_i, rg_lam, s5_a_re, s5_a_im, s5_log_dt, s5_b_re, s5_b_im, s5_c_re, s5_c_im, s5_d, w_glu, b_glu, w_out, ln1_g, ln1_b, w_router, router_bias, w_gate_up, w_down, w_shared_up, w_shared_down, ln2_g, ln2_b):
    l = 0
    bp, lp, _ = x_prompt.shape
    bs, ls, _ = x_sample.shape

    are, aim, bbre, bbim = _s5_prep(
        _row(s5_a_re[l]), _row(s5_a_im[l]),
        _row(jnp.repeat(s5_log_dt[l], S5_STATE)),
        _s5_in_tiles(s5_b_re[l]), _s5_in_tiles(s5_b_im[l]))
    params = (w_in[l].astype(bf16), conv_w[l], _row(conv_b[l]),
              _head_block_diag(rg_w_a[l]).astype(bf16), _row(rg_b_a[l]),
              _head_block_diag(rg_w_i[l]).astype(bf16), _row(rg_b_i[l]), _row(rg_lam[l]),
              are, aim, bbre, bbim,
              _s5_out_tiles(s5_c_re[l]).astype(bf16), _s5_out_tiles(s5_c_im[l]).astype(bf16),
              _row(s5_d[l]), w_glu[l].astype(bf16), _row(b_glu[l]), w_out[l].astype(bf16),
              _row(ln1_g[l]), _row(ln1_b[l]))

    tail = CONV_WIDTH - 1
    n_tok = lp * bp + ls * bs
    xs_tm = x_sample.transpose(1, 0, 2).reshape(ls * bs, D_MODEL)
    h_all, sc, sh, sre, sim = _mixer(
        xs_tm, state_rg_conv[l].transpose(1, 0, 2).reshape(tail * bs, D_RG), state_rg_h[l],
        state_s5_re[l].reshape(bs, S5_N), state_s5_im[l].reshape(bs, S5_N), params,
        nb=bs, tc=ls, name="mixer_sample",
        h_all=jnp.zeros((SUBLANES, LANES), f32), total_rows=n_tok, first_row=lp * bp)
    h_all, pc, ph, pre, pim = _mixer(
        x_prompt, jnp.zeros((tail * bp, D_RG), f32), jnp.zeros((bp, D_RG), f32),
        jnp.zeros((bp, S5_N), f32), jnp.zeros((bp, S5_N), f32), params,
        nb=bp, tc=PROMPT_CHUNK_ROWS // bp, name="mixer_prompt",
        h_all=h_all, total_rows=n_tok, first_row=0)
    wr = jnp.pad(w_router[l], ((0, 0), (0, LANES - N_EXPERTS))).astype(bf16)
    rb = jnp.pad(_row(router_bias[l]), ((0, 0), (0, LANES - N_EXPERTS)))
    yp, ys_tm = _moe(h_all, wr, rb, w_gate_up[l], w_down[l],
                     w_shared_up[l].astype(bf16), w_shared_down[l].astype(bf16),
                     _row(ln2_g[l]), _row(ln2_b[l]), bm_shape=x_prompt.shape)
    ys = ys_tm.reshape(ls, bs, D_MODEL).transpose(1, 0, 2)

    def conv_out(cv, nbatch):
        return cv.reshape(tail, nbatch, D_RG).transpose(1, 0, 2)[None]

    return (yp, ys,
            conv_out(pc, bp), ph[None],
            pre.reshape(1, bp, S5_GROUPS, S5_STATE), pim.reshape(1, bp, S5_GROUPS, S5_STATE),
            conv_out(sc, bs), sh[None],
            sre.reshape(1, bs, S5_GROUPS, S5_STATE), sim.reshape(1, bs, S5_GROUPS, S5_STATE))
```

```python
import functools
import math

import jax
import jax.numpy as jnp
from jax.experimental import pallas as pl
from jax.experimental.pallas import tpu as pltpu

D_MODEL = 1024
D_RG = 512
RG_HEADS = 8
RG_HEAD_DIM = 64
CONV_WIDTH = 4
RG_C = 8.0
D_S5 = 512
S5_GROUP = 16
S5_GROUPS = 32
S5_STATE = 64
S5_N = S5_GROUPS * S5_STATE
N_EXPERTS = 64
TOP_K = 8
D_EXPERT = 256
D_SHARED = 256
ROUTED_SCALE = 2.5
DEPTH = 1
DN_ALPHA = (2.0 * DEPTH) ** 0.25
LN_EPS = 1e-5

SUBLANES = 8
LANES = 128
MXU_DIM = 256
S5_SCAN_COLS = 512
PROMPT_CHUNK_ROWS = 512
VMEM_LIMIT_BYTES = 56 * 1024 * 1024

bf16 = jnp.bfloat16
f32 = jnp.float32


def _gelu_tanh(x):
    c = math.sqrt(2.0 / math.pi)
    return x * (0.5 * (1.0 + jnp.tanh(c * (x + 0.044715 * (x * x * x)))))


def _layer_norm(x, g, b):
    mu = jnp.mean(x, axis=-1, keepdims=True)
    xc = x - mu
    var = jnp.mean(xc * xc, axis=-1, keepdims=True)
    return xc * jax.lax.rsqrt(var + LN_EPS) * g + b


def _s5_prep_kernel(lr_ref, li_ref, ldt_ref, bre_ref, bim_ref,
                    are_ref, aim_ref, bbre_ref, bbim_ref):
    lr = lr_ref[...]
    li = li_ref[...]
    dt = jnp.exp(ldt_ref[...])
    mag = jnp.exp(lr * dt)
    abar_re = mag * jnp.cos(li * dt)
    abar_im = mag * jnp.sin(li * dt)
    den = lr * lr + li * li
    nr = abar_re - 1.0
    ni = abar_im
    coef_re = (nr * lr + ni * li) / den
    coef_im = (ni * lr - nr * li) / den
    are_ref[...] = abar_re
    aim_ref[...] = abar_im
    half = S5_N // 2
    for k in range(2):
        cre = coef_re[:, k * half:(k + 1) * half]
        cim = coef_im[:, k * half:(k + 1) * half]
        br = bre_ref[k]
        bi = bim_ref[k]
        bbre_ref[k] = (cre * br - cim * bi).astype(bf16)
        bbim_ref[k] = (cre * bi + cim * br).astype(bf16)


def _s5_prep(lr, li, ldt, bre_t, bim_t):
    half = S5_N // 2
    return pl.pallas_call(
        _s5_prep_kernel,
        out_shape=(jax.ShapeDtypeStruct((1, S5_N), f32),
                   jax.ShapeDtypeStruct((1, S5_N), f32),
                   jax.ShapeDtypeStruct((2, MXU_DIM, half), bf16),
                   jax.ShapeDtypeStruct((2, MXU_DIM, half), bf16)),
        name="s5_prep",
    )(lr, li, ldt, bre_t, bim_t)


def _mixer_kernel(h_all_ref, x_ref, conv0_ref, h0_ref, s0r_ref, s0i_ref,
                  w_in_ref, conv_w_ref, conv_b_ref, wa_ref, ba_ref, wi_ref, bi_ref, lam_ref,
                  are_ref, aim_ref, bbre_ref, bbim_ref, cre_ref, cim_ref, d_ref,
                  wglu_ref, bglu_ref, wout_ref, ln_g_ref, ln_b_ref,
                  hout_ref, conv_out_ref, hlast_ref, sre_out_ref, sim_out_ref,
                  pad_scr, a_scr, b_scr, bur_scr, bui_scr, hst_scr, sr_scr, si_scr,
                  *, nb, tc, batch_major_input):
    del h_all_ref
    rows = nb * tc
    tail = (CONV_WIDTH - 1) * nb
    c = pl.program_id(0)

    @pl.when(c == 0)
    def _():
        pad_scr[0:tail, :] = conv0_ref[...]
        hst_scr[...] = h0_ref[...]
        sr_scr[...] = s0r_ref[...]
        si_scr[...] = s0i_ref[...]

    if batch_major_input:
        x = jnp.transpose(x_ref[...], (1, 0, 2)).reshape(rows, D_MODEL)
    else:
        x = x_ref[...]
    proj = jnp.dot(x.astype(bf16), w_in_ref[...], preferred_element_type=f32)
    x_rg = proj[:, :D_RG]
    g_rg = proj[:, D_RG:2 * D_RG]
    u = proj[:, 2 * D_RG:]

    pad_scr[tail:tail + rows, :] = x_rg
    conv_w = conv_w_ref[...]
    acc = conv_w[0:1, :] * pad_scr[0:rows, :]
    for k in range(1, CONV_WIDTH):
        acc = acc + conv_w[k:k + 1, :] * pad_scr[k * nb:k * nb + rows, :]
    xc = conv_b_ref[...] + acc
    new_tail = pad_scr[rows:rows + tail, :]
    pad_scr[0:tail, :] = new_tail

    xcb = xc.astype(bf16)
    ga = []
    gi = []
    for hh in range(D_RG // MXU_DIM):
        xs = xcb[:, hh * MXU_DIM:(hh + 1) * MXU_DIM]
        ga.append(jnp.dot(xs, wa_ref[hh], preferred_element_type=f32))
        gi.append(jnp.dot(xs, wi_ref[hh], preferred_element_type=f32))
    r = jax.nn.sigmoid(jnp.concatenate(ga, axis=1) + ba_ref[...])
    i = jax.nn.sigmoid(jnp.concatenate(gi, axis=1) + bi_ref[...])
    nlam = -lam_ref[...]
    softplus = jnp.maximum(nlam, 0.0) + jnp.log1p(jnp.exp(-jnp.abs(nlam)))
    log_a = (-RG_C) * r * softplus
    a_scr[...] = jnp.exp(log_a)
    th = jnp.tanh(log_a)
    b_scr[...] = jnp.sqrt((-2.0 * th) / (1.0 - th)) * (i * xc)

    for rg in range(nb // SUBLANES):
        r0 = rg * SUBLANES
        h = hst_scr[r0:r0 + SUBLANES, :]
        for t in range(tc):
            q = t * nb + r0
            h = a_scr[q:q + SUBLANES, :] * h + b_scr[q:q + SUBLANES, :]
            b_scr[q:q + SUBLANES, :] = h
        hst_scr[r0:r0 + SUBLANES, :] = h
    y_rg = b_scr[...] * _gelu_tanh(g_rg)

    ub = u.astype(bf16)
    half = S5_N // 2
    for k in range(2):
        us = ub[:, k * MXU_DIM:(k + 1) * MXU_DIM]
        bur_scr[:, k * half:(k + 1) * half] = jnp.dot(us, bbre_ref[k], preferred_element_type=f32)
        bui_scr[:, k * half:(k + 1) * half] = jnp.dot(us, bbim_ref[k], preferred_element_type=f32)
    for rg in range(nb // SUBLANES):
        r0 = rg * SUBLANES
        for cb in range(S5_N // S5_SCAN_COLS):
            c0 = cb * S5_SCAN_COLS
            ar = jnp.broadcast_to(are_ref[:, c0:c0 + S5_SCAN_COLS], (SUBLANES, S5_SCAN_COLS))
            ai = jnp.broadcast_to(aim_ref[:, c0:c0 + S5_SCAN_COLS], (SUBLANES, S5_SCAN_COLS))
            xr = sr_scr[r0:r0 + SUBLANES, c0:c0 + S5_SCAN_COLS]
            xi = si_scr[r0:r0 + SUBLANES, c0:c0 + S5_SCAN_COLS]
            for t in range(tc):
                q = t * nb + r0
                br = bur_scr[q:q + SUBLANES, c0:c0 + S5_SCAN_COLS]
                bi_ = bui_scr[q:q + SUBLANES, c0:c0 + S5_SCAN_COLS]
                nxr = ar * xr - ai * xi + br
                nxi = ar * xi + ai * xr + bi_
                bur_scr[q:q + SUBLANES, c0:c0 + S5_SCAN_COLS] = nxr
                bui_scr[q:q + SUBLANES, c0:c0 + S5_SCAN_COLS] = nxi
                xr, xi = nxr, nxi
            sr_scr[r0:r0 + SUBLANES, c0:c0 + S5_SCAN_COLS] = xr
            si_scr[r0:r0 + SUBLANES, c0:c0 + S5_SCAN_COLS] = xi
    ys = []
    for j in range(D_S5 // MXU_DIM):
        xrb = bur_scr[:, j * half:(j + 1) * half].astype(bf16)
        xib = bui_scr[:, j * half:(j + 1) * half].astype(bf16)
        ys.append(jnp.dot(xrb, cre_ref[j], preferred_element_type=f32)
                  - jnp.dot(xib, cim_ref[j], preferred_element_type=f32))
    y_s5 = jnp.concatenate(ys, axis=1) + d_ref[...] * u
    yg = _gelu_tanh(y_s5)
    glu = jnp.dot(yg.astype(bf16), wglu_ref[...], preferred_element_type=f32) + bglu_ref[...]
    y_s5 = yg * jax.nn.sigmoid(glu)

    ycat = jnp.concatenate([y_rg, y_s5], axis=1).astype(bf16)
    mix = jnp.dot(ycat, wout_ref[...], preferred_element_type=f32)
    hout_ref[...] = _layer_norm(DN_ALPHA * x + mix, ln_g_ref[...], ln_b_ref[...])

    @pl.when(c == pl.num_programs(0) - 1)
    def _():
        conv_out_ref[...] = pad_scr[0:tail, :]
        hlast_ref[...] = hst_scr[...]
        sre_out_ref[...] = sr_scr[...]
        sim_out_ref[...] = si_scr[...]


def _full(shape):
    n = len(shape)
    return pl.BlockSpec(shape, lambda c: (0,) * n)


def _mixer(x, conv0, h0, s0r, s0i, params, *, nb, tc, name, h_all, total_rows, first_row):
    batch_major_input = x.ndim == 3
    rows = nb * tc
    if batch_major_input:
        n_chunks = x.shape[1] // tc
        x_spec = pl.BlockSpec((nb, tc, D_MODEL), lambda c: (0, c, 0))
    else:
        n_chunks = x.shape[0] // rows
        x_spec = pl.BlockSpec((rows, D_MODEL), lambda c: (c, 0))
    first_block = first_row // rows
    aliased = h_all.shape == (total_rows, D_MODEL)
    tail = (CONV_WIDTH - 1) * nb
    small = (conv0, h0, s0r, s0i) + tuple(params)
    in_specs = [pl.BlockSpec(memory_space=pl.ANY), x_spec]
    in_specs += [_full(a.shape) for a in small]
    out_shape = (jax.ShapeDtypeStruct((total_rows, D_MODEL), f32),
                 jax.ShapeDtypeStruct((tail, D_RG), f32),
                 jax.ShapeDtypeStruct((nb, D_RG), f32),
                 jax.ShapeDtypeStruct((nb, S5_N), f32),
                 jax.ShapeDtypeStruct((nb, S5_N), f32))
    out_specs = (pl.BlockSpec((rows, D_MODEL), lambda c: (c + first_block, 0)),
                 _full((tail, D_RG)), _full((nb, D_RG)), _full((nb, S5_N)), _full((nb, S5_N)))
    scratch = [pltpu.VMEM((rows + tail, D_RG), f32),
               pltpu.VMEM((rows, D_RG), f32),
               pltpu.VMEM((rows, D_RG), f32),
               pltpu.VMEM((rows, S5_N), f32),
               pltpu.VMEM((rows, S5_N), f32),
               pltpu.VMEM((nb, D_RG), f32),
               pltpu.VMEM((nb, S5_N), f32),
               pltpu.VMEM((nb, S5_N), f32)]
    return pl.pallas_call(
        functools.partial(_mixer_kernel, nb=nb, tc=tc, batch_major_input=batch_major_input),
        grid=(n_chunks,),
        in_specs=in_specs,
        out_specs=out_specs,
        out_shape=out_shape,
        scratch_shapes=scratch,
        input_output_aliases={0: 0} if aliased else {},
        compiler_params=pltpu.CompilerParams(
            dimension_semantics=("arbitrary",), vmem_limit_bytes=VMEM_LIMIT_BYTES),
        name=name,
    )(h_all, x, *small)


TOK_TILE = 256
RUN_ALIGN = 16
FIX = 32
FIX_SLOTS = N_EXPERTS * FIX
OVF_SLOTS = 2048
OVF_CHUNKS = OVF_SLOTS // RUN_ALIGN
SLOT_CHUNK = 512
FIX_ROWS_STEP = 528
ROW_TILE = 512
X_BUFFERS = 4
COPY_GROUP = 4
EXT = D_MODEL + LANES
ROUTER_TILES = 6
NO_RUN = 1.0e9


def _top_k_gates(scores, rb):
    rows = scores.shape[0]
    lane_f = jax.lax.broadcasted_iota(jnp.int32, (rows, LANES), 1).astype(f32)
    biased = jnp.where(lane_f < float(N_EXPERTS), scores + rb, -jnp.inf)
    sel = jnp.zeros((rows, LANES), f32)
    mask = jnp.zeros((rows, LANES), f32)
    for _ in range(TOP_K):
        m = jnp.max(biased, axis=1, keepdims=True)
        idx = jnp.min(jnp.where(biased == m, lane_f, float(LANES)), axis=1, keepdims=True)
        hit = lane_f == idx
        sel = jnp.where(hit, scores, sel)
        mask = jnp.where(hit, 1.0, mask)
        biased = jnp.where(hit, -jnp.inf, biased)
    gates = sel / jnp.sum(sel, axis=1, keepdims=True) * ROUTED_SCALE
    return mask, gates


def _router_kernel(h_ref, wr_ref, rb_ref, hext_ref, rankm_ref, cnt_ref):
    hb = h_ref[...].astype(bf16)
    scores = jax.nn.sigmoid(jnp.dot(hb, wr_ref[...], preferred_element_type=f32))
    mask, gates = _top_k_gates(scores, rb_ref[...])
    t_row = jax.lax.broadcasted_iota(jnp.int32, (TOK_TILE, TOK_TILE), 0)
    t_col = jax.lax.broadcasted_iota(jnp.int32, (TOK_TILE, TOK_TILE), 1)
    earlier = jnp.where(t_col < t_row, 1.0, 0.0).astype(bf16)
    for sub in range(ROUTER_TILES):
        m = mask[sub * TOK_TILE:(sub + 1) * TOK_TILE]
        rank = jnp.dot(earlier, m.astype(bf16), preferred_element_type=f32)
        rankm_ref[sub * TOK_TILE:(sub + 1) * TOK_TILE, :] = jnp.where(m > 0.0, rank, -1.0).astype(bf16)
        cnt_ref[sub] = jnp.broadcast_to(jnp.sum(m, axis=0, keepdims=True), (SUBLANES, LANES))
    g_hi = gates.astype(bf16).astype(f32)
    g_pack = g_hi + pltpu.roll(gates - g_hi, N_EXPERTS, 1)
    hext_ref[:, :D_MODEL] = hb
    hext_ref[:, D_MODEL:] = g_pack.astype(bf16)


def _router(h, wr, rb):
    n_tiles = h.shape[0] // TOK_TILE
    assert n_tiles % ROUTER_TILES == 0
    rows = ROUTER_TILES * TOK_TILE
    const = lambda shape: pl.BlockSpec(shape, lambda i: (0,) * len(shape))
    return pl.pallas_call(
        _router_kernel,
        grid=(n_tiles // ROUTER_TILES,),
        in_specs=[pl.BlockSpec((rows, D_MODEL), lambda i: (i, 0)), const(wr.shape), const(rb.shape)],
        out_specs=(pl.BlockSpec((rows, EXT), lambda i: (i, 0)),
                   pl.BlockSpec((rows, LANES), lambda i: (i, 0)),
                   pl.BlockSpec((ROUTER_TILES, SUBLANES, LANES), lambda i: (i, 0, 0))),
        out_shape=(jax.ShapeDtypeStruct((h.shape[0], EXT), bf16),
                   jax.ShapeDtypeStruct((h.shape[0], LANES), bf16),
                   jax.ShapeDtypeStruct((n_tiles, SUBLANES, LANES), f32)),
        compiler_params=pltpu.CompilerParams(
            dimension_semantics=("arbitrary",), vmem_limit_bytes=VMEM_LIMIT_BYTES),
        name="moe_router",
    )(h, wr, rb)


def _run_copy(src, dst, sem):
    return pltpu.make_async_copy(src, dst, sem)


def _fixed_ranges(shape, axis):
    e = jax.lax.broadcasted_iota(jnp.int32, shape, axis).astype(f32)
    loc = jnp.where(e < float(N_EXPERTS), e * float(FIX), NO_RUN)
    return loc, loc + float(FIX)


def _slot_rows(first_slot, loc_row, end_row, base, rankm):
    s = (jax.lax.broadcasted_iota(jnp.int32, (SLOT_CHUNK, LANES), 0) + first_slot).astype(f32)
    in_run = jnp.where(s >= loc_row, jnp.where(s < end_row, 1.0, 0.0), 0.0)
    r_col = s[:, 0:1] - jnp.sum(in_run * (loc_row - base), axis=1, keepdims=True)
    q = jax.lax.dot_general(in_run.astype(bf16), rankm, (((1,), (1,)), ((), ())),
                            preferred_element_type=f32)
    return jnp.where(q == r_col, 1.0, 0.0).astype(bf16)


def _slot_cols(first_slot, loc_col, end_col, base, rankm):
    s = (jax.lax.broadcasted_iota(jnp.int32, (LANES, SLOT_CHUNK), 1) + first_slot).astype(f32)
    in_run = jnp.where(s >= loc_col, jnp.where(s < end_col, 1.0, 0.0), 0.0)
    r_row = s[0:1, :] - jnp.sum(in_run * (loc_col - base), axis=0, keepdims=True)
    q = jnp.dot(rankm, in_run.astype(bf16), preferred_element_type=f32)
    return jnp.where(q == r_row, 1.0, 0.0).astype(bf16)


def _wait_groups(n_groups, make_copy):
    max_bits = (OVF_CHUNKS // COPY_GROUP).bit_length()
    for b in range(max_bits):
        rows = (COPY_GROUP * RUN_ALIGN) << b

        @pl.when(jnp.bitwise_and(jnp.right_shift(n_groups, b), 1) == 1)
        def _():
            make_copy(rows).wait()


def _dispatch_kernel(totc_ref, ngrp_ref, dstk_ref, zst_ref, zch_ref,
                     hext_ref, rankm_ref, locs_ref, sorted_f_ref, sorted_o_ref,
                     stage_f, stage_o, zbuf, semf, semo, zsem):
    i = pl.program_id(0)
    n = pl.num_programs(0)
    slot = i % 2
    rankm = rankm_ref[...]

    fx_loc, fx_end = _fixed_ranges((1, LANES), 1)
    experts_per_chunk = SLOT_CHUNK // FIX
    for ch in range(FIX_SLOTS // SLOT_CHUNK):
        p = _slot_rows(ch * SLOT_CHUNK, fx_loc, fx_end, 0.0, rankm)
        rows = jnp.dot(p, hext_ref[...], preferred_element_type=f32).astype(bf16)
        stage_f[slot, ch * experts_per_chunk:(ch + 1) * experts_per_chunk] = rows.reshape(
            experts_per_chunk, FIX, EXT)

    ov_loc = locs_ref[0, 0:1, :]
    ov_end = locs_ref[0, 1:2, :]
    for ch in range(OVF_SLOTS // SLOT_CHUNK):
        def sort_overflow(ch=ch):
            p = _slot_rows(FIX_SLOTS + ch * SLOT_CHUNK, ov_loc, ov_end, float(FIX), rankm)
            rows = jnp.dot(p, hext_ref[...], preferred_element_type=f32)
            stage_o[slot, ch * SLOT_CHUNK:(ch + 1) * SLOT_CHUNK, :] = rows.astype(bf16)
        pl.when(totc_ref[i] * RUN_ALIGN > ch * SLOT_CHUNK)(sort_overflow)

    def fixed_copy(tile, slot_):
        dst = sorted_f_ref.at[:, pl.ds(pl.multiple_of(tile * FIX, FIX), FIX), :]
        return _run_copy(stage_f.at[slot_], dst, semf.at[slot_])

    def overflow_wait(rows, slot_):
        return _run_copy(stage_o.at[slot_, pl.ds(0, rows)], sorted_o_ref.at[pl.ds(0, rows)],
                         semo.at[slot_])

    fixed_copy(i, slot).start()

    def start_group(g, carry):
        for u in range(COPY_GROUP):
            k = g * COPY_GROUP + u
            src = stage_o.at[slot, pl.ds(pl.multiple_of(k * RUN_ALIGN, RUN_ALIGN), RUN_ALIGN)]
            dst = sorted_o_ref.at[pl.ds(pl.multiple_of(dstk_ref[i, k], RUN_ALIGN), RUN_ALIGN)]
            _run_copy(src, dst, semo.at[slot]).start()
        return carry
    jax.lax.fori_loop(0, ngrp_ref[i], start_group, 0)

    @pl.when(i == 0)
    def _():
        zbuf[...] = jnp.zeros(zbuf.shape, bf16)

    def zero_copy(e, k):
        dst = sorted_o_ref.at[pl.ds(pl.multiple_of(zst_ref[e] + k * RUN_ALIGN, RUN_ALIGN), RUN_ALIGN)]
        return _run_copy(zbuf, dst, zsem.at[0])

    @pl.when(jnp.logical_and(i > 0, i <= N_EXPERTS))
    def _():
        def wait_zero(k, carry):
            zero_copy(i - 1, k).wait()
            return carry
        jax.lax.fori_loop(0, zch_ref[i - 1], wait_zero, 0)

    @pl.when(i < N_EXPERTS)
    def _():
        def start_zero(k, carry):
            zero_copy(i, k).start()
            return carry
        jax.lax.fori_loop(0, zch_ref[i], start_zero, 0)

    @pl.when(i > 0)
    def _():
        fixed_copy(i - 1, 1 - slot).wait()
        _wait_groups(ngrp_ref[i - 1], lambda rows: overflow_wait(rows, 1 - slot))

    @pl.when(i == n - 1)
    def _():
        fixed_copy(i, slot).wait()
        _wait_groups(ngrp_ref[i], lambda rows: overflow_wait(rows, slot))


def _dispatch(hext, rankm, locs, totc, ngrp, dstk, zst, zch, *, overflow_rows):
    n_tiles = hext.shape[0] // TOK_TILE
    assert n_tiles > N_EXPERTS
    grid_spec = pltpu.PrefetchScalarGridSpec(
        num_scalar_prefetch=5,
        grid=(n_tiles,),
        in_specs=[pl.BlockSpec((TOK_TILE, EXT), lambda i, *_: (i, 0)),
                  pl.BlockSpec((TOK_TILE, LANES), lambda i, *_: (i, 0)),
                  pl.BlockSpec((1, SUBLANES, LANES), lambda i, *_: (i, 0, 0))],
        out_specs=(pl.BlockSpec(memory_space=pl.ANY), pl.BlockSpec(memory_space=pl.ANY)),
        scratch_shapes=[pltpu.VMEM((2, N_EXPERTS, FIX, EXT), bf16),
                        pltpu.VMEM((2, OVF_SLOTS, EXT), bf16),
                        pltpu.VMEM((RUN_ALIGN, EXT), bf16),
                        pltpu.SemaphoreType.DMA((2,)),
                        pltpu.SemaphoreType.DMA((2,)),
                        pltpu.SemaphoreType.DMA((1,))])
    return pl.pallas_call(
        _dispatch_kernel,
        grid_spec=grid_spec,
        out_shape=(jax.ShapeDtypeStruct((N_EXPERTS, n_tiles * FIX, EXT), bf16),
                   jax.ShapeDtypeStruct((overflow_rows, EXT), bf16)),
        compiler_params=pltpu.CompilerParams(
            dimension_semantics=("arbitrary",), vmem_limit_bytes=VMEM_LIMIT_BYTES),
        name="moe_dispatch",
    )(totc, ngrp, dstk, zst, zch, hext, rankm, locs)


def _experts_kernel(ts_ref, xf_ref, xo_ref, wgu_ref, wd_ref, yf_ref, yo_ref,
                    xbuf, ybuf, wgu_b, wd_b, xsem, ysem):
    e = pl.program_id(0)
    n_valid = ts_ref[N_EXPERTS]
    wgu_b[...] = wgu_ref[0].astype(bf16)
    wd_b[...] = wd_ref[0].astype(bf16)

    def ffn(x_ext):
        m = x_ext.shape[0]
        gu = jnp.dot(x_ext[:, :D_MODEL], wgu_b[...], preferred_element_type=f32)
        act = (jax.nn.silu(gu[:, :D_EXPERT]) * gu[:, D_EXPERT:]).astype(bf16)
        y = jnp.dot(act, wd_b[...], preferred_element_type=f32)
        g_pack = x_ext[:, D_MODEL:].astype(f32)
        lane = jax.lax.broadcasted_iota(jnp.int32, (m, LANES), 1)
        mine = jnp.where(lane == e, g_pack, jnp.where(lane == e + N_EXPERTS, g_pack, 0.0))
        return (y * jnp.sum(mine, axis=1, keepdims=True)).astype(bf16)

    def x_copy(j, slot):
        rows = pl.ds(pl.multiple_of(j * ROW_TILE, ROW_TILE), ROW_TILE)
        return pltpu.make_async_copy(xo_ref.at[rows], xbuf.at[slot], xsem.at[slot])

    def y_copy(j, slot):
        rows = pl.ds(pl.multiple_of(j * ROW_TILE, ROW_TILE), ROW_TILE)
        return pltpu.make_async_copy(ybuf.at[slot], yo_ref.at[rows], ysem.at[slot])

    ahead = X_BUFFERS - 1

    @pl.when(e == 0)
    def _():
        for a in range(ahead):
            @pl.when(a < n_valid)
            def _():
                x_copy(a, a).start()

    for c in range(xf_ref.shape[1] // FIX_ROWS_STEP):
        rows = slice(c * FIX_ROWS_STEP, (c + 1) * FIX_ROWS_STEP)
        yf_ref[0, rows, :] = ffn(xf_ref[0, rows, :])

    def tile(j, carry):
        xslot = j % X_BUFFERS
        slot = j % 2
        x_copy(j, xslot).wait()

        @pl.when(j + ahead < n_valid)
        def _():
            x_copy(j + ahead, (j + ahead) % X_BUFFERS).start()

        @pl.when(j >= 2)
        def _():
            y_copy(j - 2, slot).wait()

        ybuf[slot] = ffn(xbuf[xslot])
        y_copy(j, slot).start()
        return carry
    jax.lax.fori_loop(ts_ref[e], ts_ref[e + 1], tile, 0)

    @pl.when(e == pl.num_programs(0) - 1)
    def _():
        @pl.when(n_valid >= 2)
        def _():
            y_copy(n_valid - 2, n_valid % 2).wait()

        @pl.when(n_valid >= 1)
        def _():
            y_copy(n_valid - 1, (n_valid - 1) % 2).wait()


def _experts(xf, xo, wgu, wd, ts, *, overflow_rows):
    fixed_rows = xf.shape[1]
    assert fixed_rows % FIX_ROWS_STEP == 0
    grid_spec = pltpu.PrefetchScalarGridSpec(
        num_scalar_prefetch=1,
        grid=(N_EXPERTS,),
        in_specs=[pl.BlockSpec((1, fixed_rows, EXT), lambda e, ts: (e, 0, 0)),
                  pl.BlockSpec(memory_space=pl.ANY),
                  pl.BlockSpec((1, D_MODEL, 2 * D_EXPERT), lambda e, ts: (e, 0, 0)),
                  pl.BlockSpec((1, D_EXPERT, D_MODEL), lambda e, ts: (e, 0, 0))],
        out_specs=(pl.BlockSpec((1, fixed_rows, D_MODEL), lambda e, ts: (e, 0, 0)),
                   pl.BlockSpec(memory_space=pl.ANY)),
        scratch_shapes=[pltpu.VMEM((X_BUFFERS, ROW_TILE, EXT), bf16),
                        pltpu.VMEM((2, ROW_TILE, D_MODEL), bf16),
                        pltpu.VMEM((D_MODEL, 2 * D_EXPERT), bf16),
                        pltpu.VMEM((D_EXPERT, D_MODEL), bf16),
                        pltpu.SemaphoreType.DMA((X_BUFFERS,)),
                        pltpu.SemaphoreType.DMA((2,))])
    return pl.pallas_call(
        _experts_kernel,
        grid_spec=grid_spec,
        out_shape=(jax.ShapeDtypeStruct((N_EXPERTS, fixed_rows, D_MODEL), bf16),
                   jax.ShapeDtypeStruct((overflow_rows, D_MODEL), bf16)),
        compiler_params=pltpu.CompilerParams(
            dimension_semantics=("arbitrary",), vmem_limit_bytes=VMEM_LIMIT_BYTES),
        name="moe_experts",
    )(ts, xf, xo, wgu, wd)


def _combine_kernel(totc_ref, ngrp_ref, srck_ref,
                    h_ref, rankm_ref, locc_ref, yf_ref, yo_ref, wsu_ref, wsd_ref, ln_g_ref, ln_b_ref,
                    out_bm_ref, out_tm_ref, yloc_f, yloc_o, acc_scr, p_scr, semf, semo,
                    *, n_bm_tiles):
    i = pl.program_id(0)
    n = pl.num_programs(0)
    slot = i % 2

    def fixed_copy(tile, slot_):
        src = yf_ref.at[:, pl.ds(pl.multiple_of(tile * FIX, FIX), FIX), :]
        return _run_copy(src, yloc_f.at[slot_], semf.at[slot_])

    def overflow_wait(rows, slot_):
        return _run_copy(yo_ref.at[pl.ds(0, rows)], yloc_o.at[slot_, pl.ds(0, rows)], semo.at[slot_])

    def fetch(tile, slot_):
        fixed_copy(tile, slot_).start()

        def start_group(g, carry):
            for u in range(COPY_GROUP):
                k = g * COPY_GROUP + u
                src = yo_ref.at[pl.ds(pl.multiple_of(srck_ref[tile, k], RUN_ALIGN), RUN_ALIGN)]
                dst = yloc_o.at[slot_, pl.ds(pl.multiple_of(k * RUN_ALIGN, RUN_ALIGN), RUN_ALIGN)]
                _run_copy(src, dst, semo.at[slot_]).start()
            return carry
        jax.lax.fori_loop(0, ngrp_ref[tile], start_group, 0)

    @pl.when(i == 0)
    def _():
        yloc_o[...] = jnp.zeros(yloc_o.shape, bf16)
        fetch(0, 0)

    @pl.when(i + 1 < n)
    def _():
        fetch(i + 1, 1 - slot)

    h = h_ref[...]
    hb = h.astype(bf16)
    su = jnp.dot(hb, wsu_ref[...], preferred_element_type=f32)
    act = (jax.nn.silu(su[:, :D_SHARED]) * su[:, D_SHARED:]).astype(bf16)
    acc_scr[...] = jnp.dot(act, wsd_ref[...], preferred_element_type=f32)

    rankm = rankm_ref[...]
    fx_loc, fx_end = _fixed_ranges((LANES, 1), 0)
    for ch in range(FIX_SLOTS // SLOT_CHUNK):
        p_scr[:, ch * SLOT_CHUNK:(ch + 1) * SLOT_CHUNK] = _slot_cols(
            ch * SLOT_CHUNK, fx_loc, fx_end, 0.0, rankm)

    fixed_copy(i, slot).wait()
    _wait_groups(ngrp_ref[i], lambda rows: overflow_wait(rows, slot))

    acc_scr[...] += jnp.dot(p_scr[...], yloc_f[slot].reshape(FIX_SLOTS, D_MODEL),
                            preferred_element_type=f32)
    ov_loc = locc_ref[0, :, 0:1]
    ov_end = locc_ref[0, :, 1:2]
    for ch in range(OVF_SLOTS // SLOT_CHUNK):
        def gather_overflow(ch=ch):
            p = _slot_cols(FIX_SLOTS + ch * SLOT_CHUNK, ov_loc, ov_end, float(FIX), rankm)
            acc_scr[...] += jnp.dot(p, yloc_o[slot, ch * SLOT_CHUNK:(ch + 1) * SLOT_CHUNK, :],
                                    preferred_element_type=f32)
        pl.when(totc_ref[i] * RUN_ALIGN > ch * SLOT_CHUNK)(gather_overflow)
    y = _layer_norm(DN_ALPHA * h + acc_scr[...], ln_g_ref[...], ln_b_ref[...])

    @pl.when(i < n_bm_tiles)
    def _():
        nb, tc, _ = out_bm_ref.shape
        out_bm_ref[...] = jnp.transpose(y.reshape(tc, nb, D_MODEL), (1, 0, 2))

    @pl.when(i >= n_bm_tiles)
    def _():
        out_tm_ref[...] = y


def _combine(h, rankm, locc, yf, yo, wsu, wsd, ln_g, ln_b, totc, ngrp, srck, *, bm_shape):
    n_tiles = h.shape[0] // TOK_TILE
    nb, length, _ = bm_shape
    tc = TOK_TILE // nb
    n_bm_tiles = length // tc
    n_tm_tiles = n_tiles - n_bm_tiles
    const = lambda shape: pl.BlockSpec(shape, lambda i, *_: (0,) * len(shape))
    grid_spec = pltpu.PrefetchScalarGridSpec(
        num_scalar_prefetch=3,
        grid=(n_tiles,),
        in_specs=[pl.BlockSpec((TOK_TILE, D_MODEL), lambda i, *_: (i, 0)),
                  pl.BlockSpec((TOK_TILE, LANES), lambda i, *_: (i, 0)),
                  pl.BlockSpec((1, LANES, 2), lambda i, *_: (i, 0, 0)),
                  pl.BlockSpec(memory_space=pl.ANY),
                  pl.BlockSpec(memory_space=pl.ANY),
                  const(wsu.shape), const(wsd.shape), const(ln_g.shape), const(ln_b.shape)],
        out_specs=(pl.BlockSpec((nb, tc, D_MODEL),
                                lambda i, *_: (0, jnp.minimum(i, n_bm_tiles - 1), 0)),
                   pl.BlockSpec((TOK_TILE, D_MODEL),
                                lambda i, *_: (jnp.maximum(i - n_bm_tiles, 0), 0))),
        scratch_shapes=[pltpu.VMEM((2, N_EXPERTS, FIX, D_MODEL), bf16),
                        pltpu.VMEM((2, OVF_SLOTS, D_MODEL), bf16),
                        pltpu.VMEM((TOK_TILE, D_MODEL), f32),
                        pltpu.VMEM((TOK_TILE, FIX_SLOTS), bf16),
                        pltpu.SemaphoreType.DMA((2,)),
                        pltpu.SemaphoreType.DMA((2,))])
    return pl.pallas_call(
        functools.partial(_combine_kernel, n_bm_tiles=n_bm_tiles),
        grid_spec=grid_spec,
        out_shape=(jax.ShapeDtypeStruct(bm_shape, f32),
                   jax.ShapeDtypeStruct((n_tm_tiles * TOK_TILE, D_MODEL), f32)),
        compiler_params=pltpu.CompilerParams(
            dimension_semantics=("arbitrary",), vmem_limit_bytes=VMEM_LIMIT_BYTES),
        name="moe_combine",
    )(totc, ngrp, srck, h, rankm, locc, yf, yo, wsu, wsd, ln_g, ln_b)


def _round_up(x, m):
    return (x + m - 1) // m * m


def _moe(h, wr, rb, wgu, wd, wsu, wsd, ln_g, ln_b, *, bm_shape):
    n_tok = h.shape[0]
    n_tiles = n_tok // TOK_TILE
    max_rows = _round_up(n_tiles * (OVF_CHUNKS - 1) * RUN_ALIGN + N_EXPERTS * (ROW_TILE - 1), ROW_TILE)

    hext, rankm, cnt = _router(h, wr, rb)

    i32 = jnp.int32
    cnt = cnt[:, 0, :N_EXPERTS].astype(i32)
    oc = (jnp.maximum(cnt - FIX, 0) + RUN_ALIGN - 1) // RUN_ALIGN
    over_tiles = jnp.cumsum(oc, axis=0)
    region_rows = over_tiles[-1] * RUN_ALIGN
    region_size = _round_up(region_rows, ROW_TILE)
    region_start = jnp.cumsum(region_size) - region_size
    run_dst = region_start[None, :] + RUN_ALIGN * (over_tiles - oc)
    ch_end = jnp.cumsum(oc, axis=1)
    ch_beg = ch_end - oc
    totc = ch_end[:, -1]
    k = jnp.arange(OVF_CHUNKS, dtype=i32)
    kk = k[None, :, None]
    mine = jnp.logical_and(ch_beg[:, None, :] <= kk, kk < ch_end[:, None, :])
    chunk_dst = jnp.sum(jnp.where(mine, run_dst[:, None, :] + RUN_ALIGN * (kk - ch_beg[:, None, :]), 0),
                        axis=-1)
    live = k[None, :] < totc[:, None]
    tile_ids = jnp.arange(n_tiles, dtype=i32)[:, None]
    spare = max_rows + (tile_ids % 2) * OVF_SLOTS + k[None, :] * RUN_ALIGN
    dstk = jnp.where(live, chunk_dst, spare).astype(i32)
    srck = jnp.where(live, chunk_dst, 0).astype(i32)
    ngrp = ((totc + COPY_GROUP - 1) // COPY_GROUP).astype(i32)
    zst = (region_start + region_rows).astype(i32)
    zch = ((region_size - region_rows) // RUN_ALIGN).astype(i32)
    tile_start = jnp.concatenate([region_start, region_start[-1:] + region_size[-1:]]) // ROW_TILE
    tile_start = tile_start.astype(i32)

    pad = ((0, 0), (0, LANES - N_EXPERTS))
    loc_f = jnp.pad((FIX_SLOTS + RUN_ALIGN * ch_beg).astype(f32), pad, constant_values=NO_RUN)
    end_f = jnp.pad((FIX_SLOTS + RUN_ALIGN * ch_end).astype(f32), pad, constant_values=NO_RUN)
    locs = jnp.concatenate([loc_f[:, None, :], end_f[:, None, :],
                            jnp.zeros((n_tiles, SUBLANES - 2, LANES), f32)], axis=1)
    locc = jnp.stack([loc_f, end_f], axis=-1)

    totc = totc.astype(i32)
    overflow_rows = max_rows + 2 * OVF_SLOTS
    xf, xo = _dispatch(hext, rankm, locs, totc, ngrp, dstk, zst, zch, overflow_rows=overflow_rows)
    yf, yo = _experts(xf, xo, wgu, wd, tile_start, overflow_rows=max_rows)
    return _combine(h, rankm, locc, yf, yo, wsu, wsd, ln_g, ln_b, totc, ngrp, srck, bm_shape=bm_shape)


def _diag_tiles(blocks, n_tiles):
    n_blocks, r, c = blocks.shape
    per = n_blocks // n_tiles
    wide = jnp.tile(blocks.reshape(n_blocks * r, c), (1, per))
    row_block = (jnp.arange(n_blocks * r) // r) % per
    col_block = jnp.arange(per * c) // c
    wide = jnp.where(row_block[:, None] == col_block[None, :], wide, 0)
    return wide.reshape(n_tiles, per * r, per * c)


def _head_block_diag(w):
    return _diag_tiles(w, D_RG // MXU_DIM)


def _s5_in_tiles(b):
    return _diag_tiles(b.transpose(0, 2, 1), 2)


def _s5_out_tiles(cw):
    return _diag_tiles(cw.transpose(0, 2, 1), 2)


def _row(v):
    return v.reshape(1, -1)


def kernel(x_prompt, x_sample, state_rg_conv, state_rg_h, state_s5_re, state_s5_im, w_in, conv_w, conv_b, rg_w_a, rg_b_a, rg_w_i, rg_b_i, rg_lam, s5_a_re, s5_a_im, s5_log_dt, s5_b_re, s5_b_im, s5_c_re, s5_c_im, s5_d, w_glu, b_glu, w_out, ln1_g, ln1_b, w_router, router_bias, w_gate_up, w_down, w_shared_up, w_shared_down, ln2_g, ln2_b):
    l = 0
    bp, lp, _ = x_prompt.shape
    bs, ls, _ = x_sample.shape

    are, aim, bbre, bbim = _s5_prep(
        _row(s5_a_re[l]), _row(s5_a_im[l]),
        _row(jnp.repeat(s5_log_dt[l], S5_STATE)),
        _s5_in_tiles(s5_b_re[l]), _s5_in_tiles(s5_b_im[l]))
    params = (w_in[l].astype(bf16), conv_w[l], _row(conv_b[l]),
              _head_block_diag(rg_w_a[l]).astype(bf16), _row(rg_b_a[l]),
              _head_block_diag(rg_w_i[l]).astype(bf16), _row(rg_b_i[l]), _row(rg_lam[l]),
              are, aim, bbre, bbim,
              _s5_out_tiles(s5_c_re[l]).astype(bf16), _s5_out_tiles(s5_c_im[l]).astype(bf16),
              _row(s5_d[l]), w_glu[l].astype(bf16), _row(b_glu[l]), w_out[l].astype(bf16),
              _row(ln1_g[l]), _row(ln1_b[l]))

    tail = CONV_WIDTH - 1
    n_tok = lp * bp + ls * bs
    xs_tm = x_sample.transpose(1, 0, 2).reshape(ls * bs, D_MODEL)
    h_all, sc, sh, sre, sim = _mixer(
        xs_tm, state_rg_conv[l].transpose(1, 0, 2).reshape(tail * bs, D_RG), state_rg_h[l],
        state_s5_re[l].reshape(bs, S5_N), state_s5_im[l].reshape(bs, S5_N), params,
        nb=bs, tc=ls, name="mixer_sample",
        h_all=jnp.zeros((SUBLANES, LANES), f32), total_rows=n_tok, first_row=lp * bp)
    h_all, pc, ph, pre, pim = _mixer(
        x_prompt, jnp.zeros((tail * bp, D_RG), f32), jnp.zeros((bp, D_RG), f32),
        jnp.zeros((bp, S5_N), f32), jnp.zeros((bp, S5_N), f32), params,
        nb=bp, tc=PROMPT_CHUNK_ROWS // bp, name="mixer_prompt",
        h_all=h_all, total_rows=n_tok, first_row=0)
    wr = jnp.pad(w_router[l], ((0, 0), (0, LANES - N_EXPERTS))).astype(bf16)
    rb = jnp.pad(_row(router_bias[l]), ((0, 0), (0, LANES - N_EXPERTS)))
    yp, ys_tm = _moe(h_all, wr, rb, w_gate_up[l], w_down[l],
                     w_shared_up[l].astype(bf16), w_shared_down[l].astype(bf16),
                     _row(ln2_g[l]), _row(ln2_b[l]), bm_shape=x_prompt.shape)
    ys = ys_tm.reshape(ls, bs, D_MODEL).transpose(1, 0, 2)

    def conv_out(cv, nbatch):
        return cv.reshape(tail, nbatch, D_RG).transpose(1, 0, 2)[None]

    return (yp, ys,
            conv_out(pc, bp), ph[None],
            pre.reshape(1, bp, S5_GROUPS, S5_STATE), pim.reshape(1, bp, S5_GROUPS, S5_STATE),
            conv_out(sc, bs), sh[None],
            sre.reshape(1, bs, S5_GROUPS, S5_STATE), sim.reshape(1, bs, S5_GROUPS, S5_STATE))
```

```python
import functools
import math

import jax
import jax.numpy as jnp
from jax.experimental import pallas as pl
from jax.experimental.pallas import tpu as pltpu

D_MODEL = 1024
D_RG = 512
RG_HEADS = 8
RG_HEAD_DIM = 64
CONV_WIDTH = 4
RG_C = 8.0
D_S5 = 512
S5_GROUP = 16
S5_GROUPS = 32
S5_STATE = 64
S5_N = S5_GROUPS * S5_STATE
N_EXPERTS = 64
TOP_K = 8
D_EXPERT = 256
D_SHARED = 256
ROUTED_SCALE = 2.5
DEPTH = 1
DN_ALPHA = (2.0 * DEPTH) ** 0.25
LN_EPS = 1e-5

SUBLANES = 8
LANES = 128
MXU_DIM = 256
S5_SCAN_COLS = 512
PROMPT_CHUNK_ROWS = 512
VMEM_LIMIT_BYTES = 56 * 1024 * 1024

bf16 = jnp.bfloat16
f32 = jnp.float32


def _gelu_tanh(x):
    c = math.sqrt(2.0 / math.pi)
    return x * (0.5 * (1.0 + jnp.tanh(c * (x + 0.044715 * (x * x * x)))))


def _layer_norm(x, g, b):
    mu = jnp.mean(x, axis=-1, keepdims=True)
    xc = x - mu
    var = jnp.mean(xc * xc, axis=-1, keepdims=True)
    return xc * jax.lax.rsqrt(var + LN_EPS) * g + b


def _s5_prep_kernel(lr_ref, li_ref, ldt_ref, bre_ref, bim_ref,
                    are_ref, aim_ref, bbre_ref, bbim_ref):
    lr = lr_ref[...]
    li = li_ref[...]
    dt = jnp.exp(ldt_ref[...])
    mag = jnp.exp(lr * dt)
    abar_re = mag * jnp.cos(li * dt)
    abar_im = mag * jnp.sin(li * dt)
    den = lr * lr + li * li
    nr = abar_re - 1.0
    ni = abar_im
    coef_re = (nr * lr + ni * li) / den
    coef_im = (ni * lr - nr * li) / den
    are_ref[...] = abar_re
    aim_ref[...] = abar_im
    half = S5_N // 2
    for k in range(2):
        cre = coef_re[:, k * half:(k + 1) * half]
        cim = coef_im[:, k * half:(k + 1) * half]
        br = bre_ref[k]
        bi = bim_ref[k]
        bbre_ref[k] = (cre * br - cim * bi).astype(bf16)
        bbim_ref[k] = (cre * bi + cim * br).astype(bf16)


def _s5_prep(lr, li, ldt, bre_t, bim_t):
    half = S5_N // 2
    return pl.pallas_call(
        _s5_prep_kernel,
        out_shape=(jax.ShapeDtypeStruct((1, S5_N), f32),
                   jax.ShapeDtypeStruct((1, S5_N), f32),
                   jax.ShapeDtypeStruct((2, MXU_DIM, half), bf16),
                   jax.ShapeDtypeStruct((2, MXU_DIM, half), bf16)),
        name="s5_prep",
    )(lr, li, ldt, bre_t, bim_t)


def _mixer_kernel(h_all_ref, x_ref, conv0_ref, h0_ref, s0r_ref, s0i_ref,
                  w_in_ref, conv_w_ref, conv_b_ref, wa_ref, ba_ref, wi_ref, bi_ref, lam_ref,
                  are_ref, aim_ref, bbre_ref, bbim_ref, cre_ref, cim_ref, d_ref,
                  wglu_ref, bglu_ref, wout_ref, ln_g_ref, ln_b_ref,
                  hout_ref, conv_out_ref, hlast_ref, sre_out_ref, sim_out_ref,
                  pad_scr, a_scr, b_scr, bur_scr, bui_scr, hst_scr, sr_scr, si_scr,
                  *, nb, tc, batch_major_input):
    del h_all_ref
    rows = nb * tc
    tail = (CONV_WIDTH - 1) * nb
    c = pl.program_id(0)

    @pl.when(c == 0)
    def _():
        pad_scr[0:tail, :] = conv0_ref[...]
        hst_scr[...] = h0_ref[...]
        sr_scr[...] = s0r_ref[...]
        si_scr[...] = s0i_ref[...]

    if batch_major_input:
        x = jnp.transpose(x_ref[...], (1, 0, 2)).reshape(rows, D_MODEL)
    else:
        x = x_ref[...]
    proj = jnp.dot(x.astype(bf16), w_in_ref[...], preferred_element_type=f32)
    x_rg = proj[:, :D_RG]
    g_rg = proj[:, D_RG:2 * D_RG]
    u = proj[:, 2 * D_RG:]

    pad_scr[tail:tail + rows, :] = x_rg
    conv_w = conv_w_ref[...]
    acc = conv_w[0:1, :] * pad_scr[0:rows, :]
    for k in range(1, CONV_WIDTH):
        acc = acc + conv_w[k:k + 1, :] * pad_scr[k * nb:k * nb + rows, :]
    xc = conv_b_ref[...] + acc
    new_tail = pad_scr[rows:rows + tail, :]
    pad_scr[0:tail, :] = new_tail

    xcb = xc.astype(bf16)
    ga = []
    gi = []
    for hh in range(D_RG // MXU_DIM):
        xs = xcb[:, hh * MXU_DIM:(hh + 1) * MXU_DIM]
        ga.append(jnp.dot(xs, wa_ref[hh], preferred_element_type=f32))
        gi.append(jnp.dot(xs, wi_ref[hh], preferred_element_type=f32))
    r = jax.nn.sigmoid(jnp.concatenate(ga, axis=1) + ba_ref[...])
    i = jax.nn.sigmoid(jnp.concatenate(gi, axis=1) + bi_ref[...])
    nlam = -lam_ref[...]
    softplus = jnp.maximum(nlam, 0.0) + jnp.log1p(jnp.exp(-jnp.abs(nlam)))
    log_a = (-RG_C) * r * softplus
    a_scr[...] = jnp.exp(log_a)
    th = jnp.tanh(log_a)
    b_scr[...] = jnp.sqrt((-2.0 * th) / (1.0 - th)) * (i * xc)

    for rg in range(nb // SUBLANES):
        r0 = rg * SUBLANES
        h = hst_scr[r0:r0 + SUBLANES, :]
        for t in range(tc):
            q = t * nb + r0
            h = a_scr[q:q + SUBLANES, :] * h + b_scr[q:q + SUBLANES, :]
            b_scr[q:q + SUBLANES, :] = h
        hst_scr[r0:r0 + SUBLANES, :] = h
    y_rg = b_scr[...] * _gelu_tanh(g_rg)

    ub = u.astype(bf16)
    half = S5_N // 2
    for k in range(2):
        us = ub[:, k * MXU_DIM:(k + 1) * MXU_DIM]
        bur_scr[:, k * half:(k + 1) * half] = jnp.dot(us, bbre_ref[k], preferred_element_type=f32)
        bui_scr[:, k * half:(k + 1) * half] = jnp.dot(us, bbim_ref[k], preferred_element_type=f32)
    for rg in range(nb // SUBLANES):
        r0 = rg * SUBLANES
        for cb in range(S5_N // S5_SCAN_COLS):
            c0 = cb * S5_SCAN_COLS
            ar = jnp.broadcast_to(are_ref[:, c0:c0 + S5_SCAN_COLS], (SUBLANES, S5_SCAN_COLS))
            ai = jnp.broadcast_to(aim_ref[:, c0:c0 + S5_SCAN_COLS], (SUBLANES, S5_SCAN_COLS))
            xr = sr_scr[r0:r0 + SUBLANES, c0:c0 + S5_SCAN_COLS]
            xi = si_scr[r0:r0 + SUBLANES, c0:c0 + S5_SCAN_COLS]
            for t in range(tc):
                q = t * nb + r0
                br = bur_scr[q:q + SUBLANES, c0:c0 + S5_SCAN_COLS]
                bi_ = bui_scr[q:q + SUBLANES, c0:c0 + S5_SCAN_COLS]
                nxr = ar * xr - ai * xi + br
                nxi = ar * xi + ai * xr + bi_
                bur_scr[q:q + SUBLANES, c0:c0 + S5_SCAN_COLS] = nxr
                bui_scr[q:q + SUBLANES, c0:c0 + S5_SCAN_COLS] = nxi
                xr, xi = nxr, nxi
            sr_scr[r0:r0 + SUBLANES, c0:c0 + S5_SCAN_COLS] = xr
            si_scr[r0:r0 + SUBLANES, c0:c0 + S5_SCAN_COLS] = xi
    ys = []
    for j in range(D_S5 // MXU_DIM):
        xrb = bur_scr[:, j * half:(j + 1) * half].astype(bf16)
        xib = bui_scr[:, j * half:(j + 1) * half].astype(bf16)
        ys.append(jnp.dot(xrb, cre_ref[j], preferred_element_type=f32)
                  - jnp.dot(xib, cim_ref[j], preferred_element_type=f32))
    y_s5 = jnp.concatenate(ys, axis=1) + d_ref[...] * u
    yg = _gelu_tanh(y_s5)
    glu = jnp.dot(yg.astype(bf16), wglu_ref[...], preferred_element_type=f32) + bglu_ref[...]
    y_s5 = yg * jax.nn.sigmoid(glu)

    ycat = jnp.concatenate([y_rg, y_s5], axis=1).astype(bf16)
    mix = jnp.dot(ycat, wout_ref[...], preferred_element_type=f32)
    hout_ref[...] = _layer_norm(DN_ALPHA * x + mix, ln_g_ref[...], ln_b_ref[...])

    @pl.when(c == pl.num_programs(0) - 1)
    def _():
        conv_out_ref[...] = pad_scr[0:tail, :]
        hlast_ref[...] = hst_scr[...]
        sre_out_ref[...] = sr_scr[...]
        sim_out_ref[...] = si_scr[...]


def _full(shape):
    n = len(shape)
    return pl.BlockSpec(shape, lambda c: (0,) * n)


def _mixer(x, conv0, h0, s0r, s0i, params, *, nb, tc, name, h_all, total_rows, first_row):
    batch_major_input = x.ndim == 3
    rows = nb * tc
    if batch_major_input:
        n_chunks = x.shape[1] // tc
        x_spec = pl.BlockSpec((nb, tc, D_MODEL), lambda c: (0, c, 0))
    else:
        n_chunks = x.shape[0] // rows
        x_spec = pl.BlockSpec((rows, D_MODEL), lambda c: (c, 0))
    first_block = first_row // rows
    aliased = h_all.shape == (total_rows, D_MODEL)
    tail = (CONV_WIDTH - 1) * nb
    small = (conv0, h0, s0r, s0i) + tuple(params)
    in_specs = [pl.BlockSpec(memory_space=pl.ANY), x_spec]
    in_specs += [_full(a.shape) for a in small]
    out_shape = (jax.ShapeDtypeStruct((total_rows, D_MODEL), f32),
                 jax.ShapeDtypeStruct((tail, D_RG), f32),
                 jax.ShapeDtypeStruct((nb, D_RG), f32),
                 jax.ShapeDtypeStruct((nb, S5_N), f32),
                 jax.ShapeDtypeStruct((nb, S5_N), f32))
    out_specs = (pl.BlockSpec((rows, D_MODEL), lambda c: (c + first_block, 0)),
                 _full((tail, D_RG)), _full((nb, D_RG)), _full((nb, S5_N)), _full((nb, S5_N)))
    scratch = [pltpu.VMEM((rows + tail, D_RG), f32),
               pltpu.VMEM((rows, D_RG), f32),
               pltpu.VMEM((rows, D_RG), f32),
               pltpu.VMEM((rows, S5_N), f32),
               pltpu.VMEM((rows, S5_N), f32),
               pltpu.VMEM((nb, D_RG), f32),
               pltpu.VMEM((nb, S5_N), f32),
               pltpu.VMEM((nb, S5_N), f32)]
    return pl.pallas_call(
        functools.partial(_mixer_kernel, nb=nb, tc=tc, batch_major_input=batch_major_input),
        grid=(n_chunks,),
        in_specs=in_specs,
        out_specs=out_specs,
        out_shape=out_shape,
        scratch_shapes=scratch,
        input_output_aliases={0: 0} if aliased else {},
        compiler_params=pltpu.CompilerParams(
            dimension_semantics=("arbitrary",), vmem_limit_bytes=VMEM_LIMIT_BYTES),
        name=name,
    )(h_all, x, *small)


TOK_TILE = 256
RUN_ALIGN = 16
FIX = 32
FIX_SLOTS = N_EXPERTS * FIX
OVF_SLOTS = 2048
OVF_CHUNKS = OVF_SLOTS // RUN_ALIGN
SLOT_CHUNK = 512
FIX_STEPS = 4
ROW_TILE = 512
X_BUFFERS = 4
COPY_GROUP = 4
EXT = D_MODEL + LANES
ROUTER_TILES = 6
NO_RUN = 1.0e9


def _top_k_gates(scores, rb):
    rows = scores.shape[0]
    lane_f = jax.lax.broadcasted_iota(jnp.int32, (rows, LANES), 1).astype(f32)
    biased = jnp.where(lane_f < float(N_EXPERTS), scores + rb, -jnp.inf)
    sel = jnp.zeros((rows, LANES), f32)
    mask = jnp.zeros((rows, LANES), f32)
    for _ in range(TOP_K):
        m = jnp.max(biased, axis=1, keepdims=True)
        idx = jnp.min(jnp.where(biased == m, lane_f, float(LANES)), axis=1, keepdims=True)
        hit = lane_f == idx
        sel = jnp.where(hit, scores, sel)
        mask = jnp.where(hit, 1.0, mask)
        biased = jnp.where(hit, -jnp.inf, biased)
    gates = sel / jnp.sum(sel, axis=1, keepdims=True) * ROUTED_SCALE
    return mask, gates


def _router_kernel(h_ref, wr_ref, rb_ref, hext_ref, rankm_ref, cnt_ref):
    hb = h_ref[...].astype(bf16)
    scores = jax.nn.sigmoid(jnp.dot(hb, wr_ref[...], preferred_element_type=f32))
    mask, gates = _top_k_gates(scores, rb_ref[...])
    t_row = jax.lax.broadcasted_iota(jnp.int32, (TOK_TILE, TOK_TILE), 0)
    t_col = jax.lax.broadcasted_iota(jnp.int32, (TOK_TILE, TOK_TILE), 1)
    earlier = jnp.where(t_col < t_row, 1.0, 0.0).astype(bf16)
    for sub in range(ROUTER_TILES):
        m = mask[sub * TOK_TILE:(sub + 1) * TOK_TILE]
        rank = jnp.dot(earlier, m.astype(bf16), preferred_element_type=f32)
        rankm_ref[sub * TOK_TILE:(sub + 1) * TOK_TILE, :] = jnp.where(m > 0.0, rank, -1.0).astype(bf16)
        cnt_ref[sub] = jnp.broadcast_to(jnp.sum(m, axis=0, keepdims=True), (SUBLANES, LANES))
    g_hi = gates.astype(bf16).astype(f32)
    g_pack = g_hi + pltpu.roll(gates - g_hi, N_EXPERTS, 1)
    hext_ref[:, :D_MODEL] = hb
    hext_ref[:, D_MODEL:] = g_pack.astype(bf16)


def _router(h, wr, rb):
    n_tiles = h.shape[0] // TOK_TILE
    assert n_tiles % ROUTER_TILES == 0
    rows = ROUTER_TILES * TOK_TILE
    const = lambda shape: pl.BlockSpec(shape, lambda i: (0,) * len(shape))
    return pl.pallas_call(
        _router_kernel,
        grid=(n_tiles // ROUTER_TILES,),
        in_specs=[pl.BlockSpec((rows, D_MODEL), lambda i: (i, 0)), const(wr.shape), const(rb.shape)],
        out_specs=(pl.BlockSpec((rows, EXT), lambda i: (i, 0)),
                   pl.BlockSpec((rows, LANES), lambda i: (i, 0)),
                   pl.BlockSpec((ROUTER_TILES, SUBLANES, LANES), lambda i: (i, 0, 0))),
        out_shape=(jax.ShapeDtypeStruct((h.shape[0], EXT), bf16),
                   jax.ShapeDtypeStruct((h.shape[0], LANES), bf16),
                   jax.ShapeDtypeStruct((n_tiles, SUBLANES, LANES), f32)),
        compiler_params=pltpu.CompilerParams(
            dimension_semantics=("arbitrary",), vmem_limit_bytes=VMEM_LIMIT_BYTES),
        name="moe_router",
    )(h, wr, rb)


def _run_copy(src, dst, sem):
    return pltpu.make_async_copy(src, dst, sem)


def _fixed_ranges(shape, axis):
    e = jax.lax.broadcasted_iota(jnp.int32, shape, axis).astype(f32)
    loc = jnp.where(e < float(N_EXPERTS), e * float(FIX), NO_RUN)
    return loc, loc + float(FIX)


def _slot_rows(first_slot, loc_row, end_row, base, rankm):
    s = (jax.lax.broadcasted_iota(jnp.int32, (SLOT_CHUNK, LANES), 0) + first_slot).astype(f32)
    in_run = jnp.where(s >= loc_row, jnp.where(s < end_row, 1.0, 0.0), 0.0)
    r_col = s[:, 0:1] - jnp.sum(in_run * (loc_row - base), axis=1, keepdims=True)
    q = jax.lax.dot_general(in_run.astype(bf16), rankm, (((1,), (1,)), ((), ())),
                            preferred_element_type=f32)
    return jnp.where(q == r_col, 1.0, 0.0).astype(bf16)


def _slot_cols(first_slot, loc_col, end_col, base, rankm):
    s = (jax.lax.broadcasted_iota(jnp.int32, (LANES, SLOT_CHUNK), 1) + first_slot).astype(f32)
    in_run = jnp.where(s >= loc_col, jnp.where(s < end_col, 1.0, 0.0), 0.0)
    r_row = s[0:1, :] - jnp.sum(in_run * (loc_col - base), axis=0, keepdims=True)
    q = jnp.dot(rankm, in_run.astype(bf16), preferred_element_type=f32)
    return jnp.where(q == r_row, 1.0, 0.0).astype(bf16)


def _wait_groups(n_groups, make_copy):
    max_bits = (OVF_CHUNKS // COPY_GROUP).bit_length()
    for b in range(max_bits):
        rows = (COPY_GROUP * RUN_ALIGN) << b

        @pl.when(jnp.bitwise_and(jnp.right_shift(n_groups, b), 1) == 1)
        def _():
            make_copy(rows).wait()


def _dispatch_kernel(totc_ref, ngrp_ref, dstk_ref, zst_ref, zch_ref,
                     hext_ref, rankm_ref, locs_ref, sorted_f_ref, sorted_o_ref,
                     stage_f, stage_o, zbuf, semf, semo, zsem):
    i = pl.program_id(0)
    n = pl.num_programs(0)
    slot = i % 2
    rankm = rankm_ref[...]

    fx_loc, fx_end = _fixed_ranges((1, LANES), 1)
    experts_per_chunk = SLOT_CHUNK // FIX
    for ch in range(FIX_SLOTS // SLOT_CHUNK):
        p = _slot_rows(ch * SLOT_CHUNK, fx_loc, fx_end, 0.0, rankm)
        rows = jnp.dot(p, hext_ref[...], preferred_element_type=f32).astype(bf16)
        stage_f[slot, ch * experts_per_chunk:(ch + 1) * experts_per_chunk] = rows.reshape(
            experts_per_chunk, FIX, EXT)

    ov_loc = locs_ref[0, 0:1, :]
    ov_end = locs_ref[0, 1:2, :]
    for ch in range(OVF_SLOTS // SLOT_CHUNK):
        def sort_overflow(ch=ch):
            p = _slot_rows(FIX_SLOTS + ch * SLOT_CHUNK, ov_loc, ov_end, float(FIX), rankm)
            rows = jnp.dot(p, hext_ref[...], preferred_element_type=f32)
            stage_o[slot, ch * SLOT_CHUNK:(ch + 1) * SLOT_CHUNK, :] = rows.astype(bf16)
        pl.when(totc_ref[i] * RUN_ALIGN > ch * SLOT_CHUNK)(sort_overflow)

    def fixed_copy(tile, slot_):
        dst = sorted_f_ref.at[:, pl.ds(pl.multiple_of(tile * FIX, FIX), FIX), :]
        return _run_copy(stage_f.at[slot_], dst, semf.at[slot_])

    def overflow_wait(rows, slot_):
        return _run_copy(stage_o.at[slot_, pl.ds(0, rows)], sorted_o_ref.at[pl.ds(0, rows)],
                         semo.at[slot_])

    fixed_copy(i, slot).start()

    def start_group(g, carry):
        for u in range(COPY_GROUP):
            k = g * COPY_GROUP + u
            src = stage_o.at[slot, pl.ds(pl.multiple_of(k * RUN_ALIGN, RUN_ALIGN), RUN_ALIGN)]
            dst = sorted_o_ref.at[pl.ds(pl.multiple_of(dstk_ref[i, k], RUN_ALIGN), RUN_ALIGN)]
            _run_copy(src, dst, semo.at[slot]).start()
        return carry
    jax.lax.fori_loop(0, ngrp_ref[i], start_group, 0)

    @pl.when(i == 0)
    def _():
        zbuf[...] = jnp.zeros(zbuf.shape, bf16)

    def zero_copy(e, k):
        dst = sorted_o_ref.at[pl.ds(pl.multiple_of(zst_ref[e] + k * RUN_ALIGN, RUN_ALIGN), RUN_ALIGN)]
        return _run_copy(zbuf, dst, zsem.at[0])

    @pl.when(jnp.logical_and(i > 0, i <= N_EXPERTS))
    def _():
        def wait_zero(k, carry):
            zero_copy(i - 1, k).wait()
            return carry
        jax.lax.fori_loop(0, zch_ref[i - 1], wait_zero, 0)

    @pl.when(i < N_EXPERTS)
    def _():
        def start_zero(k, carry):
            zero_copy(i, k).start()
            return carry
        jax.lax.fori_loop(0, zch_ref[i], start_zero, 0)

    @pl.when(i > 0)
    def _():
        fixed_copy(i - 1, 1 - slot).wait()
        _wait_groups(ngrp_ref[i - 1], lambda rows: overflow_wait(rows, 1 - slot))

    @pl.when(i == n - 1)
    def _():
        fixed_copy(i, slot).wait()
        _wait_groups(ngrp_ref[i], lambda rows: overflow_wait(rows, slot))


def _dispatch(hext, rankm, locs, totc, ngrp, dstk, zst, zch, *, overflow_rows):
    n_tiles = hext.shape[0] // TOK_TILE
    assert n_tiles > N_EXPERTS
    grid_spec = pltpu.PrefetchScalarGridSpec(
        num_scalar_prefetch=5,
        grid=(n_tiles,),
        in_specs=[pl.BlockSpec((TOK_TILE, EXT), lambda i, *_: (i, 0)),
                  pl.BlockSpec((TOK_TILE, LANES), lambda i, *_: (i, 0)),
                  pl.BlockSpec((1, SUBLANES, LANES), lambda i, *_: (i, 0, 0))],
        out_specs=(pl.BlockSpec(memory_space=pl.ANY), pl.BlockSpec(memory_space=pl.ANY)),
        scratch_shapes=[pltpu.VMEM((2, N_EXPERTS, FIX, EXT), bf16),
                        pltpu.VMEM((2, OVF_SLOTS, EXT), bf16),
                        pltpu.VMEM((RUN_ALIGN, EXT), bf16),
                        pltpu.SemaphoreType.DMA((2,)),
                        pltpu.SemaphoreType.DMA((2,)),
                        pltpu.SemaphoreType.DMA((1,))])
    return pl.pallas_call(
        _dispatch_kernel,
        grid_spec=grid_spec,
        out_shape=(jax.ShapeDtypeStruct((N_EXPERTS, n_tiles * FIX, EXT), bf16),
                   jax.ShapeDtypeStruct((overflow_rows, EXT), bf16)),
        compiler_params=pltpu.CompilerParams(
            dimension_semantics=("arbitrary",), vmem_limit_bytes=VMEM_LIMIT_BYTES),
        name="moe_dispatch",
    )(totc, ngrp, dstk, zst, zch, hext, rankm, locs)


def _experts_kernel(ts_ref, xf_ref, xo_ref, wgu_ref, wd_ref, yf_ref, yo_ref,
                    xbuf, ybuf, wgu_b, wd_b, xsem, ysem):
    e = pl.program_id(0)
    n_valid = ts_ref[N_EXPERTS]
    wgu_b[...] = wgu_ref[0].astype(bf16)
    wd_b[...] = wd_ref[0].astype(bf16)

    def ffn(x_ext):
        m = x_ext.shape[0]
        gu = jnp.dot(x_ext[:, :D_MODEL], wgu_b[...], preferred_element_type=f32)
        act = (jax.nn.silu(gu[:, :D_EXPERT]) * gu[:, D_EXPERT:]).astype(bf16)
        y = jnp.dot(act, wd_b[...], preferred_element_type=f32)
        g_pack = x_ext[:, D_MODEL:].astype(f32)
        lane = jax.lax.broadcasted_iota(jnp.int32, (m, LANES), 1)
        mine = jnp.where(lane == e, g_pack, jnp.where(lane == e + N_EXPERTS, g_pack, 0.0))
        return (y * jnp.sum(mine, axis=1, keepdims=True)).astype(bf16)

    def x_copy(j, slot):
        rows = pl.ds(pl.multiple_of(j * ROW_TILE, ROW_TILE), ROW_TILE)
        return pltpu.make_async_copy(xo_ref.at[rows], xbuf.at[slot], xsem.at[slot])

    def y_copy(j, slot):
        rows = pl.ds(pl.multiple_of(j * ROW_TILE, ROW_TILE), ROW_TILE)
        return pltpu.make_async_copy(ybuf.at[slot], yo_ref.at[rows], ysem.at[slot])

    ahead = X_BUFFERS - 1

    @pl.when(e == 0)
    def _():
        for a in range(ahead):
            @pl.when(a < n_valid)
            def _():
                x_copy(a, a).start()

    step_rows = xf_ref.shape[1] // FIX_STEPS
    for c in range(FIX_STEPS):
        rows = slice(c * step_rows, (c + 1) * step_rows)
        yf_ref[0, rows, :] = ffn(xf_ref[0, rows, :])

    def tile(j, carry):
        xslot = j % X_BUFFERS
        slot = j % 2
        x_copy(j, xslot).wait()

        @pl.when(j + ahead < n_valid)
        def _():
            x_copy(j + ahead, (j + ahead) % X_BUFFERS).start()

        @pl.when(j >= 2)
        def _():
            y_copy(j - 2, slot).wait()

        ybuf[slot] = ffn(xbuf[xslot])
        y_copy(j, slot).start()
        return carry
    jax.lax.fori_loop(ts_ref[e], ts_ref[e + 1], tile, 0)

    @pl.when(e == pl.num_programs(0) - 1)
    def _():
        @pl.when(n_valid >= 2)
        def _():
            y_copy(n_valid - 2, n_valid % 2).wait()

        @pl.when(n_valid >= 1)
        def _():
            y_copy(n_valid - 1, (n_valid - 1) % 2).wait()


def _experts(xf, xo, wgu, wd, ts, *, overflow_rows):
    fixed_rows = xf.shape[1]
    assert fixed_rows % (FIX_STEPS * RUN_ALIGN) == 0
    grid_spec = pltpu.PrefetchScalarGridSpec(
        num_scalar_prefetch=1,
        grid=(N_EXPERTS,),
        in_specs=[pl.BlockSpec((1, fixed_rows, EXT), lambda e, ts: (e, 0, 0)),
                  pl.BlockSpec(memory_space=pl.ANY),
                  pl.BlockSpec((1, D_MODEL, 2 * D_EXPERT), lambda e, ts: (e, 0, 0)),
                  pl.BlockSpec((1, D_EXPERT, D_MODEL), lambda e, ts: (e, 0, 0))],
        out_specs=(pl.BlockSpec((1, fixed_rows, D_MODEL), lambda e, ts: (e, 0, 0)),
                   pl.BlockSpec(memory_space=pl.ANY)),
        scratch_shapes=[pltpu.VMEM((X_BUFFERS, ROW_TILE, EXT), bf16),
                        pltpu.VMEM((2, ROW_TILE, D_MODEL), bf16),
                        pltpu.VMEM((D_MODEL, 2 * D_EXPERT), bf16),
                        pltpu.VMEM((D_EXPERT, D_MODEL), bf16),
                        pltpu.SemaphoreType.DMA((X_BUFFERS,)),
                        pltpu.SemaphoreType.DMA((2,))])
    return pl.pallas_call(
        _experts_kernel,
        grid_spec=grid_spec,
        out_shape=(jax.ShapeDtypeStruct((N_EXPERTS, fixed_rows, D_MODEL), bf16),
                   jax.ShapeDtypeStruct((overflow_rows, D_MODEL), bf16)),
        compiler_params=pltpu.CompilerParams(
            dimension_semantics=("arbitrary",), vmem_limit_bytes=VMEM_LIMIT_BYTES),
        name="moe_experts",
    )(ts, xf, xo, wgu, wd)


def _combine_kernel(totc_ref, ngrp_ref, srck_ref,
                    h_ref, rankm_ref, locc_ref, yf_ref, yo_ref, wsu_ref, wsd_ref, ln_g_ref, ln_b_ref,
                    out_bm_ref, out_tm_ref, yloc_f, yloc_o, acc_scr, p_scr, semf, semo,
                    *, n_bm_tiles):
    i = pl.program_id(0)
    n = pl.num_programs(0)
    slot = i % 2

    def fixed_copy(tile, slot_):
        src = yf_ref.at[:, pl.ds(pl.multiple_of(tile * FIX, FIX), FIX), :]
        return _run_copy(src, yloc_f.at[slot_], semf.at[slot_])

    def overflow_wait(rows, slot_):
        return _run_copy(yo_ref.at[pl.ds(0, rows)], yloc_o.at[slot_, pl.ds(0, rows)], semo.at[slot_])

    def fetch(tile, slot_):
        fixed_copy(tile, slot_).start()

        def start_group(g, carry):
            for u in range(COPY_GROUP):
                k = g * COPY_GROUP + u
                src = yo_ref.at[pl.ds(pl.multiple_of(srck_ref[tile, k], RUN_ALIGN), RUN_ALIGN)]
                dst = yloc_o.at[slot_, pl.ds(pl.multiple_of(k * RUN_ALIGN, RUN_ALIGN), RUN_ALIGN)]
                _run_copy(src, dst, semo.at[slot_]).start()
            return carry
        jax.lax.fori_loop(0, ngrp_ref[tile], start_group, 0)

    @pl.when(i == 0)
    def _():
        yloc_o[...] = jnp.zeros(yloc_o.shape, bf16)
        fetch(0, 0)

    @pl.when(i + 1 < n)
    def _():
        fetch(i + 1, 1 - slot)

    h = h_ref[...]
    hb = h.astype(bf16)
    su = jnp.dot(hb, wsu_ref[...], preferred_element_type=f32)
    act = (jax.nn.silu(su[:, :D_SHARED]) * su[:, D_SHARED:]).astype(bf16)
    acc_scr[...] = jnp.dot(act, wsd_ref[...], preferred_element_type=f32)

    rankm = rankm_ref[...]
    fx_loc, fx_end = _fixed_ranges((LANES, 1), 0)
    for ch in range(FIX_SLOTS // SLOT_CHUNK):
        p_scr[:, ch * SLOT_CHUNK:(ch + 1) * SLOT_CHUNK] = _slot_cols(
            ch * SLOT_CHUNK, fx_loc, fx_end, 0.0, rankm)

    fixed_copy(i, slot).wait()
    _wait_groups(ngrp_ref[i], lambda rows: overflow_wait(rows, slot))

    acc_scr[...] += jnp.dot(p_scr[...], yloc_f[slot].reshape(FIX_SLOTS, D_MODEL),
                            preferred_element_type=f32)
    ov_loc = locc_ref[0, :, 0:1]
    ov_end = locc_ref[0, :, 1:2]
    for ch in range(OVF_SLOTS // SLOT_CHUNK):
        def gather_overflow(ch=ch):
            p = _slot_cols(FIX_SLOTS + ch * SLOT_CHUNK, ov_loc, ov_end, float(FIX), rankm)
            acc_scr[...] += jnp.dot(p, yloc_o[slot, ch * SLOT_CHUNK:(ch + 1) * SLOT_CHUNK, :],
                                    preferred_element_type=f32)
        pl.when(totc_ref[i] * RUN_ALIGN > ch * SLOT_CHUNK)(gather_overflow)
    y = _layer_norm(DN_ALPHA * h + acc_scr[...], ln_g_ref[...], ln_b_ref[...])

    @pl.when(i < n_bm_tiles)
    def _():
        nb, tc, _ = out_bm_ref.shape
        out_bm_ref[...] = jnp.transpose(y.reshape(tc, nb, D_MODEL), (1, 0, 2))

    @pl.when(i >= n_bm_tiles)
    def _():
        out_tm_ref[...] = y


def _combine(h, rankm, locc, yf, yo, wsu, wsd, ln_g, ln_b, totc, ngrp, srck, *, bm_shape):
    n_tiles = h.shape[0] // TOK_TILE
    nb, length, _ = bm_shape
    tc = TOK_TILE // nb
    n_bm_tiles = length // tc
    n_tm_tiles = n_tiles - n_bm_tiles
    const = lambda shape: pl.BlockSpec(shape, lambda i, *_: (0,) * len(shape))
    grid_spec = pltpu.PrefetchScalarGridSpec(
        num_scalar_prefetch=3,
        grid=(n_tiles,),
        in_specs=[pl.BlockSpec((TOK_TILE, D_MODEL), lambda i, *_: (i, 0)),
                  pl.BlockSpec((TOK_TILE, LANES), lambda i, *_: (i, 0)),
                  pl.BlockSpec((1, LANES, 2), lambda i, *_: (i, 0, 0)),
                  pl.BlockSpec(memory_space=pl.ANY),
                  pl.BlockSpec(memory_space=pl.ANY),
                  const(wsu.shape), const(wsd.shape), const(ln_g.shape), const(ln_b.shape)],
        out_specs=(pl.BlockSpec((nb, tc, D_MODEL),
                                lambda i, *_: (0, jnp.minimum(i, n_bm_tiles - 1), 0)),
                   pl.BlockSpec((TOK_TILE, D_MODEL),
                                lambda i, *_: (jnp.maximum(i - n_bm_tiles, 0), 0))),
        scratch_shapes=[pltpu.VMEM((2, N_EXPERTS, FIX, D_MODEL), bf16),
                        pltpu.VMEM((2, OVF_SLOTS, D_MODEL), bf16),
                        pltpu.VMEM((TOK_TILE, D_MODEL), f32),
                        pltpu.VMEM((TOK_TILE, FIX_SLOTS), bf16),
                        pltpu.SemaphoreType.DMA((2,)),
                        pltpu.SemaphoreType.DMA((2,))])
    return pl.pallas_call(
        functools.partial(_combine_kernel, n_bm_tiles=n_bm_tiles),
        grid_spec=grid_spec,
        out_shape=(jax.ShapeDtypeStruct(bm_shape, f32),
                   jax.ShapeDtypeStruct((n_tm_tiles * TOK_TILE, D_MODEL), f32)),
        compiler_params=pltpu.CompilerParams(
            dimension_semantics=("arbitrary",), vmem_limit_bytes=VMEM_LIMIT_BYTES),
        name="moe_combine",
    )(totc, ngrp, srck, h, rankm, locc, yf, yo, wsu, wsd, ln_g, ln_b)


def _round_up(x, m):
    return (x + m - 1) // m * m


def _moe(h, wr, rb, wgu, wd, wsu, wsd, ln_g, ln_b, *, bm_shape):
    n_tok = h.shape[0]
    n_tiles = n_tok // TOK_TILE
    max_rows = _round_up(n_tiles * (OVF_CHUNKS - 1) * RUN_ALIGN + N_EXPERTS * (ROW_TILE - 1), ROW_TILE)

    hext, rankm, cnt = _router(h, wr, rb)

    i32 = jnp.int32
    cnt = cnt[:, 0, :N_EXPERTS].astype(i32)
    oc = (jnp.maximum(cnt - FIX, 0) + RUN_ALIGN - 1) // RUN_ALIGN
    over_tiles = jnp.cumsum(oc, axis=0)
    region_rows = over_tiles[-1] * RUN_ALIGN
    region_size = _round_up(region_rows, ROW_TILE)
    region_start = jnp.cumsum(region_size) - region_size
    run_dst = region_start[None, :] + RUN_ALIGN * (over_tiles - oc)
    ch_end = jnp.cumsum(oc, axis=1)
    ch_beg = ch_end - oc
    totc = ch_end[:, -1]
    k = jnp.arange(OVF_CHUNKS, dtype=i32)
    kk = k[None, :, None]
    mine = jnp.logical_and(ch_beg[:, None, :] <= kk, kk < ch_end[:, None, :])
    chunk_dst = jnp.sum(jnp.where(mine, run_dst[:, None, :] + RUN_ALIGN * (kk - ch_beg[:, None, :]), 0),
                        axis=-1)
    live = k[None, :] < totc[:, None]
    tile_ids = jnp.arange(n_tiles, dtype=i32)[:, None]
    spare = max_rows + (tile_ids % 2) * OVF_SLOTS + k[None, :] * RUN_ALIGN
    dstk = jnp.where(live, chunk_dst, spare).astype(i32)
    srck = jnp.where(live, chunk_dst, 0).astype(i32)
    ngrp = ((totc + COPY_GROUP - 1) // COPY_GROUP).astype(i32)
    zst = (region_start + region_rows).astype(i32)
    zch = ((region_size - region_rows) // RUN_ALIGN).astype(i32)
    tile_start = jnp.concatenate([region_start, region_start[-1:] + region_size[-1:]]) // ROW_TILE
    tile_start = tile_start.astype(i32)

    pad = ((0, 0), (0, LANES - N_EXPERTS))
    loc_f = jnp.pad((FIX_SLOTS + RUN_ALIGN * ch_beg).astype(f32), pad, constant_values=NO_RUN)
    end_f = jnp.pad((FIX_SLOTS + RUN_ALIGN * ch_end).astype(f32), pad, constant_values=NO_RUN)
    locs = jnp.concatenate([loc_f[:, None, :], end_f[:, None, :],
                            jnp.zeros((n_tiles, SUBLANES - 2, LANES), f32)], axis=1)
    locc = jnp.stack([loc_f, end_f], axis=-1)

    totc = totc.astype(i32)
    overflow_rows = max_rows + 2 * OVF_SLOTS
    xf, xo = _dispatch(hext, rankm, locs, totc, ngrp, dstk, zst, zch, overflow_rows=overflow_rows)
    yf, yo = _experts(xf, xo, wgu, wd, tile_start, overflow_rows=max_rows)
    return _combine(h, rankm, locc, yf, yo, wsu, wsd, ln_g, ln_b, totc, ngrp, srck, bm_shape=bm_shape)


def _diag_tiles(blocks, n_tiles):
    n_blocks, r, c = blocks.shape
    per = n_blocks // n_tiles
    wide = jnp.tile(blocks.reshape(n_blocks * r, c), (1, per))
    row_block = (jnp.arange(n_blocks * r) // r) % per
    col_block = jnp.arange(per * c) // c
    wide = jnp.where(row_block[:, None] == col_block[None, :], wide, 0)
    return wide.reshape(n_tiles, per * r, per * c)


def _head_block_diag(w):
    return _diag_tiles(w, D_RG // MXU_DIM)


def _s5_in_tiles(b):
    return _diag_tiles(b.transpose(0, 2, 1), 2)


def _s5_out_tiles(cw):
    return _diag_tiles(cw.transpose(0, 2, 1), 2)


def _row(v):
    return v.reshape(1, -1)


def kernel(x_prompt, x_sample, state_rg_conv, state_rg_h, state_s5_re, state_s5_im, w_in, conv_w, conv_b, rg_w_a, rg_b_a, rg_w_i, rg_b_i, rg_lam, s5_a_re, s5_a_im, s5_log_dt, s5_b_re, s5_b_im, s5_c_re, s5_c_im, s5_d, w_glu, b_glu, w_out, ln1_g, ln1_b, w_router, router_bias, w_gate_up, w_down, w_shared_up, w_shared_down, ln2_g, ln2_b):
    l = 0
    bp, lp, _ = x_prompt.shape
    bs, ls, _ = x_sample.shape

    are, aim, bbre, bbim = _s5_prep(
        _row(s5_a_re[l]), _row(s5_a_im[l]),
        _row(jnp.repeat(s5_log_dt[l], S5_STATE)),
        _s5_in_tiles(s5_b_re[l]), _s5_in_tiles(s5_b_im[l]))
    params = (w_in[l].astype(bf16), conv_w[l], _row(conv_b[l]),
              _head_block_diag(rg_w_a[l]).astype(bf16), _row(rg_b_a[l]),
              _head_block_diag(rg_w_i[l]).astype(bf16), _row(rg_b_i[l]), _row(rg_lam[l]),
              are, aim, bbre, bbim,
              _s5_out_tiles(s5_c_re[l]).astype(bf16), _s5_out_tiles(s5_c_im[l]).astype(bf16),
              _row(s5_d[l]), w_glu[l].astype(bf16), _row(b_glu[l]), w_out[l].astype(bf16),
              _row(ln1_g[l]), _row(ln1_b[l]))

    tail = CONV_WIDTH - 1
    n_tok = lp * bp + ls * bs
    xs_tm = x_sample.transpose(1, 0, 2).reshape(ls * bs, D_MODEL)
    h_all, sc, sh, sre, sim = _mixer(
        xs_tm, state_rg_conv[l].transpose(1, 0, 2).reshape(tail * bs, D_RG), state_rg_h[l],
        state_s5_re[l].reshape(bs, S5_N), state_s5_im[l].reshape(bs, S5_N), params,
        nb=bs, tc=ls, name="mixer_sample",
        h_all=jnp.zeros((SUBLANES, LANES), f32), total_rows=n_tok, first_row=lp * bp)
    h_all, pc, ph, pre, pim = _mixer(
        x_prompt, jnp.zeros((tail * bp, D_RG), f32), jnp.zeros((bp, D_RG), f32),
        jnp.zeros((bp, S5_N), f32), jnp.zeros((bp, S5_N), f32), params,
        nb=bp, tc=PROMPT_CHUNK_ROWS // bp, name="mixer_prompt",
        h_all=h_all, total_rows=n_tok, first_row=0)
    wr = jnp.pad(w_router[l], ((0, 0), (0, LANES - N_EXPERTS))).astype(bf16)
    rb = jnp.pad(_row(router_bias[l]), ((0, 0), (0, LANES - N_EXPERTS)))
    yp, ys_tm = _moe(h_all, wr, rb, w_gate_up[l], w_down[l],
                     w_shared_up[l].astype(bf16), w_shared_down[l].astype(bf16),
                     _row(ln2_g[l]), _row(ln2_b[l]), bm_shape=x_prompt.shape)
    ys = ys_tm.reshape(ls, bs, D_MODEL).transpose(1, 0, 2)

    def conv_out(cv, nbatch):
        return cv.reshape(tail, nbatch, D_RG).transpose(1, 0, 2)[None]

    return (yp, ys,
            conv_out(pc, bp), ph[None],
            pre.reshape(1, bp, S5_GROUPS, S5_STATE), pim.reshape(1, bp, S5_GROUPS, S5_STATE),
            conv_out(sc, bs), sh[None],
            sre.reshape(1, bs, S5_GROUPS, S5_STATE), sim.reshape(1, bs, S5_GROUPS, S5_STATE))
```

```python
import functools
import math

import jax
import jax.numpy as jnp
from jax.experimental import pallas as pl
from jax.experimental.pallas import tpu as pltpu

D_MODEL = 1024
D_RG = 512
RG_HEADS = 8
RG_HEAD_DIM = 64
CONV_WIDTH = 4
RG_C = 8.0
D_S5 = 512
S5_GROUP = 16
S5_GROUPS = 32
S5_STATE = 64
S5_N = S5_GROUPS * S5_STATE
N_EXPERTS = 64
TOP_K = 8
D_EXPERT = 256
D_SHARED = 256
ROUTED_SCALE = 2.5
DEPTH = 1
DN_ALPHA = (2.0 * DEPTH) ** 0.25
LN_EPS = 1e-5

SUBLANES = 8
LANES = 128
MXU_DIM = 256
S5_SCAN_COLS = 512
PROMPT_CHUNK_ROWS = 512
VMEM_LIMIT_BYTES = 56 * 1024 * 1024

bf16 = jnp.bfloat16
f32 = jnp.float32


def _gelu_tanh(x):
    c = math.sqrt(2.0 / math.pi)
    return x * (0.5 * (1.0 + jnp.tanh(c * (x + 0.044715 * (x * x * x)))))


def _layer_norm(x, g, b):
    mu = jnp.mean(x, axis=-1, keepdims=True)
    xc = x - mu
    var = jnp.mean(xc * xc, axis=-1, keepdims=True)
    return xc * jax.lax.rsqrt(var + LN_EPS) * g + b


def _s5_prep_kernel(lr_ref, li_ref, ldt_ref, bre_ref, bim_ref,
                    are_ref, aim_ref, bbre_ref, bbim_ref):
    lr = lr_ref[...]
    li = li_ref[...]
    dt = jnp.exp(ldt_ref[...])
    mag = jnp.exp(lr * dt)
    abar_re = mag * jnp.cos(li * dt)
    abar_im = mag * jnp.sin(li * dt)
    den = lr * lr + li * li
    nr = abar_re - 1.0
    ni = abar_im
    coef_re = (nr * lr + ni * li) / den
    coef_im = (ni * lr - nr * li) / den
    are_ref[...] = abar_re
    aim_ref[...] = abar_im
    half = S5_N // 2
    for k in range(2):
        cre = coef_re[:, k * half:(k + 1) * half]
        cim = coef_im[:, k * half:(k + 1) * half]
        br = bre_ref[k]
        bi = bim_ref[k]
        bbre_ref[k] = (cre * br - cim * bi).astype(bf16)
        bbim_ref[k] = (cre * bi + cim * br).astype(bf16)


def _s5_prep(lr, li, ldt, bre_t, bim_t):
    half = S5_N // 2
    return pl.pallas_call(
        _s5_prep_kernel,
        out_shape=(jax.ShapeDtypeStruct((1, S5_N), f32),
                   jax.ShapeDtypeStruct((1, S5_N), f32),
                   jax.ShapeDtypeStruct((2, MXU_DIM, half), bf16),
                   jax.ShapeDtypeStruct((2, MXU_DIM, half), bf16)),
        name="s5_prep",
    )(lr, li, ldt, bre_t, bim_t)


def _mixer_kernel(h_all_ref, x_ref, conv0_ref, h0_ref, s0r_ref, s0i_ref,
                  w_in_ref, conv_w_ref, conv_b_ref, wa_ref, ba_ref, wi_ref, bi_ref, lam_ref,
                  are_ref, aim_ref, bbre_ref, bbim_ref, cre_ref, cim_ref, d_ref,
                  wglu_ref, bglu_ref, wout_ref, ln_g_ref, ln_b_ref,
                  hout_ref, conv_out_ref, hlast_ref, sre_out_ref, sim_out_ref,
                  pad_scr, a_scr, b_scr, bur_scr, bui_scr, hst_scr, sr_scr, si_scr,
                  *, nb, tc, batch_major_input):
    del h_all_ref
    rows = nb * tc
    tail = (CONV_WIDTH - 1) * nb
    c = pl.program_id(0)

    @pl.when(c == 0)
    def _():
        pad_scr[0:tail, :] = conv0_ref[...]
        hst_scr[...] = h0_ref[...]
        sr_scr[...] = s0r_ref[...]
        si_scr[...] = s0i_ref[...]

    if batch_major_input:
        x = jnp.transpose(x_ref[...], (1, 0, 2)).reshape(rows, D_MODEL)
    else:
        x = x_ref[...]
    proj = jnp.dot(x.astype(bf16), w_in_ref[...], preferred_element_type=f32)
    x_rg = proj[:, :D_RG]
    g_rg = proj[:, D_RG:2 * D_RG]
    u = proj[:, 2 * D_RG:]

    pad_scr[tail:tail + rows, :] = x_rg
    conv_w = conv_w_ref[...]
    acc = conv_w[0:1, :] * pad_scr[0:rows, :]
    for k in range(1, CONV_WIDTH):
        acc = acc + conv_w[k:k + 1, :] * pad_scr[k * nb:k * nb + rows, :]
    xc = conv_b_ref[...] + acc
    new_tail = pad_scr[rows:rows + tail, :]
    pad_scr[0:tail, :] = new_tail

    xcb = xc.astype(bf16)
    ga = []
    gi = []
    for hh in range(D_RG // MXU_DIM):
        xs = xcb[:, hh * MXU_DIM:(hh + 1) * MXU_DIM]
        ga.append(jnp.dot(xs, wa_ref[hh], preferred_element_type=f32))
        gi.append(jnp.dot(xs, wi_ref[hh], preferred_element_type=f32))
    r = jax.nn.sigmoid(jnp.concatenate(ga, axis=1) + ba_ref[...])
    i = jax.nn.sigmoid(jnp.concatenate(gi, axis=1) + bi_ref[...])
    nlam = -lam_ref[...]
    softplus = jnp.maximum(nlam, 0.0) + jnp.log1p(jnp.exp(-jnp.abs(nlam)))
    log_a = (-RG_C) * r * softplus
    a_scr[...] = jnp.exp(log_a)
    th = jnp.tanh(log_a)
    b_scr[...] = jnp.sqrt((-2.0 * th) / (1.0 - th)) * (i * xc)

    for rg in range(nb // SUBLANES):
        r0 = rg * SUBLANES
        h = hst_scr[r0:r0 + SUBLANES, :]
        for t in range(tc):
            q = t * nb + r0
            h = a_scr[q:q + SUBLANES, :] * h + b_scr[q:q + SUBLANES, :]
            b_scr[q:q + SUBLANES, :] = h
        hst_scr[r0:r0 + SUBLANES, :] = h
    y_rg = b_scr[...] * _gelu_tanh(g_rg)

    ub = u.astype(bf16)
    half = S5_N // 2
    for k in range(2):
        us = ub[:, k * MXU_DIM:(k + 1) * MXU_DIM]
        bur_scr[:, k * half:(k + 1) * half] = jnp.dot(us, bbre_ref[k], preferred_element_type=f32)
        bui_scr[:, k * half:(k + 1) * half] = jnp.dot(us, bbim_ref[k], preferred_element_type=f32)
    for rg in range(nb // SUBLANES):
        r0 = rg * SUBLANES
        for cb in range(S5_N // S5_SCAN_COLS):
            c0 = cb * S5_SCAN_COLS
            ar = jnp.broadcast_to(are_ref[:, c0:c0 + S5_SCAN_COLS], (SUBLANES, S5_SCAN_COLS))
            ai = jnp.broadcast_to(aim_ref[:, c0:c0 + S5_SCAN_COLS], (SUBLANES, S5_SCAN_COLS))
            xr = sr_scr[r0:r0 + SUBLANES, c0:c0 + S5_SCAN_COLS]
            xi = si_scr[r0:r0 + SUBLANES, c0:c0 + S5_SCAN_COLS]
            for t in range(tc):
                q = t * nb + r0
                br = bur_scr[q:q + SUBLANES, c0:c0 + S5_SCAN_COLS]
                bi_ = bui_scr[q:q + SUBLANES, c0:c0 + S5_SCAN_COLS]
                nxr = ar * xr - ai * xi + br
                nxi = ar * xi + ai * xr + bi_
                bur_scr[q:q + SUBLANES, c0:c0 + S5_SCAN_COLS] = nxr
                bui_scr[q:q + SUBLANES, c0:c0 + S5_SCAN_COLS] = nxi
                xr, xi = nxr, nxi
            sr_scr[r0:r0 + SUBLANES, c0:c0 + S5_SCAN_COLS] = xr
            si_scr[r0:r0 + SUBLANES, c0:c0 + S5_SCAN_COLS] = xi
    ys = []
    for j in range(D_S5 // MXU_DIM):
        xrb = bur_scr[:, j * half:(j + 1) * half].astype(bf16)
        xib = bui_scr[:, j * half:(j + 1) * half].astype(bf16)
        ys.append(jnp.dot(xrb, cre_ref[j], preferred_element_type=f32)
                  - jnp.dot(xib, cim_ref[j], preferred_element_type=f32))
    y_s5 = jnp.concatenate(ys, axis=1) + d_ref[...] * u
    yg = _gelu_tanh(y_s5)
    glu = jnp.dot(yg.astype(bf16), wglu_ref[...], preferred_element_type=f32) + bglu_ref[...]
    y_s5 = yg * jax.nn.sigmoid(glu)

    ycat = jnp.concatenate([y_rg, y_s5], axis=1).astype(bf16)
    mix = jnp.dot(ycat, wout_ref[...], preferred_element_type=f32)
    hout_ref[...] = _layer_norm(DN_ALPHA * x + mix, ln_g_ref[...], ln_b_ref[...])

    @pl.when(c == pl.num_programs(0) - 1)
    def _():
        conv_out_ref[...] = pad_scr[0:tail, :]
        hlast_ref[...] = hst_scr[...]
        sre_out_ref[...] = sr_scr[...]
        sim_out_ref[...] = si_scr[...]


def _full(shape):
    n = len(shape)
    return pl.BlockSpec(shape, lambda c: (0,) * n)


def _mixer(x, conv0, h0, s0r, s0i, params, *, nb, tc, name, h_all, total_rows, first_row):
    batch_major_input = x.ndim == 3
    rows = nb * tc
    if batch_major_input:
        n_chunks = x.shape[1] // tc
        x_spec = pl.BlockSpec((nb, tc, D_MODEL), lambda c: (0, c, 0))
    else:
        n_chunks = x.shape[0] // rows
        x_spec = pl.BlockSpec((rows, D_MODEL), lambda c: (c, 0))
    first_block = first_row // rows
    aliased = h_all.shape == (total_rows, D_MODEL)
    tail = (CONV_WIDTH - 1) * nb
    small = (conv0, h0, s0r, s0i) + tuple(params)
    in_specs = [pl.BlockSpec(memory_space=pl.ANY), x_spec]
    in_specs += [_full(a.shape) for a in small]
    out_shape = (jax.ShapeDtypeStruct((total_rows, D_MODEL), f32),
                 jax.ShapeDtypeStruct((tail, D_RG), f32),
                 jax.ShapeDtypeStruct((nb, D_RG), f32),
                 jax.ShapeDtypeStruct((nb, S5_N), f32),
                 jax.ShapeDtypeStruct((nb, S5_N), f32))
    out_specs = (pl.BlockSpec((rows, D_MODEL), lambda c: (c + first_block, 0)),
                 _full((tail, D_RG)), _full((nb, D_RG)), _full((nb, S5_N)), _full((nb, S5_N)))
    scratch = [pltpu.VMEM((rows + tail, D_RG), f32),
               pltpu.VMEM((rows, D_RG), f32),
               pltpu.VMEM((rows, D_RG), f32),
               pltpu.VMEM((rows, S5_N), f32),
               pltpu.VMEM((rows, S5_N), f32),
               pltpu.VMEM((nb, D_RG), f32),
               pltpu.VMEM((nb, S5_N), f32),
               pltpu.VMEM((nb, S5_N), f32)]
    return pl.pallas_call(
        functools.partial(_mixer_kernel, nb=nb, tc=tc, batch_major_input=batch_major_input),
        grid=(n_chunks,),
        in_specs=in_specs,
        out_specs=out_specs,
        out_shape=out_shape,
        scratch_shapes=scratch,
        input_output_aliases={0: 0} if aliased else {},
        compiler_params=pltpu.CompilerParams(
            dimension_semantics=("arbitrary",), vmem_limit_bytes=VMEM_LIMIT_BYTES),
        name=name,
    )(h_all, x, *small)


TOK_TILE = 256
RUN_ALIGN = 16
OVF_ALIGN = 8
FIX = 32
FIX_SLOTS = N_EXPERTS * FIX
OVF_CHUNKS = 256
OVF_CHUNK_SLOTS = 384
OVF_SLOTS = 6 * OVF_CHUNK_SLOTS
SLOT_CHUNK = 512
FIX_STEPS = 4
ROW_TILE = 256
X_BUFFERS = 4
COPY_GROUP = 4
EXT = D_MODEL + LANES
ROUTER_TILES = 6
NO_RUN = 1.0e9


def _top_k_gates(scores, rb):
    rows = scores.shape[0]
    lane_f = jax.lax.broadcasted_iota(jnp.int32, (rows, LANES), 1).astype(f32)
    biased = jnp.where(lane_f < float(N_EXPERTS), scores + rb, -jnp.inf)
    sel = jnp.zeros((rows, LANES), f32)
    mask = jnp.zeros((rows, LANES), f32)
    for _ in range(TOP_K):
        m = jnp.max(biased, axis=1, keepdims=True)
        idx = jnp.min(jnp.where(biased == m, lane_f, float(LANES)), axis=1, keepdims=True)
        hit = lane_f == idx
        sel = jnp.where(hit, scores, sel)
        mask = jnp.where(hit, 1.0, mask)
        biased = jnp.where(hit, -jnp.inf, biased)
    gates = sel / jnp.sum(sel, axis=1, keepdims=True) * ROUTED_SCALE
    return mask, gates


def _router_kernel(h_ref, wr_ref, rb_ref, hext_ref, rankm_ref, cnt_ref):
    hb = h_ref[...].astype(bf16)
    scores = jax.nn.sigmoid(jnp.dot(hb, wr_ref[...], preferred_element_type=f32))
    mask, gates = _top_k_gates(scores, rb_ref[...])
    t_row = jax.lax.broadcasted_iota(jnp.int32, (TOK_TILE, TOK_TILE), 0)
    t_col = jax.lax.broadcasted_iota(jnp.int32, (TOK_TILE, TOK_TILE), 1)
    earlier = jnp.where(t_col < t_row, 1.0, 0.0).astype(bf16)
    for sub in range(ROUTER_TILES):
        m = mask[sub * TOK_TILE:(sub + 1) * TOK_TILE]
        rank = jnp.dot(earlier, m.astype(bf16), preferred_element_type=f32)
        rankm_ref[sub * TOK_TILE:(sub + 1) * TOK_TILE, :] = jnp.where(m > 0.0, rank, -1.0).astype(bf16)
        cnt_ref[sub] = jnp.broadcast_to(jnp.sum(m, axis=0, keepdims=True), (SUBLANES, LANES))
    g_hi = gates.astype(bf16).astype(f32)
    g_pack = g_hi + pltpu.roll(gates - g_hi, N_EXPERTS, 1)
    hext_ref[:, :D_MODEL] = hb
    hext_ref[:, D_MODEL:] = g_pack.astype(bf16)


def _router(h, wr, rb):
    n_tiles = h.shape[0] // TOK_TILE
    assert n_tiles % ROUTER_TILES == 0
    rows = ROUTER_TILES * TOK_TILE
    const = lambda shape: pl.BlockSpec(shape, lambda i: (0,) * len(shape))
    return pl.pallas_call(
        _router_kernel,
        grid=(n_tiles // ROUTER_TILES,),
        in_specs=[pl.BlockSpec((rows, D_MODEL), lambda i: (i, 0)), const(wr.shape), const(rb.shape)],
        out_specs=(pl.BlockSpec((rows, EXT), lambda i: (i, 0)),
                   pl.BlockSpec((rows, LANES), lambda i: (i, 0)),
                   pl.BlockSpec((ROUTER_TILES, SUBLANES, LANES), lambda i: (i, 0, 0))),
        out_shape=(jax.ShapeDtypeStruct((h.shape[0], EXT), bf16),
                   jax.ShapeDtypeStruct((h.shape[0], LANES), bf16),
                   jax.ShapeDtypeStruct((n_tiles, SUBLANES, LANES), f32)),
        compiler_params=pltpu.CompilerParams(
            dimension_semantics=("arbitrary",), vmem_limit_bytes=VMEM_LIMIT_BYTES),
        name="moe_router",
    )(h, wr, rb)


def _run_copy(src, dst, sem):
    return pltpu.make_async_copy(src, dst, sem)


def _fixed_ranges(shape, axis):
    e = jax.lax.broadcasted_iota(jnp.int32, shape, axis).astype(f32)
    loc = jnp.where(e < float(N_EXPERTS), e * float(FIX), NO_RUN)
    return loc, loc + float(FIX)


def _slot_rows(first_slot, loc_row, end_row, base, rankm, n_slots=SLOT_CHUNK):
    s = (jax.lax.broadcasted_iota(jnp.int32, (n_slots, LANES), 0) + first_slot).astype(f32)
    in_run = jnp.where(s >= loc_row, jnp.where(s < end_row, 1.0, 0.0), 0.0)
    r_col = s[:, 0:1] - jnp.sum(in_run * (loc_row - base), axis=1, keepdims=True)
    q = jax.lax.dot_general(in_run.astype(bf16), rankm, (((1,), (1,)), ((), ())),
                            preferred_element_type=f32)
    return jnp.where(q == r_col, 1.0, 0.0).astype(bf16)


def _slot_cols(first_slot, loc_col, end_col, base, rankm, n_slots=SLOT_CHUNK):
    s = (jax.lax.broadcasted_iota(jnp.int32, (LANES, n_slots), 1) + first_slot).astype(f32)
    in_run = jnp.where(s >= loc_col, jnp.where(s < end_col, 1.0, 0.0), 0.0)
    r_row = s[0:1, :] - jnp.sum(in_run * (loc_col - base), axis=0, keepdims=True)
    q = jnp.dot(rankm, in_run.astype(bf16), preferred_element_type=f32)
    return jnp.where(q == r_row, 1.0, 0.0).astype(bf16)


def _wait_groups(n_groups, make_copy):
    max_bits = (OVF_CHUNKS // COPY_GROUP).bit_length()
    for b in range(max_bits):
        rows = (COPY_GROUP * OVF_ALIGN) << b

        @pl.when(jnp.bitwise_and(jnp.right_shift(n_groups, b), 1) == 1)
        def _():
            make_copy(rows).wait()


def _dispatch_kernel(totc_ref, ngrp_ref, dstk_ref, zst_ref, zch_ref,
                     hext_ref, rankm_ref, locs_ref, sorted_f_ref, sorted_o_ref,
                     stage_f, stage_o, zbuf, semf, semo, zsem):
    i = pl.program_id(0)
    n = pl.num_programs(0)
    slot = i % 2
    rankm = rankm_ref[...]

    fx_loc, fx_end = _fixed_ranges((1, LANES), 1)
    experts_per_chunk = SLOT_CHUNK // FIX
    for ch in range(FIX_SLOTS // SLOT_CHUNK):
        p = _slot_rows(ch * SLOT_CHUNK, fx_loc, fx_end, 0.0, rankm)
        rows = jnp.dot(p, hext_ref[...], preferred_element_type=f32).astype(bf16)
        stage_f[slot, ch * experts_per_chunk:(ch + 1) * experts_per_chunk] = rows.reshape(
            experts_per_chunk, FIX, EXT)

    ov_loc = locs_ref[0, 0:1, :]
    ov_end = locs_ref[0, 1:2, :]
    for ch in range(OVF_SLOTS // OVF_CHUNK_SLOTS):
        def sort_overflow(ch=ch):
            p = _slot_rows(FIX_SLOTS + ch * OVF_CHUNK_SLOTS, ov_loc, ov_end, float(FIX), rankm,
                           OVF_CHUNK_SLOTS)
            stage_o[slot, ch * OVF_CHUNK_SLOTS:(ch + 1) * OVF_CHUNK_SLOTS, :] = jnp.dot(
                p, hext_ref[...], preferred_element_type=f32)
        pl.when(totc_ref[i] * OVF_ALIGN > ch * OVF_CHUNK_SLOTS)(sort_overflow)

    def fixed_copy(tile, slot_):
        dst = sorted_f_ref.at[:, pl.ds(pl.multiple_of(tile * FIX, FIX), FIX), :]
        return _run_copy(stage_f.at[slot_], dst, semf.at[slot_])

    def overflow_wait(rows, slot_):
        return _run_copy(stage_o.at[slot_, pl.ds(0, rows)], sorted_o_ref.at[pl.ds(0, rows)],
                         semo.at[slot_])

    fixed_copy(i, slot).start()

    def start_group(g, carry):
        for u in range(COPY_GROUP):
            k = g * COPY_GROUP + u
            src = stage_o.at[slot, pl.ds(pl.multiple_of(k * OVF_ALIGN, OVF_ALIGN), OVF_ALIGN)]
            dst = sorted_o_ref.at[pl.ds(pl.multiple_of(dstk_ref[i, k], OVF_ALIGN), OVF_ALIGN)]
            _run_copy(src, dst, semo.at[slot]).start()
        return carry
    jax.lax.fori_loop(0, ngrp_ref[i], start_group, 0)

    @pl.when(i == 0)
    def _():
        zbuf[...] = jnp.zeros(zbuf.shape, f32)

    def zero_copy(e, k):
        dst = sorted_o_ref.at[pl.ds(pl.multiple_of(zst_ref[e] + k * OVF_ALIGN, OVF_ALIGN), OVF_ALIGN)]
        return _run_copy(zbuf, dst, zsem.at[0])

    @pl.when(jnp.logical_and(i > 0, i <= N_EXPERTS))
    def _():
        def wait_zero(k, carry):
            zero_copy(i - 1, k).wait()
            return carry
        jax.lax.fori_loop(0, zch_ref[i - 1], wait_zero, 0)

    @pl.when(i < N_EXPERTS)
    def _():
        def start_zero(k, carry):
            zero_copy(i, k).start()
            return carry
        jax.lax.fori_loop(0, zch_ref[i], start_zero, 0)

    @pl.when(i > 0)
    def _():
        fixed_copy(i - 1, 1 - slot).wait()
        _wait_groups(ngrp_ref[i - 1], lambda rows: overflow_wait(rows, 1 - slot))

    @pl.when(i == n - 1)
    def _():
        fixed_copy(i, slot).wait()
        _wait_groups(ngrp_ref[i], lambda rows: overflow_wait(rows, slot))


def _dispatch(hext, rankm, locs, totc, ngrp, dstk, zst, zch, *, overflow_rows):
    n_tiles = hext.shape[0] // TOK_TILE
    assert n_tiles > N_EXPERTS
    grid_spec = pltpu.PrefetchScalarGridSpec(
        num_scalar_prefetch=5,
        grid=(n_tiles,),
        in_specs=[pl.BlockSpec((TOK_TILE, EXT), lambda i, *_: (i, 0)),
                  pl.BlockSpec((TOK_TILE, LANES), lambda i, *_: (i, 0)),
                  pl.BlockSpec((1, SUBLANES, LANES), lambda i, *_: (i, 0, 0))],
        out_specs=(pl.BlockSpec(memory_space=pl.ANY), pl.BlockSpec(memory_space=pl.ANY)),
        scratch_shapes=[pltpu.VMEM((2, N_EXPERTS, FIX, EXT), bf16),
                        pltpu.VMEM((2, OVF_SLOTS, EXT), f32),
                        pltpu.VMEM((OVF_ALIGN, EXT), f32),
                        pltpu.SemaphoreType.DMA((2,)),
                        pltpu.SemaphoreType.DMA((2,)),
                        pltpu.SemaphoreType.DMA((1,))])
    return pl.pallas_call(
        _dispatch_kernel,
        grid_spec=grid_spec,
        out_shape=(jax.ShapeDtypeStruct((N_EXPERTS, n_tiles * FIX, EXT), bf16),
                   jax.ShapeDtypeStruct((overflow_rows, EXT), f32)),
        compiler_params=pltpu.CompilerParams(
            dimension_semantics=("arbitrary",), vmem_limit_bytes=VMEM_LIMIT_BYTES),
        name="moe_dispatch",
    )(totc, ngrp, dstk, zst, zch, hext, rankm, locs)


def _experts_kernel(ts_ref, xf_ref, xo_ref, wgu_ref, wd_ref, yf_ref, yo_ref,
                    xbuf, ybuf, wgu_b, wd_b, xsem, ysem):
    e = pl.program_id(0)
    n_valid = ts_ref[N_EXPERTS]
    wgu_b[...] = wgu_ref[0].astype(bf16)
    wd_b[...] = wd_ref[0].astype(bf16)

    def ffn(x_ext):
        m = x_ext.shape[0]
        gu = jnp.dot(x_ext[:, :D_MODEL], wgu_b[...], preferred_element_type=f32)
        act = (jax.nn.silu(gu[:, :D_EXPERT]) * gu[:, D_EXPERT:]).astype(bf16)
        y = jnp.dot(act, wd_b[...], preferred_element_type=f32)
        g_pack = x_ext[:, D_MODEL:].astype(f32)
        lane = jax.lax.broadcasted_iota(jnp.int32, (m, LANES), 1)
        mine = jnp.where(lane == e, g_pack, jnp.where(lane == e + N_EXPERTS, g_pack, 0.0))
        return (y * jnp.sum(mine, axis=1, keepdims=True)).astype(bf16)

    def x_copy(j, slot):
        rows = pl.ds(pl.multiple_of(j * ROW_TILE, ROW_TILE), ROW_TILE)
        return pltpu.make_async_copy(xo_ref.at[rows], xbuf.at[slot], xsem.at[slot])

    def y_copy(j, slot):
        rows = pl.ds(pl.multiple_of(j * ROW_TILE, ROW_TILE), ROW_TILE)
        return pltpu.make_async_copy(ybuf.at[slot], yo_ref.at[rows], ysem.at[slot])

    ahead = X_BUFFERS - 1

    @pl.when(e == 0)
    def _():
        for a in range(ahead):
            @pl.when(a < n_valid)
            def _():
                x_copy(a, a).start()

    step_rows = xf_ref.shape[1] // FIX_STEPS
    for c in range(FIX_STEPS):
        rows = slice(c * step_rows, (c + 1) * step_rows)
        yf_ref[0, rows, :] = ffn(xf_ref[0, rows, :])

    def tile(j, carry):
        xslot = j % X_BUFFERS
        slot = j % 2
        x_copy(j, xslot).wait()

        @pl.when(j + ahead < n_valid)
        def _():
            x_copy(j + ahead, (j + ahead) % X_BUFFERS).start()

        @pl.when(j >= 2)
        def _():
            y_copy(j - 2, slot).wait()

        ybuf[slot] = ffn(xbuf[xslot].astype(bf16)).astype(f32)
        y_copy(j, slot).start()
        return carry
    jax.lax.fori_loop(ts_ref[e], ts_ref[e + 1], tile, 0)

    @pl.when(e == pl.num_programs(0) - 1)
    def _():
        @pl.when(n_valid >= 2)
        def _():
            y_copy(n_valid - 2, n_valid % 2).wait()

        @pl.when(n_valid >= 1)
        def _():
            y_copy(n_valid - 1, (n_valid - 1) % 2).wait()


def _experts(xf, xo, wgu, wd, ts, *, overflow_rows):
    fixed_rows = xf.shape[1]
    assert fixed_rows % (FIX_STEPS * RUN_ALIGN) == 0
    grid_spec = pltpu.PrefetchScalarGridSpec(
        num_scalar_prefetch=1,
        grid=(N_EXPERTS,),
        in_specs=[pl.BlockSpec((1, fixed_rows, EXT), lambda e, ts: (e, 0, 0)),
                  pl.BlockSpec(memory_space=pl.ANY),
                  pl.BlockSpec((1, D_MODEL, 2 * D_EXPERT), lambda e, ts: (e, 0, 0)),
                  pl.BlockSpec((1, D_EXPERT, D_MODEL), lambda e, ts: (e, 0, 0))],
        out_specs=(pl.BlockSpec((1, fixed_rows, D_MODEL), lambda e, ts: (e, 0, 0)),
                   pl.BlockSpec(memory_space=pl.ANY)),
        scratch_shapes=[pltpu.VMEM((X_BUFFERS, ROW_TILE, EXT), f32),
                        pltpu.VMEM((2, ROW_TILE, D_MODEL), f32),
                        pltpu.VMEM((D_MODEL, 2 * D_EXPERT), bf16),
                        pltpu.VMEM((D_EXPERT, D_MODEL), bf16),
                        pltpu.SemaphoreType.DMA((X_BUFFERS,)),
                        pltpu.SemaphoreType.DMA((2,))])
    return pl.pallas_call(
        _experts_kernel,
        grid_spec=grid_spec,
        out_shape=(jax.ShapeDtypeStruct((N_EXPERTS, fixed_rows, D_MODEL), bf16),
                   jax.ShapeDtypeStruct((overflow_rows, D_MODEL), f32)),
        compiler_params=pltpu.CompilerParams(
            dimension_semantics=("arbitrary",), vmem_limit_bytes=VMEM_LIMIT_BYTES),
        name="moe_experts",
    )(ts, xf, xo, wgu, wd)


def _combine_kernel(totc_ref, ngrp_ref, srck_ref,
                    h_ref, rankm_ref, locc_ref, yf_ref, yo_ref, wsu_ref, wsd_ref, ln_g_ref, ln_b_ref,
                    out_bm_ref, out_tm_ref, yloc_f, yloc_o, acc_scr, p_scr, semf, semo,
                    *, n_bm_tiles):
    i = pl.program_id(0)
    n = pl.num_programs(0)
    slot = i % 2

    def fixed_copy(tile, slot_):
        src = yf_ref.at[:, pl.ds(pl.multiple_of(tile * FIX, FIX), FIX), :]
        return _run_copy(src, yloc_f.at[slot_], semf.at[slot_])

    def overflow_wait(rows, slot_):
        return _run_copy(yo_ref.at[pl.ds(0, rows)], yloc_o.at[slot_, pl.ds(0, rows)], semo.at[slot_])

    def fetch(tile, slot_):
        fixed_copy(tile, slot_).start()

        def start_group(g, carry):
            for u in range(COPY_GROUP):
                k = g * COPY_GROUP + u
                src = yo_ref.at[pl.ds(pl.multiple_of(srck_ref[tile, k], OVF_ALIGN), OVF_ALIGN)]
                dst = yloc_o.at[slot_, pl.ds(pl.multiple_of(k * OVF_ALIGN, OVF_ALIGN), OVF_ALIGN)]
                _run_copy(src, dst, semo.at[slot_]).start()
            return carry
        jax.lax.fori_loop(0, ngrp_ref[tile], start_group, 0)

    @pl.when(i == 0)
    def _():
        yloc_o[...] = jnp.zeros(yloc_o.shape, f32)
        fetch(0, 0)

    @pl.when(i + 1 < n)
    def _():
        fetch(i + 1, 1 - slot)

    h = h_ref[...]
    hb = h.astype(bf16)
    su = jnp.dot(hb, wsu_ref[...], preferred_element_type=f32)
    act = (jax.nn.silu(su[:, :D_SHARED]) * su[:, D_SHARED:]).astype(bf16)
    acc_scr[...] = jnp.dot(act, wsd_ref[...], preferred_element_type=f32)

    rankm = rankm_ref[...]
    fx_loc, fx_end = _fixed_ranges((LANES, 1), 0)
    for ch in range(FIX_SLOTS // SLOT_CHUNK):
        p_scr[:, ch * SLOT_CHUNK:(ch + 1) * SLOT_CHUNK] = _slot_cols(
            ch * SLOT_CHUNK, fx_loc, fx_end, 0.0, rankm)

    fixed_copy(i, slot).wait()
    _wait_groups(ngrp_ref[i], lambda rows: overflow_wait(rows, slot))

    acc_scr[...] += jnp.dot(p_scr[...], yloc_f[slot].reshape(FIX_SLOTS, D_MODEL),
                            preferred_element_type=f32)
    ov_loc = locc_ref[0, :, 0:1]
    ov_end = locc_ref[0, :, 1:2]
    for ch in range(OVF_SLOTS // OVF_CHUNK_SLOTS):
        def gather_overflow(ch=ch):
            p = _slot_cols(FIX_SLOTS + ch * OVF_CHUNK_SLOTS, ov_loc, ov_end, float(FIX), rankm,
                           OVF_CHUNK_SLOTS)
            y_o = yloc_o[slot, ch * OVF_CHUNK_SLOTS:(ch + 1) * OVF_CHUNK_SLOTS, :].astype(bf16)
            acc_scr[...] += jnp.dot(p, y_o, preferred_element_type=f32)
        pl.when(totc_ref[i] * OVF_ALIGN > ch * OVF_CHUNK_SLOTS)(gather_overflow)
    y = _layer_norm(DN_ALPHA * h + acc_scr[...], ln_g_ref[...], ln_b_ref[...])

    @pl.when(i < n_bm_tiles)
    def _():
        nb, tc, _ = out_bm_ref.shape
        out_bm_ref[...] = jnp.transpose(y.reshape(tc, nb, D_MODEL), (1, 0, 2))

    @pl.when(i >= n_bm_tiles)
    def _():
        out_tm_ref[...] = y


def _combine(h, rankm, locc, yf, yo, wsu, wsd, ln_g, ln_b, totc, ngrp, srck, *, bm_shape):
    n_tiles = h.shape[0] // TOK_TILE
    nb, length, _ = bm_shape
    tc = TOK_TILE // nb
    n_bm_tiles = length // tc
    n_tm_tiles = n_tiles - n_bm_tiles
    const = lambda shape: pl.BlockSpec(shape, lambda i, *_: (0,) * len(shape))
    grid_spec = pltpu.PrefetchScalarGridSpec(
        num_scalar_prefetch=3,
        grid=(n_tiles,),
        in_specs=[pl.BlockSpec((TOK_TILE, D_MODEL), lambda i, *_: (i, 0)),
                  pl.BlockSpec((TOK_TILE, LANES), lambda i, *_: (i, 0)),
                  pl.BlockSpec((1, LANES, 2), lambda i, *_: (i, 0, 0)),
                  pl.BlockSpec(memory_space=pl.ANY),
                  pl.BlockSpec(memory_space=pl.ANY),
                  const(wsu.shape), const(wsd.shape), const(ln_g.shape), const(ln_b.shape)],
        out_specs=(pl.BlockSpec((nb, tc, D_MODEL),
                                lambda i, *_: (0, jnp.minimum(i, n_bm_tiles - 1), 0)),
                   pl.BlockSpec((TOK_TILE, D_MODEL),
                                lambda i, *_: (jnp.maximum(i - n_bm_tiles, 0), 0))),
        scratch_shapes=[pltpu.VMEM((2, N_EXPERTS, FIX, D_MODEL), bf16),
                        pltpu.VMEM((2, OVF_SLOTS, D_MODEL), f32),
                        pltpu.VMEM((TOK_TILE, D_MODEL), f32),
                        pltpu.VMEM((TOK_TILE, FIX_SLOTS), bf16),
                        pltpu.SemaphoreType.DMA((2,)),
                        pltpu.SemaphoreType.DMA((2,))])
    return pl.pallas_call(
        functools.partial(_combine_kernel, n_bm_tiles=n_bm_tiles),
        grid_spec=grid_spec,
        out_shape=(jax.ShapeDtypeStruct(bm_shape, f32),
                   jax.ShapeDtypeStruct((n_tm_tiles * TOK_TILE, D_MODEL), f32)),
        compiler_params=pltpu.CompilerParams(
            dimension_semantics=("arbitrary",), vmem_limit_bytes=VMEM_LIMIT_BYTES),
        name="moe_combine",
    )(totc, ngrp, srck, h, rankm, locc, yf, yo, wsu, wsd, ln_g, ln_b)


def _round_up(x, m):
    return (x + m - 1) // m * m


def _moe(h, wr, rb, wgu, wd, wsu, wsd, ln_g, ln_b, *, bm_shape):
    n_tok = h.shape[0]
    n_tiles = n_tok // TOK_TILE
    max_rows = _round_up(n_tiles * (OVF_CHUNKS - 1) * OVF_ALIGN + N_EXPERTS * (ROW_TILE - 1), ROW_TILE)

    hext, rankm, cnt = _router(h, wr, rb)

    i32 = jnp.int32
    cnt = cnt[:, 0, :N_EXPERTS].astype(i32)
    oc = (jnp.maximum(cnt - FIX, 0) + OVF_ALIGN - 1) // OVF_ALIGN
    over_tiles = jnp.cumsum(oc, axis=0)
    region_rows = over_tiles[-1] * OVF_ALIGN
    region_size = _round_up(region_rows, ROW_TILE)
    region_start = jnp.cumsum(region_size) - region_size
    run_dst = region_start[None, :] + OVF_ALIGN * (over_tiles - oc)
    ch_end = jnp.cumsum(oc, axis=1)
    ch_beg = ch_end - oc
    totc = ch_end[:, -1]
    k = jnp.arange(OVF_CHUNKS, dtype=i32)
    kk = k[None, :, None]
    mine = jnp.logical_and(ch_beg[:, None, :] <= kk, kk < ch_end[:, None, :])
    chunk_dst = jnp.sum(jnp.where(mine, run_dst[:, None, :] + OVF_ALIGN * (kk - ch_beg[:, None, :]), 0),
                        axis=-1)
    live = k[None, :] < totc[:, None]
    tile_ids = jnp.arange(n_tiles, dtype=i32)[:, None]
    spare = max_rows + (tile_ids % 2) * OVF_SLOTS + k[None, :] * OVF_ALIGN
    dstk = jnp.where(live, chunk_dst, spare).astype(i32)
    srck = jnp.where(live, chunk_dst, 0).astype(i32)
    ngrp = ((totc + COPY_GROUP - 1) // COPY_GROUP).astype(i32)
    zst = (region_start + region_rows).astype(i32)
    zch = ((region_size - region_rows) // OVF_ALIGN).astype(i32)
    tile_start = jnp.concatenate([region_start, region_start[-1:] + region_size[-1:]]) // ROW_TILE
    tile_start = tile_start.astype(i32)

    pad = ((0, 0), (0, LANES - N_EXPERTS))
    loc_f = jnp.pad((FIX_SLOTS + OVF_ALIGN * ch_beg).astype(f32), pad, constant_values=NO_RUN)
    end_f = jnp.pad((FIX_SLOTS + OVF_ALIGN * ch_end).astype(f32), pad, constant_values=NO_RUN)
    locs = jnp.concatenate([loc_f[:, None, :], end_f[:, None, :],
                            jnp.zeros((n_tiles, SUBLANES - 2, LANES), f32)], axis=1)
    locc = jnp.stack([loc_f, end_f], axis=-1)

    totc = totc.astype(i32)
    overflow_rows = max_rows + 2 * OVF_SLOTS
    xf, xo = _dispatch(hext, rankm, locs, totc, ngrp, dstk, zst, zch, overflow_rows=overflow_rows)
    yf, yo = _experts(xf, xo, wgu, wd, tile_start, overflow_rows=max_rows)
    return _combine(h, rankm, locc, yf, yo, wsu, wsd, ln_g, ln_b, totc, ngrp, srck, bm_shape=bm_shape)


def _diag_tiles(blocks, n_tiles):
    n_blocks, r, c = blocks.shape
    per = n_blocks // n_tiles
    wide = jnp.tile(blocks.reshape(n_blocks * r, c), (1, per))
    row_block = (jnp.arange(n_blocks * r) // r) % per
    col_block = jnp.arange(per * c) // c
    wide = jnp.where(row_block[:, None] == col_block[None, :], wide, 0)
    return wide.reshape(n_tiles, per * r, per * c)


def _head_block_diag(w):
    return _diag_tiles(w, D_RG // MXU_DIM)


def _s5_in_tiles(b):
    return _diag_tiles(b.transpose(0, 2, 1), 2)


def _s5_out_tiles(cw):
    return _diag_tiles(cw.transpose(0, 2, 1), 2)


def _row(v):
    return v.reshape(1, -1)


def kernel(x_prompt, x_sample, state_rg_conv, state_rg_h, state_s5_re, state_s5_im, w_in, conv_w, conv_b, rg_w_a, rg_b_a, rg_w_i, rg_b_i, rg_lam, s5_a_re, s5_a_im, s5_log_dt, s5_b_re, s5_b_im, s5_c_re, s5_c_im, s5_d, w_glu, b_glu, w_out, ln1_g, ln1_b, w_router, router_bias, w_gate_up, w_down, w_shared_up, w_shared_down, ln2_g, ln2_b):
    l = 0
    bp, lp, _ = x_prompt.shape
    bs, ls, _ = x_sample.shape

    are, aim, bbre, bbim = _s5_prep(
        _row(s5_a_re[l]), _row(s5_a_im[l]),
        _row(jnp.repeat(s5_log_dt[l], S5_STATE)),
        _s5_in_tiles(s5_b_re[l]), _s5_in_tiles(s5_b_im[l]))
    params = (w_in[l].astype(bf16), conv_w[l], _row(conv_b[l]),
              _head_block_diag(rg_w_a[l]).astype(bf16), _row(rg_b_a[l]),
              _head_block_diag(rg_w_i[l]).astype(bf16), _row(rg_b_i[l]), _row(rg_lam[l]),
              are, aim, bbre, bbim,
              _s5_out_tiles(s5_c_re[l]).astype(bf16), _s5_out_tiles(s5_c_im[l]).astype(bf16),
              _row(s5_d[l]), w_glu[l].astype(bf16), _row(b_glu[l]), w_out[l].astype(bf16),
              _row(ln1_g[l]), _row(ln1_b[l]))

    tail = CONV_WIDTH - 1
    n_tok = lp * bp + ls * bs
    xs_tm = x_sample.transpose(1, 0, 2).reshape(ls * bs, D_MODEL)
    h_all, sc, sh, sre, sim = _mixer(
        xs_tm, state_rg_conv[l].transpose(1, 0, 2).reshape(tail * bs, D_RG), state_rg_h[l],
        state_s5_re[l].reshape(bs, S5_N), state_s5_im[l].reshape(bs, S5_N), params,
        nb=bs, tc=ls, name="mixer_sample",
        h_all=jnp.zeros((SUBLANES, LANES), f32), total_rows=n_tok, first_row=lp * bp)
    h_all, pc, ph, pre, pim = _mixer(
        x_prompt, jnp.zeros((tail * bp, D_RG), f32), jnp.zeros((bp, D_RG), f32),
        jnp.zeros((bp, S5_N), f32), jnp.zeros((bp, S5_N), f32), params,
        nb=bp, tc=PROMPT_CHUNK_ROWS // bp, name="mixer_prompt",
        h_all=h_all, total_rows=n_tok, first_row=0)
    wr = jnp.pad(w_router[l], ((0, 0), (0, LANES - N_EXPERTS))).astype(bf16)
    rb = jnp.pad(_row(router_bias[l]), ((0, 0), (0, LANES - N_EXPERTS)))
    yp, ys_tm = _moe(h_all, wr, rb, w_gate_up[l], w_down[l],
                     w_shared_up[l].astype(bf16), w_shared_down[l].astype(bf16),
                     _row(ln2_g[l]), _row(ln2_b[l]), bm_shape=x_prompt.shape)
    ys = ys_tm.reshape(ls, bs, D_MODEL).transpose(1, 0, 2)

    def conv_out(cv, nbatch):
        return cv.reshape(tail, nbatch, D_RG).transpose(1, 0, 2)[None]

    return (yp, ys,
            conv_out(pc, bp), ph[None],
            pre.reshape(1, bp, S5_GROUPS, S5_STATE), pim.reshape(1, bp, S5_GROUPS, S5_STATE),
            conv_out(sc, bs), sh[None],
            sre.reshape(1, bs, S5_GROUPS, S5_STATE), sim.reshape(1, bs, S5_GROUPS, S5_STATE))
```

```python
import functools
import math

import jax
import jax.numpy as jnp
from jax.experimental import pallas as pl
from jax.experimental.pallas import tpu as pltpu

D_MODEL = 1024
D_RG = 512
RG_HEADS = 8
RG_HEAD_DIM = 64
CONV_WIDTH = 4
RG_C = 8.0
D_S5 = 512
S5_GROUP = 16
S5_GROUPS = 32
S5_STATE = 64
S5_N = S5_GROUPS * S5_STATE
N_EXPERTS = 64
TOP_K = 8
D_EXPERT = 256
D_SHARED = 256
ROUTED_SCALE = 2.5
DEPTH = 1
DN_ALPHA = (2.0 * DEPTH) ** 0.25
LN_EPS = 1e-5

SUBLANES = 8
LANES = 128
MXU_DIM = 256
S5_SCAN_COLS = 512
PROMPT_CHUNK_ROWS = 512
VMEM_LIMIT_BYTES = 56 * 1024 * 1024

bf16 = jnp.bfloat16
f32 = jnp.float32


def _gelu_tanh(x):
    c = math.sqrt(2.0 / math.pi)
    return x * (0.5 * (1.0 + jnp.tanh(c * (x + 0.044715 * (x * x * x)))))


def _layer_norm(x, g, b):
    mu = jnp.mean(x, axis=-1, keepdims=True)
    xc = x - mu
    var = jnp.mean(xc * xc, axis=-1, keepdims=True)
    return xc * jax.lax.rsqrt(var + LN_EPS) * g + b


def _s5_prep_kernel(lr_ref, li_ref, ldt_ref, bre_ref, bim_ref,
                    are_ref, aim_ref, bbre_ref, bbim_ref):
    lr = lr_ref[...]
    li = li_ref[...]
    dt = jnp.exp(ldt_ref[...])
    mag = jnp.exp(lr * dt)
    abar_re = mag * jnp.cos(li * dt)
    abar_im = mag * jnp.sin(li * dt)
    den = lr * lr + li * li
    nr = abar_re - 1.0
    ni = abar_im
    coef_re = (nr * lr + ni * li) / den
    coef_im = (ni * lr - nr * li) / den
    are_ref[...] = abar_re
    aim_ref[...] = abar_im
    half = S5_N // 2
    for k in range(2):
        cre = coef_re[:, k * half:(k + 1) * half]
        cim = coef_im[:, k * half:(k + 1) * half]
        br = bre_ref[k]
        bi = bim_ref[k]
        bbre_ref[k] = (cre * br - cim * bi).astype(bf16)
        bbim_ref[k] = (cre * bi + cim * br).astype(bf16)


def _s5_prep(lr, li, ldt, bre_t, bim_t):
    half = S5_N // 2
    return pl.pallas_call(
        _s5_prep_kernel,
        out_shape=(jax.ShapeDtypeStruct((1, S5_N), f32),
                   jax.ShapeDtypeStruct((1, S5_N), f32),
                   jax.ShapeDtypeStruct((2, MXU_DIM, half), bf16),
                   jax.ShapeDtypeStruct((2, MXU_DIM, half), bf16)),
        name="s5_prep",
    )(lr, li, ldt, bre_t, bim_t)


def _mixer_kernel(h_all_ref, x_ref, conv0_ref, h0_ref, s0r_ref, s0i_ref,
                  w_in_ref, conv_w_ref, conv_b_ref, wa_ref, ba_ref, wi_ref, bi_ref, lam_ref,
                  are_ref, aim_ref, bbre_ref, bbim_ref, cre_ref, cim_ref, d_ref,
                  wglu_ref, bglu_ref, wout_ref, ln_g_ref, ln_b_ref,
                  hout_ref, conv_out_ref, hlast_ref, sre_out_ref, sim_out_ref,
                  pad_scr, a_scr, b_scr, bur_scr, bui_scr, hst_scr, sr_scr, si_scr,
                  *, nb, tc, batch_major_input):
    del h_all_ref
    rows = nb * tc
    tail = (CONV_WIDTH - 1) * nb
    c = pl.program_id(0)

    @pl.when(c == 0)
    def _():
        pad_scr[0:tail, :] = conv0_ref[...]
        hst_scr[...] = h0_ref[...]
        sr_scr[...] = s0r_ref[...]
        si_scr[...] = s0i_ref[...]

    if batch_major_input:
        x = jnp.transpose(x_ref[...], (1, 0, 2)).reshape(rows, D_MODEL)
    else:
        x = x_ref[...]
    proj = jnp.dot(x.astype(bf16), w_in_ref[...], preferred_element_type=f32)
    x_rg = proj[:, :D_RG]
    g_rg = proj[:, D_RG:2 * D_RG]
    u = proj[:, 2 * D_RG:]

    pad_scr[tail:tail + rows, :] = x_rg
    conv_w = conv_w_ref[...]
    acc = conv_w[0:1, :] * pad_scr[0:rows, :]
    for k in range(1, CONV_WIDTH):
        acc = acc + conv_w[k:k + 1, :] * pad_scr[k * nb:k * nb + rows, :]
    xc = conv_b_ref[...] + acc
    new_tail = pad_scr[rows:rows + tail, :]
    pad_scr[0:tail, :] = new_tail

    xcb = xc.astype(bf16)
    ga = []
    gi = []
    for hh in range(D_RG // MXU_DIM):
        xs = xcb[:, hh * MXU_DIM:(hh + 1) * MXU_DIM]
        ga.append(jnp.dot(xs, wa_ref[hh], preferred_element_type=f32))
        gi.append(jnp.dot(xs, wi_ref[hh], preferred_element_type=f32))
    r = jax.nn.sigmoid(jnp.concatenate(ga, axis=1) + ba_ref[...])
    i = jax.nn.sigmoid(jnp.concatenate(gi, axis=1) + bi_ref[...])
    nlam = -lam_ref[...]
    softplus = jnp.maximum(nlam, 0.0) + jnp.log1p(jnp.exp(-jnp.abs(nlam)))
    log_a = (-RG_C) * r * softplus
    a_scr[...] = jnp.exp(log_a)
    th = jnp.tanh(log_a)
    b_scr[...] = jnp.sqrt((-2.0 * th) / (1.0 - th)) * (i * xc)

    for rg in range(nb // SUBLANES):
        r0 = rg * SUBLANES
        h = hst_scr[r0:r0 + SUBLANES, :]
        for t in range(tc):
            q = t * nb + r0
            h = a_scr[q:q + SUBLANES, :] * h + b_scr[q:q + SUBLANES, :]
            b_scr[q:q + SUBLANES, :] = h
        hst_scr[r0:r0 + SUBLANES, :] = h
    y_rg = b_scr[...] * _gelu_tanh(g_rg)

    ub = u.astype(bf16)
    half = S5_N // 2
    for k in range(2):
        us = ub[:, k * MXU_DIM:(k + 1) * MXU_DIM]
        bur_scr[:, k * half:(k + 1) * half] = jnp.dot(us, bbre_ref[k], preferred_element_type=f32)
        bui_scr[:, k * half:(k + 1) * half] = jnp.dot(us, bbim_ref[k], preferred_element_type=f32)
    for rg in range(nb // SUBLANES):
        r0 = rg * SUBLANES
        for cb in range(S5_N // S5_SCAN_COLS):
            c0 = cb * S5_SCAN_COLS
            ar = jnp.broadcast_to(are_ref[:, c0:c0 + S5_SCAN_COLS], (SUBLANES, S5_SCAN_COLS))
            ai = jnp.broadcast_to(aim_ref[:, c0:c0 + S5_SCAN_COLS], (SUBLANES, S5_SCAN_COLS))
            xr = sr_scr[r0:r0 + SUBLANES, c0:c0 + S5_SCAN_COLS]
            xi = si_scr[r0:r0 + SUBLANES, c0:c0 + S5_SCAN_COLS]
            for t in range(tc):
                q = t * nb + r0
                br = bur_scr[q:q + SUBLANES, c0:c0 + S5_SCAN_COLS]
                bi_ = bui_scr[q:q + SUBLANES, c0:c0 + S5_SCAN_COLS]
                nxr = ar * xr - ai * xi + br
                nxi = ar * xi + ai * xr + bi_
                bur_scr[q:q + SUBLANES, c0:c0 + S5_SCAN_COLS] = nxr
                bui_scr[q:q + SUBLANES, c0:c0 + S5_SCAN_COLS] = nxi
                xr, xi = nxr, nxi
            sr_scr[r0:r0 + SUBLANES, c0:c0 + S5_SCAN_COLS] = xr
            si_scr[r0:r0 + SUBLANES, c0:c0 + S5_SCAN_COLS] = xi
    ys = []
    for j in range(D_S5 // MXU_DIM):
        xrb = bur_scr[:, j * half:(j + 1) * half].astype(bf16)
        xib = bui_scr[:, j * half:(j + 1) * half].astype(bf16)
        ys.append(jnp.dot(xrb, cre_ref[j], preferred_element_type=f32)
                  - jnp.dot(xib, cim_ref[j], preferred_element_type=f32))
    y_s5 = jnp.concatenate(ys, axis=1) + d_ref[...] * u
    yg = _gelu_tanh(y_s5)
    glu = jnp.dot(yg.astype(bf16), wglu_ref[...], preferred_element_type=f32) + bglu_ref[...]
    y_s5 = yg * jax.nn.sigmoid(glu)

    ycat = jnp.concatenate([y_rg, y_s5], axis=1).astype(bf16)
    mix = jnp.dot(ycat, wout_ref[...], preferred_element_type=f32)
    hout_ref[...] = _layer_norm(DN_ALPHA * x + mix, ln_g_ref[...], ln_b_ref[...])

    @pl.when(c == pl.num_programs(0) - 1)
    def _():
        conv_out_ref[...] = pad_scr[0:tail, :]
        hlast_ref[...] = hst_scr[...]
        sre_out_ref[...] = sr_scr[...]
        sim_out_ref[...] = si_scr[...]


def _full(shape):
    n = len(shape)
    return pl.BlockSpec(shape, lambda c: (0,) * n)


def _mixer(x, conv0, h0, s0r, s0i, params, *, nb, tc, name, h_all, total_rows, first_row):
    batch_major_input = x.ndim == 3
    rows = nb * tc
    if batch_major_input:
        n_chunks = x.shape[1] // tc
        x_spec = pl.BlockSpec((nb, tc, D_MODEL), lambda c: (0, c, 0))
    else:
        n_chunks = x.shape[0] // rows
        x_spec = pl.BlockSpec((rows, D_MODEL), lambda c: (c, 0))
    first_block = first_row // rows
    aliased = h_all.shape == (total_rows, D_MODEL)
    tail = (CONV_WIDTH - 1) * nb
    small = (conv0, h0, s0r, s0i) + tuple(params)
    in_specs = [pl.BlockSpec(memory_space=pl.ANY), x_spec]
    in_specs += [_full(a.shape) for a in small]
    out_shape = (jax.ShapeDtypeStruct((total_rows, D_MODEL), f32),
                 jax.ShapeDtypeStruct((tail, D_RG), f32),
                 jax.ShapeDtypeStruct((nb, D_RG), f32),
                 jax.ShapeDtypeStruct((nb, S5_N), f32),
                 jax.ShapeDtypeStruct((nb, S5_N), f32))
    out_specs = (pl.BlockSpec((rows, D_MODEL), lambda c: (c + first_block, 0)),
                 _full((tail, D_RG)), _full((nb, D_RG)), _full((nb, S5_N)), _full((nb, S5_N)))
    scratch = [pltpu.VMEM((rows + tail, D_RG), f32),
               pltpu.VMEM((rows, D_RG), f32),
               pltpu.VMEM((rows, D_RG), f32),
               pltpu.VMEM((rows, S5_N), f32),
               pltpu.VMEM((rows, S5_N), f32),
               pltpu.VMEM((nb, D_RG), f32),
               pltpu.VMEM((nb, S5_N), f32),
               pltpu.VMEM((nb, S5_N), f32)]
    return pl.pallas_call(
        functools.partial(_mixer_kernel, nb=nb, tc=tc, batch_major_input=batch_major_input),
        grid=(n_chunks,),
        in_specs=in_specs,
        out_specs=out_specs,
        out_shape=out_shape,
        scratch_shapes=scratch,
        input_output_aliases={0: 0} if aliased else {},
        compiler_params=pltpu.CompilerParams(
            dimension_semantics=("arbitrary",), vmem_limit_bytes=VMEM_LIMIT_BYTES),
        name=name,
    )(h_all, x, *small)


TOK_TILE = 256
RUN_ALIGN = 16
OVF_ALIGN = 8
FIX = 32
FIX_SLOTS = N_EXPERTS * FIX
OVF_CHUNKS = 256
OVF_CHUNK_SLOTS = 384
OVF_SLOTS = 6 * OVF_CHUNK_SLOTS
SLOT_CHUNK = 512
FIX_STEPS = 4
ROW_TILE = 256
X_BUFFERS = 4
COPY_GROUP = 4
ROUTER_TILES = 6
NO_RUN = 1.0e9


def _top_k_gates(scores, rb):
    rows = scores.shape[0]
    lane_f = jax.lax.broadcasted_iota(jnp.int32, (rows, LANES), 1).astype(f32)
    biased = jnp.where(lane_f < float(N_EXPERTS), scores + rb, -jnp.inf)
    sel = jnp.zeros((rows, LANES), f32)
    mask = jnp.zeros((rows, LANES), f32)
    for _ in range(TOP_K):
        m = jnp.max(biased, axis=1, keepdims=True)
        idx = jnp.min(jnp.where(biased == m, lane_f, float(LANES)), axis=1, keepdims=True)
        hit = lane_f == idx
        sel = jnp.where(hit, scores, sel)
        mask = jnp.where(hit, 1.0, mask)
        biased = jnp.where(hit, -jnp.inf, biased)
    gates = sel / jnp.sum(sel, axis=1, keepdims=True) * ROUTED_SCALE
    return mask, gates


def _router_kernel(h_ref, wr_ref, rb_ref, hb_ref, rankm_ref, gates_ref, cnt_ref):
    hb = h_ref[...].astype(bf16)
    scores = jax.nn.sigmoid(jnp.dot(hb, wr_ref[...], preferred_element_type=f32))
    mask, gates = _top_k_gates(scores, rb_ref[...])
    t_row = jax.lax.broadcasted_iota(jnp.int32, (TOK_TILE, TOK_TILE), 0)
    t_col = jax.lax.broadcasted_iota(jnp.int32, (TOK_TILE, TOK_TILE), 1)
    earlier = jnp.where(t_col < t_row, 1.0, 0.0).astype(bf16)
    for sub in range(ROUTER_TILES):
        m = mask[sub * TOK_TILE:(sub + 1) * TOK_TILE]
        rank = jnp.dot(earlier, m.astype(bf16), preferred_element_type=f32)
        rankm_ref[sub * TOK_TILE:(sub + 1) * TOK_TILE, :] = jnp.where(m > 0.0, rank, -1.0).astype(bf16)
        cnt_ref[sub] = jnp.broadcast_to(jnp.sum(m, axis=0, keepdims=True), (SUBLANES, LANES))
    hb_ref[...] = hb
    gates_ref[...] = gates.astype(bf16)


def _router(h, wr, rb):
    n_tiles = h.shape[0] // TOK_TILE
    assert n_tiles % ROUTER_TILES == 0
    rows = ROUTER_TILES * TOK_TILE
    const = lambda shape: pl.BlockSpec(shape, lambda i: (0,) * len(shape))
    return pl.pallas_call(
        _router_kernel,
        grid=(n_tiles // ROUTER_TILES,),
        in_specs=[pl.BlockSpec((rows, D_MODEL), lambda i: (i, 0)), const(wr.shape), const(rb.shape)],
        out_specs=(pl.BlockSpec((rows, D_MODEL), lambda i: (i, 0)),
                   pl.BlockSpec((rows, LANES), lambda i: (i, 0)),
                   pl.BlockSpec((rows, LANES), lambda i: (i, 0)),
                   pl.BlockSpec((ROUTER_TILES, SUBLANES, LANES), lambda i: (i, 0, 0))),
        out_shape=(jax.ShapeDtypeStruct((h.shape[0], D_MODEL), bf16),
                   jax.ShapeDtypeStruct((h.shape[0], LANES), bf16),
                   jax.ShapeDtypeStruct((h.shape[0], LANES), bf16),
                   jax.ShapeDtypeStruct((n_tiles, SUBLANES, LANES), f32)),
        compiler_params=pltpu.CompilerParams(
            dimension_semantics=("arbitrary",), vmem_limit_bytes=VMEM_LIMIT_BYTES),
        name="moe_router",
    )(h, wr, rb)


def _run_copy(src, dst, sem):
    return pltpu.make_async_copy(src, dst, sem)


def _fixed_ranges(shape, axis):
    e = jax.lax.broadcasted_iota(jnp.int32, shape, axis).astype(f32)
    loc = jnp.where(e < float(N_EXPERTS), e * float(FIX), NO_RUN)
    return loc, loc + float(FIX)


def _slot_rows(first_slot, loc_row, end_row, base, rankm, n_slots=SLOT_CHUNK):
    s = (jax.lax.broadcasted_iota(jnp.int32, (n_slots, LANES), 0) + first_slot).astype(f32)
    in_run = jnp.where(s >= loc_row, jnp.where(s < end_row, 1.0, 0.0), 0.0)
    r_col = s[:, 0:1] - jnp.sum(in_run * (loc_row - base), axis=1, keepdims=True)
    q = jax.lax.dot_general(in_run.astype(bf16), rankm, (((1,), (1,)), ((), ())),
                            preferred_element_type=f32)
    return jnp.where(q == r_col, 1.0, 0.0).astype(bf16)


def _slot_cols(first_slot, loc_col, end_col, base, rankm, gates, n_slots=SLOT_CHUNK):
    s = (jax.lax.broadcasted_iota(jnp.int32, (LANES, n_slots), 1) + first_slot).astype(f32)
    in_run = jnp.where(s >= loc_col, jnp.where(s < end_col, 1.0, 0.0), 0.0)
    r_row = s[0:1, :] - jnp.sum(in_run * (loc_col - base), axis=0, keepdims=True)
    owner = in_run.astype(bf16)
    q = jnp.dot(rankm, owner, preferred_element_type=f32)
    g = jnp.dot(gates, owner, preferred_element_type=f32)
    return jnp.where(q == r_row, g, 0.0).astype(bf16)


def _wait_groups(n_groups, make_copy):
    max_bits = (OVF_CHUNKS // COPY_GROUP).bit_length()
    for b in range(max_bits):
        rows = (COPY_GROUP * OVF_ALIGN) << b

        @pl.when(jnp.bitwise_and(jnp.right_shift(n_groups, b), 1) == 1)
        def _():
            make_copy(rows).wait()


def _dispatch_kernel(totc_ref, ngrp_ref, dstk_ref, zst_ref, zch_ref,
                     hb_ref, rankm_ref, locs_ref, sorted_f_ref, sorted_o_ref,
                     stage_f, stage_o, zbuf, semf, semo, zsem):
    i = pl.program_id(0)
    n = pl.num_programs(0)
    slot = i % 2
    rankm = rankm_ref[...]

    fx_loc, fx_end = _fixed_ranges((1, LANES), 1)
    experts_per_chunk = SLOT_CHUNK // FIX
    for ch in range(FIX_SLOTS // SLOT_CHUNK):
        p = _slot_rows(ch * SLOT_CHUNK, fx_loc, fx_end, 0.0, rankm)
        rows = jnp.dot(p, hb_ref[...], preferred_element_type=f32).astype(bf16)
        stage_f[slot, ch * experts_per_chunk:(ch + 1) * experts_per_chunk] = rows.reshape(
            experts_per_chunk, FIX, D_MODEL)

    ov_loc = locs_ref[0, 0:1, :]
    ov_end = locs_ref[0, 1:2, :]
    for ch in range(OVF_SLOTS // OVF_CHUNK_SLOTS):
        def sort_overflow(ch=ch):
            p = _slot_rows(FIX_SLOTS + ch * OVF_CHUNK_SLOTS, ov_loc, ov_end, float(FIX), rankm,
                           OVF_CHUNK_SLOTS)
            stage_o[slot, ch * OVF_CHUNK_SLOTS:(ch + 1) * OVF_CHUNK_SLOTS, :] = jnp.dot(
                p, hb_ref[...], preferred_element_type=f32)
        pl.when(totc_ref[i] * OVF_ALIGN > ch * OVF_CHUNK_SLOTS)(sort_overflow)

    def fixed_copy(tile, slot_):
        dst = sorted_f_ref.at[:, pl.ds(pl.multiple_of(tile * FIX, FIX), FIX), :]
        return _run_copy(stage_f.at[slot_], dst, semf.at[slot_])

    def overflow_wait(rows, slot_):
        return _run_copy(stage_o.at[slot_, pl.ds(0, rows)], sorted_o_ref.at[pl.ds(0, rows)],
                         semo.at[slot_])

    fixed_copy(i, slot).start()

    def start_group(g, carry):
        for u in range(COPY_GROUP):
            k = g * COPY_GROUP + u
            src = stage_o.at[slot, pl.ds(pl.multiple_of(k * OVF_ALIGN, OVF_ALIGN), OVF_ALIGN)]
            dst = sorted_o_ref.at[pl.ds(pl.multiple_of(dstk_ref[i, k], OVF_ALIGN), OVF_ALIGN)]
            _run_copy(src, dst, semo.at[slot]).start()
        return carry
    jax.lax.fori_loop(0, ngrp_ref[i], start_group, 0)

    @pl.when(i == 0)
    def _():
        zbuf[...] = jnp.zeros(zbuf.shape, f32)

    def zero_copy(e, k):
        dst = sorted_o_ref.at[pl.ds(pl.multiple_of(zst_ref[e] + k * OVF_ALIGN, OVF_ALIGN), OVF_ALIGN)]
        return _run_copy(zbuf, dst, zsem.at[0])

    @pl.when(jnp.logical_and(i > 0, i <= N_EXPERTS))
    def _():
        def wait_zero(k, carry):
            zero_copy(i - 1, k).wait()
            return carry
        jax.lax.fori_loop(0, zch_ref[i - 1], wait_zero, 0)

    @pl.when(i < N_EXPERTS)
    def _():
        def start_zero(k, carry):
            zero_copy(i, k).start()
            return carry
        jax.lax.fori_loop(0, zch_ref[i], start_zero, 0)

    @pl.when(i > 0)
    def _():
        fixed_copy(i - 1, 1 - slot).wait()
        _wait_groups(ngrp_ref[i - 1], lambda rows: overflow_wait(rows, 1 - slot))

    @pl.when(i == n - 1)
    def _():
        fixed_copy(i, slot).wait()
        _wait_groups(ngrp_ref[i], lambda rows: overflow_wait(rows, slot))


def _dispatch(hb, rankm, locs, totc, ngrp, dstk, zst, zch, *, overflow_rows):
    n_tiles = hb.shape[0] // TOK_TILE
    assert n_tiles > N_EXPERTS
    grid_spec = pltpu.PrefetchScalarGridSpec(
        num_scalar_prefetch=5,
        grid=(n_tiles,),
        in_specs=[pl.BlockSpec((TOK_TILE, D_MODEL), lambda i, *_: (i, 0)),
                  pl.BlockSpec((TOK_TILE, LANES), lambda i, *_: (i, 0)),
                  pl.BlockSpec((1, SUBLANES, LANES), lambda i, *_: (i, 0, 0))],
        out_specs=(pl.BlockSpec(memory_space=pl.ANY), pl.BlockSpec(memory_space=pl.ANY)),
        scratch_shapes=[pltpu.VMEM((2, N_EXPERTS, FIX, D_MODEL), bf16),
                        pltpu.VMEM((2, OVF_SLOTS, D_MODEL), f32),
                        pltpu.VMEM((OVF_ALIGN, D_MODEL), f32),
                        pltpu.SemaphoreType.DMA((2,)),
                        pltpu.SemaphoreType.DMA((2,)),
                        pltpu.SemaphoreType.DMA((1,))])
    return pl.pallas_call(
        _dispatch_kernel,
        grid_spec=grid_spec,
        out_shape=(jax.ShapeDtypeStruct((N_EXPERTS, n_tiles * FIX, D_MODEL), bf16),
                   jax.ShapeDtypeStruct((overflow_rows, D_MODEL), f32)),
        compiler_params=pltpu.CompilerParams(
            dimension_semantics=("arbitrary",), vmem_limit_bytes=VMEM_LIMIT_BYTES),
        name="moe_dispatch",
    )(totc, ngrp, dstk, zst, zch, hb, rankm, locs)


def _experts_kernel(ts_ref, xf_ref, xo_ref, wgu_ref, wd_ref, yf_ref, yo_ref,
                    xbuf, ybuf, wgu_b, wd_b, xsem, ysem):
    e = pl.program_id(0)
    n_valid = ts_ref[N_EXPERTS]
    wgu_b[...] = wgu_ref[0].astype(bf16)
    wd_b[...] = wd_ref[0].astype(bf16)

    def ffn(x):
        gu = jnp.dot(x, wgu_b[...], preferred_element_type=f32)
        act = (jax.nn.silu(gu[:, :D_EXPERT]) * gu[:, D_EXPERT:]).astype(bf16)
        return jnp.dot(act, wd_b[...], preferred_element_type=f32).astype(bf16)

    def x_copy(j, slot):
        rows = pl.ds(pl.multiple_of(j * ROW_TILE, ROW_TILE), ROW_TILE)
        return pltpu.make_async_copy(xo_ref.at[rows], xbuf.at[slot], xsem.at[slot])

    def y_copy(j, slot):
        rows = pl.ds(pl.multiple_of(j * ROW_TILE, ROW_TILE), ROW_TILE)
        return pltpu.make_async_copy(ybuf.at[slot], yo_ref.at[rows], ysem.at[slot])

    ahead = X_BUFFERS - 1

    @pl.when(e == 0)
    def _():
        for a in range(ahead):
            @pl.when(a < n_valid)
            def _():
                x_copy(a, a).start()

    step_rows = xf_ref.shape[1] // FIX_STEPS
    for c in range(FIX_STEPS):
        rows = slice(c * step_rows, (c + 1) * step_rows)
        yf_ref[0, rows, :] = ffn(xf_ref[0, rows, :])

    def tile(j, carry):
        xslot = j % X_BUFFERS
        slot = j % 2
        x_copy(j, xslot).wait()

        @pl.when(j + ahead < n_valid)
        def _():
            x_copy(j + ahead, (j + ahead) % X_BUFFERS).start()

        @pl.when(j >= 2)
        def _():
            y_copy(j - 2, slot).wait()

        ybuf[slot] = ffn(xbuf[xslot].astype(bf16)).astype(f32)
        y_copy(j, slot).start()
        return carry
    jax.lax.fori_loop(ts_ref[e], ts_ref[e + 1], tile, 0)

    @pl.when(e == pl.num_programs(0) - 1)
    def _():
        @pl.when(n_valid >= 2)
        def _():
            y_copy(n_valid - 2, n_valid % 2).wait()

        @pl.when(n_valid >= 1)
        def _():
            y_copy(n_valid - 1, (n_valid - 1) % 2).wait()


def _experts(xf, xo, wgu, wd, ts, *, overflow_rows):
    fixed_rows = xf.shape[1]
    assert fixed_rows % (FIX_STEPS * RUN_ALIGN) == 0
    grid_spec = pltpu.PrefetchScalarGridSpec(
        num_scalar_prefetch=1,
        grid=(N_EXPERTS,),
        in_specs=[pl.BlockSpec((1, fixed_rows, D_MODEL), lambda e, ts: (e, 0, 0)),
                  pl.BlockSpec(memory_space=pl.ANY),
                  pl.BlockSpec((1, D_MODEL, 2 * D_EXPERT), lambda e, ts: (e, 0, 0)),
                  pl.BlockSpec((1, D_EXPERT, D_MODEL), lambda e, ts: (e, 0, 0))],
        out_specs=(pl.BlockSpec((1, fixed_rows, D_MODEL), lambda e, ts: (e, 0, 0)),
                   pl.BlockSpec(memory_space=pl.ANY)),
        scratch_shapes=[pltpu.VMEM((X_BUFFERS, ROW_TILE, D_MODEL), f32),
                        pltpu.VMEM((2, ROW_TILE, D_MODEL), f32),
                        pltpu.VMEM((D_MODEL, 2 * D_EXPERT), bf16),
                        pltpu.VMEM((D_EXPERT, D_MODEL), bf16),
                        pltpu.SemaphoreType.DMA((X_BUFFERS,)),
                        pltpu.SemaphoreType.DMA((2,))])
    return pl.pallas_call(
        _experts_kernel,
        grid_spec=grid_spec,
        out_shape=(jax.ShapeDtypeStruct((N_EXPERTS, fixed_rows, D_MODEL), bf16),
                   jax.ShapeDtypeStruct((overflow_rows, D_MODEL), f32)),
        compiler_params=pltpu.CompilerParams(
            dimension_semantics=("arbitrary",), vmem_limit_bytes=VMEM_LIMIT_BYTES),
        name="moe_experts",
    )(ts, xf, xo, wgu, wd)


def _combine_kernel(totc_ref, ngrp_ref, srck_ref,
                    h_ref, rankm_ref, gates_ref, locc_ref, yf_ref, yo_ref, wsu_ref, wsd_ref, ln_g_ref, ln_b_ref,
                    out_bm_ref, out_tm_ref, yloc_f, yloc_o, acc_scr, p_scr, semf, semo,
                    *, n_bm_tiles):
    i = pl.program_id(0)
    n = pl.num_programs(0)
    slot = i % 2

    def fixed_copy(tile, slot_):
        src = yf_ref.at[:, pl.ds(pl.multiple_of(tile * FIX, FIX), FIX), :]
        return _run_copy(src, yloc_f.at[slot_], semf.at[slot_])

    def overflow_wait(rows, slot_):
        return _run_copy(yo_ref.at[pl.ds(0, rows)], yloc_o.at[slot_, pl.ds(0, rows)], semo.at[slot_])

    def fetch(tile, slot_):
        fixed_copy(tile, slot_).start()

        def start_group(g, carry):
            for u in range(COPY_GROUP):
                k = g * COPY_GROUP + u
                src = yo_ref.at[pl.ds(pl.multiple_of(srck_ref[tile, k], OVF_ALIGN), OVF_ALIGN)]
                dst = yloc_o.at[slot_, pl.ds(pl.multiple_of(k * OVF_ALIGN, OVF_ALIGN), OVF_ALIGN)]
                _run_copy(src, dst, semo.at[slot_]).start()
            return carry
        jax.lax.fori_loop(0, ngrp_ref[tile], start_group, 0)

    @pl.when(i == 0)
    def _():
        yloc_o[...] = jnp.zeros(yloc_o.shape, f32)
        fetch(0, 0)

    @pl.when(i + 1 < n)
    def _():
        fetch(i + 1, 1 - slot)

    h = h_ref[...]
    hb = h.astype(bf16)
    su = jnp.dot(hb, wsu_ref[...], preferred_element_type=f32)
    act = (jax.nn.silu(su[:, :D_SHARED]) * su[:, D_SHARED:]).astype(bf16)
    acc_scr[...] = jnp.dot(act, wsd_ref[...], preferred_element_type=f32)

    rankm = rankm_ref[...]
    gates = gates_ref[...]
    fx_loc, fx_end = _fixed_ranges((LANES, 1), 0)
    for ch in range(FIX_SLOTS // SLOT_CHUNK):
        p_scr[:, ch * SLOT_CHUNK:(ch + 1) * SLOT_CHUNK] = _slot_cols(
            ch * SLOT_CHUNK, fx_loc, fx_end, 0.0, rankm, gates)

    fixed_copy(i, slot).wait()
    _wait_groups(ngrp_ref[i], lambda rows: overflow_wait(rows, slot))

    acc_scr[...] += jnp.dot(p_scr[...], yloc_f[slot].reshape(FIX_SLOTS, D_MODEL),
                            preferred_element_type=f32)
    ov_loc = locc_ref[0, :, 0:1]
    ov_end = locc_ref[0, :, 1:2]
    for ch in range(OVF_SLOTS // OVF_CHUNK_SLOTS):
        def gather_overflow(ch=ch):
            p = _slot_cols(FIX_SLOTS + ch * OVF_CHUNK_SLOTS, ov_loc, ov_end, float(FIX), rankm, gates,
                           OVF_CHUNK_SLOTS)
            y_o = yloc_o[slot, ch * OVF_CHUNK_SLOTS:(ch + 1) * OVF_CHUNK_SLOTS, :].astype(bf16)
            acc_scr[...] += jnp.dot(p, y_o, preferred_element_type=f32)
        pl.when(totc_ref[i] * OVF_ALIGN > ch * OVF_CHUNK_SLOTS)(gather_overflow)
    y = _layer_norm(DN_ALPHA * h + acc_scr[...], ln_g_ref[...], ln_b_ref[...])

    @pl.when(i < n_bm_tiles)
    def _():
        nb, tc, _ = out_bm_ref.shape
        out_bm_ref[...] = jnp.transpose(y.reshape(tc, nb, D_MODEL), (1, 0, 2))

    @pl.when(i >= n_bm_tiles)
    def _():
        out_tm_ref[...] = y


def _combine(h, rankm, gates, locc, yf, yo, wsu, wsd, ln_g, ln_b, totc, ngrp, srck, *, bm_shape):
    n_tiles = h.shape[0] // TOK_TILE
    nb, length, _ = bm_shape
    tc = TOK_TILE // nb
    n_bm_tiles = length // tc
    n_tm_tiles = n_tiles - n_bm_tiles
    const = lambda shape: pl.BlockSpec(shape, lambda i, *_: (0,) * len(shape))
    grid_spec = pltpu.PrefetchScalarGridSpec(
        num_scalar_prefetch=3,
        grid=(n_tiles,),
        in_specs=[pl.BlockSpec((TOK_TILE, D_MODEL), lambda i, *_: (i, 0)),
                  pl.BlockSpec((TOK_TILE, LANES), lambda i, *_: (i, 0)),
                  pl.BlockSpec((TOK_TILE, LANES), lambda i, *_: (i, 0)),
                  pl.BlockSpec((1, LANES, 2), lambda i, *_: (i, 0, 0)),
                  pl.BlockSpec(memory_space=pl.ANY),
                  pl.BlockSpec(memory_space=pl.ANY),
                  const(wsu.shape), const(wsd.shape), const(ln_g.shape), const(ln_b.shape)],
        out_specs=(pl.BlockSpec((nb, tc, D_MODEL),
                                lambda i, *_: (0, jnp.minimum(i, n_bm_tiles - 1), 0)),
                   pl.BlockSpec((TOK_TILE, D_MODEL),
                                lambda i, *_: (jnp.maximum(i - n_bm_tiles, 0), 0))),
        scratch_shapes=[pltpu.VMEM((2, N_EXPERTS, FIX, D_MODEL), bf16),
                        pltpu.VMEM((2, OVF_SLOTS, D_MODEL), f32),
                        pltpu.VMEM((TOK_TILE, D_MODEL), f32),
                        pltpu.VMEM((TOK_TILE, FIX_SLOTS), bf16),
                        pltpu.SemaphoreType.DMA((2,)),
                        pltpu.SemaphoreType.DMA((2,))])
    return pl.pallas_call(
        functools.partial(_combine_kernel, n_bm_tiles=n_bm_tiles),
        grid_spec=grid_spec,
        out_shape=(jax.ShapeDtypeStruct(bm_shape, f32),
                   jax.ShapeDtypeStruct((n_tm_tiles * TOK_TILE, D_MODEL), f32)),
        compiler_params=pltpu.CompilerParams(
            dimension_semantics=("arbitrary",), vmem_limit_bytes=VMEM_LIMIT_BYTES),
        name="moe_combine",
    )(totc, ngrp, srck, h, rankm, gates, locc, yf, yo, wsu, wsd, ln_g, ln_b)


def _round_up(x, m):
    return (x + m - 1) // m * m


def _moe(h, wr, rb, wgu, wd, wsu, wsd, ln_g, ln_b, *, bm_shape):
    n_tok = h.shape[0]
    n_tiles = n_tok // TOK_TILE
    max_rows = _round_up(n_tiles * (OVF_CHUNKS - 1) * OVF_ALIGN + N_EXPERTS * (ROW_TILE - 1), ROW_TILE)

    hb, rankm, gates, cnt = _router(h, wr, rb)

    i32 = jnp.int32
    cnt = cnt[:, 0, :N_EXPERTS].astype(i32)
    oc = (jnp.maximum(cnt - FIX, 0) + OVF_ALIGN - 1) // OVF_ALIGN
    over_tiles = jnp.cumsum(oc, axis=0)
    region_rows = over_tiles[-1] * OVF_ALIGN
    region_size = _round_up(region_rows, ROW_TILE)
    region_start = jnp.cumsum(region_size) - region_size
    run_dst = region_start[None, :] + OVF_ALIGN * (over_tiles - oc)
    ch_end = jnp.cumsum(oc, axis=1)
    ch_beg = ch_end - oc
    totc = ch_end[:, -1]
    k = jnp.arange(OVF_CHUNKS, dtype=i32)
    kk = k[None, :, None]
    mine = jnp.logical_and(ch_beg[:, None, :] <= kk, kk < ch_end[:, None, :])
    chunk_dst = jnp.sum(jnp.where(mine, run_dst[:, None, :] + OVF_ALIGN * (kk - ch_beg[:, None, :]), 0),
                        axis=-1)
    live = k[None, :] < totc[:, None]
    tile_ids = jnp.arange(n_tiles, dtype=i32)[:, None]
    spare = max_rows + (tile_ids % 2) * OVF_SLOTS + k[None, :] * OVF_ALIGN
    dstk = jnp.where(live, chunk_dst, spare).astype(i32)
    srck = jnp.where(live, chunk_dst, 0).astype(i32)
    ngrp = ((totc + COPY_GROUP - 1) // COPY_GROUP).astype(i32)
    zst = (region_start + region_rows).astype(i32)
    zch = ((region_size - region_rows) // OVF_ALIGN).astype(i32)
    tile_start = jnp.concatenate([region_start, region_start[-1:] + region_size[-1:]]) // ROW_TILE
    tile_start = tile_start.astype(i32)

    pad = ((0, 0), (0, LANES - N_EXPERTS))
    loc_f = jnp.pad((FIX_SLOTS + OVF_ALIGN * ch_beg).astype(f32), pad, constant_values=NO_RUN)
    end_f = jnp.pad((FIX_SLOTS + OVF_ALIGN * ch_end).astype(f32), pad, constant_values=NO_RUN)
    locs = jnp.concatenate([loc_f[:, None, :], end_f[:, None, :],
                            jnp.zeros((n_tiles, SUBLANES - 2, LANES), f32)], axis=1)
    locc = jnp.stack([loc_f, end_f], axis=-1)

    totc = totc.astype(i32)
    overflow_rows = max_rows + 2 * OVF_SLOTS
    xf, xo = _dispatch(hb, rankm, locs, totc, ngrp, dstk, zst, zch, overflow_rows=overflow_rows)
    yf, yo = _experts(xf, xo, wgu, wd, tile_start, overflow_rows=max_rows)
    return _combine(h, rankm, gates, locc, yf, yo, wsu, wsd, ln_g, ln_b, totc, ngrp, srck, bm_shape=bm_shape)


def _diag_tiles(blocks, n_tiles):
    n_blocks, r, c = blocks.shape
    per = n_blocks // n_tiles
    wide = jnp.tile(blocks.reshape(n_blocks * r, c), (1, per))
    row_block = (jnp.arange(n_blocks * r) // r) % per
    col_block = jnp.arange(per * c) // c
    wide = jnp.where(row_block[:, None] == col_block[None, :], wide, 0)
    return wide.reshape(n_tiles, per * r, per * c)


def _head_block_diag(w):
    return _diag_tiles(w, D_RG // MXU_DIM)


def _s5_in_tiles(b):
    return _diag_tiles(b.transpose(0, 2, 1), 2)


def _s5_out_tiles(cw):
    return _diag_tiles(cw.transpose(0, 2, 1), 2)


def _row(v):
    return v.reshape(1, -1)


def kernel(x_prompt, x_sample, state_rg_conv, state_rg_h, state_s5_re, state_s5_im, w_in, conv_w, conv_b, rg_w_a, rg_b_a, rg_w_i, rg_b_i, rg_lam, s5_a_re, s5_a_im, s5_log_dt, s5_b_re, s5_b_im, s5_c_re, s5_c_im, s5_d, w_glu, b_glu, w_out, ln1_g, ln1_b, w_router, router_bias, w_gate_up, w_down, w_shared_up, w_shared_down, ln2_g, ln2_b):
    l = 0
    bp, lp, _ = x_prompt.shape
    bs, ls, _ = x_sample.shape

    are, aim, bbre, bbim = _s5_prep(
        _row(s5_a_re[l]), _row(s5_a_im[l]),
        _row(jnp.repeat(s5_log_dt[l], S5_STATE)),
        _s5_in_tiles(s5_b_re[l]), _s5_in_tiles(s5_b_im[l]))
    params = (w_in[l].astype(bf16), conv_w[l], _row(conv_b[l]),
              _head_block_diag(rg_w_a[l]).astype(bf16), _row(rg_b_a[l]),
              _head_block_diag(rg_w_i[l]).astype(bf16), _row(rg_b_i[l]), _row(rg_lam[l]),
              are, aim, bbre, bbim,
              _s5_out_tiles(s5_c_re[l]).astype(bf16), _s5_out_tiles(s5_c_im[l]).astype(bf16),
              _row(s5_d[l]), w_glu[l].astype(bf16), _row(b_glu[l]), w_out[l].astype(bf16),
              _row(ln1_g[l]), _row(ln1_b[l]))

    tail = CONV_WIDTH - 1
    n_tok = lp * bp + ls * bs
    xs_tm = x_sample.transpose(1, 0, 2).reshape(ls * bs, D_MODEL)
    h_all, sc, sh, sre, sim = _mixer(
        xs_tm, state_rg_conv[l].transpose(1, 0, 2).reshape(tail * bs, D_RG), state_rg_h[l],
        state_s5_re[l].reshape(bs, S5_N), state_s5_im[l].reshape(bs, S5_N), params,
        nb=bs, tc=ls, name="mixer_sample",
        h_all=jnp.zeros((SUBLANES, LANES), f32), total_rows=n_tok, first_row=lp * bp)
    h_all, pc, ph, pre, pim = _mixer(
        x_prompt, jnp.zeros((tail * bp, D_RG), f32), jnp.zeros((bp, D_RG), f32),
        jnp.zeros((bp, S5_N), f32), jnp.zeros((bp, S5_N), f32), params,
        nb=bp, tc=PROMPT_CHUNK_ROWS // bp, name="mixer_prompt",
        h_all=h_all, total_rows=n_tok, first_row=0)
    wr = jnp.pad(w_router[l], ((0, 0), (0, LANES - N_EXPERTS))).astype(bf16)
    rb = jnp.pad(_row(router_bias[l]), ((0, 0), (0, LANES - N_EXPERTS)))
    yp, ys_tm = _moe(h_all, wr, rb, w_gate_up[l], w_down[l],
                     w_shared_up[l].astype(bf16), w_shared_down[l].astype(bf16),
                     _row(ln2_g[l]), _row(ln2_b[l]), bm_shape=x_prompt.shape)
    ys = ys_tm.reshape(ls, bs, D_MODEL).transpose(1, 0, 2)

    def conv_out(cv, nbatch):
        return cv.reshape(tail, nbatch, D_RG).transpose(1, 0, 2)[None]

    return (yp, ys,
            conv_out(pc, bp), ph[None],
            pre.reshape(1, bp, S5_GROUPS, S5_STATE), pim.reshape(1, bp, S5_GROUPS, S5_STATE),
            conv_out(sc, bs), sh[None],
            sre.reshape(1, bs, S5_GROUPS, S5_STATE), sim.reshape(1, bs, S5_GROUPS, S5_STATE))
```

```python
import functools
import math

import jax
import jax.numpy as jnp
from jax.experimental import pallas as pl
from jax.experimental.pallas import tpu as pltpu

D_MODEL = 1024
D_RG = 512
RG_HEADS = 8
RG_HEAD_DIM = 64
CONV_WIDTH = 4
RG_C = 8.0
D_S5 = 512
S5_GROUP = 16
S5_GROUPS = 32
S5_STATE = 64
S5_N = S5_GROUPS * S5_STATE
N_EXPERTS = 64
TOP_K = 8
D_EXPERT = 256
D_SHARED = 256
ROUTED_SCALE = 2.5
DEPTH = 1
DN_ALPHA = (2.0 * DEPTH) ** 0.25
LN_EPS = 1e-5

SUBLANES = 8
LANES = 128
MXU_DIM = 256
S5_SCAN_COLS = 512
PROMPT_CHUNK_ROWS = 512
VMEM_LIMIT_BYTES = 56 * 1024 * 1024

bf16 = jnp.bfloat16
f32 = jnp.float32


def _gelu_tanh(x):
    c = math.sqrt(2.0 / math.pi)
    return x * (0.5 * (1.0 + jnp.tanh(c * (x + 0.044715 * (x * x * x)))))


def _layer_norm(x, g, b):
    mu = jnp.mean(x, axis=-1, keepdims=True)
    xc = x - mu
    var = jnp.mean(xc * xc, axis=-1, keepdims=True)
    return xc * jax.lax.rsqrt(var + LN_EPS) * g + b


def _s5_prep_kernel(lr_ref, li_ref, ldt_ref, bre_ref, bim_ref,
                    are_ref, aim_ref, bbre_ref, bbim_ref):
    lr = lr_ref[...]
    li = li_ref[...]
    dt = jnp.exp(ldt_ref[...])
    mag = jnp.exp(lr * dt)
    abar_re = mag * jnp.cos(li * dt)
    abar_im = mag * jnp.sin(li * dt)
    den = lr * lr + li * li
    nr = abar_re - 1.0
    ni = abar_im
    coef_re = (nr * lr + ni * li) / den
    coef_im = (ni * lr - nr * li) / den
    are_ref[...] = abar_re
    aim_ref[...] = abar_im
    half = S5_N // 2
    for k in range(2):
        cre = coef_re[:, k * half:(k + 1) * half]
        cim = coef_im[:, k * half:(k + 1) * half]
        br = bre_ref[k]
        bi = bim_ref[k]
        bbre_ref[k] = (cre * br - cim * bi).astype(bf16)
        bbim_ref[k] = (cre * bi + cim * br).astype(bf16)


def _s5_prep(lr, li, ldt, bre_t, bim_t):
    half = S5_N // 2
    return pl.pallas_call(
        _s5_prep_kernel,
        out_shape=(jax.ShapeDtypeStruct((1, S5_N), f32),
                   jax.ShapeDtypeStruct((1, S5_N), f32),
                   jax.ShapeDtypeStruct((2, MXU_DIM, half), bf16),
                   jax.ShapeDtypeStruct((2, MXU_DIM, half), bf16)),
        name="s5_prep",
    )(lr, li, ldt, bre_t, bim_t)


def _mixer_kernel(h_all_ref, x_ref, conv0_ref, h0_ref, s0r_ref, s0i_ref,
                  w_in_ref, conv_w_ref, conv_b_ref, wa_ref, ba_ref, wi_ref, bi_ref, lam_ref,
                  are_ref, aim_ref, bbre_ref, bbim_ref, cre_ref, cim_ref, d_ref,
                  wglu_ref, bglu_ref, wout_ref, ln_g_ref, ln_b_ref,
                  hout_ref, conv_out_ref, hlast_ref, sre_out_ref, sim_out_ref,
                  pad_scr, a_scr, b_scr, bur_scr, bui_scr, hst_scr, sr_scr, si_scr,
                  *, nb, tc, batch_major_input):
    del h_all_ref
    rows = nb * tc
    tail = (CONV_WIDTH - 1) * nb
    c = pl.program_id(0)

    @pl.when(c == 0)
    def _():
        pad_scr[0:tail, :] = conv0_ref[...]
        hst_scr[...] = h0_ref[...]
        sr_scr[...] = s0r_ref[...]
        si_scr[...] = s0i_ref[...]

    if batch_major_input:
        x = jnp.transpose(x_ref[...], (1, 0, 2)).reshape(rows, D_MODEL)
    else:
        x = x_ref[...]
    proj = jnp.dot(x.astype(bf16), w_in_ref[...], preferred_element_type=f32)
    x_rg = proj[:, :D_RG]
    g_rg = proj[:, D_RG:2 * D_RG]
    u = proj[:, 2 * D_RG:]

    pad_scr[tail:tail + rows, :] = x_rg
    conv_w = conv_w_ref[...]
    acc = conv_w[0:1, :] * pad_scr[0:rows, :]
    for k in range(1, CONV_WIDTH):
        acc = acc + conv_w[k:k + 1, :] * pad_scr[k * nb:k * nb + rows, :]
    xc = conv_b_ref[...] + acc
    new_tail = pad_scr[rows:rows + tail, :]
    pad_scr[0:tail, :] = new_tail

    xcb = xc.astype(bf16)
    ga = []
    gi = []
    for hh in range(D_RG // MXU_DIM):
        xs = xcb[:, hh * MXU_DIM:(hh + 1) * MXU_DIM]
        ga.append(jnp.dot(xs, wa_ref[hh], preferred_element_type=f32))
        gi.append(jnp.dot(xs, wi_ref[hh], preferred_element_type=f32))
    r = jax.nn.sigmoid(jnp.concatenate(ga, axis=1) + ba_ref[...])
    i = jax.nn.sigmoid(jnp.concatenate(gi, axis=1) + bi_ref[...])
    nlam = -lam_ref[...]
    softplus = jnp.maximum(nlam, 0.0) + jnp.log1p(jnp.exp(-jnp.abs(nlam)))
    log_a = (-RG_C) * r * softplus
    a_scr[...] = jnp.exp(log_a)
    th = jnp.tanh(log_a)
    b_scr[...] = jnp.sqrt((-2.0 * th) / (1.0 - th)) * (i * xc)

    for rg in range(nb // SUBLANES):
        r0 = rg * SUBLANES
        h = hst_scr[r0:r0 + SUBLANES, :]
        for t in range(tc):
            q = t * nb + r0
            h = a_scr[q:q + SUBLANES, :] * h + b_scr[q:q + SUBLANES, :]
            b_scr[q:q + SUBLANES, :] = h
        hst_scr[r0:r0 + SUBLANES, :] = h
    y_rg = b_scr[...] * _gelu_tanh(g_rg)

    ub = u.astype(bf16)
    half = S5_N // 2
    for k in range(2):
        us = ub[:, k * MXU_DIM:(k + 1) * MXU_DIM]
        bur_scr[:, k * half:(k + 1) * half] = jnp.dot(us, bbre_ref[k], preferred_element_type=f32)
        bui_scr[:, k * half:(k + 1) * half] = jnp.dot(us, bbim_ref[k], preferred_element_type=f32)
    for rg in range(nb // SUBLANES):
        r0 = rg * SUBLANES
        for cb in range(S5_N // S5_SCAN_COLS):
            c0 = cb * S5_SCAN_COLS
            ar = jnp.broadcast_to(are_ref[:, c0:c0 + S5_SCAN_COLS], (SUBLANES, S5_SCAN_COLS))
            ai = jnp.broadcast_to(aim_ref[:, c0:c0 + S5_SCAN_COLS], (SUBLANES, S5_SCAN_COLS))
            xr = sr_scr[r0:r0 + SUBLANES, c0:c0 + S5_SCAN_COLS]
            xi = si_scr[r0:r0 + SUBLANES, c0:c0 + S5_SCAN_COLS]
            for t in range(tc):
                q = t * nb + r0
                br = bur_scr[q:q + SUBLANES, c0:c0 + S5_SCAN_COLS]
                bi_ = bui_scr[q:q + SUBLANES, c0:c0 + S5_SCAN_COLS]
                nxr = ar * xr - ai * xi + br
                nxi = ar * xi + ai * xr + bi_
                bur_scr[q:q + SUBLANES, c0:c0 + S5_SCAN_COLS] = nxr
                bui_scr[q:q + SUBLANES, c0:c0 + S5_SCAN_COLS] = nxi
                xr, xi = nxr, nxi
            sr_scr[r0:r0 + SUBLANES, c0:c0 + S5_SCAN_COLS] = xr
            si_scr[r0:r0 + SUBLANES, c0:c0 + S5_SCAN_COLS] = xi
    ys = []
    for j in range(D_S5 // MXU_DIM):
        xrb = bur_scr[:, j * half:(j + 1) * half].astype(bf16)
        xib = bui_scr[:, j * half:(j + 1) * half].astype(bf16)
        ys.append(jnp.dot(xrb, cre_ref[j], preferred_element_type=f32)
                  - jnp.dot(xib, cim_ref[j], preferred_element_type=f32))
    y_s5 = jnp.concatenate(ys, axis=1) + d_ref[...] * u
    yg = _gelu_tanh(y_s5)
    glu = jnp.dot(yg.astype(bf16), wglu_ref[...], preferred_element_type=f32) + bglu_ref[...]
    y_s5 = yg * jax.nn.sigmoid(glu)

    ycat = jnp.concatenate([y_rg, y_s5], axis=1).astype(bf16)
    mix = jnp.dot(ycat, wout_ref[...], preferred_element_type=f32)
    hout_ref[...] = _layer_norm(DN_ALPHA * x + mix, ln_g_ref[...], ln_b_ref[...])

    @pl.when(c == pl.num_programs(0) - 1)
    def _():
        conv_out_ref[...] = pad_scr[0:tail, :]
        hlast_ref[...] = hst_scr[...]
        sre_out_ref[...] = sr_scr[...]
        sim_out_ref[...] = si_scr[...]


def _full(shape):
    n = len(shape)
    return pl.BlockSpec(shape, lambda c: (0,) * n)


def _mixer(x, conv0, h0, s0r, s0i, params, *, nb, tc, name, h_all, total_rows, first_row):
    batch_major_input = x.ndim == 3
    rows = nb * tc
    if batch_major_input:
        n_chunks = x.shape[1] // tc
        x_spec = pl.BlockSpec((nb, tc, D_MODEL), lambda c: (0, c, 0))
    else:
        n_chunks = x.shape[0] // rows
        x_spec = pl.BlockSpec((rows, D_MODEL), lambda c: (c, 0))
    first_block = first_row // rows
    aliased = h_all.shape == (total_rows, D_MODEL)
    tail = (CONV_WIDTH - 1) * nb
    small = (conv0, h0, s0r, s0i) + tuple(params)
    in_specs = [pl.BlockSpec(memory_space=pl.ANY), x_spec]
    in_specs += [_full(a.shape) for a in small]
    out_shape = (jax.ShapeDtypeStruct((total_rows, D_MODEL), f32),
                 jax.ShapeDtypeStruct((tail, D_RG), f32),
                 jax.ShapeDtypeStruct((nb, D_RG), f32),
                 jax.ShapeDtypeStruct((nb, S5_N), f32),
                 jax.ShapeDtypeStruct((nb, S5_N), f32))
    out_specs = (pl.BlockSpec((rows, D_MODEL), lambda c: (c + first_block, 0)),
                 _full((tail, D_RG)), _full((nb, D_RG)), _full((nb, S5_N)), _full((nb, S5_N)))
    scratch = [pltpu.VMEM((rows + tail, D_RG), f32),
               pltpu.VMEM((rows, D_RG), f32),
               pltpu.VMEM((rows, D_RG), f32),
               pltpu.VMEM((rows, S5_N), f32),
               pltpu.VMEM((rows, S5_N), f32),
               pltpu.VMEM((nb, D_RG), f32),
               pltpu.VMEM((nb, S5_N), f32),
               pltpu.VMEM((nb, S5_N), f32)]
    return pl.pallas_call(
        functools.partial(_mixer_kernel, nb=nb, tc=tc, batch_major_input=batch_major_input),
        grid=(n_chunks,),
        in_specs=in_specs,
        out_specs=out_specs,
        out_shape=out_shape,
        scratch_shapes=scratch,
        input_output_aliases={0: 0} if aliased else {},
        compiler_params=pltpu.CompilerParams(
            dimension_semantics=("arbitrary",), vmem_limit_bytes=VMEM_LIMIT_BYTES),
        name=name,
    )(h_all, x, *small)


TOK_TILE = 256
RUN_ALIGN = 16
OVF_ALIGN = 8
FIX = 32
FIX_SLOTS = N_EXPERTS * FIX
OVF_CHUNKS = 256
OVF_CHUNK_SLOTS = 384
OVF_SLOTS = 6 * OVF_CHUNK_SLOTS
SLOT_CHUNK = 512
FIX_STEPS = 4
ROW_TILE = 256
X_BUFFERS = 4
FIXED_BUFFERS = 3
COPY_GROUP = 4
EXT = D_MODEL + LANES
ROUTER_TILES = 6
NO_RUN = 1.0e9


def _top_k_gates(scores, rb):
    rows = scores.shape[0]
    lane_f = jax.lax.broadcasted_iota(jnp.int32, (rows, LANES), 1).astype(f32)
    biased = jnp.where(lane_f < float(N_EXPERTS), scores + rb, -jnp.inf)
    sel = jnp.zeros((rows, LANES), f32)
    mask = jnp.zeros((rows, LANES), f32)
    for _ in range(TOP_K):
        m = jnp.max(biased, axis=1, keepdims=True)
        idx = jnp.min(jnp.where(biased == m, lane_f, float(LANES)), axis=1, keepdims=True)
        hit = lane_f == idx
        sel = jnp.where(hit, scores, sel)
        mask = jnp.where(hit, 1.0, mask)
        biased = jnp.where(hit, -jnp.inf, biased)
    gates = sel / jnp.sum(sel, axis=1, keepdims=True) * ROUTED_SCALE
    return mask, gates


def _router_kernel(h_ref, wr_ref, rb_ref, hext_ref, rankm_ref, cnt_ref):
    hb = h_ref[...].astype(bf16)
    scores = jax.nn.sigmoid(jnp.dot(hb, wr_ref[...], preferred_element_type=f32))
    mask, gates = _top_k_gates(scores, rb_ref[...])
    t_row = jax.lax.broadcasted_iota(jnp.int32, (TOK_TILE, TOK_TILE), 0)
    t_col = jax.lax.broadcasted_iota(jnp.int32, (TOK_TILE, TOK_TILE), 1)
    earlier = jnp.where(t_col < t_row, 1.0, 0.0).astype(bf16)
    for sub in range(ROUTER_TILES):
        m = mask[sub * TOK_TILE:(sub + 1) * TOK_TILE]
        rank = jnp.dot(earlier, m.astype(bf16), preferred_element_type=f32)
        rankm_ref[sub * TOK_TILE:(sub + 1) * TOK_TILE, :] = jnp.where(m > 0.0, rank, -1.0).astype(bf16)
        cnt_ref[sub] = jnp.broadcast_to(jnp.sum(m, axis=0, keepdims=True), (SUBLANES, LANES))
    g_hi = gates.astype(bf16).astype(f32)
    g_pack = g_hi + pltpu.roll(gates - g_hi, N_EXPERTS, 1)
    hext_ref[:, :D_MODEL] = hb
    hext_ref[:, D_MODEL:] = g_pack.astype(bf16)


def _router(h, wr, rb):
    n_tiles = h.shape[0] // TOK_TILE
    assert n_tiles % ROUTER_TILES == 0
    rows = ROUTER_TILES * TOK_TILE
    const = lambda shape: pl.BlockSpec(shape, lambda i: (0,) * len(shape))
    return pl.pallas_call(
        _router_kernel,
        grid=(n_tiles // ROUTER_TILES,),
        in_specs=[pl.BlockSpec((rows, D_MODEL), lambda i: (i, 0)), const(wr.shape), const(rb.shape)],
        out_specs=(pl.BlockSpec((rows, EXT), lambda i: (i, 0)),
                   pl.BlockSpec((rows, LANES), lambda i: (i, 0)),
                   pl.BlockSpec((ROUTER_TILES, SUBLANES, LANES), lambda i: (i, 0, 0))),
        out_shape=(jax.ShapeDtypeStruct((h.shape[0], EXT), bf16),
                   jax.ShapeDtypeStruct((h.shape[0], LANES), bf16),
                   jax.ShapeDtypeStruct((n_tiles, SUBLANES, LANES), f32)),
        compiler_params=pltpu.CompilerParams(
            dimension_semantics=("arbitrary",), vmem_limit_bytes=VMEM_LIMIT_BYTES),
        name="moe_router",
    )(h, wr, rb)


def _run_copy(src, dst, sem):
    return pltpu.make_async_copy(src, dst, sem)


def _fixed_ranges(shape, axis):
    e = jax.lax.broadcasted_iota(jnp.int32, shape, axis).astype(f32)
    loc = jnp.where(e < float(N_EXPERTS), e * float(FIX), NO_RUN)
    return loc, loc + float(FIX)


def _slot_rows(first_slot, loc_row, end_row, base, rankm, n_slots=SLOT_CHUNK):
    s = (jax.lax.broadcasted_iota(jnp.int32, (n_slots, LANES), 0) + first_slot).astype(f32)
    in_run = jnp.where(s >= loc_row, jnp.where(s < end_row, 1.0, 0.0), 0.0)
    r_col = s[:, 0:1] - jnp.sum(in_run * (loc_row - base), axis=1, keepdims=True)
    q = jax.lax.dot_general(in_run.astype(bf16), rankm, (((1,), (1,)), ((), ())),
                            preferred_element_type=f32)
    return jnp.where(q == r_col, 1.0, 0.0).astype(bf16)


def _slot_cols(first_slot, loc_col, end_col, base, rankm, n_slots=SLOT_CHUNK):
    s = (jax.lax.broadcasted_iota(jnp.int32, (LANES, n_slots), 1) + first_slot).astype(f32)
    in_run = jnp.where(s >= loc_col, jnp.where(s < end_col, 1.0, 0.0), 0.0)
    r_row = s[0:1, :] - jnp.sum(in_run * (loc_col - base), axis=0, keepdims=True)
    q = jnp.dot(rankm, in_run.astype(bf16), preferred_element_type=f32)
    return jnp.where(q == r_row, 1.0, 0.0).astype(bf16)


def _wait_groups(n_groups, make_copy):
    max_bits = (OVF_CHUNKS // COPY_GROUP).bit_length()
    for b in range(max_bits):
        rows = (COPY_GROUP * OVF_ALIGN) << b

        @pl.when(jnp.bitwise_and(jnp.right_shift(n_groups, b), 1) == 1)
        def _():
            make_copy(rows).wait()


def _dispatch_kernel(totc_ref, ngrp_ref, dstk_ref, zst_ref, zch_ref,
                     hext_ref, rankm_ref, locs_ref, sorted_f_ref, sorted_o_ref,
                     stage_f, stage_o, zbuf, semf, semo, zsem):
    i = pl.program_id(0)
    n = pl.num_programs(0)
    slot = i % 2
    rankm = rankm_ref[...]

    fx_loc, fx_end = _fixed_ranges((1, LANES), 1)
    experts_per_chunk = SLOT_CHUNK // FIX
    for ch in range(FIX_SLOTS // SLOT_CHUNK):
        p = _slot_rows(ch * SLOT_CHUNK, fx_loc, fx_end, 0.0, rankm)
        rows = jnp.dot(p, hext_ref[...], preferred_element_type=f32).astype(bf16)
        stage_f[slot, ch * experts_per_chunk:(ch + 1) * experts_per_chunk] = rows.reshape(
            experts_per_chunk, FIX, EXT)

    ov_loc = locs_ref[0, 0:1, :]
    ov_end = locs_ref[0, 1:2, :]
    for ch in range(OVF_SLOTS // OVF_CHUNK_SLOTS):
        def sort_overflow(ch=ch):
            p = _slot_rows(FIX_SLOTS + ch * OVF_CHUNK_SLOTS, ov_loc, ov_end, float(FIX), rankm,
                           OVF_CHUNK_SLOTS)
            stage_o[slot, ch * OVF_CHUNK_SLOTS:(ch + 1) * OVF_CHUNK_SLOTS, :] = jnp.dot(
                p, hext_ref[...], preferred_element_type=f32)
        pl.when(totc_ref[i] * OVF_ALIGN > ch * OVF_CHUNK_SLOTS)(sort_overflow)

    def fixed_copy(tile, slot_):
        dst = sorted_f_ref.at[:, pl.ds(pl.multiple_of(tile * FIX, FIX), FIX), :]
        return _run_copy(stage_f.at[slot_], dst, semf.at[slot_])

    def overflow_wait(rows, slot_):
        return _run_copy(stage_o.at[slot_, pl.ds(0, rows)], sorted_o_ref.at[pl.ds(0, rows)],
                         semo.at[slot_])

    fixed_copy(i, slot).start()

    def start_group(g, carry):
        for u in range(COPY_GROUP):
            k = g * COPY_GROUP + u
            src = stage_o.at[slot, pl.ds(pl.multiple_of(k * OVF_ALIGN, OVF_ALIGN), OVF_ALIGN)]
            dst = sorted_o_ref.at[pl.ds(pl.multiple_of(dstk_ref[i, k], OVF_ALIGN), OVF_ALIGN)]
            _run_copy(src, dst, semo.at[slot]).start()
        return carry
    jax.lax.fori_loop(0, ngrp_ref[i], start_group, 0)

    @pl.when(i == 0)
    def _():
        zbuf[...] = jnp.zeros(zbuf.shape, f32)

    def zero_copy(e, k):
        dst = sorted_o_ref.at[pl.ds(pl.multiple_of(zst_ref[e] + k * OVF_ALIGN, OVF_ALIGN), OVF_ALIGN)]
        return _run_copy(zbuf, dst, zsem.at[0])

    @pl.when(jnp.logical_and(i > 0, i <= N_EXPERTS))
    def _():
        def wait_zero(k, carry):
            zero_copy(i - 1, k).wait()
            return carry
        jax.lax.fori_loop(0, zch_ref[i - 1], wait_zero, 0)

    @pl.when(i < N_EXPERTS)
    def _():
        def start_zero(k, carry):
            zero_copy(i, k).start()
            return carry
        jax.lax.fori_loop(0, zch_ref[i], start_zero, 0)

    @pl.when(i > 0)
    def _():
        fixed_copy(i - 1, 1 - slot).wait()
        _wait_groups(ngrp_ref[i - 1], lambda rows: overflow_wait(rows, 1 - slot))

    @pl.when(i == n - 1)
    def _():
        fixed_copy(i, slot).wait()
        _wait_groups(ngrp_ref[i], lambda rows: overflow_wait(rows, slot))


def _dispatch(hext, rankm, locs, totc, ngrp, dstk, zst, zch, *, overflow_rows):
    n_tiles = hext.shape[0] // TOK_TILE
    assert n_tiles > N_EXPERTS
    grid_spec = pltpu.PrefetchScalarGridSpec(
        num_scalar_prefetch=5,
        grid=(n_tiles,),
        in_specs=[pl.BlockSpec((TOK_TILE, EXT), lambda i, *_: (i, 0)),
                  pl.BlockSpec((TOK_TILE, LANES), lambda i, *_: (i, 0)),
                  pl.BlockSpec((1, SUBLANES, LANES), lambda i, *_: (i, 0, 0))],
        out_specs=(pl.BlockSpec(memory_space=pl.ANY), pl.BlockSpec(memory_space=pl.ANY)),
        scratch_shapes=[pltpu.VMEM((2, N_EXPERTS, FIX, EXT), bf16),
                        pltpu.VMEM((2, OVF_SLOTS, EXT), f32),
                        pltpu.VMEM((OVF_ALIGN, EXT), f32),
                        pltpu.SemaphoreType.DMA((2,)),
                        pltpu.SemaphoreType.DMA((2,)),
                        pltpu.SemaphoreType.DMA((1,))])
    return pl.pallas_call(
        _dispatch_kernel,
        grid_spec=grid_spec,
        out_shape=(jax.ShapeDtypeStruct((N_EXPERTS, n_tiles * FIX, EXT), bf16),
                   jax.ShapeDtypeStruct((overflow_rows, EXT), f32)),
        compiler_params=pltpu.CompilerParams(
            dimension_semantics=("arbitrary",), vmem_limit_bytes=VMEM_LIMIT_BYTES),
        name="moe_dispatch",
    )(totc, ngrp, dstk, zst, zch, hext, rankm, locs)


def _experts_kernel(ts_ref, xf_ref, xo_ref, wgu_ref, wd_ref, yf_ref, yo_ref,
                    xfbuf, xbuf, ybuf, wgu_b, wd_b, fsem, xsem, ysem):
    e = pl.program_id(0)
    n_valid = ts_ref[N_EXPERTS]

    def fixed_copy(expert, slot):
        return pltpu.make_async_copy(xf_ref.at[expert], xfbuf.at[slot], fsem.at[slot])

    @pl.when(e == 0)
    def _():
        for a in range(FIXED_BUFFERS - 1):
            fixed_copy(a, a).start()

    @pl.when(e + FIXED_BUFFERS - 1 < pl.num_programs(0))
    def _():
        fixed_copy(e + FIXED_BUFFERS - 1, (e + FIXED_BUFFERS - 1) % FIXED_BUFFERS).start()

    wgu_b[...] = wgu_ref[0].astype(bf16)
    wd_b[...] = wd_ref[0].astype(bf16)

    def ffn(x_ext):
        m = x_ext.shape[0]
        gu = jnp.dot(x_ext[:, :D_MODEL], wgu_b[...], preferred_element_type=f32)
        act = (jax.nn.silu(gu[:, :D_EXPERT]) * gu[:, D_EXPERT:]).astype(bf16)
        y = jnp.dot(act, wd_b[...], preferred_element_type=f32)
        g_pack = x_ext[:, D_MODEL:].astype(f32)
        lane = jax.lax.broadcasted_iota(jnp.int32, (m, LANES), 1)
        mine = jnp.where(lane == e, g_pack, jnp.where(lane == e + N_EXPERTS, g_pack, 0.0))
        return (y * jnp.sum(mine, axis=1, keepdims=True)).astype(bf16)

    def x_copy(j, slot):
        rows = pl.ds(pl.multiple_of(j * ROW_TILE, ROW_TILE), ROW_TILE)
        return pltpu.make_async_copy(xo_ref.at[rows], xbuf.at[slot], xsem.at[slot])

    def y_copy(j, slot):
        rows = pl.ds(pl.multiple_of(j * ROW_TILE, ROW_TILE), ROW_TILE)
        return pltpu.make_async_copy(ybuf.at[slot], yo_ref.at[rows], ysem.at[slot])

    ahead = X_BUFFERS - 1

    @pl.when(e == 0)
    def _():
        for a in range(ahead):
            @pl.when(a < n_valid)
            def _():
                x_copy(a, a).start()

    fslot = e % FIXED_BUFFERS
    fixed_copy(e, fslot).wait()
    step_rows = xfbuf.shape[1] // FIX_STEPS
    for c in range(FIX_STEPS):
        rows = slice(c * step_rows, (c + 1) * step_rows)
        yf_ref[0, rows, :] = ffn(xfbuf[fslot, rows, :])

    def tile(j, carry):
        xslot = j % X_BUFFERS
        slot = j % 2
        x_copy(j, xslot).wait()

        @pl.when(j + ahead < n_valid)
        def _():
            x_copy(j + ahead, (j + ahead) % X_BUFFERS).start()

        @pl.when(j >= 2)
        def _():
            y_copy(j - 2, slot).wait()

        ybuf[slot] = ffn(xbuf[xslot].astype(bf16)).astype(f32)
        y_copy(j, slot).start()
        return carry
    jax.lax.fori_loop(ts_ref[e], ts_ref[e + 1], tile, 0)

    @pl.when(e == pl.num_programs(0) - 1)
    def _():
        @pl.when(n_valid >= 2)
        def _():
            y_copy(n_valid - 2, n_valid % 2).wait()

        @pl.when(n_valid >= 1)
        def _():
            y_copy(n_valid - 1, (n_valid - 1) % 2).wait()


def _experts(xf, xo, wgu, wd, ts, *, overflow_rows):
    fixed_rows = xf.shape[1]
    assert fixed_rows % (FIX_STEPS * RUN_ALIGN) == 0
    grid_spec = pltpu.PrefetchScalarGridSpec(
        num_scalar_prefetch=1,
        grid=(N_EXPERTS,),
        in_specs=[pl.BlockSpec(memory_space=pl.ANY),
                  pl.BlockSpec(memory_space=pl.ANY),
                  pl.BlockSpec((1, D_MODEL, 2 * D_EXPERT), lambda e, ts: (e, 0, 0)),
                  pl.BlockSpec((1, D_EXPERT, D_MODEL), lambda e, ts: (e, 0, 0))],
        out_specs=(pl.BlockSpec((1, fixed_rows, D_MODEL), lambda e, ts: (e, 0, 0)),
                   pl.BlockSpec(memory_space=pl.ANY)),
        scratch_shapes=[pltpu.VMEM((FIXED_BUFFERS, fixed_rows, EXT), bf16),
                        pltpu.VMEM((X_BUFFERS, ROW_TILE, EXT), f32),
                        pltpu.VMEM((2, ROW_TILE, D_MODEL), f32),
                        pltpu.VMEM((D_MODEL, 2 * D_EXPERT), bf16),
                        pltpu.VMEM((D_EXPERT, D_MODEL), bf16),
                        pltpu.SemaphoreType.DMA((FIXED_BUFFERS,)),
                        pltpu.SemaphoreType.DMA((X_BUFFERS,)),
                        pltpu.SemaphoreType.DMA((2,))])
    return pl.pallas_call(
        _experts_kernel,
        grid_spec=grid_spec,
        out_shape=(jax.ShapeDtypeStruct((N_EXPERTS, fixed_rows, D_MODEL), bf16),
                   jax.ShapeDtypeStruct((overflow_rows, D_MODEL), f32)),
        compiler_params=pltpu.CompilerParams(
            dimension_semantics=("arbitrary",), vmem_limit_bytes=VMEM_LIMIT_BYTES),
        name="moe_experts",
    )(ts, xf, xo, wgu, wd)


def _combine_kernel(totc_ref, ngrp_ref, srck_ref,
                    h_ref, rankm_ref, locc_ref, yf_ref, yo_ref, wsu_ref, wsd_ref, ln_g_ref, ln_b_ref,
                    out_bm_ref, out_tm_ref, yloc_f, yloc_o, acc_scr, p_scr, semf, semo,
                    *, n_bm_tiles):
    i = pl.program_id(0)
    n = pl.num_programs(0)
    slot = i % 2

    def fixed_copy(tile, slot_):
        src = yf_ref.at[:, pl.ds(pl.multiple_of(tile * FIX, FIX), FIX), :]
        return _run_copy(src, yloc_f.at[slot_], semf.at[slot_])

    def overflow_wait(rows, slot_):
        return _run_copy(yo_ref.at[pl.ds(0, rows)], yloc_o.at[slot_, pl.ds(0, rows)], semo.at[slot_])

    def fetch(tile, slot_):
        fixed_copy(tile, slot_).start()

        def start_group(g, carry):
            for u in range(COPY_GROUP):
                k = g * COPY_GROUP + u
                src = yo_ref.at[pl.ds(pl.multiple_of(srck_ref[tile, k], OVF_ALIGN), OVF_ALIGN)]
                dst = yloc_o.at[slot_, pl.ds(pl.multiple_of(k * OVF_ALIGN, OVF_ALIGN), OVF_ALIGN)]
                _run_copy(src, dst, semo.at[slot_]).start()
            return carry
        jax.lax.fori_loop(0, ngrp_ref[tile], start_group, 0)

    @pl.when(i == 0)
    def _():
        yloc_o[...] = jnp.zeros(yloc_o.shape, f32)
        fetch(0, 0)

    @pl.when(i + 1 < n)
    def _():
        fetch(i + 1, 1 - slot)

    h = h_ref[...]
    hb = h.astype(bf16)
    su = jnp.dot(hb, wsu_ref[...], preferred_element_type=f32)
    act = (jax.nn.silu(su[:, :D_SHARED]) * su[:, D_SHARED:]).astype(bf16)
    acc_scr[...] = jnp.dot(act, wsd_ref[...], preferred_element_type=f32)

    rankm = rankm_ref[...]
    fx_loc, fx_end = _fixed_ranges((LANES, 1), 0)
    for ch in range(FIX_SLOTS // SLOT_CHUNK):
        p_scr[:, ch * SLOT_CHUNK:(ch + 1) * SLOT_CHUNK] = _slot_cols(
            ch * SLOT_CHUNK, fx_loc, fx_end, 0.0, rankm)

    fixed_copy(i, slot).wait()
    _wait_groups(ngrp_ref[i], lambda rows: overflow_wait(rows, slot))

    acc_scr[...] += jnp.dot(p_scr[...], yloc_f[slot].reshape(FIX_SLOTS, D_MODEL),
                            preferred_element_type=f32)
    ov_loc = locc_ref[0, :, 0:1]
    ov_end = locc_ref[0, :, 1:2]
    for ch in range(OVF_SLOTS // OVF_CHUNK_SLOTS):
        def gather_overflow(ch=ch):
            p = _slot_cols(FIX_SLOTS + ch * OVF_CHUNK_SLOTS, ov_loc, ov_end, float(FIX), rankm,
                           OVF_CHUNK_SLOTS)
            y_o = yloc_o[slot, ch * OVF_CHUNK_SLOTS:(ch + 1) * OVF_CHUNK_SLOTS, :].astype(bf16)
            acc_scr[...] += jnp.dot(p, y_o, preferred_element_type=f32)
        pl.when(totc_ref[i] * OVF_ALIGN > ch * OVF_CHUNK_SLOTS)(gather_overflow)
    y = _layer_norm(DN_ALPHA * h + acc_scr[...], ln_g_ref[...], ln_b_ref[...])

    @pl.when(i < n_bm_tiles)
    def _():
        nb, tc, _ = out_bm_ref.shape
        out_bm_ref[...] = jnp.transpose(y.reshape(tc, nb, D_MODEL), (1, 0, 2))

    @pl.when(i >= n_bm_tiles)
    def _():
        out_tm_ref[...] = y


def _combine(h, rankm, locc, yf, yo, wsu, wsd, ln_g, ln_b, totc, ngrp, srck, *, bm_shape):
    n_tiles = h.shape[0] // TOK_TILE
    nb, length, _ = bm_shape
    tc = TOK_TILE // nb
    n_bm_tiles = length // tc
    n_tm_tiles = n_tiles - n_bm_tiles
    const = lambda shape: pl.BlockSpec(shape, lambda i, *_: (0,) * len(shape))
    grid_spec = pltpu.PrefetchScalarGridSpec(
        num_scalar_prefetch=3,
        grid=(n_tiles,),
        in_specs=[pl.BlockSpec((TOK_TILE, D_MODEL), lambda i, *_: (i, 0)),
                  pl.BlockSpec((TOK_TILE, LANES), lambda i, *_: (i, 0)),
                  pl.BlockSpec((1, LANES, 2), lambda i, *_: (i, 0, 0)),
                  pl.BlockSpec(memory_space=pl.ANY),
                  pl.BlockSpec(memory_space=pl.ANY),
                  const(wsu.shape), const(wsd.shape), const(ln_g.shape), const(ln_b.shape)],
        out_specs=(pl.BlockSpec((nb, tc, D_MODEL),
                                lambda i, *_: (0, jnp.minimum(i, n_bm_tiles - 1), 0)),
                   pl.BlockSpec((TOK_TILE, D_MODEL),
                                lambda i, *_: (jnp.maximum(i - n_bm_tiles, 0), 0))),
        scratch_shapes=[pltpu.VMEM((2, N_EXPERTS, FIX, D_MODEL), bf16),
                        pltpu.VMEM((2, OVF_SLOTS, D_MODEL), f32),
                        pltpu.VMEM((TOK_TILE, D_MODEL), f32),
                        pltpu.VMEM((TOK_TILE, FIX_SLOTS), bf16),
                        pltpu.SemaphoreType.DMA((2,)),
                        pltpu.SemaphoreType.DMA((2,))])
    return pl.pallas_call(
        functools.partial(_combine_kernel, n_bm_tiles=n_bm_tiles),
        grid_spec=grid_spec,
        out_shape=(jax.ShapeDtypeStruct(bm_shape, f32),
                   jax.ShapeDtypeStruct((n_tm_tiles * TOK_TILE, D_MODEL), f32)),
        compiler_params=pltpu.CompilerParams(
            dimension_semantics=("arbitrary",), vmem_limit_bytes=VMEM_LIMIT_BYTES),
        name="moe_combine",
    )(totc, ngrp, srck, h, rankm, locc, yf, yo, wsu, wsd, ln_g, ln_b)


def _round_up(x, m):
    return (x + m - 1) // m * m


def _moe(h, wr, rb, wgu, wd, wsu, wsd, ln_g, ln_b, *, bm_shape):
    n_tok = h.shape[0]
    n_tiles = n_tok // TOK_TILE
    max_rows = _round_up(n_tiles * (OVF_CHUNKS - 1) * OVF_ALIGN + N_EXPERTS * (ROW_TILE - 1), ROW_TILE)

    hext, rankm, cnt = _router(h, wr, rb)

    i32 = jnp.int32
    cnt = cnt[:, 0, :N_EXPERTS].astype(i32)
    oc = (jnp.maximum(cnt - FIX, 0) + OVF_ALIGN - 1) // OVF_ALIGN
    over_tiles = jnp.cumsum(oc, axis=0)
    region_rows = over_tiles[-1] * OVF_ALIGN
    region_size = _round_up(region_rows, ROW_TILE)
    region_start = jnp.cumsum(region_size) - region_size
    run_dst = region_start[None, :] + OVF_ALIGN * (over_tiles - oc)
    ch_end = jnp.cumsum(oc, axis=1)
    ch_beg = ch_end - oc
    totc = ch_end[:, -1]
    k = jnp.arange(OVF_CHUNKS, dtype=i32)
    kk = k[None, :, None]
    mine = jnp.logical_and(ch_beg[:, None, :] <= kk, kk < ch_end[:, None, :])
    chunk_dst = jnp.sum(jnp.where(mine, run_dst[:, None, :] + OVF_ALIGN * (kk - ch_beg[:, None, :]), 0),
                        axis=-1)
    live = k[None, :] < totc[:, None]
    tile_ids = jnp.arange(n_tiles, dtype=i32)[:, None]
    spare = max_rows + (tile_ids % 2) * OVF_SLOTS + k[None, :] * OVF_ALIGN
    dstk = jnp.where(live, chunk_dst, spare).astype(i32)
    srck = jnp.where(live, chunk_dst, 0).astype(i32)
    ngrp = ((totc + COPY_GROUP - 1) // COPY_GROUP).astype(i32)
    zst = (region_start + region_rows).astype(i32)
    zch = ((region_size - region_rows) // OVF_ALIGN).astype(i32)
    tile_start = jnp.concatenate([region_start, region_start[-1:] + region_size[-1:]]) // ROW_TILE
    tile_start = tile_start.astype(i32)

    pad = ((0, 0), (0, LANES - N_EXPERTS))
    loc_f = jnp.pad((FIX_SLOTS + OVF_ALIGN * ch_beg).astype(f32), pad, constant_values=NO_RUN)
    end_f = jnp.pad((FIX_SLOTS + OVF_ALIGN * ch_end).astype(f32), pad, constant_values=NO_RUN)
    locs = jnp.concatenate([loc_f[:, None, :], end_f[:, None, :],
                            jnp.zeros((n_tiles, SUBLANES - 2, LANES), f32)], axis=1)
    locc = jnp.stack([loc_f, end_f], axis=-1)

    totc = totc.astype(i32)
    overflow_rows = max_rows + 2 * OVF_SLOTS
    xf, xo = _dispatch(hext, rankm, locs, totc, ngrp, dstk, zst, zch, overflow_rows=overflow_rows)
    yf, yo = _experts(xf, xo, wgu, wd, tile_start, overflow_rows=max_rows)
    return _combine(h, rankm, locc, yf, yo, wsu, wsd, ln_g, ln_b, totc, ngrp, srck, bm_shape=bm_shape)


def _diag_tiles(blocks, n_tiles):
    n_blocks, r, c = blocks.shape
    per = n_blocks // n_tiles
    wide = jnp.tile(blocks.reshape(n_blocks * r, c), (1, per))
    row_block = (jnp.arange(n_blocks * r) // r) % per
    col_block = jnp.arange(per * c) // c
    wide = jnp.where(row_block[:, None] == col_block[None, :], wide, 0)
    return wide.reshape(n_tiles, per * r, per * c)


def _head_block_diag(w):
    return _diag_tiles(w, D_RG // MXU_DIM)


def _s5_in_tiles(b):
    return _diag_tiles(b.transpose(0, 2, 1), 2)


def _s5_out_tiles(cw):
    return _diag_tiles(cw.transpose(0, 2, 1), 2)


def _row(v):
    return v.reshape(1, -1)


def kernel(x_prompt, x_sample, state_rg_conv, state_rg_h, state_s5_re, state_s5_im, w_in, conv_w, conv_b, rg_w_a, rg_b_a, rg_w_i, rg_b_i, rg_lam, s5_a_re, s5_a_im, s5_log_dt, s5_b_re, s5_b_im, s5_c_re, s5_c_im, s5_d, w_glu, b_glu, w_out, ln1_g, ln1_b, w_router, router_bias, w_gate_up, w_down, w_shared_up, w_shared_down, ln2_g, ln2_b):
    l = 0
    bp, lp, _ = x_prompt.shape
    bs, ls, _ = x_sample.shape

    are, aim, bbre, bbim = _s5_prep(
        _row(s5_a_re[l]), _row(s5_a_im[l]),
        _row(jnp.repeat(s5_log_dt[l], S5_STATE)),
        _s5_in_tiles(s5_b_re[l]), _s5_in_tiles(s5_b_im[l]))
    params = (w_in[l].astype(bf16), conv_w[l], _row(conv_b[l]),
              _head_block_diag(rg_w_a[l]).astype(bf16), _row(rg_b_a[l]),
              _head_block_diag(rg_w_i[l]).astype(bf16), _row(rg_b_i[l]), _row(rg_lam[l]),
              are, aim, bbre, bbim,
              _s5_out_tiles(s5_c_re[l]).astype(bf16), _s5_out_tiles(s5_c_im[l]).astype(bf16),
              _row(s5_d[l]), w_glu[l].astype(bf16), _row(b_glu[l]), w_out[l].astype(bf16),
              _row(ln1_g[l]), _row(ln1_b[l]))

    tail = CONV_WIDTH - 1
    n_tok = lp * bp + ls * bs
    xs_tm = x_sample.transpose(1, 0, 2).reshape(ls * bs, D_MODEL)
    h_all, sc, sh, sre, sim = _mixer(
        xs_tm, state_rg_conv[l].transpose(1, 0, 2).reshape(tail * bs, D_RG), state_rg_h[l],
        state_s5_re[l].reshape(bs, S5_N), state_s5_im[l].reshape(bs, S5_N), params,
        nb=bs, tc=ls, name="mixer_sample",
        h_all=jnp.zeros((SUBLANES, LANES), f32), total_rows=n_tok, first_row=lp * bp)
    h_all, pc, ph, pre, pim = _mixer(
        x_prompt, jnp.zeros((tail * bp, D_RG), f32), jnp.zeros((bp, D_RG), f32),
        jnp.zeros((bp, S5_N), f32), jnp.zeros((bp, S5_N), f32), params,
        nb=bp, tc=PROMPT_CHUNK_ROWS // bp, name="mixer_prompt",
        h_all=h_all, total_rows=n_tok, first_row=0)
    wr = jnp.pad(w_router[l], ((0, 0), (0, LANES - N_EXPERTS))).astype(bf16)
    rb = jnp.pad(_row(router_bias[l]), ((0, 0), (0, LANES - N_EXPERTS)))
    yp, ys_tm = _moe(h_all, wr, rb, w_gate_up[l], w_down[l],
                     w_shared_up[l].astype(bf16), w_shared_down[l].astype(bf16),
                     _row(ln2_g[l]), _row(ln2_b[l]), bm_shape=x_prompt.shape)
    ys = ys_tm.reshape(ls, bs, D_MODEL).transpose(1, 0, 2)

    def conv_out(cv, nbatch):
        return cv.reshape(tail, nbatch, D_RG).transpose(1, 0, 2)[None]

    return (yp, ys,
            conv_out(pc, bp), ph[None],
            pre.reshape(1, bp, S5_GROUPS, S5_STATE), pim.reshape(1, bp, S5_GROUPS, S5_STATE),
            conv_out(sc, bs), sh[None],
            sre.reshape(1, bs, S5_GROUPS, S5_STATE), sim.reshape(1, bs, S5_GROUPS, S5_STATE))
```

```python
import functools
import math

import jax
import jax.numpy as jnp
from jax.experimental import pallas as pl
from jax.experimental.pallas import tpu as pltpu

D_MODEL = 1024
D_RG = 512
RG_HEADS = 8
RG_HEAD_DIM = 64
CONV_WIDTH = 4
RG_C = 8.0
D_S5 = 512
S5_GROUP = 16
S5_GROUPS = 32
S5_STATE = 64
S5_N = S5_GROUPS * S5_STATE
N_EXPERTS = 64
TOP_K = 8
D_EXPERT = 256
D_SHARED = 256
ROUTED_SCALE = 2.5
DEPTH = 1
DN_ALPHA = (2.0 * DEPTH) ** 0.25
LN_EPS = 1e-5

SUBLANES = 8
LANES = 128
MXU_DIM = 256
S5_SCAN_COLS = 512
PROMPT_CHUNK_ROWS = 512
VMEM_LIMIT_BYTES = 56 * 1024 * 1024

bf16 = jnp.bfloat16
f32 = jnp.float32


def _gelu_tanh(x):
    c = math.sqrt(2.0 / math.pi)
    return x * (0.5 * (1.0 + jnp.tanh(c * (x + 0.044715 * (x * x * x)))))


def _layer_norm(x, g, b):
    mu = jnp.mean(x, axis=-1, keepdims=True)
    xc = x - mu
    var = jnp.mean(xc * xc, axis=-1, keepdims=True)
    return xc * jax.lax.rsqrt(var + LN_EPS) * g + b


def _s5_prep_kernel(lr_ref, li_ref, ldt_ref, bre_ref, bim_ref,
                    are_ref, aim_ref, bbre_ref, bbim_ref):
    lr = lr_ref[...]
    li = li_ref[...]
    dt = jnp.exp(ldt_ref[...])
    mag = jnp.exp(lr * dt)
    abar_re = mag * jnp.cos(li * dt)
    abar_im = mag * jnp.sin(li * dt)
    den = lr * lr + li * li
    nr = abar_re - 1.0
    ni = abar_im
    coef_re = (nr * lr + ni * li) / den
    coef_im = (ni * lr - nr * li) / den
    are_ref[...] = abar_re
    aim_ref[...] = abar_im
    half = S5_N // 2
    for k in range(2):
        cre = coef_re[:, k * half:(k + 1) * half]
        cim = coef_im[:, k * half:(k + 1) * half]
        br = bre_ref[k]
        bi = bim_ref[k]
        bbre_ref[k] = (cre * br - cim * bi).astype(bf16)
        bbim_ref[k] = (cre * bi + cim * br).astype(bf16)


def _s5_prep(lr, li, ldt, bre_t, bim_t):
    half = S5_N // 2
    return pl.pallas_call(
        _s5_prep_kernel,
        out_shape=(jax.ShapeDtypeStruct((1, S5_N), f32),
                   jax.ShapeDtypeStruct((1, S5_N), f32),
                   jax.ShapeDtypeStruct((2, MXU_DIM, half), bf16),
                   jax.ShapeDtypeStruct((2, MXU_DIM, half), bf16)),
        name="s5_prep",
    )(lr, li, ldt, bre_t, bim_t)


def _mixer_kernel(h_all_ref, x_ref, conv0_ref, h0_ref, s0r_ref, s0i_ref,
                  w_in_ref, conv_w_ref, conv_b_ref, wa_ref, ba_ref, wi_ref, bi_ref, lam_ref,
                  are_ref, aim_ref, bbre_ref, bbim_ref, cre_ref, cim_ref, d_ref,
                  wglu_ref, bglu_ref, wout_ref, ln_g_ref, ln_b_ref,
                  hout_ref, conv_out_ref, hlast_ref, sre_out_ref, sim_out_ref,
                  pad_scr, a_scr, b_scr, bur_scr, bui_scr, hst_scr, sr_scr, si_scr,
                  *, nb, tc, batch_major_input):
    del h_all_ref
    rows = nb * tc
    tail = (CONV_WIDTH - 1) * nb
    c = pl.program_id(0)

    @pl.when(c == 0)
    def _():
        pad_scr[0:tail, :] = conv0_ref[...]
        hst_scr[...] = h0_ref[...]
        sr_scr[...] = s0r_ref[...]
        si_scr[...] = s0i_ref[...]

    if batch_major_input:
        x = jnp.transpose(x_ref[...], (1, 0, 2)).reshape(rows, D_MODEL)
    else:
        x = x_ref[...]
    proj = jnp.dot(x.astype(bf16), w_in_ref[...], preferred_element_type=f32)
    x_rg = proj[:, :D_RG]
    g_rg = proj[:, D_RG:2 * D_RG]
    u = proj[:, 2 * D_RG:]

    pad_scr[tail:tail + rows, :] = x_rg
    conv_w = conv_w_ref[...]
    acc = conv_w[0:1, :] * pad_scr[0:rows, :]
    for k in range(1, CONV_WIDTH):
        acc = acc + conv_w[k:k + 1, :] * pad_scr[k * nb:k * nb + rows, :]
    xc = conv_b_ref[...] + acc
    new_tail = pad_scr[rows:rows + tail, :]
    pad_scr[0:tail, :] = new_tail

    xcb = xc.astype(bf16)
    ga = []
    gi = []
    for hh in range(D_RG // MXU_DIM):
        xs = xcb[:, hh * MXU_DIM:(hh + 1) * MXU_DIM]
        ga.append(jnp.dot(xs, wa_ref[hh], preferred_element_type=f32))
        gi.append(jnp.dot(xs, wi_ref[hh], preferred_element_type=f32))
    r = jax.nn.sigmoid(jnp.concatenate(ga, axis=1) + ba_ref[...])
    i = jax.nn.sigmoid(jnp.concatenate(gi, axis=1) + bi_ref[...])
    nlam = -lam_ref[...]
    softplus = jnp.maximum(nlam, 0.0) + jnp.log1p(jnp.exp(-jnp.abs(nlam)))
    log_a = (-RG_C) * r * softplus
    a_scr[...] = jnp.exp(log_a)
    th = jnp.tanh(log_a)
    b_scr[...] = jnp.sqrt((-2.0 * th) / (1.0 - th)) * (i * xc)

    for rg in range(nb // SUBLANES):
        r0 = rg * SUBLANES
        h = hst_scr[r0:r0 + SUBLANES, :]
        for t in range(tc):
            q = t * nb + r0
            h = a_scr[q:q + SUBLANES, :] * h + b_scr[q:q + SUBLANES, :]
            b_scr[q:q + SUBLANES, :] = h
        hst_scr[r0:r0 + SUBLANES, :] = h
    y_rg = b_scr[...] * _gelu_tanh(g_rg)

    ub = u.astype(bf16)
    half = S5_N // 2
    for k in range(2):
        us = ub[:, k * MXU_DIM:(k + 1) * MXU_DIM]
        bur_scr[:, k * half:(k + 1) * half] = jnp.dot(us, bbre_ref[k], preferred_element_type=f32)
        bui_scr[:, k * half:(k + 1) * half] = jnp.dot(us, bbim_ref[k], preferred_element_type=f32)
    for rg in range(nb // SUBLANES):
        r0 = rg * SUBLANES
        for cb in range(S5_N // S5_SCAN_COLS):
            c0 = cb * S5_SCAN_COLS
            ar = jnp.broadcast_to(are_ref[:, c0:c0 + S5_SCAN_COLS], (SUBLANES, S5_SCAN_COLS))
            ai = jnp.broadcast_to(aim_ref[:, c0:c0 + S5_SCAN_COLS], (SUBLANES, S5_SCAN_COLS))
            xr = sr_scr[r0:r0 + SUBLANES, c0:c0 + S5_SCAN_COLS]
            xi = si_scr[r0:r0 + SUBLANES, c0:c0 + S5_SCAN_COLS]
            for t in range(tc):
                q = t * nb + r0
                br = bur_scr[q:q + SUBLANES, c0:c0 + S5_SCAN_COLS]
                bi_ = bui_scr[q:q + SUBLANES, c0:c0 + S5_SCAN_COLS]
                nxr = ar * xr - ai * xi + br
                nxi = ar * xi + ai * xr + bi_
                bur_scr[q:q + SUBLANES, c0:c0 + S5_SCAN_COLS] = nxr
                bui_scr[q:q + SUBLANES, c0:c0 + S5_SCAN_COLS] = nxi
                xr, xi = nxr, nxi
            sr_scr[r0:r0 + SUBLANES, c0:c0 + S5_SCAN_COLS] = xr
            si_scr[r0:r0 + SUBLANES, c0:c0 + S5_SCAN_COLS] = xi
    ys = []
    for j in range(D_S5 // MXU_DIM):
        xrb = bur_scr[:, j * half:(j + 1) * half].astype(bf16)
        xib = bui_scr[:, j * half:(j + 1) * half].astype(bf16)
        ys.append(jnp.dot(xrb, cre_ref[j], preferred_element_type=f32)
                  - jnp.dot(xib, cim_ref[j], preferred_element_type=f32))
    y_s5 = jnp.concatenate(ys, axis=1) + d_ref[...] * u
    yg = _gelu_tanh(y_s5)
    glu = jnp.dot(yg.astype(bf16), wglu_ref[...], preferred_element_type=f32) + bglu_ref[...]
    y_s5 = yg * jax.nn.sigmoid(glu)

    ycat = jnp.concatenate([y_rg, y_s5], axis=1).astype(bf16)
    mix = jnp.dot(ycat, wout_ref[...], preferred_element_type=f32)
    hout_ref[...] = _layer_norm(DN_ALPHA * x + mix, ln_g_ref[...], ln_b_ref[...])

    @pl.when(c == pl.num_programs(0) - 1)
    def _():
        conv_out_ref[...] = pad_scr[0:tail, :]
        hlast_ref[...] = hst_scr[...]
        sre_out_ref[...] = sr_scr[...]
        sim_out_ref[...] = si_scr[...]


def _full(shape):
    n = len(shape)
    return pl.BlockSpec(shape, lambda c: (0,) * n)


def _mixer(x, conv0, h0, s0r, s0i, params, *, nb, tc, name, h_all, total_rows, first_row):
    batch_major_input = x.ndim == 3
    rows = nb * tc
    if batch_major_input:
        n_chunks = x.shape[1] // tc
        x_spec = pl.BlockSpec((nb, tc, D_MODEL), lambda c: (0, c, 0))
    else:
        n_chunks = x.shape[0] // rows
        x_spec = pl.BlockSpec((rows, D_MODEL), lambda c: (c, 0))
    first_block = first_row // rows
    aliased = h_all.shape == (total_rows, D_MODEL)
    tail = (CONV_WIDTH - 1) * nb
    small = (conv0, h0, s0r, s0i) + tuple(params)
    in_specs = [pl.BlockSpec(memory_space=pl.ANY), x_spec]
    in_specs += [_full(a.shape) for a in small]
    out_shape = (jax.ShapeDtypeStruct((total_rows, D_MODEL), f32),
                 jax.ShapeDtypeStruct((tail, D_RG), f32),
                 jax.ShapeDtypeStruct((nb, D_RG), f32),
                 jax.ShapeDtypeStruct((nb, S5_N), f32),
                 jax.ShapeDtypeStruct((nb, S5_N), f32))
    out_specs = (pl.BlockSpec((rows, D_MODEL), lambda c: (c + first_block, 0)),
                 _full((tail, D_RG)), _full((nb, D_RG)), _full((nb, S5_N)), _full((nb, S5_N)))
    scratch = [pltpu.VMEM((rows + tail, D_RG), f32),
               pltpu.VMEM((rows, D_RG), f32),
               pltpu.VMEM((rows, D_RG), f32),
               pltpu.VMEM((rows, S5_N), f32),
               pltpu.VMEM((rows, S5_N), f32),
               pltpu.VMEM((nb, D_RG), f32),
               pltpu.VMEM((nb, S5_N), f32),
               pltpu.VMEM((nb, S5_N), f32)]
    return pl.pallas_call(
        functools.partial(_mixer_kernel, nb=nb, tc=tc, batch_major_input=batch_major_input),
        grid=(n_chunks,),
        in_specs=in_specs,
        out_specs=out_specs,
        out_shape=out_shape,
        scratch_shapes=scratch,
        input_output_aliases={0: 0} if aliased else {},
        compiler_params=pltpu.CompilerParams(
            dimension_semantics=("arbitrary",), vmem_limit_bytes=VMEM_LIMIT_BYTES),
        name=name,
    )(h_all, x, *small)


TOK_TILE = 256
RUN_ALIGN = 16
OVF_ALIGN = 8
FIX = 32
FIX_SLOTS = N_EXPERTS * FIX
OVF_CHUNKS = 256
OVF_CHUNK_SLOTS = 384
OVF_SLOTS = 6 * OVF_CHUNK_SLOTS
SLOT_CHUNK = 512
FIX_STEPS = 4
ROW_TILE = 256
X_BUFFERS = 4
FIXED_BUFFERS = 3
COPY_GROUP = 4
EXT = D_MODEL + LANES
ROUTER_TILES = 6
NO_RUN = 1.0e9


def _top_k_gates(scores, rb):
    rows = scores.shape[0]
    lane_f = jax.lax.broadcasted_iota(jnp.int32, (rows, LANES), 1).astype(f32)
    biased = jnp.where(lane_f < float(N_EXPERTS), scores + rb, -jnp.inf)
    sel = jnp.zeros((rows, LANES), f32)
    mask = jnp.zeros((rows, LANES), f32)
    for _ in range(TOP_K):
        m = jnp.max(biased, axis=1, keepdims=True)
        idx = jnp.min(jnp.where(biased == m, lane_f, float(LANES)), axis=1, keepdims=True)
        hit = lane_f == idx
        sel = jnp.where(hit, scores, sel)
        mask = jnp.where(hit, 1.0, mask)
        biased = jnp.where(hit, -jnp.inf, biased)
    gates = sel / jnp.sum(sel, axis=1, keepdims=True) * ROUTED_SCALE
    return mask, gates


def _router_kernel(h_ref, wr_ref, rb_ref, hext_ref, rankm_ref, cnt_ref):
    hb = h_ref[...].astype(bf16)
    scores = jax.nn.sigmoid(jnp.dot(hb, wr_ref[...], preferred_element_type=f32))
    mask, gates = _top_k_gates(scores, rb_ref[...])
    t_row = jax.lax.broadcasted_iota(jnp.int32, (TOK_TILE, TOK_TILE), 0)
    t_col = jax.lax.broadcasted_iota(jnp.int32, (TOK_TILE, TOK_TILE), 1)
    earlier = jnp.where(t_col < t_row, 1.0, 0.0).astype(bf16)
    for sub in range(ROUTER_TILES):
        m = mask[sub * TOK_TILE:(sub + 1) * TOK_TILE]
        rank = jnp.dot(earlier, m.astype(bf16), preferred_element_type=f32)
        rankm_ref[sub * TOK_TILE:(sub + 1) * TOK_TILE, :] = jnp.where(m > 0.0, rank, -1.0).astype(bf16)
        cnt_ref[sub] = jnp.broadcast_to(jnp.sum(m, axis=0, keepdims=True), (SUBLANES, LANES))
    g_hi = gates.astype(bf16).astype(f32)
    g_pack = g_hi + pltpu.roll(gates - g_hi, N_EXPERTS, 1)
    hext_ref[:, :D_MODEL] = hb
    hext_ref[:, D_MODEL:] = g_pack.astype(bf16)


def _router(h, wr, rb):
    n_tiles = h.shape[0] // TOK_TILE
    assert n_tiles % ROUTER_TILES == 0
    rows = ROUTER_TILES * TOK_TILE
    const = lambda shape: pl.BlockSpec(shape, lambda i: (0,) * len(shape))
    return pl.pallas_call(
        _router_kernel,
        grid=(n_tiles // ROUTER_TILES,),
        in_specs=[pl.BlockSpec((rows, D_MODEL), lambda i: (i, 0)), const(wr.shape), const(rb.shape)],
        out_specs=(pl.BlockSpec((rows, EXT), lambda i: (i, 0)),
                   pl.BlockSpec((rows, LANES), lambda i: (i, 0)),
                   pl.BlockSpec((ROUTER_TILES, SUBLANES, LANES), lambda i: (i, 0, 0))),
        out_shape=(jax.ShapeDtypeStruct((h.shape[0], EXT), bf16),
                   jax.ShapeDtypeStruct((h.shape[0], LANES), bf16),
                   jax.ShapeDtypeStruct((n_tiles, SUBLANES, LANES), f32)),
        compiler_params=pltpu.CompilerParams(
            dimension_semantics=("arbitrary",), vmem_limit_bytes=VMEM_LIMIT_BYTES),
        name="moe_router",
    )(h, wr, rb)


def _run_copy(src, dst, sem):
    return pltpu.make_async_copy(src, dst, sem)


def _fixed_ranges(shape, axis):
    e = jax.lax.broadcasted_iota(jnp.int32, shape, axis).astype(f32)
    loc = jnp.where(e < float(N_EXPERTS), e * float(FIX), NO_RUN)
    return loc, loc + float(FIX)


def _slot_rows(first_slot, loc_row, end_row, base, rankm, n_slots=SLOT_CHUNK):
    s = (jax.lax.broadcasted_iota(jnp.int32, (n_slots, LANES), 0) + first_slot).astype(f32)
    in_run = jnp.where(s >= loc_row, jnp.where(s < end_row, 1.0, 0.0), 0.0)
    r_col = s[:, 0:1] - jnp.sum(in_run * (loc_row - base), axis=1, keepdims=True)
    q = jax.lax.dot_general(in_run.astype(bf16), rankm, (((1,), (1,)), ((), ())),
                            preferred_element_type=f32)
    return jnp.where(q == r_col, 1.0, 0.0).astype(bf16)


def _slot_cols(first_slot, loc_col, end_col, base, rankm, n_slots=SLOT_CHUNK):
    s = (jax.lax.broadcasted_iota(jnp.int32, (LANES, n_slots), 1) + first_slot).astype(f32)
    in_run = jnp.where(s >= loc_col, jnp.where(s < end_col, 1.0, 0.0), 0.0)
    r_row = s[0:1, :] - jnp.sum(in_run * (loc_col - base), axis=0, keepdims=True)
    q = jnp.dot(rankm, in_run.astype(bf16), preferred_element_type=f32)
    return jnp.where(q == r_row, 1.0, 0.0).astype(bf16)


def _wait_groups(n_groups, make_copy):
    max_bits = (OVF_CHUNKS // COPY_GROUP).bit_length()
    for b in range(max_bits):
        rows = (COPY_GROUP * OVF_ALIGN) << b

        @pl.when(jnp.bitwise_and(jnp.right_shift(n_groups, b), 1) == 1)
        def _():
            make_copy(rows).wait()


def _dispatch_kernel(totc_ref, ngrp_ref, dstk_ref, zst_ref, zch_ref,
                     hext_ref, rankm_ref, locs_ref, sorted_f_ref, sorted_o_ref,
                     stage_f, stage_o, zbuf, semf, semo, zsem):
    i = pl.program_id(0)
    n = pl.num_programs(0)
    slot = i % 2
    rankm = rankm_ref[...]

    fx_loc, fx_end = _fixed_ranges((1, LANES), 1)
    experts_per_chunk = SLOT_CHUNK // FIX
    for ch in range(FIX_SLOTS // SLOT_CHUNK):
        p = _slot_rows(ch * SLOT_CHUNK, fx_loc, fx_end, 0.0, rankm)
        rows = jnp.dot(p, hext_ref[...], preferred_element_type=f32).astype(bf16)
        stage_f[slot, ch * experts_per_chunk:(ch + 1) * experts_per_chunk] = rows.reshape(
            experts_per_chunk, FIX, EXT)

    ov_loc = locs_ref[0, 0:1, :]
    ov_end = locs_ref[0, 1:2, :]
    for ch in range(OVF_SLOTS // OVF_CHUNK_SLOTS):
        def sort_overflow(ch=ch):
            p = _slot_rows(FIX_SLOTS + ch * OVF_CHUNK_SLOTS, ov_loc, ov_end, float(FIX), rankm,
                           OVF_CHUNK_SLOTS)
            stage_o[slot, ch * OVF_CHUNK_SLOTS:(ch + 1) * OVF_CHUNK_SLOTS, :] = jnp.dot(
                p, hext_ref[...], preferred_element_type=f32)
        pl.when(totc_ref[i] * OVF_ALIGN > ch * OVF_CHUNK_SLOTS)(sort_overflow)

    def fixed_copy(tile, slot_):
        dst = sorted_f_ref.at[:, pl.ds(pl.multiple_of(tile * FIX, FIX), FIX), :]
        return _run_copy(stage_f.at[slot_], dst, semf.at[slot_])

    def overflow_wait(rows, slot_):
        return _run_copy(stage_o.at[slot_, pl.ds(0, rows)], sorted_o_ref.at[pl.ds(0, rows)],
                         semo.at[slot_])

    fixed_copy(i, slot).start()

    def start_group(g, carry):
        for u in range(COPY_GROUP):
            k = g * COPY_GROUP + u
            src = stage_o.at[slot, pl.ds(pl.multiple_of(k * OVF_ALIGN, OVF_ALIGN), OVF_ALIGN)]
            dst = sorted_o_ref.at[pl.ds(pl.multiple_of(dstk_ref[i, k], OVF_ALIGN), OVF_ALIGN)]
            _run_copy(src, dst, semo.at[slot]).start()
        return carry
    jax.lax.fori_loop(0, ngrp_ref[i], start_group, 0)

    @pl.when(i == 0)
    def _():
        zbuf[...] = jnp.zeros(zbuf.shape, f32)

    def zero_copy(e, k):
        dst = sorted_o_ref.at[pl.ds(pl.multiple_of(zst_ref[e] + k * OVF_ALIGN, OVF_ALIGN), OVF_ALIGN)]
        return _run_copy(zbuf, dst, zsem.at[0])

    @pl.when(jnp.logical_and(i > 0, i <= N_EXPERTS))
    def _():
        def wait_zero(k, carry):
            zero_copy(i - 1, k).wait()
            return carry
        jax.lax.fori_loop(0, zch_ref[i - 1], wait_zero, 0)

    @pl.when(i < N_EXPERTS)
    def _():
        def start_zero(k, carry):
            zero_copy(i, k).start()
            return carry
        jax.lax.fori_loop(0, zch_ref[i], start_zero, 0)

    @pl.when(i > 0)
    def _():
        fixed_copy(i - 1, 1 - slot).wait()
        _wait_groups(ngrp_ref[i - 1], lambda rows: overflow_wait(rows, 1 - slot))

    @pl.when(i == n - 1)
    def _():
        fixed_copy(i, slot).wait()
        _wait_groups(ngrp_ref[i], lambda rows: overflow_wait(rows, slot))


def _dispatch(hext, rankm, locs, totc, ngrp, dstk, zst, zch, *, overflow_rows):
    n_tiles = hext.shape[0] // TOK_TILE
    assert n_tiles > N_EXPERTS
    grid_spec = pltpu.PrefetchScalarGridSpec(
        num_scalar_prefetch=5,
        grid=(n_tiles,),
        in_specs=[pl.BlockSpec((TOK_TILE, EXT), lambda i, *_: (i, 0)),
                  pl.BlockSpec((TOK_TILE, LANES), lambda i, *_: (i, 0)),
                  pl.BlockSpec((1, SUBLANES, LANES), lambda i, *_: (i, 0, 0))],
        out_specs=(pl.BlockSpec(memory_space=pl.ANY), pl.BlockSpec(memory_space=pl.ANY)),
        scratch_shapes=[pltpu.VMEM((2, N_EXPERTS, FIX, EXT), bf16),
                        pltpu.VMEM((2, OVF_SLOTS, EXT), f32),
                        pltpu.VMEM((OVF_ALIGN, EXT), f32),
                        pltpu.SemaphoreType.DMA((2,)),
                        pltpu.SemaphoreType.DMA((2,)),
                        pltpu.SemaphoreType.DMA((1,))])
    return pl.pallas_call(
        _dispatch_kernel,
        grid_spec=grid_spec,
        out_shape=(jax.ShapeDtypeStruct((N_EXPERTS, n_tiles * FIX, EXT), bf16),
                   jax.ShapeDtypeStruct((overflow_rows, EXT), f32)),
        compiler_params=pltpu.CompilerParams(
            dimension_semantics=("arbitrary",), vmem_limit_bytes=VMEM_LIMIT_BYTES),
        name="moe_dispatch",
    )(totc, ngrp, dstk, zst, zch, hext, rankm, locs)


def _experts_kernel(ts_ref, xf_ref, xo_ref, wgu_ref, wd_ref, yf_ref, yo_ref,
                    xfbuf, wgu_f, wd_f, xbuf, ybuf, wgu_b, wd_b, fsem, xsem, ysem):
    e = pl.program_id(0)
    n_valid = ts_ref[N_EXPERTS]

    def ring_copies(expert, slot):
        return (pltpu.make_async_copy(xf_ref.at[expert], xfbuf.at[slot], fsem.at[0, slot]),
                pltpu.make_async_copy(wgu_ref.at[expert], wgu_f.at[slot], fsem.at[1, slot]),
                pltpu.make_async_copy(wd_ref.at[expert], wd_f.at[slot], fsem.at[2, slot]))

    @pl.when(e == 0)
    def _():
        for a in range(FIXED_BUFFERS - 1):
            for copy in ring_copies(a, a):
                copy.start()

    @pl.when(e + FIXED_BUFFERS - 1 < pl.num_programs(0))
    def _():
        for copy in ring_copies(e + FIXED_BUFFERS - 1, (e + FIXED_BUFFERS - 1) % FIXED_BUFFERS):
            copy.start()

    fslot = e % FIXED_BUFFERS
    for copy in ring_copies(e, fslot):
        copy.wait()
    wgu_b[...] = wgu_f[fslot].astype(bf16)
    wd_b[...] = wd_f[fslot].astype(bf16)

    def ffn(x_ext):
        m = x_ext.shape[0]
        gu = jnp.dot(x_ext[:, :D_MODEL], wgu_b[...], preferred_element_type=f32)
        act = (jax.nn.silu(gu[:, :D_EXPERT]) * gu[:, D_EXPERT:]).astype(bf16)
        y = jnp.dot(act, wd_b[...], preferred_element_type=f32)
        g_pack = x_ext[:, D_MODEL:].astype(f32)
        lane = jax.lax.broadcasted_iota(jnp.int32, (m, LANES), 1)
        mine = jnp.where(lane == e, g_pack, jnp.where(lane == e + N_EXPERTS, g_pack, 0.0))
        return (y * jnp.sum(mine, axis=1, keepdims=True)).astype(bf16)

    def x_copy(j, slot):
        rows = pl.ds(pl.multiple_of(j * ROW_TILE, ROW_TILE), ROW_TILE)
        return pltpu.make_async_copy(xo_ref.at[rows], xbuf.at[slot], xsem.at[slot])

    def y_copy(j, slot):
        rows = pl.ds(pl.multiple_of(j * ROW_TILE, ROW_TILE), ROW_TILE)
        return pltpu.make_async_copy(ybuf.at[slot], yo_ref.at[rows], ysem.at[slot])

    ahead = X_BUFFERS - 1

    @pl.when(e == 0)
    def _():
        for a in range(ahead):
            @pl.when(a < n_valid)
            def _():
                x_copy(a, a).start()

    step_rows = xfbuf.shape[1] // FIX_STEPS
    for c in range(FIX_STEPS):
        rows = slice(c * step_rows, (c + 1) * step_rows)
        yf_ref[0, rows, :] = ffn(xfbuf[fslot, rows, :])

    def tile(j, carry):
        xslot = j % X_BUFFERS
        slot = j % 2
        x_copy(j, xslot).wait()

        @pl.when(j + ahead < n_valid)
        def _():
            x_copy(j + ahead, (j + ahead) % X_BUFFERS).start()

        @pl.when(j >= 2)
        def _():
            y_copy(j - 2, slot).wait()

        ybuf[slot] = ffn(xbuf[xslot].astype(bf16)).astype(f32)
        y_copy(j, slot).start()
        return carry
    jax.lax.fori_loop(ts_ref[e], ts_ref[e + 1], tile, 0)

    @pl.when(e == pl.num_programs(0) - 1)
    def _():
        @pl.when(n_valid >= 2)
        def _():
            y_copy(n_valid - 2, n_valid % 2).wait()

        @pl.when(n_valid >= 1)
        def _():
            y_copy(n_valid - 1, (n_valid - 1) % 2).wait()


def _experts(xf, xo, wgu, wd, ts, *, overflow_rows):
    fixed_rows = xf.shape[1]
    assert fixed_rows % (FIX_STEPS * RUN_ALIGN) == 0
    grid_spec = pltpu.PrefetchScalarGridSpec(
        num_scalar_prefetch=1,
        grid=(N_EXPERTS,),
        in_specs=[pl.BlockSpec(memory_space=pl.ANY),
                  pl.BlockSpec(memory_space=pl.ANY),
                  pl.BlockSpec(memory_space=pl.ANY),
                  pl.BlockSpec(memory_space=pl.ANY)],
        out_specs=(pl.BlockSpec((1, fixed_rows, D_MODEL), lambda e, ts: (e, 0, 0)),
                   pl.BlockSpec(memory_space=pl.ANY)),
        scratch_shapes=[pltpu.VMEM((FIXED_BUFFERS, fixed_rows, EXT), bf16),
                        pltpu.VMEM((FIXED_BUFFERS, D_MODEL, 2 * D_EXPERT), f32),
                        pltpu.VMEM((FIXED_BUFFERS, D_EXPERT, D_MODEL), f32),
                        pltpu.VMEM((X_BUFFERS, ROW_TILE, EXT), f32),
                        pltpu.VMEM((2, ROW_TILE, D_MODEL), f32),
                        pltpu.VMEM((D_MODEL, 2 * D_EXPERT), bf16),
                        pltpu.VMEM((D_EXPERT, D_MODEL), bf16),
                        pltpu.SemaphoreType.DMA((3, FIXED_BUFFERS)),
                        pltpu.SemaphoreType.DMA((X_BUFFERS,)),
                        pltpu.SemaphoreType.DMA((2,))])
    return pl.pallas_call(
        _experts_kernel,
        grid_spec=grid_spec,
        out_shape=(jax.ShapeDtypeStruct((N_EXPERTS, fixed_rows, D_MODEL), bf16),
                   jax.ShapeDtypeStruct((overflow_rows, D_MODEL), f32)),
        compiler_params=pltpu.CompilerParams(
            dimension_semantics=("arbitrary",), vmem_limit_bytes=VMEM_LIMIT_BYTES),
        name="moe_experts",
    )(ts, xf, xo, wgu, wd)


def _combine_kernel(totc_ref, ngrp_ref, srck_ref,
                    h_ref, rankm_ref, locc_ref, yf_ref, yo_ref, wsu_ref, wsd_ref, ln_g_ref, ln_b_ref,
                    out_bm_ref, out_tm_ref, yloc_f, yloc_o, acc_scr, p_scr, semf, semo,
                    *, n_bm_tiles):
    i = pl.program_id(0)
    n = pl.num_programs(0)
    slot = i % 2

    def fixed_copy(tile, slot_):
        src = yf_ref.at[:, pl.ds(pl.multiple_of(tile * FIX, FIX), FIX), :]
        return _run_copy(src, yloc_f.at[slot_], semf.at[slot_])

    def overflow_wait(rows, slot_):
        return _run_copy(yo_ref.at[pl.ds(0, rows)], yloc_o.at[slot_, pl.ds(0, rows)], semo.at[slot_])

    def fetch(tile, slot_):
        fixed_copy(tile, slot_).start()

        def start_group(g, carry):
            for u in range(COPY_GROUP):
                k = g * COPY_GROUP + u
                src = yo_ref.at[pl.ds(pl.multiple_of(srck_ref[tile, k], OVF_ALIGN), OVF_ALIGN)]
                dst = yloc_o.at[slot_, pl.ds(pl.multiple_of(k * OVF_ALIGN, OVF_ALIGN), OVF_ALIGN)]
                _run_copy(src, dst, semo.at[slot_]).start()
            return carry
        jax.lax.fori_loop(0, ngrp_ref[tile], start_group, 0)

    @pl.when(i == 0)
    def _():
        yloc_o[...] = jnp.zeros(yloc_o.shape, f32)
        fetch(0, 0)

    @pl.when(i + 1 < n)
    def _():
        fetch(i + 1, 1 - slot)

    h = h_ref[...]
    hb = h.astype(bf16)
    su = jnp.dot(hb, wsu_ref[...], preferred_element_type=f32)
    act = (jax.nn.silu(su[:, :D_SHARED]) * su[:, D_SHARED:]).astype(bf16)
    acc_scr[...] = jnp.dot(act, wsd_ref[...], preferred_element_type=f32)

    rankm = rankm_ref[...]
    fx_loc, fx_end = _fixed_ranges((LANES, 1), 0)
    for ch in range(FIX_SLOTS // SLOT_CHUNK):
        p_scr[:, ch * SLOT_CHUNK:(ch + 1) * SLOT_CHUNK] = _slot_cols(
            ch * SLOT_CHUNK, fx_loc, fx_end, 0.0, rankm)

    fixed_copy(i, slot).wait()
    _wait_groups(ngrp_ref[i], lambda rows: overflow_wait(rows, slot))

    acc_scr[...] += jnp.dot(p_scr[...], yloc_f[slot].reshape(FIX_SLOTS, D_MODEL),
                            preferred_element_type=f32)
    ov_loc = locc_ref[0, :, 0:1]
    ov_end = locc_ref[0, :, 1:2]
    for ch in range(OVF_SLOTS // OVF_CHUNK_SLOTS):
        def gather_overflow(ch=ch):
            p = _slot_cols(FIX_SLOTS + ch * OVF_CHUNK_SLOTS, ov_loc, ov_end, float(FIX), rankm,
                           OVF_CHUNK_SLOTS)
            y_o = yloc_o[slot, ch * OVF_CHUNK_SLOTS:(ch + 1) * OVF_CHUNK_SLOTS, :].astype(bf16)
            acc_scr[...] += jnp.dot(p, y_o, preferred_element_type=f32)
        pl.when(totc_ref[i] * OVF_ALIGN > ch * OVF_CHUNK_SLOTS)(gather_overflow)
    y = _layer_norm(DN_ALPHA * h + acc_scr[...], ln_g_ref[...], ln_b_ref[...])

    @pl.when(i < n_bm_tiles)
    def _():
        nb, tc, _ = out_bm_ref.shape
        out_bm_ref[...] = jnp.transpose(y.reshape(tc, nb, D_MODEL), (1, 0, 2))

    @pl.when(i >= n_bm_tiles)
    def _():
        out_tm_ref[...] = y


def _combine(h, rankm, locc, yf, yo, wsu, wsd, ln_g, ln_b, totc, ngrp, srck, *, bm_shape):
    n_tiles = h.shape[0] // TOK_TILE
    nb, length, _ = bm_shape
    tc = TOK_TILE // nb
    n_bm_tiles = length // tc
    n_tm_tiles = n_tiles - n_bm_tiles
    const = lambda shape: pl.BlockSpec(shape, lambda i, *_: (0,) * len(shape))
    grid_spec = pltpu.PrefetchScalarGridSpec(
        num_scalar_prefetch=3,
        grid=(n_tiles,),
        in_specs=[pl.BlockSpec((TOK_TILE, D_MODEL), lambda i, *_: (i, 0)),
                  pl.BlockSpec((TOK_TILE, LANES), lambda i, *_: (i, 0)),
                  pl.BlockSpec((1, LANES, 2), lambda i, *_: (i, 0, 0)),
                  pl.BlockSpec(memory_space=pl.ANY),
                  pl.BlockSpec(memory_space=pl.ANY),
                  const(wsu.shape), const(wsd.shape), const(ln_g.shape), const(ln_b.shape)],
        out_specs=(pl.BlockSpec((nb, tc, D_MODEL),
                                lambda i, *_: (0, jnp.minimum(i, n_bm_tiles - 1), 0)),
                   pl.BlockSpec((TOK_TILE, D_MODEL),
                                lambda i, *_: (jnp.maximum(i - n_bm_tiles, 0), 0))),
        scratch_shapes=[pltpu.VMEM((2, N_EXPERTS, FIX, D_MODEL), bf16),
                        pltpu.VMEM((2, OVF_SLOTS, D_MODEL), f32),
                        pltpu.VMEM((TOK_TILE, D_MODEL), f32),
                        pltpu.VMEM((TOK_TILE, FIX_SLOTS), bf16),
                        pltpu.SemaphoreType.DMA((2,)),
                        pltpu.SemaphoreType.DMA((2,))])
    return pl.pallas_call(
        functools.partial(_combine_kernel, n_bm_tiles=n_bm_tiles),
        grid_spec=grid_spec,
        out_shape=(jax.ShapeDtypeStruct(bm_shape, f32),
                   jax.ShapeDtypeStruct((n_tm_tiles * TOK_TILE, D_MODEL), f32)),
        compiler_params=pltpu.CompilerParams(
            dimension_semantics=("arbitrary",), vmem_limit_bytes=VMEM_LIMIT_BYTES),
        name="moe_combine",
    )(totc, ngrp, srck, h, rankm, locc, yf, yo, wsu, wsd, ln_g, ln_b)


def _round_up(x, m):
    return (x + m - 1) // m * m


def _moe(h, wr, rb, wgu, wd, wsu, wsd, ln_g, ln_b, *, bm_shape):
    n_tok = h.shape[0]
    n_tiles = n_tok // TOK_TILE
    max_rows = _round_up(n_tiles * (OVF_CHUNKS - 1) * OVF_ALIGN + N_EXPERTS * (ROW_TILE - 1), ROW_TILE)

    hext, rankm, cnt = _router(h, wr, rb)

    i32 = jnp.int32
    cnt = cnt[:, 0, :N_EXPERTS].astype(i32)
    oc = (jnp.maximum(cnt - FIX, 0) + OVF_ALIGN - 1) // OVF_ALIGN
    over_tiles = jnp.cumsum(oc, axis=0)
    region_rows = over_tiles[-1] * OVF_ALIGN
    region_size = _round_up(region_rows, ROW_TILE)
    region_start = jnp.cumsum(region_size) - region_size
    run_dst = region_start[None, :] + OVF_ALIGN * (over_tiles - oc)
    ch_end = jnp.cumsum(oc, axis=1)
    ch_beg = ch_end - oc
    totc = ch_end[:, -1]
    k = jnp.arange(OVF_CHUNKS, dtype=i32)
    kk = k[None, :, None]
    mine = jnp.logical_and(ch_beg[:, None, :] <= kk, kk < ch_end[:, None, :])
    chunk_dst = jnp.sum(jnp.where(mine, run_dst[:, None, :] + OVF_ALIGN * (kk - ch_beg[:, None, :]), 0),
                        axis=-1)
    live = k[None, :] < totc[:, None]
    tile_ids = jnp.arange(n_tiles, dtype=i32)[:, None]
    spare = max_rows + (tile_ids % 2) * OVF_SLOTS + k[None, :] * OVF_ALIGN
    dstk = jnp.where(live, chunk_dst, spare).astype(i32)
    srck = jnp.where(live, chunk_dst, 0).astype(i32)
    ngrp = ((totc + COPY_GROUP - 1) // COPY_GROUP).astype(i32)
    zst = (region_start + region_rows).astype(i32)
    zch = ((region_size - region_rows) // OVF_ALIGN).astype(i32)
    tile_start = jnp.concatenate([region_start, region_start[-1:] + region_size[-1:]]) // ROW_TILE
    tile_start = tile_start.astype(i32)

    pad = ((0, 0), (0, LANES - N_EXPERTS))
    loc_f = jnp.pad((FIX_SLOTS + OVF_ALIGN * ch_beg).astype(f32), pad, constant_values=NO_RUN)
    end_f = jnp.pad((FIX_SLOTS + OVF_ALIGN * ch_end).astype(f32), pad, constant_values=NO_RUN)
    locs = jnp.concatenate([loc_f[:, None, :], end_f[:, None, :],
                            jnp.zeros((n_tiles, SUBLANES - 2, LANES), f32)], axis=1)
    locc = jnp.stack([loc_f, end_f], axis=-1)

    totc = totc.astype(i32)
    overflow_rows = max_rows + 2 * OVF_SLOTS
    xf, xo = _dispatch(hext, rankm, locs, totc, ngrp, dstk, zst, zch, overflow_rows=overflow_rows)
    yf, yo = _experts(xf, xo, wgu, wd, tile_start, overflow_rows=max_rows)
    return _combine(h, rankm, locc, yf, yo, wsu, wsd, ln_g, ln_b, totc, ngrp, srck, bm_shape=bm_shape)


def _diag_tiles(blocks, n_tiles):
    n_blocks, r, c = blocks.shape
    per = n_blocks // n_tiles
    wide = jnp.tile(blocks.reshape(n_blocks * r, c), (1, per))
    row_block = (jnp.arange(n_blocks * r) // r) % per
    col_block = jnp.arange(per * c) // c
    wide = jnp.where(row_block[:, None] == col_block[None, :], wide, 0)
    return wide.reshape(n_tiles, per * r, per * c)


def _head_block_diag(w):
    return _diag_tiles(w, D_RG // MXU_DIM)


def _s5_in_tiles(b):
    return _diag_tiles(b.transpose(0, 2, 1), 2)


def _s5_out_tiles(cw):
    return _diag_tiles(cw.transpose(0, 2, 1), 2)


def _row(v):
    return v.reshape(1, -1)


def kernel(x_prompt, x_sample, state_rg_conv, state_rg_h, state_s5_re, state_s5_im, w_in, conv_w, conv_b, rg_w_a, rg_b_a, rg_w_i, rg_b_i, rg_lam, s5_a_re, s5_a_im, s5_log_dt, s5_b_re, s5_b_im, s5_c_re, s5_c_im, s5_d, w_glu, b_glu, w_out, ln1_g, ln1_b, w_router, router_bias, w_gate_up, w_down, w_shared_up, w_shared_down, ln2_g, ln2_b):
    l = 0
    bp, lp, _ = x_prompt.shape
    bs, ls, _ = x_sample.shape

    are, aim, bbre, bbim = _s5_prep(
        _row(s5_a_re[l]), _row(s5_a_im[l]),
        _row(jnp.repeat(s5_log_dt[l], S5_STATE)),
        _s5_in_tiles(s5_b_re[l]), _s5_in_tiles(s5_b_im[l]))
    params = (w_in[l].astype(bf16), conv_w[l], _row(conv_b[l]),
              _head_block_diag(rg_w_a[l]).astype(bf16), _row(rg_b_a[l]),
              _head_block_diag(rg_w_i[l]).astype(bf16), _row(rg_b_i[l]), _row(rg_lam[l]),
              are, aim, bbre, bbim,
              _s5_out_tiles(s5_c_re[l]).astype(bf16), _s5_out_tiles(s5_c_im[l]).astype(bf16),
              _row(s5_d[l]), w_glu[l].astype(bf16), _row(b_glu[l]), w_out[l].astype(bf16),
              _row(ln1_g[l]), _row(ln1_b[l]))

    tail = CONV_WIDTH - 1
    n_tok = lp * bp + ls * bs
    xs_tm = x_sample.transpose(1, 0, 2).reshape(ls * bs, D_MODEL)
    h_all, sc, sh, sre, sim = _mixer(
        xs_tm, state_rg_conv[l].transpose(1, 0, 2).reshape(tail * bs, D_RG), state_rg_h[l],
        state_s5_re[l].reshape(bs, S5_N), state_s5_im[l].reshape(bs, S5_N), params,
        nb=bs, tc=ls, name="mixer_sample",
        h_all=jnp.zeros((SUBLANES, LANES), f32), total_rows=n_tok, first_row=lp * bp)
    h_all, pc, ph, pre, pim = _mixer(
        x_prompt, jnp.zeros((tail * bp, D_RG), f32), jnp.zeros((bp, D_RG), f32),
        jnp.zeros((bp, S5_N), f32), jnp.zeros((bp, S5_N), f32), params,
        nb=bp, tc=PROMPT_CHUNK_ROWS // bp, name="mixer_prompt",
        h_all=h_all, total_rows=n_tok, first_row=0)
    wr = jnp.pad(w_router[l], ((0, 0), (0, LANES - N_EXPERTS))).astype(bf16)
    rb = jnp.pad(_row(router_bias[l]), ((0, 0), (0, LANES - N_EXPERTS)))
    yp, ys_tm = _moe(h_all, wr, rb, w_gate_up[l], w_down[l],
                     w_shared_up[l].astype(bf16), w_shared_down[l].astype(bf16),
                     _row(ln2_g[l]), _row(ln2_b[l]), bm_shape=x_prompt.shape)
    ys = ys_tm.reshape(ls, bs, D_MODEL).transpose(1, 0, 2)

    def conv_out(cv, nbatch):
        return cv.reshape(tail, nbatch, D_RG).transpose(1, 0, 2)[None]

    return (yp, ys,
            conv_out(pc, bp), ph[None],
            pre.reshape(1, bp, S5_GROUPS, S5_STATE), pim.reshape(1, bp, S5_GROUPS, S5_STATE),
            conv_out(sc, bs), sh[None],
            sre.reshape(1, bs, S5_GROUPS, S5_STATE), sim.reshape(1, bs, S5_GROUPS, S5_STATE))
```

```python
import functools
import math

import jax
import jax.numpy as jnp
from jax.experimental import pallas as pl
from jax.experimental.pallas import tpu as pltpu

D_MODEL = 1024
D_RG = 512
RG_HEADS = 8
RG_HEAD_DIM = 64
CONV_WIDTH = 4
RG_C = 8.0
D_S5 = 512
S5_GROUP = 16
S5_GROUPS = 32
S5_STATE = 64
S5_N = S5_GROUPS * S5_STATE
N_EXPERTS = 64
TOP_K = 8
D_EXPERT = 256
D_SHARED = 256
ROUTED_SCALE = 2.5
DEPTH = 1
DN_ALPHA = (2.0 * DEPTH) ** 0.25
LN_EPS = 1e-5

SUBLANES = 8
LANES = 128
MXU_DIM = 256
S5_SCAN_COLS = 512
PROMPT_CHUNK_ROWS = 512
VMEM_LIMIT_BYTES = 56 * 1024 * 1024

bf16 = jnp.bfloat16
f32 = jnp.float32


def _gelu_tanh(x):
    c = math.sqrt(2.0 / math.pi)
    return x * (0.5 * (1.0 + jnp.tanh(c * (x + 0.044715 * (x * x * x)))))


def _layer_norm(x, g, b):
    mu = jnp.mean(x, axis=-1, keepdims=True)
    xc = x - mu
    var = jnp.mean(xc * xc, axis=-1, keepdims=True)
    return xc * jax.lax.rsqrt(var + LN_EPS) * g + b


def _s5_prep_kernel(lr_ref, li_ref, ldt_ref, bre_ref, bim_ref,
                    are_ref, aim_ref, bbre_ref, bbim_ref):
    lr = lr_ref[...]
    li = li_ref[...]
    dt = jnp.exp(ldt_ref[...])
    mag = jnp.exp(lr * dt)
    abar_re = mag * jnp.cos(li * dt)
    abar_im = mag * jnp.sin(li * dt)
    den = lr * lr + li * li
    nr = abar_re - 1.0
    ni = abar_im
    coef_re = (nr * lr + ni * li) / den
    coef_im = (ni * lr - nr * li) / den
    are_ref[...] = abar_re
    aim_ref[...] = abar_im
    half = S5_N // 2
    for k in range(2):
        cre = coef_re[:, k * half:(k + 1) * half]
        cim = coef_im[:, k * half:(k + 1) * half]
        br = bre_ref[k]
        bi = bim_ref[k]
        bbre_ref[k] = (cre * br - cim * bi).astype(bf16)
        bbim_ref[k] = (cre * bi + cim * br).astype(bf16)


def _s5_prep(lr, li, ldt, bre_t, bim_t):
    half = S5_N // 2
    return pl.pallas_call(
        _s5_prep_kernel,
        out_shape=(jax.ShapeDtypeStruct((1, S5_N), f32),
                   jax.ShapeDtypeStruct((1, S5_N), f32),
                   jax.ShapeDtypeStruct((2, MXU_DIM, half), bf16),
                   jax.ShapeDtypeStruct((2, MXU_DIM, half), bf16)),
        name="s5_prep",
    )(lr, li, ldt, bre_t, bim_t)


def _mixer_kernel(h_all_ref, x_ref, conv0_ref, h0_ref, s0r_ref, s0i_ref,
                  w_in_ref, conv_w_ref, conv_b_ref, wa_ref, ba_ref, wi_ref, bi_ref, lam_ref,
                  are_ref, aim_ref, bbre_ref, bbim_ref, cre_ref, cim_ref, d_ref,
                  wglu_ref, bglu_ref, wout_ref, ln_g_ref, ln_b_ref,
                  hout_ref, conv_out_ref, hlast_ref, sre_out_ref, sim_out_ref,
                  pad_scr, a_scr, b_scr, bur_scr, bui_scr, hst_scr, sr_scr, si_scr,
                  *, nb, tc, batch_major_input):
    del h_all_ref
    rows = nb * tc
    tail = (CONV_WIDTH - 1) * nb
    c = pl.program_id(0)

    @pl.when(c == 0)
    def _():
        pad_scr[0:tail, :] = conv0_ref[...]
        hst_scr[...] = h0_ref[...]
        sr_scr[...] = s0r_ref[...]
        si_scr[...] = s0i_ref[...]

    if batch_major_input:
        x = jnp.transpose(x_ref[...], (1, 0, 2)).reshape(rows, D_MODEL)
    else:
        x = x_ref[...]
    proj = jnp.dot(x.astype(bf16), w_in_ref[...], preferred_element_type=f32)
    x_rg = proj[:, :D_RG]
    g_rg = proj[:, D_RG:2 * D_RG]
    u = proj[:, 2 * D_RG:]

    pad_scr[tail:tail + rows, :] = x_rg
    conv_w = conv_w_ref[...]
    acc = conv_w[0:1, :] * pad_scr[0:rows, :]
    for k in range(1, CONV_WIDTH):
        acc = acc + conv_w[k:k + 1, :] * pad_scr[k * nb:k * nb + rows, :]
    xc = conv_b_ref[...] + acc
    new_tail = pad_scr[rows:rows + tail, :]
    pad_scr[0:tail, :] = new_tail

    xcb = xc.astype(bf16)
    ga = []
    gi = []
    for hh in range(D_RG // MXU_DIM):
        xs = xcb[:, hh * MXU_DIM:(hh + 1) * MXU_DIM]
        ga.append(jnp.dot(xs, wa_ref[hh], preferred_element_type=f32))
        gi.append(jnp.dot(xs, wi_ref[hh], preferred_element_type=f32))
    r = jax.nn.sigmoid(jnp.concatenate(ga, axis=1) + ba_ref[...])
    i = jax.nn.sigmoid(jnp.concatenate(gi, axis=1) + bi_ref[...])
    nlam = -lam_ref[...]
    softplus = jnp.maximum(nlam, 0.0) + jnp.log1p(jnp.exp(-jnp.abs(nlam)))
    log_a = (-RG_C) * r * softplus
    a_scr[...] = jnp.exp(log_a)
    th = jnp.tanh(log_a)
    b_scr[...] = jnp.sqrt((-2.0 * th) / (1.0 - th)) * (i * xc)

    for rg in range(nb // SUBLANES):
        r0 = rg * SUBLANES
        h = hst_scr[r0:r0 + SUBLANES, :]
        for t in range(tc):
            q = t * nb + r0
            h = a_scr[q:q + SUBLANES, :] * h + b_scr[q:q + SUBLANES, :]
            b_scr[q:q + SUBLANES, :] = h
        hst_scr[r0:r0 + SUBLANES, :] = h
    y_rg = b_scr[...] * _gelu_tanh(g_rg)

    ub = u.astype(bf16)
    half = S5_N // 2
    for k in range(2):
        us = ub[:, k * MXU_DIM:(k + 1) * MXU_DIM]
        bur_scr[:, k * half:(k + 1) * half] = jnp.dot(us, bbre_ref[k], preferred_element_type=f32)
        bui_scr[:, k * half:(k + 1) * half] = jnp.dot(us, bbim_ref[k], preferred_element_type=f32)
    for rg in range(nb // SUBLANES):
        r0 = rg * SUBLANES
        for cb in range(S5_N // S5_SCAN_COLS):
            c0 = cb * S5_SCAN_COLS
            ar = jnp.broadcast_to(are_ref[:, c0:c0 + S5_SCAN_COLS], (SUBLANES, S5_SCAN_COLS))
            ai = jnp.broadcast_to(aim_ref[:, c0:c0 + S5_SCAN_COLS], (SUBLANES, S5_SCAN_COLS))
            xr = sr_scr[r0:r0 + SUBLANES, c0:c0 + S5_SCAN_COLS]
            xi = si_scr[r0:r0 + SUBLANES, c0:c0 + S5_SCAN_COLS]
            for t in range(tc):
                q = t * nb + r0
                br = bur_scr[q:q + SUBLANES, c0:c0 + S5_SCAN_COLS]
                bi_ = bui_scr[q:q + SUBLANES, c0:c0 + S5_SCAN_COLS]
                nxr = ar * xr - ai * xi + br
                nxi = ar * xi + ai * xr + bi_
                bur_scr[q:q + SUBLANES, c0:c0 + S5_SCAN_COLS] = nxr
                bui_scr[q:q + SUBLANES, c0:c0 + S5_SCAN_COLS] = nxi
                xr, xi = nxr, nxi
            sr_scr[r0:r0 + SUBLANES, c0:c0 + S5_SCAN_COLS] = xr
            si_scr[r0:r0 + SUBLANES, c0:c0 + S5_SCAN_COLS] = xi
    ys = []
    for j in range(D_S5 // MXU_DIM):
        xrb = bur_scr[:, j * half:(j + 1) * half].astype(bf16)
        xib = bui_scr[:, j * half:(j + 1) * half].astype(bf16)
        ys.append(jnp.dot(xrb, cre_ref[j], preferred_element_type=f32)
                  - jnp.dot(xib, cim_ref[j], preferred_element_type=f32))
    y_s5 = jnp.concatenate(ys, axis=1) + d_ref[...] * u
    yg = _gelu_tanh(y_s5)
    glu = jnp.dot(yg.astype(bf16), wglu_ref[...], preferred_element_type=f32) + bglu_ref[...]
    y_s5 = yg * jax.nn.sigmoid(glu)

    ycat = jnp.concatenate([y_rg, y_s5], axis=1).astype(bf16)
    mix = jnp.dot(ycat, wout_ref[...], preferred_element_type=f32)
    hout_ref[...] = _layer_norm(DN_ALPHA * x + mix, ln_g_ref[...], ln_b_ref[...])

    @pl.when(c == pl.num_programs(0) - 1)
    def _():
        conv_out_ref[...] = pad_scr[0:tail, :]
        hlast_ref[...] = hst_scr[...]
        sre_out_ref[...] = sr_scr[...]
        sim_out_ref[...] = si_scr[...]


def _full(shape):
    n = len(shape)
    return pl.BlockSpec(shape, lambda c: (0,) * n)


def _mixer(x, conv0, h0, s0r, s0i, params, *, nb, tc, name, h_all, total_rows, first_row):
    batch_major_input = x.ndim == 3
    rows = nb * tc
    if batch_major_input:
        n_chunks = x.shape[1] // tc
        x_spec = pl.BlockSpec((nb, tc, D_MODEL), lambda c: (0, c, 0))
    else:
        n_chunks = x.shape[0] // rows
        x_spec = pl.BlockSpec((rows, D_MODEL), lambda c: (c, 0))
    first_block = first_row // rows
    aliased = h_all.shape == (total_rows, D_MODEL)
    tail = (CONV_WIDTH - 1) * nb
    small = (conv0, h0, s0r, s0i) + tuple(params)
    in_specs = [pl.BlockSpec(memory_space=pl.ANY), x_spec]
    in_specs += [_full(a.shape) for a in small]
    out_shape = (jax.ShapeDtypeStruct((total_rows, D_MODEL), f32),
                 jax.ShapeDtypeStruct((tail, D_RG), f32),
                 jax.ShapeDtypeStruct((nb, D_RG), f32),
                 jax.ShapeDtypeStruct((nb, S5_N), f32),
                 jax.ShapeDtypeStruct((nb, S5_N), f32))
    out_specs = (pl.BlockSpec((rows, D_MODEL), lambda c: (c + first_block, 0)),
                 _full((tail, D_RG)), _full((nb, D_RG)), _full((nb, S5_N)), _full((nb, S5_N)))
    scratch = [pltpu.VMEM((rows + tail, D_RG), f32),
               pltpu.VMEM((rows, D_RG), f32),
               pltpu.VMEM((rows, D_RG), f32),
               pltpu.VMEM((rows, S5_N), f32),
               pltpu.VMEM((rows, S5_N), f32),
               pltpu.VMEM((nb, D_RG), f32),
               pltpu.VMEM((nb, S5_N), f32),
               pltpu.VMEM((nb, S5_N), f32)]
    return pl.pallas_call(
        functools.partial(_mixer_kernel, nb=nb, tc=tc, batch_major_input=batch_major_input),
        grid=(n_chunks,),
        in_specs=in_specs,
        out_specs=out_specs,
        out_shape=out_shape,
        scratch_shapes=scratch,
        input_output_aliases={0: 0} if aliased else {},
        compiler_params=pltpu.CompilerParams(
            dimension_semantics=("arbitrary",), vmem_limit_bytes=VMEM_LIMIT_BYTES),
        name=name,
    )(h_all, x, *small)


TOK_TILE = 256
RUN_ALIGN = 16
OVF_ALIGN = 8
FIX = 32
FIX_SLOTS = N_EXPERTS * FIX
OVF_CHUNKS = 256
OVF_CHUNK_SLOTS = 384
OVF_SLOTS = 6 * OVF_CHUNK_SLOTS
SLOT_CHUNK = 512
FIX_STEPS = 4
ROW_TILE = 256
X_BUFFERS = 4
FIXED_BUFFERS = 4
COPY_GROUP = 4
EXT = D_MODEL + LANES
ROUTER_TILES = 6
NO_RUN = 1.0e9


def _top_k_gates(scores, rb):
    rows = scores.shape[0]
    lane_f = jax.lax.broadcasted_iota(jnp.int32, (rows, LANES), 1).astype(f32)
    biased = jnp.where(lane_f < float(N_EXPERTS), scores + rb, -jnp.inf)
    sel = jnp.zeros((rows, LANES), f32)
    mask = jnp.zeros((rows, LANES), f32)
    for _ in range(TOP_K):
        m = jnp.max(biased, axis=1, keepdims=True)
        idx = jnp.min(jnp.where(biased == m, lane_f, float(LANES)), axis=1, keepdims=True)
        hit = lane_f == idx
        sel = jnp.where(hit, scores, sel)
        mask = jnp.where(hit, 1.0, mask)
        biased = jnp.where(hit, -jnp.inf, biased)
    gates = sel / jnp.sum(sel, axis=1, keepdims=True) * ROUTED_SCALE
    return mask, gates


def _router_kernel(h_ref, wr_ref, rb_ref, hext_ref, rankm_ref, cnt_ref):
    hb = h_ref[...].astype(bf16)
    scores = jax.nn.sigmoid(jnp.dot(hb, wr_ref[...], preferred_element_type=f32))
    mask, gates = _top_k_gates(scores, rb_ref[...])
    t_row = jax.lax.broadcasted_iota(jnp.int32, (TOK_TILE, TOK_TILE), 0)
    t_col = jax.lax.broadcasted_iota(jnp.int32, (TOK_TILE, TOK_TILE), 1)
    earlier = jnp.where(t_col < t_row, 1.0, 0.0).astype(bf16)
    for sub in range(ROUTER_TILES):
        m = mask[sub * TOK_TILE:(sub + 1) * TOK_TILE]
        rank = jnp.dot(earlier, m.astype(bf16), preferred_element_type=f32)
        rankm_ref[sub * TOK_TILE:(sub + 1) * TOK_TILE, :] = jnp.where(m > 0.0, rank, -1.0).astype(bf16)
        cnt_ref[sub] = jnp.broadcast_to(jnp.sum(m, axis=0, keepdims=True), (SUBLANES, LANES))
    g_hi = gates.astype(bf16).astype(f32)
    g_pack = g_hi + pltpu.roll(gates - g_hi, N_EXPERTS, 1)
    hext_ref[:, :D_MODEL] = hb
    hext_ref[:, D_MODEL:] = g_pack.astype(bf16)


def _router(h, wr, rb):
    n_tiles = h.shape[0] // TOK_TILE
    assert n_tiles % ROUTER_TILES == 0
    rows = ROUTER_TILES * TOK_TILE
    const = lambda shape: pl.BlockSpec(shape, lambda i: (0,) * len(shape))
    return pl.pallas_call(
        _router_kernel,
        grid=(n_tiles // ROUTER_TILES,),
        in_specs=[pl.BlockSpec((rows, D_MODEL), lambda i: (i, 0)), const(wr.shape), const(rb.shape)],
        out_specs=(pl.BlockSpec((rows, EXT), lambda i: (i, 0)),
                   pl.BlockSpec((rows, LANES), lambda i: (i, 0)),
                   pl.BlockSpec((ROUTER_TILES, SUBLANES, LANES), lambda i: (i, 0, 0))),
        out_shape=(jax.ShapeDtypeStruct((h.shape[0], EXT), bf16),
                   jax.ShapeDtypeStruct((h.shape[0], LANES), bf16),
                   jax.ShapeDtypeStruct((n_tiles, SUBLANES, LANES), f32)),
        compiler_params=pltpu.CompilerParams(
            dimension_semantics=("arbitrary",), vmem_limit_bytes=VMEM_LIMIT_BYTES),
        name="moe_router",
    )(h, wr, rb)


def _run_copy(src, dst, sem):
    return pltpu.make_async_copy(src, dst, sem)


def _fixed_ranges(shape, axis):
    e = jax.lax.broadcasted_iota(jnp.int32, shape, axis).astype(f32)
    loc = jnp.where(e < float(N_EXPERTS), e * float(FIX), NO_RUN)
    return loc, loc + float(FIX)


def _slot_rows(first_slot, loc_row, end_row, base, rankm, n_slots=SLOT_CHUNK):
    s = (jax.lax.broadcasted_iota(jnp.int32, (n_slots, LANES), 0) + first_slot).astype(f32)
    in_run = jnp.where(s >= loc_row, jnp.where(s < end_row, 1.0, 0.0), 0.0)
    r_col = s[:, 0:1] - jnp.sum(in_run * (loc_row - base), axis=1, keepdims=True)
    q = jax.lax.dot_general(in_run.astype(bf16), rankm, (((1,), (1,)), ((), ())),
                            preferred_element_type=f32)
    return jnp.where(q == r_col, 1.0, 0.0).astype(bf16)


def _slot_cols(first_slot, loc_col, end_col, base, rankm, n_slots=SLOT_CHUNK):
    s = (jax.lax.broadcasted_iota(jnp.int32, (LANES, n_slots), 1) + first_slot).astype(f32)
    in_run = jnp.where(s >= loc_col, jnp.where(s < end_col, 1.0, 0.0), 0.0)
    r_row = s[0:1, :] - jnp.sum(in_run * (loc_col - base), axis=0, keepdims=True)
    q = jnp.dot(rankm, in_run.astype(bf16), preferred_element_type=f32)
    return jnp.where(q == r_row, 1.0, 0.0).astype(bf16)


def _wait_groups(n_groups, make_copy):
    max_bits = (OVF_CHUNKS // COPY_GROUP).bit_length()
    for b in range(max_bits):
        rows = (COPY_GROUP * OVF_ALIGN) << b

        @pl.when(jnp.bitwise_and(jnp.right_shift(n_groups, b), 1) == 1)
        def _():
            make_copy(rows).wait()


def _dispatch_kernel(totc_ref, ngrp_ref, dstk_ref, zst_ref, zch_ref,
                     hext_ref, rankm_ref, locs_ref, sorted_f_ref, sorted_o_ref,
                     stage_f, stage_o, zbuf, semf, semo, zsem):
    i = pl.program_id(0)
    n = pl.num_programs(0)
    slot = i % 2
    rankm = rankm_ref[...]

    fx_loc, fx_end = _fixed_ranges((1, LANES), 1)
    experts_per_chunk = SLOT_CHUNK // FIX
    for ch in range(FIX_SLOTS // SLOT_CHUNK):
        p = _slot_rows(ch * SLOT_CHUNK, fx_loc, fx_end, 0.0, rankm)
        rows = jnp.dot(p, hext_ref[...], preferred_element_type=f32).astype(bf16)
        stage_f[slot, ch * experts_per_chunk:(ch + 1) * experts_per_chunk] = rows.reshape(
            experts_per_chunk, FIX, EXT)

    ov_loc = locs_ref[0, 0:1, :]
    ov_end = locs_ref[0, 1:2, :]
    for ch in range(OVF_SLOTS // OVF_CHUNK_SLOTS):
        def sort_overflow(ch=ch):
            p = _slot_rows(FIX_SLOTS + ch * OVF_CHUNK_SLOTS, ov_loc, ov_end, float(FIX), rankm,
                           OVF_CHUNK_SLOTS)
            stage_o[slot, ch * OVF_CHUNK_SLOTS:(ch + 1) * OVF_CHUNK_SLOTS, :] = jnp.dot(
                p, hext_ref[...], preferred_element_type=f32)
        pl.when(totc_ref[i] * OVF_ALIGN > ch * OVF_CHUNK_SLOTS)(sort_overflow)

    def fixed_copy(tile, slot_):
        dst = sorted_f_ref.at[:, pl.ds(pl.multiple_of(tile * FIX, FIX), FIX), :]
        return _run_copy(stage_f.at[slot_], dst, semf.at[slot_])

    def overflow_wait(rows, slot_):
        return _run_copy(stage_o.at[slot_, pl.ds(0, rows)], sorted_o_ref.at[pl.ds(0, rows)],
                         semo.at[slot_])

    fixed_copy(i, slot).start()

    def start_group(g, carry):
        for u in range(COPY_GROUP):
            k = g * COPY_GROUP + u
            src = stage_o.at[slot, pl.ds(pl.multiple_of(k * OVF_ALIGN, OVF_ALIGN), OVF_ALIGN)]
            dst = sorted_o_ref.at[pl.ds(pl.multiple_of(dstk_ref[i, k], OVF_ALIGN), OVF_ALIGN)]
            _run_copy(src, dst, semo.at[slot]).start()
        return carry
    jax.lax.fori_loop(0, ngrp_ref[i], start_group, 0)

    @pl.when(i == 0)
    def _():
        zbuf[...] = jnp.zeros(zbuf.shape, f32)

    def zero_copy(e, k):
        dst = sorted_o_ref.at[pl.ds(pl.multiple_of(zst_ref[e] + k * OVF_ALIGN, OVF_ALIGN), OVF_ALIGN)]
        return _run_copy(zbuf, dst, zsem.at[0])

    @pl.when(jnp.logical_and(i > 0, i <= N_EXPERTS))
    def _():
        def wait_zero(k, carry):
            zero_copy(i - 1, k).wait()
            return carry
        jax.lax.fori_loop(0, zch_ref[i - 1], wait_zero, 0)

    @pl.when(i < N_EXPERTS)
    def _():
        def start_zero(k, carry):
            zero_copy(i, k).start()
            return carry
        jax.lax.fori_loop(0, zch_ref[i], start_zero, 0)

    @pl.when(i > 0)
    def _():
        fixed_copy(i - 1, 1 - slot).wait()
        _wait_groups(ngrp_ref[i - 1], lambda rows: overflow_wait(rows, 1 - slot))

    @pl.when(i == n - 1)
    def _():
        fixed_copy(i, slot).wait()
        _wait_groups(ngrp_ref[i], lambda rows: overflow_wait(rows, slot))


def _dispatch(hext, rankm, locs, totc, ngrp, dstk, zst, zch, *, overflow_rows):
    n_tiles = hext.shape[0] // TOK_TILE
    assert n_tiles > N_EXPERTS
    grid_spec = pltpu.PrefetchScalarGridSpec(
        num_scalar_prefetch=5,
        grid=(n_tiles,),
        in_specs=[pl.BlockSpec((TOK_TILE, EXT), lambda i, *_: (i, 0)),
                  pl.BlockSpec((TOK_TILE, LANES), lambda i, *_: (i, 0)),
                  pl.BlockSpec((1, SUBLANES, LANES), lambda i, *_: (i, 0, 0))],
        out_specs=(pl.BlockSpec(memory_space=pl.ANY), pl.BlockSpec(memory_space=pl.ANY)),
        scratch_shapes=[pltpu.VMEM((2, N_EXPERTS, FIX, EXT), bf16),
                        pltpu.VMEM((2, OVF_SLOTS, EXT), f32),
                        pltpu.VMEM((OVF_ALIGN, EXT), f32),
                        pltpu.SemaphoreType.DMA((2,)),
                        pltpu.SemaphoreType.DMA((2,)),
                        pltpu.SemaphoreType.DMA((1,))])
    return pl.pallas_call(
        _dispatch_kernel,
        grid_spec=grid_spec,
        out_shape=(jax.ShapeDtypeStruct((N_EXPERTS, n_tiles * FIX, EXT), bf16),
                   jax.ShapeDtypeStruct((overflow_rows, EXT), f32)),
        compiler_params=pltpu.CompilerParams(
            dimension_semantics=("arbitrary",), vmem_limit_bytes=VMEM_LIMIT_BYTES),
        name="moe_dispatch",
    )(totc, ngrp, dstk, zst, zch, hext, rankm, locs)


def _experts_kernel(ts_ref, xf_ref, xo_ref, wgu_ref, wd_ref, yf_ref, yo_ref,
                    xfbuf, wgu_f, wd_f, xbuf, ybuf, wgu_b, wd_b, fsem, xsem, ysem):
    e = pl.program_id(0)
    n_valid = ts_ref[N_EXPERTS]

    def ring_copies(expert, slot):
        return (pltpu.make_async_copy(xf_ref.at[expert], xfbuf.at[slot], fsem.at[0, slot]),
                pltpu.make_async_copy(wgu_ref.at[expert], wgu_f.at[slot], fsem.at[1, slot]),
                pltpu.make_async_copy(wd_ref.at[expert], wd_f.at[slot], fsem.at[2, slot]))

    @pl.when(e == 0)
    def _():
        for a in range(FIXED_BUFFERS - 1):
            for copy in ring_copies(a, a):
                copy.start()

    @pl.when(e + FIXED_BUFFERS - 1 < pl.num_programs(0))
    def _():
        for copy in ring_copies(e + FIXED_BUFFERS - 1, (e + FIXED_BUFFERS - 1) % FIXED_BUFFERS):
            copy.start()

    fslot = e % FIXED_BUFFERS
    for copy in ring_copies(e, fslot):
        copy.wait()
    wgu_b[...] = wgu_f[fslot].astype(bf16)
    wd_b[...] = wd_f[fslot].astype(bf16)

    def ffn(x_ext):
        m = x_ext.shape[0]
        gu = jnp.dot(x_ext[:, :D_MODEL], wgu_b[...], preferred_element_type=f32)
        act = (jax.nn.silu(gu[:, :D_EXPERT]) * gu[:, D_EXPERT:]).astype(bf16)
        y = jnp.dot(act, wd_b[...], preferred_element_type=f32)
        g_pack = x_ext[:, D_MODEL:].astype(f32)
        lane = jax.lax.broadcasted_iota(jnp.int32, (m, LANES), 1)
        mine = jnp.where(lane == e, g_pack, jnp.where(lane == e + N_EXPERTS, g_pack, 0.0))
        return (y * jnp.sum(mine, axis=1, keepdims=True)).astype(bf16)

    def x_copy(j, slot):
        rows = pl.ds(pl.multiple_of(j * ROW_TILE, ROW_TILE), ROW_TILE)
        return pltpu.make_async_copy(xo_ref.at[rows], xbuf.at[slot], xsem.at[slot])

    def y_copy(j, slot):
        rows = pl.ds(pl.multiple_of(j * ROW_TILE, ROW_TILE), ROW_TILE)
        return pltpu.make_async_copy(ybuf.at[slot], yo_ref.at[rows], ysem.at[slot])

    ahead = X_BUFFERS - 1

    @pl.when(e == 0)
    def _():
        for a in range(ahead):
            @pl.when(a < n_valid)
            def _():
                x_copy(a, a).start()

    step_rows = xfbuf.shape[1] // FIX_STEPS
    for c in range(FIX_STEPS):
        rows = slice(c * step_rows, (c + 1) * step_rows)
        yf_ref[0, rows, :] = ffn(xfbuf[fslot, rows, :])

    def tile(j, carry):
        xslot = j % X_BUFFERS
        slot = j % 2
        x_copy(j, xslot).wait()

        @pl.when(j + ahead < n_valid)
        def _():
            x_copy(j + ahead, (j + ahead) % X_BUFFERS).start()

        @pl.when(j >= 2)
        def _():
            y_copy(j - 2, slot).wait()

        ybuf[slot] = ffn(xbuf[xslot].astype(bf16)).astype(f32)
        y_copy(j, slot).start()
        return carry
    jax.lax.fori_loop(ts_ref[e], ts_ref[e + 1], tile, 0)

    @pl.when(e == pl.num_programs(0) - 1)
    def _():
        @pl.when(n_valid >= 2)
        def _():
            y_copy(n_valid - 2, n_valid % 2).wait()

        @pl.when(n_valid >= 1)
        def _():
            y_copy(n_valid - 1, (n_valid - 1) % 2).wait()


def _experts(xf, xo, wgu, wd, ts, *, overflow_rows):
    fixed_rows = xf.shape[1]
    assert fixed_rows % (FIX_STEPS * RUN_ALIGN) == 0
    grid_spec = pltpu.PrefetchScalarGridSpec(
        num_scalar_prefetch=1,
        grid=(N_EXPERTS,),
        in_specs=[pl.BlockSpec(memory_space=pl.ANY),
                  pl.BlockSpec(memory_space=pl.ANY),
                  pl.BlockSpec(memory_space=pl.ANY),
                  pl.BlockSpec(memory_space=pl.ANY)],
        out_specs=(pl.BlockSpec((1, fixed_rows, D_MODEL), lambda e, ts: (e, 0, 0)),
                   pl.BlockSpec(memory_space=pl.ANY)),
        scratch_shapes=[pltpu.VMEM((FIXED_BUFFERS, fixed_rows, EXT), bf16),
                        pltpu.VMEM((FIXED_BUFFERS, D_MODEL, 2 * D_EXPERT), f32),
                        pltpu.VMEM((FIXED_BUFFERS, D_EXPERT, D_MODEL), f32),
                        pltpu.VMEM((X_BUFFERS, ROW_TILE, EXT), f32),
                        pltpu.VMEM((2, ROW_TILE, D_MODEL), f32),
                        pltpu.VMEM((D_MODEL, 2 * D_EXPERT), bf16),
                        pltpu.VMEM((D_EXPERT, D_MODEL), bf16),
                        pltpu.SemaphoreType.DMA((3, FIXED_BUFFERS)),
                        pltpu.SemaphoreType.DMA((X_BUFFERS,)),
                        pltpu.SemaphoreType.DMA((2,))])
    return pl.pallas_call(
        _experts_kernel,
        grid_spec=grid_spec,
        out_shape=(jax.ShapeDtypeStruct((N_EXPERTS, fixed_rows, D_MODEL), bf16),
                   jax.ShapeDtypeStruct((overflow_rows, D_MODEL), f32)),
        compiler_params=pltpu.CompilerParams(
            dimension_semantics=("arbitrary",), vmem_limit_bytes=VMEM_LIMIT_BYTES),
        name="moe_experts",
    )(ts, xf, xo, wgu, wd)


def _combine_kernel(totc_ref, ngrp_ref, srck_ref,
                    h_ref, rankm_ref, locc_ref, yf_ref, yo_ref, wsu_ref, wsd_ref, ln_g_ref, ln_b_ref,
                    out_bm_ref, out_tm_ref, yloc_f, yloc_o, acc_scr, p_scr, semf, semo,
                    *, n_bm_tiles):
    i = pl.program_id(0)
    n = pl.num_programs(0)
    slot = i % 2

    def fixed_copy(tile, slot_):
        src = yf_ref.at[:, pl.ds(pl.multiple_of(tile * FIX, FIX), FIX), :]
        return _run_copy(src, yloc_f.at[slot_], semf.at[slot_])

    def overflow_wait(rows, slot_):
        return _run_copy(yo_ref.at[pl.ds(0, rows)], yloc_o.at[slot_, pl.ds(0, rows)], semo.at[slot_])

    def fetch(tile, slot_):
        fixed_copy(tile, slot_).start()

        def start_group(g, carry):
            for u in range(COPY_GROUP):
                k = g * COPY_GROUP + u
                src = yo_ref.at[pl.ds(pl.multiple_of(srck_ref[tile, k], OVF_ALIGN), OVF_ALIGN)]
                dst = yloc_o.at[slot_, pl.ds(pl.multiple_of(k * OVF_ALIGN, OVF_ALIGN), OVF_ALIGN)]
                _run_copy(src, dst, semo.at[slot_]).start()
            return carry
        jax.lax.fori_loop(0, ngrp_ref[tile], start_group, 0)

    @pl.when(i == 0)
    def _():
        yloc_o[...] = jnp.zeros(yloc_o.shape, f32)
        fetch(0, 0)

    @pl.when(i + 1 < n)
    def _():
        fetch(i + 1, 1 - slot)

    h = h_ref[...]
    hb = h.astype(bf16)
    su = jnp.dot(hb, wsu_ref[...], preferred_element_type=f32)
    act = (jax.nn.silu(su[:, :D_SHARED]) * su[:, D_SHARED:]).astype(bf16)
    acc_scr[...] = jnp.dot(act, wsd_ref[...], preferred_element_type=f32)

    rankm = rankm_ref[...]
    fx_loc, fx_end = _fixed_ranges((LANES, 1), 0)
    for ch in range(FIX_SLOTS // SLOT_CHUNK):
        p_scr[:, ch * SLOT_CHUNK:(ch + 1) * SLOT_CHUNK] = _slot_cols(
            ch * SLOT_CHUNK, fx_loc, fx_end, 0.0, rankm)

    fixed_copy(i, slot).wait()
    _wait_groups(ngrp_ref[i], lambda rows: overflow_wait(rows, slot))

    acc_scr[...] += jnp.dot(p_scr[...], yloc_f[slot].reshape(FIX_SLOTS, D_MODEL),
                            preferred_element_type=f32)
    ov_loc = locc_ref[0, :, 0:1]
    ov_end = locc_ref[0, :, 1:2]
    for ch in range(OVF_SLOTS // OVF_CHUNK_SLOTS):
        def gather_overflow(ch=ch):
            p = _slot_cols(FIX_SLOTS + ch * OVF_CHUNK_SLOTS, ov_loc, ov_end, float(FIX), rankm,
                           OVF_CHUNK_SLOTS)
            y_o = yloc_o[slot, ch * OVF_CHUNK_SLOTS:(ch + 1) * OVF_CHUNK_SLOTS, :].astype(bf16)
            acc_scr[...] += jnp.dot(p, y_o, preferred_element_type=f32)
        pl.when(totc_ref[i] * OVF_ALIGN > ch * OVF_CHUNK_SLOTS)(gather_overflow)
    y = _layer_norm(DN_ALPHA * h + acc_scr[...], ln_g_ref[...], ln_b_ref[...])

    @pl.when(i < n_bm_tiles)
    def _():
        nb, tc, _ = out_bm_ref.shape
        out_bm_ref[...] = jnp.transpose(y.reshape(tc, nb, D_MODEL), (1, 0, 2))

    @pl.when(i >= n_bm_tiles)
    def _():
        out_tm_ref[...] = y


def _combine(h, rankm, locc, yf, yo, wsu, wsd, ln_g, ln_b, totc, ngrp, srck, *, bm_shape):
    n_tiles = h.shape[0] // TOK_TILE
    nb, length, _ = bm_shape
    tc = TOK_TILE // nb
    n_bm_tiles = length // tc
    n_tm_tiles = n_tiles - n_bm_tiles
    const = lambda shape: pl.BlockSpec(shape, lambda i, *_: (0,) * len(shape))
    grid_spec = pltpu.PrefetchScalarGridSpec(
        num_scalar_prefetch=3,
        grid=(n_tiles,),
        in_specs=[pl.BlockSpec((TOK_TILE, D_MODEL), lambda i, *_: (i, 0)),
                  pl.BlockSpec((TOK_TILE, LANES), lambda i, *_: (i, 0)),
                  pl.BlockSpec((1, LANES, 2), lambda i, *_: (i, 0, 0)),
                  pl.BlockSpec(memory_space=pl.ANY),
                  pl.BlockSpec(memory_space=pl.ANY),
                  const(wsu.shape), const(wsd.shape), const(ln_g.shape), const(ln_b.shape)],
        out_specs=(pl.BlockSpec((nb, tc, D_MODEL),
                                lambda i, *_: (0, jnp.minimum(i, n_bm_tiles - 1), 0)),
                   pl.BlockSpec((TOK_TILE, D_MODEL),
                                lambda i, *_: (jnp.maximum(i - n_bm_tiles, 0), 0))),
        scratch_shapes=[pltpu.VMEM((2, N_EXPERTS, FIX, D_MODEL), bf16),
                        pltpu.VMEM((2, OVF_SLOTS, D_MODEL), f32),
                        pltpu.VMEM((TOK_TILE, D_MODEL), f32),
                        pltpu.VMEM((TOK_TILE, FIX_SLOTS), bf16),
                        pltpu.SemaphoreType.DMA((2,)),
                        pltpu.SemaphoreType.DMA((2,))])
    return pl.pallas_call(
        functools.partial(_combine_kernel, n_bm_tiles=n_bm_tiles),
        grid_spec=grid_spec,
        out_shape=(jax.ShapeDtypeStruct(bm_shape, f32),
                   jax.ShapeDtypeStruct((n_tm_tiles * TOK_TILE, D_MODEL), f32)),
        compiler_params=pltpu.CompilerParams(
            dimension_semantics=("arbitrary",), vmem_limit_bytes=VMEM_LIMIT_BYTES),
        name="moe_combine",
    )(totc, ngrp, srck, h, rankm, locc, yf, yo, wsu, wsd, ln_g, ln_b)


def _round_up(x, m):
    return (x + m - 1) // m * m


def _moe(h, wr, rb, wgu, wd, wsu, wsd, ln_g, ln_b, *, bm_shape):
    n_tok = h.shape[0]
    n_tiles = n_tok // TOK_TILE
    max_rows = _round_up(n_tiles * (OVF_CHUNKS - 1) * OVF_ALIGN + N_EXPERTS * (ROW_TILE - 1), ROW_TILE)

    hext, rankm, cnt = _router(h, wr, rb)

    i32 = jnp.int32
    cnt = cnt[:, 0, :N_EXPERTS].astype(i32)
    oc = (jnp.maximum(cnt - FIX, 0) + OVF_ALIGN - 1) // OVF_ALIGN
    over_tiles = jnp.cumsum(oc, axis=0)
    region_rows = over_tiles[-1] * OVF_ALIGN
    region_size = _round_up(region_rows, ROW_TILE)
    region_start = jnp.cumsum(region_size) - region_size
    run_dst = region_start[None, :] + OVF_ALIGN * (over_tiles - oc)
    ch_end = jnp.cumsum(oc, axis=1)
    ch_beg = ch_end - oc
    totc = ch_end[:, -1]
    k = jnp.arange(OVF_CHUNKS, dtype=i32)
    kk = k[None, :, None]
    mine = jnp.logical_and(ch_beg[:, None, :] <= kk, kk < ch_end[:, None, :])
    chunk_dst = jnp.sum(jnp.where(mine, run_dst[:, None, :] + OVF_ALIGN * (kk - ch_beg[:, None, :]), 0),
                        axis=-1)
    live = k[None, :] < totc[:, None]
    tile_ids = jnp.arange(n_tiles, dtype=i32)[:, None]
    spare = max_rows + (tile_ids % 2) * OVF_SLOTS + k[None, :] * OVF_ALIGN
    dstk = jnp.where(live, chunk_dst, spare).astype(i32)
    srck = jnp.where(live, chunk_dst, 0).astype(i32)
    ngrp = ((totc + COPY_GROUP - 1) // COPY_GROUP).astype(i32)
    zst = (region_start + region_rows).astype(i32)
    zch = ((region_size - region_rows) // OVF_ALIGN).astype(i32)
    tile_start = jnp.concatenate([region_start, region_start[-1:] + region_size[-1:]]) // ROW_TILE
    tile_start = tile_start.astype(i32)

    pad = ((0, 0), (0, LANES - N_EXPERTS))
    loc_f = jnp.pad((FIX_SLOTS + OVF_ALIGN * ch_beg).astype(f32), pad, constant_values=NO_RUN)
    end_f = jnp.pad((FIX_SLOTS + OVF_ALIGN * ch_end).astype(f32), pad, constant_values=NO_RUN)
    locs = jnp.concatenate([loc_f[:, None, :], end_f[:, None, :],
                            jnp.zeros((n_tiles, SUBLANES - 2, LANES), f32)], axis=1)
    locc = jnp.stack([loc_f, end_f], axis=-1)

    totc = totc.astype(i32)
    overflow_rows = max_rows + 2 * OVF_SLOTS
    xf, xo = _dispatch(hext, rankm, locs, totc, ngrp, dstk, zst, zch, overflow_rows=overflow_rows)
    yf, yo = _experts(xf, xo, wgu, wd, tile_start, overflow_rows=max_rows)
    return _combine(h, rankm, locc, yf, yo, wsu, wsd, ln_g, ln_b, totc, ngrp, srck, bm_shape=bm_shape)


def _diag_tiles(blocks, n_tiles):
    n_blocks, r, c = blocks.shape
    per = n_blocks // n_tiles
    wide = jnp.tile(blocks.reshape(n_blocks * r, c), (1, per))
    row_block = (jnp.arange(n_blocks * r) // r) % per
    col_block = jnp.arange(per * c) // c
    wide = jnp.where(row_block[:, None] == col_block[None, :], wide, 0)
    return wide.reshape(n_tiles, per * r, per * c)


def _head_block_diag(w):
    return _diag_tiles(w, D_RG // MXU_DIM)


def _s5_in_tiles(b):
    return _diag_tiles(b.transpose(0, 2, 1), 2)


def _s5_out_tiles(cw):
    return _diag_tiles(cw.transpose(0, 2, 1), 2)


def _row(v):
    return v.reshape(1, -1)


def kernel(x_prompt, x_sample, state_rg_conv, state_rg_h, state_s5_re, state_s5_im, w_in, conv_w, conv_b, rg_w_a, rg_b_a, rg_w_i, rg_b_i, rg_lam, s5_a_re, s5_a_im, s5_log_dt, s5_b_re, s5_b_im, s5_c_re, s5_c_im, s5_d, w_glu, b_glu, w_out, ln1_g, ln1_b, w_router, router_bias, w_gate_up, w_down, w_shared_up, w_shared_down, ln2_g, ln2_b):
    l = 0
    bp, lp, _ = x_prompt.shape
    bs, ls, _ = x_sample.shape

    are, aim, bbre, bbim = _s5_prep(
        _row(s5_a_re[l]), _row(s5_a_im[l]),
        _row(jnp.repeat(s5_log_dt[l], S5_STATE)),
        _s5_in_tiles(s5_b_re[l]), _s5_in_tiles(s5_b_im[l]))
    params = (w_in[l].astype(bf16), conv_w[l], _row(conv_b[l]),
              _head_block_diag(rg_w_a[l]).astype(bf16), _row(rg_b_a[l]),
              _head_block_diag(rg_w_i[l]).astype(bf16), _row(rg_b_i[l]), _row(rg_lam[l]),
              are, aim, bbre, bbim,
              _s5_out_tiles(s5_c_re[l]).astype(bf16), _s5_out_tiles(s5_c_im[l]).astype(bf16),
              _row(s5_d[l]), w_glu[l].astype(bf16), _row(b_glu[l]), w_out[l].astype(bf16),
              _row(ln1_g[l]), _row(ln1_b[l]))

    tail = CONV_WIDTH - 1
    n_tok = lp * bp + ls * bs
    xs_tm = x_sample.transpose(1, 0, 2).reshape(ls * bs, D_MODEL)
    h_all, sc, sh, sre, sim = _mixer(
        xs_tm, state_rg_conv[l].transpose(1, 0, 2).reshape(tail * bs, D_RG), state_rg_h[l],
        state_s5_re[l].reshape(bs, S5_N), state_s5_im[l].reshape(bs, S5_N), params,
        nb=bs, tc=ls, name="mixer_sample",
        h_all=jnp.zeros((SUBLANES, LANES), f32), total_rows=n_tok, first_row=lp * bp)
    h_all, pc, ph, pre, pim = _mixer(
        x_prompt, jnp.zeros((tail * bp, D_RG), f32), jnp.zeros((bp, D_RG), f32),
        jnp.zeros((bp, S5_N), f32), jnp.zeros((bp, S5_N), f32), params,
        nb=bp, tc=PROMPT_CHUNK_ROWS // bp, name="mixer_prompt",
        h_all=h_all, total_rows=n_tok, first_row=0)
    wr = jnp.pad(w_router[l], ((0, 0), (0, LANES - N_EXPERTS))).astype(bf16)
    rb = jnp.pad(_row(router_bias[l]), ((0, 0), (0, LANES - N_EXPERTS)))
    yp, ys_tm = _moe(h_all, wr, rb, w_gate_up[l], w_down[l],
                     w_shared_up[l].astype(bf16), w_shared_down[l].astype(bf16),
                     _row(ln2_g[l]), _row(ln2_b[l]), bm_shape=x_prompt.shape)
    ys = ys_tm.reshape(ls, bs, D_MODEL).transpose(1, 0, 2)

    def conv_out(cv, nbatch):
        return cv.reshape(tail, nbatch, D_RG).transpose(1, 0, 2)[None]

    return (yp, ys,
            conv_out(pc, bp), ph[None],
            pre.reshape(1, bp, S5_GROUPS, S5_STATE), pim.reshape(1, bp, S5_GROUPS, S5_STATE),
            conv_out(sc, bs), sh[None],
            sre.reshape(1, bs, S5_GROUPS, S5_STATE), sim.reshape(1, bs, S5_GROUPS, S5_STATE))
```

```python
import functools
import math

import jax
import jax.numpy as jnp
from jax.experimental import pallas as pl
from jax.experimental.pallas import tpu as pltpu

D_MODEL = 1024
D_RG = 512
RG_HEADS = 8
RG_HEAD_DIM = 64
CONV_WIDTH = 4
RG_C = 8.0
D_S5 = 512
S5_GROUP = 16
S5_GROUPS = 32
S5_STATE = 64
S5_N = S5_GROUPS * S5_STATE
N_EXPERTS = 64
TOP_K = 8
D_EXPERT = 256
D_SHARED = 256
ROUTED_SCALE = 2.5
DEPTH = 1
DN_ALPHA = (2.0 * DEPTH) ** 0.25
LN_EPS = 1e-5

SUBLANES = 8
LANES = 128
MXU_DIM = 256
S5_SCAN_COLS = 512
PROMPT_CHUNK_ROWS = 512
VMEM_LIMIT_BYTES = 56 * 1024 * 1024

bf16 = jnp.bfloat16
f32 = jnp.float32


def _gelu_tanh(x):
    c = math.sqrt(2.0 / math.pi)
    return x * (0.5 * (1.0 + jnp.tanh(c * (x + 0.044715 * (x * x * x)))))


def _layer_norm(x, g, b):
    mu = jnp.mean(x, axis=-1, keepdims=True)
    xc = x - mu
    var = jnp.mean(xc * xc, axis=-1, keepdims=True)
    return xc * jax.lax.rsqrt(var + LN_EPS) * g + b


def _s5_prep_kernel(lr_ref, li_ref, ldt_ref, bre_ref, bim_ref,
                    are_ref, aim_ref, bbre_ref, bbim_ref):
    lr = lr_ref[...]
    li = li_ref[...]
    dt = jnp.exp(ldt_ref[...])
    mag = jnp.exp(lr * dt)
    abar_re = mag * jnp.cos(li * dt)
    abar_im = mag * jnp.sin(li * dt)
    den = lr * lr + li * li
    nr = abar_re - 1.0
    ni = abar_im
    coef_re = (nr * lr + ni * li) / den
    coef_im = (ni * lr - nr * li) / den
    are_ref[...] = abar_re
    aim_ref[...] = abar_im
    half = S5_N // 2
    for k in range(2):
        cre = coef_re[:, k * half:(k + 1) * half]
        cim = coef_im[:, k * half:(k + 1) * half]
        br = bre_ref[k]
        bi = bim_ref[k]
        bbre_ref[k] = (cre * br - cim * bi).astype(bf16)
        bbim_ref[k] = (cre * bi + cim * br).astype(bf16)


def _s5_prep(lr, li, ldt, bre_t, bim_t):
    half = S5_N // 2
    return pl.pallas_call(
        _s5_prep_kernel,
        out_shape=(jax.ShapeDtypeStruct((1, S5_N), f32),
                   jax.ShapeDtypeStruct((1, S5_N), f32),
                   jax.ShapeDtypeStruct((2, MXU_DIM, half), bf16),
                   jax.ShapeDtypeStruct((2, MXU_DIM, half), bf16)),
        name="s5_prep",
    )(lr, li, ldt, bre_t, bim_t)


def _mixer_kernel(h_all_ref, x_ref, conv0_ref, h0_ref, s0r_ref, s0i_ref,
                  w_in_ref, conv_w_ref, conv_b_ref, wa_ref, ba_ref, wi_ref, bi_ref, lam_ref,
                  are_ref, aim_ref, bbre_ref, bbim_ref, cre_ref, cim_ref, d_ref,
                  wglu_ref, bglu_ref, wout_ref, ln_g_ref, ln_b_ref,
                  hout_ref, conv_out_ref, hlast_ref, sre_out_ref, sim_out_ref,
                  pad_scr, a_scr, b_scr, bur_scr, bui_scr, hst_scr, sr_scr, si_scr,
                  *, nb, tc, batch_major_input):
    del h_all_ref
    rows = nb * tc
    tail = (CONV_WIDTH - 1) * nb
    c = pl.program_id(0)

    @pl.when(c == 0)
    def _():
        pad_scr[0:tail, :] = conv0_ref[...]
        hst_scr[...] = h0_ref[...]
        sr_scr[...] = s0r_ref[...]
        si_scr[...] = s0i_ref[...]

    if batch_major_input:
        x = jnp.transpose(x_ref[...], (1, 0, 2)).reshape(rows, D_MODEL)
    else:
        x = x_ref[...]
    proj = jnp.dot(x.astype(bf16), w_in_ref[...], preferred_element_type=f32)
    x_rg = proj[:, :D_RG]
    g_rg = proj[:, D_RG:2 * D_RG]
    u = proj[:, 2 * D_RG:]

    pad_scr[tail:tail + rows, :] = x_rg
    conv_w = conv_w_ref[...]
    acc = conv_w[0:1, :] * pad_scr[0:rows, :]
    for k in range(1, CONV_WIDTH):
        acc = acc + conv_w[k:k + 1, :] * pad_scr[k * nb:k * nb + rows, :]
    xc = conv_b_ref[...] + acc
    new_tail = pad_scr[rows:rows + tail, :]
    pad_scr[0:tail, :] = new_tail

    xcb = xc.astype(bf16)
    ga = []
    gi = []
    for hh in range(D_RG // MXU_DIM):
        xs = xcb[:, hh * MXU_DIM:(hh + 1) * MXU_DIM]
        ga.append(jnp.dot(xs, wa_ref[hh], preferred_element_type=f32))
        gi.append(jnp.dot(xs, wi_ref[hh], preferred_element_type=f32))
    r = jax.nn.sigmoid(jnp.concatenate(ga, axis=1) + ba_ref[...])
    i = jax.nn.sigmoid(jnp.concatenate(gi, axis=1) + bi_ref[...])
    nlam = -lam_ref[...]
    softplus = jnp.maximum(nlam, 0.0) + jnp.log1p(jnp.exp(-jnp.abs(nlam)))
    log_a = (-RG_C) * r * softplus
    a_scr[...] = jnp.exp(log_a)
    th = jnp.tanh(log_a)
    b_scr[...] = jnp.sqrt((-2.0 * th) / (1.0 - th)) * (i * xc)

    for rg in range(nb // SUBLANES):
        r0 = rg * SUBLANES
        h = hst_scr[r0:r0 + SUBLANES, :]
        for t in range(tc):
            q = t * nb + r0
            h = a_scr[q:q + SUBLANES, :] * h + b_scr[q:q + SUBLANES, :]
            b_scr[q:q + SUBLANES, :] = h
        hst_scr[r0:r0 + SUBLANES, :] = h
    y_rg = b_scr[...] * _gelu_tanh(g_rg)

    ub = u.astype(bf16)
    half = S5_N // 2
    for k in range(2):
        us = ub[:, k * MXU_DIM:(k + 1) * MXU_DIM]
        bur_scr[:, k * half:(k + 1) * half] = jnp.dot(us, bbre_ref[k], preferred_element_type=f32)
        bui_scr[:, k * half:(k + 1) * half] = jnp.dot(us, bbim_ref[k], preferred_element_type=f32)
    for rg in range(nb // SUBLANES):
        r0 = rg * SUBLANES
        for cb in range(S5_N // S5_SCAN_COLS):
            c0 = cb * S5_SCAN_COLS
            ar = jnp.broadcast_to(are_ref[:, c0:c0 + S5_SCAN_COLS], (SUBLANES, S5_SCAN_COLS))
            ai = jnp.broadcast_to(aim_ref[:, c0:c0 + S5_SCAN_COLS], (SUBLANES, S5_SCAN_COLS))
            xr = sr_scr[r0:r0 + SUBLANES, c0:c0 + S5_SCAN_COLS]
            xi = si_scr[r0:r0 + SUBLANES, c0:c0 + S5_SCAN_COLS]
            for t in range(tc):
                q = t * nb + r0
                br = bur_scr[q:q + SUBLANES, c0:c0 + S5_SCAN_COLS]
                bi_ = bui_scr[q:q + SUBLANES, c0:c0 + S5_SCAN_COLS]
                nxr = ar * xr - ai * xi + br
                nxi = ar * xi + ai * xr + bi_
                bur_scr[q:q + SUBLANES, c0:c0 + S5_SCAN_COLS] = nxr
                bui_scr[q:q + SUBLANES, c0:c0 + S5_SCAN_COLS] = nxi
                xr, xi = nxr, nxi
            sr_scr[r0:r0 + SUBLANES, c0:c0 + S5_SCAN_COLS] = xr
            si_scr[r0:r0 + SUBLANES, c0:c0 + S5_SCAN_COLS] = xi
    ys = []
    for j in range(D_S5 // MXU_DIM):
        xrb = bur_scr[:, j * half:(j + 1) * half].astype(bf16)
        xib = bui_scr[:, j * half:(j + 1) * half].astype(bf16)
        ys.append(jnp.dot(xrb, cre_ref[j], preferred_element_type=f32)
                  - jnp.dot(xib, cim_ref[j], preferred_element_type=f32))
    y_s5 = jnp.concatenate(ys, axis=1) + d_ref[...] * u
    yg = _gelu_tanh(y_s5)
    glu = jnp.dot(yg.astype(bf16), wglu_ref[...], preferred_element_type=f32) + bglu_ref[...]
    y_s5 = yg * jax.nn.sigmoid(glu)

    ycat = jnp.concatenate([y_rg, y_s5], axis=1).astype(bf16)
    mix = jnp.dot(ycat, wout_ref[...], preferred_element_type=f32)
    hout_ref[...] = _layer_norm(DN_ALPHA * x + mix, ln_g_ref[...], ln_b_ref[...])

    @pl.when(c == pl.num_programs(0) - 1)
    def _():
        conv_out_ref[...] = pad_scr[0:tail, :]
        hlast_ref[...] = hst_scr[...]
        sre_out_ref[...] = sr_scr[...]
        sim_out_ref[...] = si_scr[...]


def _full(shape):
    n = len(shape)
    return pl.BlockSpec(shape, lambda c: (0,) * n)


def _mixer(x, conv0, h0, s0r, s0i, params, *, nb, tc, name, h_all, total_rows, first_row):
    batch_major_input = x.ndim == 3
    rows = nb * tc
    if batch_major_input:
        n_chunks = x.shape[1] // tc
        x_spec = pl.BlockSpec((nb, tc, D_MODEL), lambda c: (0, c, 0))
    else:
        n_chunks = x.shape[0] // rows
        x_spec = pl.BlockSpec((rows, D_MODEL), lambda c: (c, 0))
    first_block = first_row // rows
    aliased = h_all.shape == (total_rows, D_MODEL)
    tail = (CONV_WIDTH - 1) * nb
    small = (conv0, h0, s0r, s0i) + tuple(params)
    in_specs = [pl.BlockSpec(memory_space=pl.ANY), x_spec]
    in_specs += [_full(a.shape) for a in small]
    out_shape = (jax.ShapeDtypeStruct((total_rows, D_MODEL), f32),
                 jax.ShapeDtypeStruct((tail, D_RG), f32),
                 jax.ShapeDtypeStruct((nb, D_RG), f32),
                 jax.ShapeDtypeStruct((nb, S5_N), f32),
                 jax.ShapeDtypeStruct((nb, S5_N), f32))
    out_specs = (pl.BlockSpec((rows, D_MODEL), lambda c: (c + first_block, 0)),
                 _full((tail, D_RG)), _full((nb, D_RG)), _full((nb, S5_N)), _full((nb, S5_N)))
    scratch = [pltpu.VMEM((rows + tail, D_RG), f32),
               pltpu.VMEM((rows, D_RG), f32),
               pltpu.VMEM((rows, D_RG), f32),
               pltpu.VMEM((rows, S5_N), f32),
               pltpu.VMEM((rows, S5_N), f32),
               pltpu.VMEM((nb, D_RG), f32),
               pltpu.VMEM((nb, S5_N), f32),
               pltpu.VMEM((nb, S5_N), f32)]
    return pl.pallas_call(
        functools.partial(_mixer_kernel, nb=nb, tc=tc, batch_major_input=batch_major_input),
        grid=(n_chunks,),
        in_specs=in_specs,
        out_specs=out_specs,
        out_shape=out_shape,
        scratch_shapes=scratch,
        input_output_aliases={0: 0} if aliased else {},
        compiler_params=pltpu.CompilerParams(
            dimension_semantics=("arbitrary",), vmem_limit_bytes=VMEM_LIMIT_BYTES),
        name=name,
    )(h_all, x, *small)


TOK_TILE = 256
RUN_ALIGN = 16
OVF_ALIGN = 8
FIX = 32
FIX_SLOTS = N_EXPERTS * FIX
OVF_CHUNKS = 256
OVF_CHUNK_SLOTS = 384
OVF_SLOTS = 6 * OVF_CHUNK_SLOTS
SLOT_CHUNK = 512
FIX_STEPS = 4
ROW_TILE = 256
X_BUFFERS = 6
FIXED_BUFFERS = 3
COPY_GROUP = 4
EXT = D_MODEL + LANES
ROUTER_TILES = 6
NO_RUN = 1.0e9


def _top_k_gates(scores, rb):
    rows = scores.shape[0]
    lane_f = jax.lax.broadcasted_iota(jnp.int32, (rows, LANES), 1).astype(f32)
    biased = jnp.where(lane_f < float(N_EXPERTS), scores + rb, -jnp.inf)
    sel = jnp.zeros((rows, LANES), f32)
    mask = jnp.zeros((rows, LANES), f32)
    for _ in range(TOP_K):
        m = jnp.max(biased, axis=1, keepdims=True)
        idx = jnp.min(jnp.where(biased == m, lane_f, float(LANES)), axis=1, keepdims=True)
        hit = lane_f == idx
        sel = jnp.where(hit, scores, sel)
        mask = jnp.where(hit, 1.0, mask)
        biased = jnp.where(hit, -jnp.inf, biased)
    gates = sel / jnp.sum(sel, axis=1, keepdims=True) * ROUTED_SCALE
    return mask, gates


def _router_kernel(h_ref, wr_ref, rb_ref, hext_ref, rankm_ref, cnt_ref):
    hb = h_ref[...].astype(bf16)
    scores = jax.nn.sigmoid(jnp.dot(hb, wr_ref[...], preferred_element_type=f32))
    mask, gates = _top_k_gates(scores, rb_ref[...])
    t_row = jax.lax.broadcasted_iota(jnp.int32, (TOK_TILE, TOK_TILE), 0)
    t_col = jax.lax.broadcasted_iota(jnp.int32, (TOK_TILE, TOK_TILE), 1)
    earlier = jnp.where(t_col < t_row, 1.0, 0.0).astype(bf16)
    for sub in range(ROUTER_TILES):
        m = mask[sub * TOK_TILE:(sub + 1) * TOK_TILE]
        rank = jnp.dot(earlier, m.astype(bf16), preferred_element_type=f32)
        rankm_ref[sub * TOK_TILE:(sub + 1) * TOK_TILE, :] = jnp.where(m > 0.0, rank, -1.0).astype(bf16)
        cnt_ref[sub] = jnp.broadcast_to(jnp.sum(m, axis=0, keepdims=True), (SUBLANES, LANES))
    g_hi = gates.astype(bf16).astype(f32)
    g_pack = g_hi + pltpu.roll(gates - g_hi, N_EXPERTS, 1)
    hext_ref[:, :D_MODEL] = hb
    hext_ref[:, D_MODEL:] = g_pack.astype(bf16)


def _router(h, wr, rb):
    n_tiles = h.shape[0] // TOK_TILE
    assert n_tiles % ROUTER_TILES == 0
    rows = ROUTER_TILES * TOK_TILE
    const = lambda shape: pl.BlockSpec(shape, lambda i: (0,) * len(shape))
    return pl.pallas_call(
        _router_kernel,
        grid=(n_tiles // ROUTER_TILES,),
        in_specs=[pl.BlockSpec((rows, D_MODEL), lambda i: (i, 0)), const(wr.shape), const(rb.shape)],
        out_specs=(pl.BlockSpec((rows, EXT), lambda i: (i, 0)),
                   pl.BlockSpec((rows, LANES), lambda i: (i, 0)),
                   pl.BlockSpec((ROUTER_TILES, SUBLANES, LANES), lambda i: (i, 0, 0))),
        out_shape=(jax.ShapeDtypeStruct((h.shape[0], EXT), bf16),
                   jax.ShapeDtypeStruct((h.shape[0], LANES), bf16),
                   jax.ShapeDtypeStruct((n_tiles, SUBLANES, LANES), f32)),
        compiler_params=pltpu.CompilerParams(
            dimension_semantics=("arbitrary",), vmem_limit_bytes=VMEM_LIMIT_BYTES),
        name="moe_router",
    )(h, wr, rb)


def _run_copy(src, dst, sem):
    return pltpu.make_async_copy(src, dst, sem)


def _fixed_ranges(shape, axis):
    e = jax.lax.broadcasted_iota(jnp.int32, shape, axis).astype(f32)
    loc = jnp.where(e < float(N_EXPERTS), e * float(FIX), NO_RUN)
    return loc, loc + float(FIX)


def _slot_rows(first_slot, loc_row, end_row, base, rankm, n_slots=SLOT_CHUNK):
    s = (jax.lax.broadcasted_iota(jnp.int32, (n_slots, LANES), 0) + first_slot).astype(f32)
    in_run = jnp.where(s >= loc_row, jnp.where(s < end_row, 1.0, 0.0), 0.0)
    r_col = s[:, 0:1] - jnp.sum(in_run * (loc_row - base), axis=1, keepdims=True)
    q = jax.lax.dot_general(in_run.astype(bf16), rankm, (((1,), (1,)), ((), ())),
                            preferred_element_type=f32)
    return jnp.where(q == r_col, 1.0, 0.0).astype(bf16)


def _slot_cols(first_slot, loc_col, end_col, base, rankm, n_slots=SLOT_CHUNK):
    s = (jax.lax.broadcasted_iota(jnp.int32, (LANES, n_slots), 1) + first_slot).astype(f32)
    in_run = jnp.where(s >= loc_col, jnp.where(s < end_col, 1.0, 0.0), 0.0)
    r_row = s[0:1, :] - jnp.sum(in_run * (loc_col - base), axis=0, keepdims=True)
    q = jnp.dot(rankm, in_run.astype(bf16), preferred_element_type=f32)
    return jnp.where(q == r_row, 1.0, 0.0).astype(bf16)


def _wait_groups(n_groups, make_copy):
    max_bits = (OVF_CHUNKS // COPY_GROUP).bit_length()
    for b in range(max_bits):
        rows = (COPY_GROUP * OVF_ALIGN) << b

        @pl.when(jnp.bitwise_and(jnp.right_shift(n_groups, b), 1) == 1)
        def _():
            make_copy(rows).wait()


def _dispatch_kernel(totc_ref, ngrp_ref, dstk_ref, zst_ref, zch_ref,
                     hext_ref, rankm_ref, locs_ref, sorted_f_ref, sorted_o_ref,
                     stage_f, stage_o, zbuf, semf, semo, zsem):
    i = pl.program_id(0)
    n = pl.num_programs(0)
    slot = i % 2
    rankm = rankm_ref[...]

    fx_loc, fx_end = _fixed_ranges((1, LANES), 1)
    experts_per_chunk = SLOT_CHUNK // FIX
    for ch in range(FIX_SLOTS // SLOT_CHUNK):
        p = _slot_rows(ch * SLOT_CHUNK, fx_loc, fx_end, 0.0, rankm)
        rows = jnp.dot(p, hext_ref[...], preferred_element_type=f32).astype(bf16)
        stage_f[slot, ch * experts_per_chunk:(ch + 1) * experts_per_chunk] = rows.reshape(
            experts_per_chunk, FIX, EXT)

    ov_loc = locs_ref[0, 0:1, :]
    ov_end = locs_ref[0, 1:2, :]
    for ch in range(OVF_SLOTS // OVF_CHUNK_SLOTS):
        def sort_overflow(ch=ch):
            p = _slot_rows(FIX_SLOTS + ch * OVF_CHUNK_SLOTS, ov_loc, ov_end, float(FIX), rankm,
                           OVF_CHUNK_SLOTS)
            stage_o[slot, ch * OVF_CHUNK_SLOTS:(ch + 1) * OVF_CHUNK_SLOTS, :] = jnp.dot(
                p, hext_ref[...], preferred_element_type=f32)
        pl.when(totc_ref[i] * OVF_ALIGN > ch * OVF_CHUNK_SLOTS)(sort_overflow)

    def fixed_copy(tile, slot_):
        dst = sorted_f_ref.at[:, pl.ds(pl.multiple_of(tile * FIX, FIX), FIX), :]
        return _run_copy(stage_f.at[slot_], dst, semf.at[slot_])

    def overflow_wait(rows, slot_):
        return _run_copy(stage_o.at[slot_, pl.ds(0, rows)], sorted_o_ref.at[pl.ds(0, rows)],
                         semo.at[slot_])

    fixed_copy(i, slot).start()

    def start_group(g, carry):
        for u in range(COPY_GROUP):
            k = g * COPY_GROUP + u
            src = stage_o.at[slot, pl.ds(pl.multiple_of(k * OVF_ALIGN, OVF_ALIGN), OVF_ALIGN)]
            dst = sorted_o_ref.at[pl.ds(pl.multiple_of(dstk_ref[i, k], OVF_ALIGN), OVF_ALIGN)]
            _run_copy(src, dst, semo.at[slot]).start()
        return carry
    jax.lax.fori_loop(0, ngrp_ref[i], start_group, 0)

    @pl.when(i == 0)
    def _():
        zbuf[...] = jnp.zeros(zbuf.shape, f32)

    def zero_copy(e, k):
        dst = sorted_o_ref.at[pl.ds(pl.multiple_of(zst_ref[e] + k * OVF_ALIGN, OVF_ALIGN), OVF_ALIGN)]
        return _run_copy(zbuf, dst, zsem.at[0])

    @pl.when(jnp.logical_and(i > 0, i <= N_EXPERTS))
    def _():
        def wait_zero(k, carry):
            zero_copy(i - 1, k).wait()
            return carry
        jax.lax.fori_loop(0, zch_ref[i - 1], wait_zero, 0)

    @pl.when(i < N_EXPERTS)
    def _():
        def start_zero(k, carry):
            zero_copy(i, k).start()
            return carry
        jax.lax.fori_loop(0, zch_ref[i], start_zero, 0)

    @pl.when(i > 0)
    def _():
        fixed_copy(i - 1, 1 - slot).wait()
        _wait_groups(ngrp_ref[i - 1], lambda rows: overflow_wait(rows, 1 - slot))

    @pl.when(i == n - 1)
    def _():
        fixed_copy(i, slot).wait()
        _wait_groups(ngrp_ref[i], lambda rows: overflow_wait(rows, slot))


def _dispatch(hext, rankm, locs, totc, ngrp, dstk, zst, zch, *, overflow_rows):
    n_tiles = hext.shape[0] // TOK_TILE
    assert n_tiles > N_EXPERTS
    grid_spec = pltpu.PrefetchScalarGridSpec(
        num_scalar_prefetch=5,
        grid=(n_tiles,),
        in_specs=[pl.BlockSpec((TOK_TILE, EXT), lambda i, *_: (i, 0)),
                  pl.BlockSpec((TOK_TILE, LANES), lambda i, *_: (i, 0)),
                  pl.BlockSpec((1, SUBLANES, LANES), lambda i, *_: (i, 0, 0))],
        out_specs=(pl.BlockSpec(memory_space=pl.ANY), pl.BlockSpec(memory_space=pl.ANY)),
        scratch_shapes=[pltpu.VMEM((2, N_EXPERTS, FIX, EXT), bf16),
                        pltpu.VMEM((2, OVF_SLOTS, EXT), f32),
                        pltpu.VMEM((OVF_ALIGN, EXT), f32),
                        pltpu.SemaphoreType.DMA((2,)),
                        pltpu.SemaphoreType.DMA((2,)),
                        pltpu.SemaphoreType.DMA((1,))])
    return pl.pallas_call(
        _dispatch_kernel,
        grid_spec=grid_spec,
        out_shape=(jax.ShapeDtypeStruct((N_EXPERTS, n_tiles * FIX, EXT), bf16),
                   jax.ShapeDtypeStruct((overflow_rows, EXT), f32)),
        compiler_params=pltpu.CompilerParams(
            dimension_semantics=("arbitrary",), vmem_limit_bytes=VMEM_LIMIT_BYTES),
        name="moe_dispatch",
    )(totc, ngrp, dstk, zst, zch, hext, rankm, locs)


def _experts_kernel(ts_ref, xf_ref, xo_ref, wgu_ref, wd_ref, yf_ref, yo_ref,
                    xfbuf, wgu_f, wd_f, xbuf, ybuf, wgu_b, wd_b, fsem, xsem, ysem):
    e = pl.program_id(0)
    n_valid = ts_ref[N_EXPERTS]

    def ring_copies(expert, slot):
        return (pltpu.make_async_copy(xf_ref.at[expert], xfbuf.at[slot], fsem.at[0, slot]),
                pltpu.make_async_copy(wgu_ref.at[expert], wgu_f.at[slot], fsem.at[1, slot]),
                pltpu.make_async_copy(wd_ref.at[expert], wd_f.at[slot], fsem.at[2, slot]))

    @pl.when(e == 0)
    def _():
        for a in range(FIXED_BUFFERS - 1):
            for copy in ring_copies(a, a):
                copy.start()

    @pl.when(e + FIXED_BUFFERS - 1 < pl.num_programs(0))
    def _():
        for copy in ring_copies(e + FIXED_BUFFERS - 1, (e + FIXED_BUFFERS - 1) % FIXED_BUFFERS):
            copy.start()

    fslot = e % FIXED_BUFFERS
    for copy in ring_copies(e, fslot):
        copy.wait()
    wgu_b[...] = wgu_f[fslot].astype(bf16)
    wd_b[...] = wd_f[fslot].astype(bf16)

    def ffn(x_ext):
        m = x_ext.shape[0]
        gu = jnp.dot(x_ext[:, :D_MODEL], wgu_b[...], preferred_element_type=f32)
        act = (jax.nn.silu(gu[:, :D_EXPERT]) * gu[:, D_EXPERT:]).astype(bf16)
        y = jnp.dot(act, wd_b[...], preferred_element_type=f32)
        g_pack = x_ext[:, D_MODEL:].astype(f32)
        lane = jax.lax.broadcasted_iota(jnp.int32, (m, LANES), 1)
        mine = jnp.where(lane == e, g_pack, jnp.where(lane == e + N_EXPERTS, g_pack, 0.0))
        return (y * jnp.sum(mine, axis=1, keepdims=True)).astype(bf16)

    def x_copy(j, slot):
        rows = pl.ds(pl.multiple_of(j * ROW_TILE, ROW_TILE), ROW_TILE)
        return pltpu.make_async_copy(xo_ref.at[rows], xbuf.at[slot], xsem.at[slot])

    def y_copy(j, slot):
        rows = pl.ds(pl.multiple_of(j * ROW_TILE, ROW_TILE), ROW_TILE)
        return pltpu.make_async_copy(ybuf.at[slot], yo_ref.at[rows], ysem.at[slot])

    ahead = X_BUFFERS - 1

    @pl.when(e == 0)
    def _():
        for a in range(ahead):
            @pl.when(a < n_valid)
            def _():
                x_copy(a, a).start()

    step_rows = xfbuf.shape[1] // FIX_STEPS
    for c in range(FIX_STEPS):
        rows = slice(c * step_rows, (c + 1) * step_rows)
        yf_ref[0, rows, :] = ffn(xfbuf[fslot, rows, :])

    def tile(j, carry):
        xslot = j % X_BUFFERS
        slot = j % 2
        x_copy(j, xslot).wait()

        @pl.when(j + ahead < n_valid)
        def _():
            x_copy(j + ahead, (j + ahead) % X_BUFFERS).start()

        @pl.when(j >= 2)
        def _():
            y_copy(j - 2, slot).wait()

        ybuf[slot] = ffn(xbuf[xslot].astype(bf16)).astype(f32)
        y_copy(j, slot).start()
        return carry
    jax.lax.fori_loop(ts_ref[e], ts_ref[e + 1], tile, 0)

    @pl.when(e == pl.num_programs(0) - 1)
    def _():
        @pl.when(n_valid >= 2)
        def _():
            y_copy(n_valid - 2, n_valid % 2).wait()

        @pl.when(n_valid >= 1)
        def _():
            y_copy(n_valid - 1, (n_valid - 1) % 2).wait()


def _experts(xf, xo, wgu, wd, ts, *, overflow_rows):
    fixed_rows = xf.shape[1]
    assert fixed_rows % (FIX_STEPS * RUN_ALIGN) == 0
    grid_spec = pltpu.PrefetchScalarGridSpec(
        num_scalar_prefetch=1,
        grid=(N_EXPERTS,),
        in_specs=[pl.BlockSpec(memory_space=pl.ANY),
                  pl.BlockSpec(memory_space=pl.ANY),
                  pl.BlockSpec(memory_space=pl.ANY),
                  pl.BlockSpec(memory_space=pl.ANY)],
        out_specs=(pl.BlockSpec((1, fixed_rows, D_MODEL), lambda e, ts: (e, 0, 0)),
                   pl.BlockSpec(memory_space=pl.ANY)),
        scratch_shapes=[pltpu.VMEM((FIXED_BUFFERS, fixed_rows, EXT), bf16),
                        pltpu.VMEM((FIXED_BUFFERS, D_MODEL, 2 * D_EXPERT), f32),
                        pltpu.VMEM((FIXED_BUFFERS, D_EXPERT, D_MODEL), f32),
                        pltpu.VMEM((X_BUFFERS, ROW_TILE, EXT), f32),
                        pltpu.VMEM((2, ROW_TILE, D_MODEL), f32),
                        pltpu.VMEM((D_MODEL, 2 * D_EXPERT), bf16),
                        pltpu.VMEM((D_EXPERT, D_MODEL), bf16),
                        pltpu.SemaphoreType.DMA((3, FIXED_BUFFERS)),
                        pltpu.SemaphoreType.DMA((X_BUFFERS,)),
                        pltpu.SemaphoreType.DMA((2,))])
    return pl.pallas_call(
        _experts_kernel,
        grid_spec=grid_spec,
        out_shape=(jax.ShapeDtypeStruct((N_EXPERTS, fixed_rows, D_MODEL), bf16),
                   jax.ShapeDtypeStruct((overflow_rows, D_MODEL), f32)),
        compiler_params=pltpu.CompilerParams(
            dimension_semantics=("arbitrary",), vmem_limit_bytes=VMEM_LIMIT_BYTES),
        name="moe_experts",
    )(ts, xf, xo, wgu, wd)


def _combine_kernel(totc_ref, ngrp_ref, srck_ref,
                    h_ref, rankm_ref, locc_ref, yf_ref, yo_ref, wsu_ref, wsd_ref, ln_g_ref, ln_b_ref,
                    out_bm_ref, out_tm_ref, yloc_f, yloc_o, acc_scr, p_scr, semf, semo,
                    *, n_bm_tiles):
    i = pl.program_id(0)
    n = pl.num_programs(0)
    slot = i % 2

    def fixed_copy(tile, slot_):
        src = yf_ref.at[:, pl.ds(pl.multiple_of(tile * FIX, FIX), FIX), :]
        return _run_copy(src, yloc_f.at[slot_], semf.at[slot_])

    def overflow_wait(rows, slot_):
        return _run_copy(yo_ref.at[pl.ds(0, rows)], yloc_o.at[slot_, pl.ds(0, rows)], semo.at[slot_])

    def fetch(tile, slot_):
        fixed_copy(tile, slot_).start()

        def start_group(g, carry):
            for u in range(COPY_GROUP):
                k = g * COPY_GROUP + u
                src = yo_ref.at[pl.ds(pl.multiple_of(srck_ref[tile, k], OVF_ALIGN), OVF_ALIGN)]
                dst = yloc_o.at[slot_, pl.ds(pl.multiple_of(k * OVF_ALIGN, OVF_ALIGN), OVF_ALIGN)]
                _run_copy(src, dst, semo.at[slot_]).start()
            return carry
        jax.lax.fori_loop(0, ngrp_ref[tile], start_group, 0)

    @pl.when(i == 0)
    def _():
        yloc_o[...] = jnp.zeros(yloc_o.shape, f32)
        fetch(0, 0)

    @pl.when(i + 1 < n)
    def _():
        fetch(i + 1, 1 - slot)

    h = h_ref[...]
    hb = h.astype(bf16)
    su = jnp.dot(hb, wsu_ref[...], preferred_element_type=f32)
    act = (jax.nn.silu(su[:, :D_SHARED]) * su[:, D_SHARED:]).astype(bf16)
    acc_scr[...] = jnp.dot(act, wsd_ref[...], preferred_element_type=f32)

    rankm = rankm_ref[...]
    fx_loc, fx_end = _fixed_ranges((LANES, 1), 0)
    for ch in range(FIX_SLOTS // SLOT_CHUNK):
        p_scr[:, ch * SLOT_CHUNK:(ch + 1) * SLOT_CHUNK] = _slot_cols(
            ch * SLOT_CHUNK, fx_loc, fx_end, 0.0, rankm)

    fixed_copy(i, slot).wait()
    _wait_groups(ngrp_ref[i], lambda rows: overflow_wait(rows, slot))

    acc_scr[...] += jnp.dot(p_scr[...], yloc_f[slot].reshape(FIX_SLOTS, D_MODEL),
                            preferred_element_type=f32)
    ov_loc = locc_ref[0, :, 0:1]
    ov_end = locc_ref[0, :, 1:2]
    for ch in range(OVF_SLOTS // OVF_CHUNK_SLOTS):
        def gather_overflow(ch=ch):
            p = _slot_cols(FIX_SLOTS + ch * OVF_CHUNK_SLOTS, ov_loc, ov_end, float(FIX), rankm,
                           OVF_CHUNK_SLOTS)
            y_o = yloc_o[slot, ch * OVF_CHUNK_SLOTS:(ch + 1) * OVF_CHUNK_SLOTS, :].astype(bf16)
            acc_scr[...] += jnp.dot(p, y_o, preferred_element_type=f32)
        pl.when(totc_ref[i] * OVF_ALIGN > ch * OVF_CHUNK_SLOTS)(gather_overflow)
    y = _layer_norm(DN_ALPHA * h + acc_scr[...], ln_g_ref[...], ln_b_ref[...])

    @pl.when(i < n_bm_tiles)
    def _():
        nb, tc, _ = out_bm_ref.shape
        out_bm_ref[...] = jnp.transpose(y.reshape(tc, nb, D_MODEL), (1, 0, 2))

    @pl.when(i >= n_bm_tiles)
    def _():
        out_tm_ref[...] = y


def _combine(h, rankm, locc, yf, yo, wsu, wsd, ln_g, ln_b, totc, ngrp, srck, *, bm_shape):
    n_tiles = h.shape[0] // TOK_TILE
    nb, length, _ = bm_shape
    tc = TOK_TILE // nb
    n_bm_tiles = length // tc
    n_tm_tiles = n_tiles - n_bm_tiles
    const = lambda shape: pl.BlockSpec(shape, lambda i, *_: (0,) * len(shape))
    grid_spec = pltpu.PrefetchScalarGridSpec(
        num_scalar_prefetch=3,
        grid=(n_tiles,),
        in_specs=[pl.BlockSpec((TOK_TILE, D_MODEL), lambda i, *_: (i, 0)),
                  pl.BlockSpec((TOK_TILE, LANES), lambda i, *_: (i, 0)),
                  pl.BlockSpec((1, LANES, 2), lambda i, *_: (i, 0, 0)),
                  pl.BlockSpec(memory_space=pl.ANY),
                  pl.BlockSpec(memory_space=pl.ANY),
                  const(wsu.shape), const(wsd.shape), const(ln_g.shape), const(ln_b.shape)],
        out_specs=(pl.BlockSpec((nb, tc, D_MODEL),
                                lambda i, *_: (0, jnp.minimum(i, n_bm_tiles - 1), 0)),
                   pl.BlockSpec((TOK_TILE, D_MODEL),
                                lambda i, *_: (jnp.maximum(i - n_bm_tiles, 0), 0))),
        scratch_shapes=[pltpu.VMEM((2, N_EXPERTS, FIX, D_MODEL), bf16),
                        pltpu.VMEM((2, OVF_SLOTS, D_MODEL), f32),
                        pltpu.VMEM((TOK_TILE, D_MODEL), f32),
                        pltpu.VMEM((TOK_TILE, FIX_SLOTS), bf16),
                        pltpu.SemaphoreType.DMA((2,)),
                        pltpu.SemaphoreType.DMA((2,))])
    return pl.pallas_call(
        functools.partial(_combine_kernel, n_bm_tiles=n_bm_tiles),
        grid_spec=grid_spec,
        out_shape=(jax.ShapeDtypeStruct(bm_shape, f32),
                   jax.ShapeDtypeStruct((n_tm_tiles * TOK_TILE, D_MODEL), f32)),
        compiler_params=pltpu.CompilerParams(
            dimension_semantics=("arbitrary",), vmem_limit_bytes=VMEM_LIMIT_BYTES),
        name="moe_combine",
    )(totc, ngrp, srck, h, rankm, locc, yf, yo, wsu, wsd, ln_g, ln_b)


def _round_up(x, m):
    return (x + m - 1) // m * m


def _moe(h, wr, rb, wgu, wd, wsu, wsd, ln_g, ln_b, *, bm_shape):
    n_tok = h.shape[0]
    n_tiles = n_tok // TOK_TILE
    max_rows = _round_up(n_tiles * (OVF_CHUNKS - 1) * OVF_ALIGN + N_EXPERTS * (ROW_TILE - 1), ROW_TILE)

    hext, rankm, cnt = _router(h, wr, rb)

    i32 = jnp.int32
    cnt = cnt[:, 0, :N_EXPERTS].astype(i32)
    oc = (jnp.maximum(cnt - FIX, 0) + OVF_ALIGN - 1) // OVF_ALIGN
    over_tiles = jnp.cumsum(oc, axis=0)
    region_rows = over_tiles[-1] * OVF_ALIGN
    region_size = _round_up(region_rows, ROW_TILE)
    region_start = jnp.cumsum(region_size) - region_size
    run_dst = region_start[None, :] + OVF_ALIGN * (over_tiles - oc)
    ch_end = jnp.cumsum(oc, axis=1)
    ch_beg = ch_end - oc
    totc = ch_end[:, -1]
    k = jnp.arange(OVF_CHUNKS, dtype=i32)
    kk = k[None, :, None]
    mine = jnp.logical_and(ch_beg[:, None, :] <= kk, kk < ch_end[:, None, :])
    chunk_dst = jnp.sum(jnp.where(mine, run_dst[:, None, :] + OVF_ALIGN * (kk - ch_beg[:, None, :]), 0),
                        axis=-1)
    live = k[None, :] < totc[:, None]
    tile_ids = jnp.arange(n_tiles, dtype=i32)[:, None]
    spare = max_rows + (tile_ids % 2) * OVF_SLOTS + k[None, :] * OVF_ALIGN
    dstk = jnp.where(live, chunk_dst, spare).astype(i32)
    srck = jnp.where(live, chunk_dst, 0).astype(i32)
    ngrp = ((totc + COPY_GROUP - 1) // COPY_GROUP).astype(i32)
    zst = (region_start + region_rows).astype(i32)
    zch = ((region_size - region_rows) // OVF_ALIGN).astype(i32)
    tile_start = jnp.concatenate([region_start, region_start[-1:] + region_size[-1:]]) // ROW_TILE
    tile_start = tile_start.astype(i32)

    pad = ((0, 0), (0, LANES - N_EXPERTS))
    loc_f = jnp.pad((FIX_SLOTS + OVF_ALIGN * ch_beg).astype(f32), pad, constant_values=NO_RUN)
    end_f = jnp.pad((FIX_SLOTS + OVF_ALIGN * ch_end).astype(f32), pad, constant_values=NO_RUN)
    locs = jnp.concatenate([loc_f[:, None, :], end_f[:, None, :],
                            jnp.zeros((n_tiles, SUBLANES - 2, LANES), f32)], axis=1)
    locc = jnp.stack([loc_f, end_f], axis=-1)

    totc = totc.astype(i32)
    overflow_rows = max_rows + 2 * OVF_SLOTS
    xf, xo = _dispatch(hext, rankm, locs, totc, ngrp, dstk, zst, zch, overflow_rows=overflow_rows)
    yf, yo = _experts(xf, xo, wgu, wd, tile_start, overflow_rows=max_rows)
    return _combine(h, rankm, locc, yf, yo, wsu, wsd, ln_g, ln_b, totc, ngrp, srck, bm_shape=bm_shape)


def _diag_tiles(blocks, n_tiles):
    n_blocks, r, c = blocks.shape
    per = n_blocks // n_tiles
    wide = jnp.tile(blocks.reshape(n_blocks * r, c), (1, per))
    row_block = (jnp.arange(n_blocks * r) // r) % per
    col_block = jnp.arange(per * c) // c
    wide = jnp.where(row_block[:, None] == col_block[None, :], wide, 0)
    return wide.reshape(n_tiles, per * r, per * c)


def _head_block_diag(w):
    return _diag_tiles(w, D_RG // MXU_DIM)


def _s5_in_tiles(b):
    return _diag_tiles(b.transpose(0, 2, 1), 2)


def _s5_out_tiles(cw):
    return _diag_tiles(cw.transpose(0, 2, 1), 2)


def _row(v):
    return v.reshape(1, -1)


def kernel(x_prompt, x_sample, state_rg_conv, state_rg_h, state_s5_re, state_s5_im, w_in, conv_w, conv_b, rg_w_a, rg_b_a, rg_w_i, rg_b_i, rg_lam, s5_a_re, s5_a_im, s5_log_dt, s5_b_re, s5_b_im, s5_c_re, s5_c_im, s5_d, w_glu, b_glu, w_out, ln1_g, ln1_b, w_router, router_bias, w_gate_up, w_down, w_shared_up, w_shared_down, ln2_g, ln2_b):
    l = 0
    bp, lp, _ = x_prompt.shape
    bs, ls, _ = x_sample.shape

    are, aim, bbre, bbim = _s5_prep(
        _row(s5_a_re[l]), _row(s5_a_im[l]),
        _row(jnp.repeat(s5_log_dt[l], S5_STATE)),
        _s5_in_tiles(s5_b_re[l]), _s5_in_tiles(s5_b_im[l]))
    params = (w_in[l].astype(bf16), conv_w[l], _row(conv_b[l]),
              _head_block_diag(rg_w_a[l]).astype(bf16), _row(rg_b_a[l]),
              _head_block_diag(rg_w_i[l]).astype(bf16), _row(rg_b_i[l]), _row(rg_lam[l]),
              are, aim, bbre, bbim,
              _s5_out_tiles(s5_c_re[l]).astype(bf16), _s5_out_tiles(s5_c_im[l]).astype(bf16),
              _row(s5_d[l]), w_glu[l].astype(bf16), _row(b_glu[l]), w_out[l].astype(bf16),
              _row(ln1_g[l]), _row(ln1_b[l]))

    tail = CONV_WIDTH - 1
    n_tok = lp * bp + ls * bs
    xs_tm = x_sample.transpose(1, 0, 2).reshape(ls * bs, D_MODEL)
    h_all, sc, sh, sre, sim = _mixer(
        xs_tm, state_rg_conv[l].transpose(1, 0, 2).reshape(tail * bs, D_RG), state_rg_h[l],
        state_s5_re[l].reshape(bs, S5_N), state_s5_im[l].reshape(bs, S5_N), params,
        nb=bs, tc=ls, name="mixer_sample",
        h_all=jnp.zeros((SUBLANES, LANES), f32), total_rows=n_tok, first_row=lp * bp)
    h_all, pc, ph, pre, pim = _mixer(
        x_prompt, jnp.zeros((tail * bp, D_RG), f32), jnp.zeros((bp, D_RG), f32),
        jnp.zeros((bp, S5_N), f32), jnp.zeros((bp, S5_N), f32), params,
        nb=bp, tc=PROMPT_CHUNK_ROWS // bp, name="mixer_prompt",
        h_all=h_all, total_rows=n_tok, first_row=0)
    wr = jnp.pad(w_router[l], ((0, 0), (0, LANES - N_EXPERTS))).astype(bf16)
    rb = jnp.pad(_row(router_bias[l]), ((0, 0), (0, LANES - N_EXPERTS)))
    yp, ys_tm = _moe(h_all, wr, rb, w_gate_up[l], w_down[l],
                     w_shared_up[l].astype(bf16), w_shared_down[l].astype(bf16),
                     _row(ln2_g[l]), _row(ln2_b[l]), bm_shape=x_prompt.shape)
    ys = ys_tm.reshape(ls, bs, D_MODEL).transpose(1, 0, 2)

    def conv_out(cv, nbatch):
        return cv.reshape(tail, nbatch, D_RG).transpose(1, 0, 2)[None]

    return (yp, ys,
            conv_out(pc, bp), ph[None],
            pre.reshape(1, bp, S5_GROUPS, S5_STATE), pim.reshape(1, bp, S5_GROUPS, S5_STATE),
            conv_out(sc, bs), sh[None],
            sre.reshape(1, bs, S5_GROUPS, S5_STATE), sim.reshape(1, bs, S5_GROUPS, S5_STATE))
```

```python
import functools
import math

import jax
import jax.numpy as jnp
from jax.experimental import pallas as pl
from jax.experimental.pallas import tpu as pltpu

D_MODEL = 1024
D_RG = 512
RG_HEADS = 8
RG_HEAD_DIM = 64
CONV_WIDTH = 4
RG_C = 8.0
D_S5 = 512
S5_GROUP = 16
S5_GROUPS = 32
S5_STATE = 64
S5_N = S5_GROUPS * S5_STATE
N_EXPERTS = 64
TOP_K = 8
D_EXPERT = 256
D_SHARED = 256
ROUTED_SCALE = 2.5
DEPTH = 1
DN_ALPHA = (2.0 * DEPTH) ** 0.25
LN_EPS = 1e-5

SUBLANES = 8
LANES = 128
MXU_DIM = 256
S5_SCAN_COLS = 512
PROMPT_CHUNK_ROWS = 512
VMEM_LIMIT_BYTES = 56 * 1024 * 1024

bf16 = jnp.bfloat16
f32 = jnp.float32


def _gelu_tanh(x):
    c = math.sqrt(2.0 / math.pi)
    return x * (0.5 * (1.0 + jnp.tanh(c * (x + 0.044715 * (x * x * x)))))


def _layer_norm(x, g, b):
    mu = jnp.mean(x, axis=-1, keepdims=True)
    xc = x - mu
    var = jnp.mean(xc * xc, axis=-1, keepdims=True)
    return xc * jax.lax.rsqrt(var + LN_EPS) * g + b


def _s5_prep_kernel(lr_ref, li_ref, ldt_ref, bre_ref, bim_ref,
                    are_ref, aim_ref, bbre_ref, bbim_ref):
    lr = lr_ref[...]
    li = li_ref[...]
    dt = jnp.exp(ldt_ref[...])
    mag = jnp.exp(lr * dt)
    abar_re = mag * jnp.cos(li * dt)
    abar_im = mag * jnp.sin(li * dt)
    den = lr * lr + li * li
    nr = abar_re - 1.0
    ni = abar_im
    coef_re = (nr * lr + ni * li) / den
    coef_im = (ni * lr - nr * li) / den
    are_ref[...] = abar_re
    aim_ref[...] = abar_im
    half = S5_N // 2
    for k in range(2):
        cre = coef_re[:, k * half:(k + 1) * half]
        cim = coef_im[:, k * half:(k + 1) * half]
        br = bre_ref[k]
        bi = bim_ref[k]
        bbre_ref[k] = (cre * br - cim * bi).astype(bf16)
        bbim_ref[k] = (cre * bi + cim * br).astype(bf16)


def _s5_prep(lr, li, ldt, bre_t, bim_t):
    half = S5_N // 2
    return pl.pallas_call(
        _s5_prep_kernel,
        out_shape=(jax.ShapeDtypeStruct((1, S5_N), f32),
                   jax.ShapeDtypeStruct((1, S5_N), f32),
                   jax.ShapeDtypeStruct((2, MXU_DIM, half), bf16),
                   jax.ShapeDtypeStruct((2, MXU_DIM, half), bf16)),
        name="s5_prep",
    )(lr, li, ldt, bre_t, bim_t)


def _mixer_kernel(h_all_ref, x_ref, conv0_ref, h0_ref, s0r_ref, s0i_ref,
                  w_in_ref, conv_w_ref, conv_b_ref, wa_ref, ba_ref, wi_ref, bi_ref, lam_ref,
                  are_ref, aim_ref, bbre_ref, bbim_ref, cre_ref, cim_ref, d_ref,
                  wglu_ref, bglu_ref, wout_ref, ln_g_ref, ln_b_ref,
                  hout_ref, conv_out_ref, hlast_ref, sre_out_ref, sim_out_ref,
                  pad_scr, a_scr, b_scr, bur_scr, bui_scr, hst_scr, sr_scr, si_scr,
                  *, nb, tc, batch_major_input):
    del h_all_ref
    rows = nb * tc
    tail = (CONV_WIDTH - 1) * nb
    c = pl.program_id(0)

    @pl.when(c == 0)
    def _():
        pad_scr[0:tail, :] = conv0_ref[...]
        hst_scr[...] = h0_ref[...]
        sr_scr[...] = s0r_ref[...]
        si_scr[...] = s0i_ref[...]

    if batch_major_input:
        x = jnp.transpose(x_ref[...], (1, 0, 2)).reshape(rows, D_MODEL)
    else:
        x = x_ref[...]
    proj = jnp.dot(x.astype(bf16), w_in_ref[...], preferred_element_type=f32)
    x_rg = proj[:, :D_RG]
    g_rg = proj[:, D_RG:2 * D_RG]
    u = proj[:, 2 * D_RG:]

    pad_scr[tail:tail + rows, :] = x_rg
    conv_w = conv_w_ref[...]
    acc = conv_w[0:1, :] * pad_scr[0:rows, :]
    for k in range(1, CONV_WIDTH):
        acc = acc + conv_w[k:k + 1, :] * pad_scr[k * nb:k * nb + rows, :]
    xc = conv_b_ref[...] + acc
    new_tail = pad_scr[rows:rows + tail, :]
    pad_scr[0:tail, :] = new_tail

    xcb = xc.astype(bf16)
    ga = []
    gi = []
    for hh in range(D_RG // MXU_DIM):
        xs = xcb[:, hh * MXU_DIM:(hh + 1) * MXU_DIM]
        ga.append(jnp.dot(xs, wa_ref[hh], preferred_element_type=f32))
        gi.append(jnp.dot(xs, wi_ref[hh], preferred_element_type=f32))
    r = jax.nn.sigmoid(jnp.concatenate(ga, axis=1) + ba_ref[...])
    i = jax.nn.sigmoid(jnp.concatenate(gi, axis=1) + bi_ref[...])
    nlam = -lam_ref[...]
    softplus = jnp.maximum(nlam, 0.0) + jnp.log1p(jnp.exp(-jnp.abs(nlam)))
    log_a = (-RG_C) * r * softplus
    a_scr[...] = jnp.exp(log_a)
    th = jnp.tanh(log_a)
    b_scr[...] = jnp.sqrt((-2.0 * th) / (1.0 - th)) * (i * xc)

    for rg in range(nb // SUBLANES):
        r0 = rg * SUBLANES
        h = hst_scr[r0:r0 + SUBLANES, :]
        for t in range(tc):
            q = t * nb + r0
            h = a_scr[q:q + SUBLANES, :] * h + b_scr[q:q + SUBLANES, :]
            b_scr[q:q + SUBLANES, :] = h
        hst_scr[r0:r0 + SUBLANES, :] = h
    y_rg = b_scr[...] * _gelu_tanh(g_rg)

    ub = u.astype(bf16)
    half = S5_N // 2
    for k in range(2):
        us = ub[:, k * MXU_DIM:(k + 1) * MXU_DIM]
        bur_scr[:, k * half:(k + 1) * half] = jnp.dot(us, bbre_ref[k], preferred_element_type=f32)
        bui_scr[:, k * half:(k + 1) * half] = jnp.dot(us, bbim_ref[k], preferred_element_type=f32)
    for rg in range(nb // SUBLANES):
        r0 = rg * SUBLANES
        for cb in range(S5_N // S5_SCAN_COLS):
            c0 = cb * S5_SCAN_COLS
            ar = jnp.broadcast_to(are_ref[:, c0:c0 + S5_SCAN_COLS], (SUBLANES, S5_SCAN_COLS))
            ai = jnp.broadcast_to(aim_ref[:, c0:c0 + S5_SCAN_COLS], (SUBLANES, S5_SCAN_COLS))
            xr = sr_scr[r0:r0 + SUBLANES, c0:c0 + S5_SCAN_COLS]
            xi = si_scr[r0:r0 + SUBLANES, c0:c0 + S5_SCAN_COLS]
            for t in range(tc):
                q = t * nb + r0
                br = bur_scr[q:q + SUBLANES, c0:c0 + S5_SCAN_COLS]
                bi_ = bui_scr[q:q + SUBLANES, c0:c0 + S5_SCAN_COLS]
                nxr = ar * xr - ai * xi + br
                nxi = ar * xi + ai * xr + bi_
                bur_scr[q:q + SUBLANES, c0:c0 + S5_SCAN_COLS] = nxr
                bui_scr[q:q + SUBLANES, c0:c0 + S5_SCAN_COLS] = nxi
                xr, xi = nxr, nxi
            sr_scr[r0:r0 + SUBLANES, c0:c0 + S5_SCAN_COLS] = xr
            si_scr[r0:r0 + SUBLANES, c0:c0 + S5_SCAN_COLS] = xi
    ys = []
    for j in range(D_S5 // MXU_DIM):
        xrb = bur_scr[:, j * half:(j + 1) * half].astype(bf16)
        xib = bui_scr[:, j * half:(j + 1) * half].astype(bf16)
        ys.append(jnp.dot(xrb, cre_ref[j], preferred_element_type=f32)
                  - jnp.dot(xib, cim_ref[j], preferred_element_type=f32))
    y_s5 = jnp.concatenate(ys, axis=1) + d_ref[...] * u
    yg = _gelu_tanh(y_s5)
    glu = jnp.dot(yg.astype(bf16), wglu_ref[...], preferred_element_type=f32) + bglu_ref[...]
    y_s5 = yg * jax.nn.sigmoid(glu)

    ycat = jnp.concatenate([y_rg, y_s5], axis=1).astype(bf16)
    mix = jnp.dot(ycat, wout_ref[...], preferred_element_type=f32)
    hout_ref[...] = _layer_norm(DN_ALPHA * x + mix, ln_g_ref[...], ln_b_ref[...])

    @pl.when(c == pl.num_programs(0) - 1)
    def _():
        conv_out_ref[...] = pad_scr[0:tail, :]
        hlast_ref[...] = hst_scr[...]
        sre_out_ref[...] = sr_scr[...]
        sim_out_ref[...] = si_scr[...]


def _full(shape):
    n = len(shape)
    return pl.BlockSpec(shape, lambda c: (0,) * n)


def _mixer(x, conv0, h0, s0r, s0i, params, *, nb, tc, name, h_all, total_rows, first_row):
    batch_major_input = x.ndim == 3
    rows = nb * tc
    if batch_major_input:
        n_chunks = x.shape[1] // tc
        x_spec = pl.BlockSpec((nb, tc, D_MODEL), lambda c: (0, c, 0))
    else:
        n_chunks = x.shape[0] // rows
        x_spec = pl.BlockSpec((rows, D_MODEL), lambda c: (c, 0))
    first_block = first_row // rows
    aliased = h_all.shape == (total_rows, D_MODEL)
    tail = (CONV_WIDTH - 1) * nb
    small = (conv0, h0, s0r, s0i) + tuple(params)
    in_specs = [pl.BlockSpec(memory_space=pl.ANY), x_spec]
    in_specs += [_full(a.shape) for a in small]
    out_shape = (jax.ShapeDtypeStruct((total_rows, D_MODEL), f32),
                 jax.ShapeDtypeStruct((tail, D_RG), f32),
                 jax.ShapeDtypeStruct((nb, D_RG), f32),
                 jax.ShapeDtypeStruct((nb, S5_N), f32),
                 jax.ShapeDtypeStruct((nb, S5_N), f32))
    out_specs = (pl.BlockSpec((rows, D_MODEL), lambda c: (c + first_block, 0)),
                 _full((tail, D_RG)), _full((nb, D_RG)), _full((nb, S5_N)), _full((nb, S5_N)))
    scratch = [pltpu.VMEM((rows + tail, D_RG), f32),
               pltpu.VMEM((rows, D_RG), f32),
               pltpu.VMEM((rows, D_RG), f32),
               pltpu.VMEM((rows, S5_N), f32),
               pltpu.VMEM((rows, S5_N), f32),
               pltpu.VMEM((nb, D_RG), f32),
               pltpu.VMEM((nb, S5_N), f32),
               pltpu.VMEM((nb, S5_N), f32)]
    return pl.pallas_call(
        functools.partial(_mixer_kernel, nb=nb, tc=tc, batch_major_input=batch_major_input),
        grid=(n_chunks,),
        in_specs=in_specs,
        out_specs=out_specs,
        out_shape=out_shape,
        scratch_shapes=scratch,
        input_output_aliases={0: 0} if aliased else {},
        compiler_params=pltpu.CompilerParams(
            dimension_semantics=("arbitrary",), vmem_limit_bytes=VMEM_LIMIT_BYTES),
        name=name,
    )(h_all, x, *small)


TOK_TILE = 256
RUN_ALIGN = 16
OVF_ALIGN = 8
FIX = 32
FIX_SLOTS = N_EXPERTS * FIX
OVF_CHUNKS = 256
OVF_CHUNK_SLOTS = 384
OVF_SLOTS = 6 * OVF_CHUNK_SLOTS
SLOT_CHUNK = 512
FIX_STEPS = 4
ROW_TILE = 256
X_BUFFERS = 4
FIXED_BUFFERS = 3
COPY_GROUP = 4
EXT = D_MODEL + LANES
ROUTER_TILES = 6
NO_RUN = 1.0e9


def _top_k_gates(scores, rb):
    rows = scores.shape[0]
    lane_f = jax.lax.broadcasted_iota(jnp.int32, (rows, LANES), 1).astype(f32)
    biased = jnp.where(lane_f < float(N_EXPERTS), scores + rb, -jnp.inf)
    sel = jnp.zeros((rows, LANES), f32)
    mask = jnp.zeros((rows, LANES), f32)
    for _ in range(TOP_K):
        m = jnp.max(biased, axis=1, keepdims=True)
        idx = jnp.min(jnp.where(biased == m, lane_f, float(LANES)), axis=1, keepdims=True)
        hit = lane_f == idx
        sel = jnp.where(hit, scores, sel)
        mask = jnp.where(hit, 1.0, mask)
        biased = jnp.where(hit, -jnp.inf, biased)
    gates = sel / jnp.sum(sel, axis=1, keepdims=True) * ROUTED_SCALE
    return mask, gates


def _router_kernel(h_ref, wr_ref, rb_ref, hext_ref, rankm_ref, cnt_ref):
    hb = h_ref[...].astype(bf16)
    scores = jax.nn.sigmoid(jnp.dot(hb, wr_ref[...], preferred_element_type=f32))
    mask, gates = _top_k_gates(scores, rb_ref[...])
    t_row = jax.lax.broadcasted_iota(jnp.int32, (TOK_TILE, TOK_TILE), 0)
    t_col = jax.lax.broadcasted_iota(jnp.int32, (TOK_TILE, TOK_TILE), 1)
    earlier = jnp.where(t_col < t_row, 1.0, 0.0).astype(bf16)
    for sub in range(ROUTER_TILES):
        m = mask[sub * TOK_TILE:(sub + 1) * TOK_TILE]
        rank = jnp.dot(earlier, m.astype(bf16), preferred_element_type=f32)
        rankm_ref[sub * TOK_TILE:(sub + 1) * TOK_TILE, :] = jnp.where(m > 0.0, rank, -1.0).astype(bf16)
        cnt_ref[sub] = jnp.broadcast_to(jnp.sum(m, axis=0, keepdims=True), (SUBLANES, LANES))
    g_hi = gates.astype(bf16).astype(f32)
    g_pack = g_hi + pltpu.roll(gates - g_hi, N_EXPERTS, 1)
    hext_ref[:, :D_MODEL] = hb
    hext_ref[:, D_MODEL:] = g_pack.astype(bf16)


def _router(h, wr, rb):
    n_tiles = h.shape[0] // TOK_TILE
    assert n_tiles % ROUTER_TILES == 0
    rows = ROUTER_TILES * TOK_TILE
    const = lambda shape: pl.BlockSpec(shape, lambda i: (0,) * len(shape))
    return pl.pallas_call(
        _router_kernel,
        grid=(n_tiles // ROUTER_TILES,),
        in_specs=[pl.BlockSpec((rows, D_MODEL), lambda i: (i, 0)), const(wr.shape), const(rb.shape)],
        out_specs=(pl.BlockSpec((rows, EXT), lambda i: (i, 0)),
                   pl.BlockSpec((rows, LANES), lambda i: (i, 0)),
                   pl.BlockSpec((ROUTER_TILES, SUBLANES, LANES), lambda i: (i, 0, 0))),
        out_shape=(jax.ShapeDtypeStruct((h.shape[0], EXT), bf16),
                   jax.ShapeDtypeStruct((h.shape[0], LANES), bf16),
                   jax.ShapeDtypeStruct((n_tiles, SUBLANES, LANES), f32)),
        compiler_params=pltpu.CompilerParams(
            dimension_semantics=("arbitrary",), vmem_limit_bytes=VMEM_LIMIT_BYTES),
        name="moe_router",
    )(h, wr, rb)


def _run_copy(src, dst, sem):
    return pltpu.make_async_copy(src, dst, sem)


def _fixed_ranges(shape, axis):
    e = jax.lax.broadcasted_iota(jnp.int32, shape, axis).astype(f32)
    loc = jnp.where(e < float(N_EXPERTS), e * float(FIX), NO_RUN)
    return loc, loc + float(FIX)


def _slot_rows(first_slot, loc_row, end_row, base, rankm, n_slots=SLOT_CHUNK):
    s = (jax.lax.broadcasted_iota(jnp.int32, (n_slots, LANES), 0) + first_slot).astype(f32)
    in_run = jnp.where(s >= loc_row, jnp.where(s < end_row, 1.0, 0.0), 0.0)
    r_col = s[:, 0:1] - jnp.sum(in_run * (loc_row - base), axis=1, keepdims=True)
    q = jax.lax.dot_general(in_run.astype(bf16), rankm, (((1,), (1,)), ((), ())),
                            preferred_element_type=f32)
    return jnp.where(q == r_col, 1.0, 0.0).astype(bf16)


def _slot_cols(first_slot, loc_col, end_col, base, rankm, n_slots=SLOT_CHUNK):
    s = (jax.lax.broadcasted_iota(jnp.int32, (LANES, n_slots), 1) + first_slot).astype(f32)
    in_run = jnp.where(s >= loc_col, jnp.where(s < end_col, 1.0, 0.0), 0.0)
    r_row = s[0:1, :] - jnp.sum(in_run * (loc_col - base), axis=0, keepdims=True)
    q = jnp.dot(rankm, in_run.astype(bf16), preferred_element_type=f32)
    return jnp.where(q == r_row, 1.0, 0.0).astype(bf16)


def _wait_groups(n_groups, make_copy):
    max_bits = (OVF_CHUNKS // COPY_GROUP).bit_length()
    for b in range(max_bits):
        rows = (COPY_GROUP * OVF_ALIGN) << b

        @pl.when(jnp.bitwise_and(jnp.right_shift(n_groups, b), 1) == 1)
        def _():
            make_copy(rows).wait()


def _dispatch_kernel(totc_ref, ngrp_ref, dstk_ref, zst_ref, zch_ref,
                     hext_ref, rankm_ref, locs_ref, sorted_f_ref, sorted_o_ref,
                     stage_f, stage_o, zbuf, semf, semo, zsem):
    i = pl.program_id(0)
    n = pl.num_programs(0)
    slot = i % 2
    rankm = rankm_ref[...]

    fx_loc, fx_end = _fixed_ranges((1, LANES), 1)
    experts_per_chunk = SLOT_CHUNK // FIX
    for ch in range(FIX_SLOTS // SLOT_CHUNK):
        p = _slot_rows(ch * SLOT_CHUNK, fx_loc, fx_end, 0.0, rankm)
        rows = jnp.dot(p, hext_ref[...], preferred_element_type=f32).astype(bf16)
        stage_f[slot, ch * experts_per_chunk:(ch + 1) * experts_per_chunk] = rows.reshape(
            experts_per_chunk, FIX, EXT)

    ov_loc = locs_ref[0, 0:1, :]
    ov_end = locs_ref[0, 1:2, :]
    for ch in range(OVF_SLOTS // OVF_CHUNK_SLOTS):
        def sort_overflow(ch=ch):
            p = _slot_rows(FIX_SLOTS + ch * OVF_CHUNK_SLOTS, ov_loc, ov_end, float(FIX), rankm,
                           OVF_CHUNK_SLOTS)
            stage_o[slot, ch * OVF_CHUNK_SLOTS:(ch + 1) * OVF_CHUNK_SLOTS, :] = jnp.dot(
                p, hext_ref[...], preferred_element_type=f32)
        pl.when(totc_ref[i] * OVF_ALIGN > ch * OVF_CHUNK_SLOTS)(sort_overflow)

    def fixed_copy(tile, slot_):
        dst = sorted_f_ref.at[:, pl.ds(pl.multiple_of(tile * FIX, FIX), FIX), :]
        return _run_copy(stage_f.at[slot_], dst, semf.at[slot_])

    def overflow_wait(rows, slot_):
        return _run_copy(stage_o.at[slot_, pl.ds(0, rows)], sorted_o_ref.at[pl.ds(0, rows)],
                         semo.at[slot_])

    fixed_copy(i, slot).start()

    def start_group(g, carry):
        for u in range(COPY_GROUP):
            k = g * COPY_GROUP + u
            src = stage_o.at[slot, pl.ds(pl.multiple_of(k * OVF_ALIGN, OVF_ALIGN), OVF_ALIGN)]
            dst = sorted_o_ref.at[pl.ds(pl.multiple_of(dstk_ref[i, k], OVF_ALIGN), OVF_ALIGN)]
            _run_copy(src, dst, semo.at[slot]).start(priority=u % 2)
        return carry
    jax.lax.fori_loop(0, ngrp_ref[i], start_group, 0)

    @pl.when(i == 0)
    def _():
        zbuf[...] = jnp.zeros(zbuf.shape, f32)

    def zero_copy(e, k):
        dst = sorted_o_ref.at[pl.ds(pl.multiple_of(zst_ref[e] + k * OVF_ALIGN, OVF_ALIGN), OVF_ALIGN)]
        return _run_copy(zbuf, dst, zsem.at[0])

    @pl.when(jnp.logical_and(i > 0, i <= N_EXPERTS))
    def _():
        def wait_zero(k, carry):
            zero_copy(i - 1, k).wait()
            return carry
        jax.lax.fori_loop(0, zch_ref[i - 1], wait_zero, 0)

    @pl.when(i < N_EXPERTS)
    def _():
        def start_zero(k, carry):
            zero_copy(i, k).start()
            return carry
        jax.lax.fori_loop(0, zch_ref[i], start_zero, 0)

    @pl.when(i > 0)
    def _():
        fixed_copy(i - 1, 1 - slot).wait()
        _wait_groups(ngrp_ref[i - 1], lambda rows: overflow_wait(rows, 1 - slot))

    @pl.when(i == n - 1)
    def _():
        fixed_copy(i, slot).wait()
        _wait_groups(ngrp_ref[i], lambda rows: overflow_wait(rows, slot))


def _dispatch(hext, rankm, locs, totc, ngrp, dstk, zst, zch, *, overflow_rows):
    n_tiles = hext.shape[0] // TOK_TILE
    assert n_tiles > N_EXPERTS
    grid_spec = pltpu.PrefetchScalarGridSpec(
        num_scalar_prefetch=5,
        grid=(n_tiles,),
        in_specs=[pl.BlockSpec((TOK_TILE, EXT), lambda i, *_: (i, 0)),
                  pl.BlockSpec((TOK_TILE, LANES), lambda i, *_: (i, 0)),
                  pl.BlockSpec((1, SUBLANES, LANES), lambda i, *_: (i, 0, 0))],
        out_specs=(pl.BlockSpec(memory_space=pl.ANY), pl.BlockSpec(memory_space=pl.ANY)),
        scratch_shapes=[pltpu.VMEM((2, N_EXPERTS, FIX, EXT), bf16),
                        pltpu.VMEM((2, OVF_SLOTS, EXT), f32),
                        pltpu.VMEM((OVF_ALIGN, EXT), f32),
                        pltpu.SemaphoreType.DMA((2,)),
                        pltpu.SemaphoreType.DMA((2,)),
                        pltpu.SemaphoreType.DMA((1,))])
    return pl.pallas_call(
        _dispatch_kernel,
        grid_spec=grid_spec,
        out_shape=(jax.ShapeDtypeStruct((N_EXPERTS, n_tiles * FIX, EXT), bf16),
                   jax.ShapeDtypeStruct((overflow_rows, EXT), f32)),
        compiler_params=pltpu.CompilerParams(
            dimension_semantics=("arbitrary",), vmem_limit_bytes=VMEM_LIMIT_BYTES),
        name="moe_dispatch",
    )(totc, ngrp, dstk, zst, zch, hext, rankm, locs)


def _experts_kernel(ts_ref, xf_ref, xo_ref, wgu_ref, wd_ref, yf_ref, yo_ref,
                    xfbuf, wgu_f, wd_f, xbuf, ybuf, wgu_b, wd_b, fsem, xsem, ysem):
    e = pl.program_id(0)
    n_valid = ts_ref[N_EXPERTS]

    def ring_copies(expert, slot):
        return (pltpu.make_async_copy(xf_ref.at[expert], xfbuf.at[slot], fsem.at[0, slot]),
                pltpu.make_async_copy(wgu_ref.at[expert], wgu_f.at[slot], fsem.at[1, slot]),
                pltpu.make_async_copy(wd_ref.at[expert], wd_f.at[slot], fsem.at[2, slot]))

    @pl.when(e == 0)
    def _():
        for a in range(FIXED_BUFFERS - 1):
            for copy in ring_copies(a, a):
                copy.start()

    @pl.when(e + FIXED_BUFFERS - 1 < pl.num_programs(0))
    def _():
        for copy in ring_copies(e + FIXED_BUFFERS - 1, (e + FIXED_BUFFERS - 1) % FIXED_BUFFERS):
            copy.start()

    fslot = e % FIXED_BUFFERS
    for copy in ring_copies(e, fslot):
        copy.wait()
    wgu_b[...] = wgu_f[fslot].astype(bf16)
    wd_b[...] = wd_f[fslot].astype(bf16)

    def ffn(x_ext):
        m = x_ext.shape[0]
        gu = jnp.dot(x_ext[:, :D_MODEL], wgu_b[...], preferred_element_type=f32)
        act = (jax.nn.silu(gu[:, :D_EXPERT]) * gu[:, D_EXPERT:]).astype(bf16)
        y = jnp.dot(act, wd_b[...], preferred_element_type=f32)
        g_pack = x_ext[:, D_MODEL:].astype(f32)
        lane = jax.lax.broadcasted_iota(jnp.int32, (m, LANES), 1)
        mine = jnp.where(lane == e, g_pack, jnp.where(lane == e + N_EXPERTS, g_pack, 0.0))
        return (y * jnp.sum(mine, axis=1, keepdims=True)).astype(bf16)

    def x_copy(j, slot):
        rows = pl.ds(pl.multiple_of(j * ROW_TILE, ROW_TILE), ROW_TILE)
        return pltpu.make_async_copy(xo_ref.at[rows], xbuf.at[slot], xsem.at[slot])

    def y_copy(j, slot):
        rows = pl.ds(pl.multiple_of(j * ROW_TILE, ROW_TILE), ROW_TILE)
        return pltpu.make_async_copy(ybuf.at[slot], yo_ref.at[rows], ysem.at[slot])

    ahead = X_BUFFERS - 1

    @pl.when(e == 0)
    def _():
        for a in range(ahead):
            @pl.when(a < n_valid)
            def _():
                x_copy(a, a).start()

    step_rows = xfbuf.shape[1] // FIX_STEPS
    for c in range(FIX_STEPS):
        rows = slice(c * step_rows, (c + 1) * step_rows)
        yf_ref[0, rows, :] = ffn(xfbuf[fslot, rows, :])

    def tile(j, carry):
        xslot = j % X_BUFFERS
        slot = j % 2
        x_copy(j, xslot).wait()

        @pl.when(j + ahead < n_valid)
        def _():
            x_copy(j + ahead, (j + ahead) % X_BUFFERS).start()

        @pl.when(j >= 2)
        def _():
            y_copy(j - 2, slot).wait()

        ybuf[slot] = ffn(xbuf[xslot].astype(bf16)).astype(f32)
        y_copy(j, slot).start()
        return carry
    jax.lax.fori_loop(ts_ref[e], ts_ref[e + 1], tile, 0)

    @pl.when(e == pl.num_programs(0) - 1)
    def _():
        @pl.when(n_valid >= 2)
        def _():
            y_copy(n_valid - 2, n_valid % 2).wait()

        @pl.when(n_valid >= 1)
        def _():
            y_copy(n_valid - 1, (n_valid - 1) % 2).wait()


def _experts(xf, xo, wgu, wd, ts, *, overflow_rows):
    fixed_rows = xf.shape[1]
    assert fixed_rows % (FIX_STEPS * RUN_ALIGN) == 0
    grid_spec = pltpu.PrefetchScalarGridSpec(
        num_scalar_prefetch=1,
        grid=(N_EXPERTS,),
        in_specs=[pl.BlockSpec(memory_space=pl.ANY),
                  pl.BlockSpec(memory_space=pl.ANY),
                  pl.BlockSpec(memory_space=pl.ANY),
                  pl.BlockSpec(memory_space=pl.ANY)],
        out_specs=(pl.BlockSpec((1, fixed_rows, D_MODEL), lambda e, ts: (e, 0, 0)),
                   pl.BlockSpec(memory_space=pl.ANY)),
        scratch_shapes=[pltpu.VMEM((FIXED_BUFFERS, fixed_rows, EXT), bf16),
                        pltpu.VMEM((FIXED_BUFFERS, D_MODEL, 2 * D_EXPERT), f32),
                        pltpu.VMEM((FIXED_BUFFERS, D_EXPERT, D_MODEL), f32),
                        pltpu.VMEM((X_BUFFERS, ROW_TILE, EXT), f32),
                        pltpu.VMEM((2, ROW_TILE, D_MODEL), f32),
                        pltpu.VMEM((D_MODEL, 2 * D_EXPERT), bf16),
                        pltpu.VMEM((D_EXPERT, D_MODEL), bf16),
                        pltpu.SemaphoreType.DMA((3, FIXED_BUFFERS)),
                        pltpu.SemaphoreType.DMA((X_BUFFERS,)),
                        pltpu.SemaphoreType.DMA((2,))])
    return pl.pallas_call(
        _experts_kernel,
        grid_spec=grid_spec,
        out_shape=(jax.ShapeDtypeStruct((N_EXPERTS, fixed_rows, D_MODEL), bf16),
                   jax.ShapeDtypeStruct((overflow_rows, D_MODEL), f32)),
        compiler_params=pltpu.CompilerParams(
            dimension_semantics=("arbitrary",), vmem_limit_bytes=VMEM_LIMIT_BYTES),
        name="moe_experts",
    )(ts, xf, xo, wgu, wd)


def _combine_kernel(totc_ref, ngrp_ref, srck_ref,
                    h_ref, rankm_ref, locc_ref, yf_ref, yo_ref, wsu_ref, wsd_ref, ln_g_ref, ln_b_ref,
                    out_bm_ref, out_tm_ref, yloc_f, yloc_o, acc_scr, p_scr, semf, semo,
                    *, n_bm_tiles):
    i = pl.program_id(0)
    n = pl.num_programs(0)
    slot = i % 2

    def fixed_copy(tile, slot_):
        src = yf_ref.at[:, pl.ds(pl.multiple_of(tile * FIX, FIX), FIX), :]
        return _run_copy(src, yloc_f.at[slot_], semf.at[slot_])

    def overflow_wait(rows, slot_):
        return _run_copy(yo_ref.at[pl.ds(0, rows)], yloc_o.at[slot_, pl.ds(0, rows)], semo.at[slot_])

    def fetch(tile, slot_):
        fixed_copy(tile, slot_).start()

        def start_group(g, carry):
            for u in range(COPY_GROUP):
                k = g * COPY_GROUP + u
                src = yo_ref.at[pl.ds(pl.multiple_of(srck_ref[tile, k], OVF_ALIGN), OVF_ALIGN)]
                dst = yloc_o.at[slot_, pl.ds(pl.multiple_of(k * OVF_ALIGN, OVF_ALIGN), OVF_ALIGN)]
                _run_copy(src, dst, semo.at[slot_]).start(priority=u % 2)
            return carry
        jax.lax.fori_loop(0, ngrp_ref[tile], start_group, 0)

    @pl.when(i == 0)
    def _():
        yloc_o[...] = jnp.zeros(yloc_o.shape, f32)
        fetch(0, 0)

    @pl.when(i + 1 < n)
    def _():
        fetch(i + 1, 1 - slot)

    h = h_ref[...]
    hb = h.astype(bf16)
    su = jnp.dot(hb, wsu_ref[...], preferred_element_type=f32)
    act = (jax.nn.silu(su[:, :D_SHARED]) * su[:, D_SHARED:]).astype(bf16)
    acc_scr[...] = jnp.dot(act, wsd_ref[...], preferred_element_type=f32)

    rankm = rankm_ref[...]
    fx_loc, fx_end = _fixed_ranges((LANES, 1), 0)
    for ch in range(FIX_SLOTS // SLOT_CHUNK):
        p_scr[:, ch * SLOT_CHUNK:(ch + 1) * SLOT_CHUNK] = _slot_cols(
            ch * SLOT_CHUNK, fx_loc, fx_end, 0.0, rankm)

    fixed_copy(i, slot).wait()
    _wait_groups(ngrp_ref[i], lambda rows: overflow_wait(rows, slot))

    acc_scr[...] += jnp.dot(p_scr[...], yloc_f[slot].reshape(FIX_SLOTS, D_MODEL),
                            preferred_element_type=f32)
    ov_loc = locc_ref[0, :, 0:1]
    ov_end = locc_ref[0, :, 1:2]
    for ch in range(OVF_SLOTS // OVF_CHUNK_SLOTS):
        def gather_overflow(ch=ch):
            p = _slot_cols(FIX_SLOTS + ch * OVF_CHUNK_SLOTS, ov_loc, ov_end, float(FIX), rankm,
                           OVF_CHUNK_SLOTS)
            y_o = yloc_o[slot, ch * OVF_CHUNK_SLOTS:(ch + 1) * OVF_CHUNK_SLOTS, :].astype(bf16)
            acc_scr[...] += jnp.dot(p, y_o, preferred_element_type=f32)
        pl.when(totc_ref[i] * OVF_ALIGN > ch * OVF_CHUNK_SLOTS)(gather_overflow)
    y = _layer_norm(DN_ALPHA * h + acc_scr[...], ln_g_ref[...], ln_b_ref[...])

    @pl.when(i < n_bm_tiles)
    def _():
        nb, tc, _ = out_bm_ref.shape
        out_bm_ref[...] = jnp.transpose(y.reshape(tc, nb, D_MODEL), (1, 0, 2))

    @pl.when(i >= n_bm_tiles)
    def _():
        out_tm_ref[...] = y


def _combine(h, rankm, locc, yf, yo, wsu, wsd, ln_g, ln_b, totc, ngrp, srck, *, bm_shape):
    n_tiles = h.shape[0] // TOK_TILE
    nb, length, _ = bm_shape
    tc = TOK_TILE // nb
    n_bm_tiles = length // tc
    n_tm_tiles = n_tiles - n_bm_tiles
    const = lambda shape: pl.BlockSpec(shape, lambda i, *_: (0,) * len(shape))
    grid_spec = pltpu.PrefetchScalarGridSpec(
        num_scalar_prefetch=3,
        grid=(n_tiles,),
        in_specs=[pl.BlockSpec((TOK_TILE, D_MODEL), lambda i, *_: (i, 0)),
                  pl.BlockSpec((TOK_TILE, LANES), lambda i, *_: (i, 0)),
                  pl.BlockSpec((1, LANES, 2), lambda i, *_: (i, 0, 0)),
                  pl.BlockSpec(memory_space=pl.ANY),
                  pl.BlockSpec(memory_space=pl.ANY),
                  const(wsu.shape), const(wsd.shape), const(ln_g.shape), const(ln_b.shape)],
        out_specs=(pl.BlockSpec((nb, tc, D_MODEL),
                                lambda i, *_: (0, jnp.minimum(i, n_bm_tiles - 1), 0)),
                   pl.BlockSpec((TOK_TILE, D_MODEL),
                                lambda i, *_: (jnp.maximum(i - n_bm_tiles, 0), 0))),
        scratch_shapes=[pltpu.VMEM((2, N_EXPERTS, FIX, D_MODEL), bf16),
                        pltpu.VMEM((2, OVF_SLOTS, D_MODEL), f32),
                        pltpu.VMEM((TOK_TILE, D_MODEL), f32),
                        pltpu.VMEM((TOK_TILE, FIX_SLOTS), bf16),
                        pltpu.SemaphoreType.DMA((2,)),
                        pltpu.SemaphoreType.DMA((2,))])
    return pl.pallas_call(
        functools.partial(_combine_kernel, n_bm_tiles=n_bm_tiles),
        grid_spec=grid_spec,
        out_shape=(jax.ShapeDtypeStruct(bm_shape, f32),
                   jax.ShapeDtypeStruct((n_tm_tiles * TOK_TILE, D_MODEL), f32)),
        compiler_params=pltpu.CompilerParams(
            dimension_semantics=("arbitrary",), vmem_limit_bytes=VMEM_LIMIT_BYTES),
        name="moe_combine",
    )(totc, ngrp, srck, h, rankm, locc, yf, yo, wsu, wsd, ln_g, ln_b)


def _round_up(x, m):
    return (x + m - 1) // m * m


def _moe(h, wr, rb, wgu, wd, wsu, wsd, ln_g, ln_b, *, bm_shape):
    n_tok = h.shape[0]
    n_tiles = n_tok // TOK_TILE
    max_rows = _round_up(n_tiles * (OVF_CHUNKS - 1) * OVF_ALIGN + N_EXPERTS * (ROW_TILE - 1), ROW_TILE)

    hext, rankm, cnt = _router(h, wr, rb)

    i32 = jnp.int32
    cnt = cnt[:, 0, :N_EXPERTS].astype(i32)
    oc = (jnp.maximum(cnt - FIX, 0) + OVF_ALIGN - 1) // OVF_ALIGN
    over_tiles = jnp.cumsum(oc, axis=0)
    region_rows = over_tiles[-1] * OVF_ALIGN
    region_size = _round_up(region_rows, ROW_TILE)
    region_start = jnp.cumsum(region_size) - region_size
    run_dst = region_start[None, :] + OVF_ALIGN * (over_tiles - oc)
    ch_end = jnp.cumsum(oc, axis=1)
    ch_beg = ch_end - oc
    totc = ch_end[:, -1]
    k = jnp.arange(OVF_CHUNKS, dtype=i32)
    kk = k[None, :, None]
    mine = jnp.logical_and(ch_beg[:, None, :] <= kk, kk < ch_end[:, None, :])
    chunk_dst = jnp.sum(jnp.where(mine, run_dst[:, None, :] + OVF_ALIGN * (kk - ch_beg[:, None, :]), 0),
                        axis=-1)
    live = k[None, :] < totc[:, None]
    tile_ids = jnp.arange(n_tiles, dtype=i32)[:, None]
    spare = max_rows + (tile_ids % 2) * OVF_SLOTS + k[None, :] * OVF_ALIGN
    dstk = jnp.where(live, chunk_dst, spare).astype(i32)
    srck = jnp.where(live, chunk_dst, 0).astype(i32)
    ngrp = ((totc + COPY_GROUP - 1) // COPY_GROUP).astype(i32)
    zst = (region_start + region_rows).astype(i32)
    zch = ((region_size - region_rows) // OVF_ALIGN).astype(i32)
    tile_start = jnp.concatenate([region_start, region_start[-1:] + region_size[-1:]]) // ROW_TILE
    tile_start = tile_start.astype(i32)

    pad = ((0, 0), (0, LANES - N_EXPERTS))
    loc_f = jnp.pad((FIX_SLOTS + OVF_ALIGN * ch_beg).astype(f32), pad, constant_values=NO_RUN)
    end_f = jnp.pad((FIX_SLOTS + OVF_ALIGN * ch_end).astype(f32), pad, constant_values=NO_RUN)
    locs = jnp.concatenate([loc_f[:, None, :], end_f[:, None, :],
                            jnp.zeros((n_tiles, SUBLANES - 2, LANES), f32)], axis=1)
    locc = jnp.stack([loc_f, end_f], axis=-1)

    totc = totc.astype(i32)
    overflow_rows = max_rows + 2 * OVF_SLOTS
    xf, xo = _dispatch(hext, rankm, locs, totc, ngrp, dstk, zst, zch, overflow_rows=overflow_rows)
    yf, yo = _experts(xf, xo, wgu, wd, tile_start, overflow_rows=max_rows)
    return _combine(h, rankm, locc, yf, yo, wsu, wsd, ln_g, ln_b, totc, ngrp, srck, bm_shape=bm_shape)


def _diag_tiles(blocks, n_tiles):
    n_blocks, r, c = blocks.shape
    per = n_blocks // n_tiles
    wide = jnp.tile(blocks.reshape(n_blocks * r, c), (1, per))
    row_block = (jnp.arange(n_blocks * r) // r) % per
    col_block = jnp.arange(per * c) // c
    wide = jnp.where(row_block[:, None] == col_block[None, :], wide, 0)
    return wide.reshape(n_tiles, per * r, per * c)


def _head_block_diag(w):
    return _diag_tiles(w, D_RG // MXU_DIM)


def _s5_in_tiles(b):
    return _diag_tiles(b.transpose(0, 2, 1), 2)


def _s5_out_tiles(cw):
    return _diag_tiles(cw.transpose(0, 2, 1), 2)


def _row(v):
    return v.reshape(1, -1)


def kernel(x_prompt, x_sample, state_rg_conv, state_rg_h, state_s5_re, state_s5_im, w_in, conv_w, conv_b, rg_w_a, rg_b_a, rg_w_i, rg_b_i, rg_lam, s5_a_re, s5_a_im, s5_log_dt, s5_b_re, s5_b_im, s5_c_re, s5_c_im, s5_d, w_glu, b_glu, w_out, ln1_g, ln1_b, w_router, router_bias, w_gate_up, w_down, w_shared_up, w_shared_down, ln2_g, ln2_b):
    l = 0
    bp, lp, _ = x_prompt.shape
    bs, ls, _ = x_sample.shape

    are, aim, bbre, bbim = _s5_prep(
        _row(s5_a_re[l]), _row(s5_a_im[l]),
        _row(jnp.repeat(s5_log_dt[l], S5_STATE)),
        _s5_in_tiles(s5_b_re[l]), _s5_in_tiles(s5_b_im[l]))
    params = (w_in[l].astype(bf16), conv_w[l], _row(conv_b[l]),
              _head_block_diag(rg_w_a[l]).astype(bf16), _row(rg_b_a[l]),
              _head_block_diag(rg_w_i[l]).astype(bf16), _row(rg_b_i[l]), _row(rg_lam[l]),
              are, aim, bbre, bbim,
              _s5_out_tiles(s5_c_re[l]).astype(bf16), _s5_out_tiles(s5_c_im[l]).astype(bf16),
              _row(s5_d[l]), w_glu[l].astype(bf16), _row(b_glu[l]), w_out[l].astype(bf16),
              _row(ln1_g[l]), _row(ln1_b[l]))

    tail = CONV_WIDTH - 1
    n_tok = lp * bp + ls * bs
    xs_tm = x_sample.transpose(1, 0, 2).reshape(ls * bs, D_MODEL)
    h_all, sc, sh, sre, sim = _mixer(
        xs_tm, state_rg_conv[l].transpose(1, 0, 2).reshape(tail * bs, D_RG), state_rg_h[l],
        state_s5_re[l].reshape(bs, S5_N), state_s5_im[l].reshape(bs, S5_N), params,
        nb=bs, tc=ls, name="mixer_sample",
        h_all=jnp.zeros((SUBLANES, LANES), f32), total_rows=n_tok, first_row=lp * bp)
    h_all, pc, ph, pre, pim = _mixer(
        x_prompt, jnp.zeros((tail * bp, D_RG), f32), jnp.zeros((bp, D_RG), f32),
        jnp.zeros((bp, S5_N), f32), jnp.zeros((bp, S5_N), f32), params,
        nb=bp, tc=PROMPT_CHUNK_ROWS // bp, name="mixer_prompt",
        h_all=h_all, total_rows=n_tok, first_row=0)
    wr = jnp.pad(w_router[l], ((0, 0), (0, LANES - N_EXPERTS))).astype(bf16)
    rb = jnp.pad(_row(router_bias[l]), ((0, 0), (0, LANES - N_EXPERTS)))
    yp, ys_tm = _moe(h_all, wr, rb, w_gate_up[l], w_down[l],
                     w_shared_up[l].astype(bf16), w_shared_down[l].astype(bf16),
                     _row(ln2_g[l]), _row(ln2_b[l]), bm_shape=x_prompt.shape)
    ys = ys_tm.reshape(ls, bs, D_MODEL).transpose(1, 0, 2)

    def conv_out(cv, nbatch):
        return cv.reshape(tail, nbatch, D_RG).transpose(1, 0, 2)[None]

    return (yp, ys,
            conv_out(pc, bp), ph[None],
            pre.reshape(1, bp, S5_GROUPS, S5_STATE), pim.reshape(1, bp, S5_GROUPS, S5_STATE),
            conv_out(sc, bs), sh[None],
            sre.reshape(1, bs, S5_GROUPS, S5_STATE), sim.reshape(1, bs, S5_GROUPS, S5_STATE))
```
